```python
import jax
import jax.numpy as jnp
from jax import lax
import numpy as np

D_MODEL = 2048
BATCH = 4
SEQ = 2048
DEPTH = 1

N_HEADS = 16
HEAD_DIM = 128
N_KV_GROUPS = 4
HEADS_PER_GROUP = N_HEADS // N_KV_GROUPS
ATTN_WIDTH = N_HEADS * HEAD_DIM
KV_WIDTH = N_KV_GROUPS * HEAD_DIM
CMP_BLOCK = 32
CMP_STRIDE = 16
CMP_HIDDEN = 256
SEL_BLOCK = 64
SEL_TOP_N = 8
SEL_QUERY_CHUNK = 64
WINDOW = 512
WIN_QBLOCK = 128
SCALE = HEAD_DIM ** -0.5
CONV_WIDTH = D_MODEL
CONV_K = 3
N_BRANCHES = 2
RMS_EPS = 1e-6
NEG_INF = -1e30
FORCED_SCORE = 1e4

IN_SPLIT_SIZES = (ATTN_WIDTH,
                  KV_WIDTH, KV_WIDTH,
                  KV_WIDTH, KV_WIDTH,
                  KV_WIDTH, KV_WIDTH,
                  3 * N_HEADS,
                  ATTN_WIDTH,
                  CONV_WIDTH, CONV_WIDTH, CONV_WIDTH,
                  CONV_WIDTH,
                  N_BRANCHES * D_MODEL)
N_IN = sum(IN_SPLIT_SIZES)

kernel_name = "nsa_shortconv_gated_hybrid"


def rmsnorm(x, g):
    xf = x.astype(jnp.float32)
    xf = xf * lax.rsqrt(jnp.mean(xf * xf, axis=-1, keepdims=True) + RMS_EPS)
    return xf.astype(x.dtype) * g


def masked_softmax(s, mask):
    s = jnp.where(mask, s.astype(jnp.float32), NEG_INF)
    p = jax.nn.softmax(s, axis=-1)
    return jnp.where(mask, p, 0.0)


def compress_blocks(kv, pe, w1, w2):
    B, S, G, Dh = kv.shape
    n_cmp = (S - CMP_BLOCK) // CMP_STRIDE + 1
    idx = np.arange(n_cmp)[:, None] * CMP_STRIDE + np.arange(CMP_BLOCK)[None, :]
    blocks = kv[:, idx] + pe[None, None, :, None, :]
    flat = blocks.transpose(0, 1, 3, 2, 4).reshape(B, n_cmp, G, CMP_BLOCK * Dh)
    return jax.nn.silu(flat @ w1) @ w2


def compressed_attention(q, kcb, vcb):
    S = q.shape[1]
    n_cmp = kcb.shape[1]
    t = np.arange(S)
    blk_end = np.arange(n_cmp) * CMP_STRIDE + CMP_BLOCK - 1
    mask = jnp.asarray(blk_end[None, :] <= t[:, None])
    s = jnp.einsum('bsghd,bngd->bghsn', q, kcb) * SCALE
    p = masked_softmax(s, mask)
    o = jnp.einsum('bghsn,bngd->bsghd', p.astype(vcb.dtype), vcb)
    return o, p


def select_blocks(p_cmp, S):
    n_cmp = p_cmp.shape[-1]
    n_slc = S // SEL_BLOCK
    c0 = np.arange(n_cmp)[:, None] * CMP_STRIDE
    s0 = np.arange(n_slc)[None, :] * SEL_BLOCK
    overlap = np.maximum(0, np.minimum(c0 + CMP_BLOCK, s0 + SEL_BLOCK) - np.maximum(c0, s0))
    agg = jnp.asarray(overlap / CMP_BLOCK, dtype=jnp.float32)
    imp = jnp.einsum('bghsn,nj->bgsj', p_cmp, agg)
    t = np.arange(S)[:, None]
    j = np.arange(n_slc)[None, :]
    cur = t // SEL_BLOCK
    causal = j * SEL_BLOCK <= t
    forced = ((j == 0) | (j == cur) | (j == cur - 1)) & causal
    imp = jnp.where(jnp.asarray(forced), FORCED_SCORE, imp)
    imp = jnp.where(jnp.asarray(causal), imp, -1.0)
    _, idx = lax.top_k(imp, min(SEL_TOP_N, n_slc))
    return idx


def selected_attention(q, k, v, idx):
    B, S, G, h, Dh = q.shape
    n_slc = S // SEL_BLOCK
    n_top = idx.shape[-1]
    nc = S // SEL_QUERY_CHUNK
    kb = k.reshape(B, n_slc, SEL_BLOCK, G, Dh).transpose(0, 3, 1, 2, 4)
    vb = v.reshape(B, n_slc, SEL_BLOCK, G, Dh).transpose(0, 3, 1, 2, 4)
    q_ch = q.reshape(B, nc, SEL_QUERY_CHUNK, G, h, Dh).transpose(1, 0, 2, 3, 4, 5)
    i_ch = idx.reshape(B, G, nc, SEL_QUERY_CHUNK, n_top).transpose(2, 0, 1, 3, 4)
    t_ch = jnp.arange(S, dtype=jnp.int32).reshape(nc, SEL_QUERY_CHUNK)
    bi = jnp.arange(B)[:, None, None, None]
    gi = jnp.arange(G)[None, :, None, None]
    offs = jnp.arange(SEL_BLOCK, dtype=jnp.int32)

    def chunk(args):
        qc, ic, tc = args
        kg = kb[bi, gi, ic]
        vg = vb[bi, gi, ic]
        s = jnp.einsum('bqghd,bgqnkd->bgqhnk', qc, kg) * SCALE
        kpos = ic[..., None] * SEL_BLOCK + offs
        mask = kpos <= tc[None, None, :, None, None]
        qn = qc.shape[1]
        s = s.reshape(B, G, qn, h, n_top * SEL_BLOCK)
        mask = mask.reshape(B, G, qn, 1, n_top * SEL_BLOCK)
        p = masked_softmax(s, mask).astype(vg.dtype)
        return jnp.einsum('bgqhm,bgqmd->bqghd', p, vg.reshape(B, G, qn, n_top * SEL_BLOCK, Dh))

    o = lax.map(chunk, (q_ch, i_ch, t_ch))
    return o.transpose(1, 0, 2, 3, 4, 5).reshape(B, S, G, h, Dh)


def window_attention(q, k, v):
    B, S, G, h, Dh = q.shape
    nqb = S // WIN_QBLOCK
    n_off = WINDOW // WIN_QBLOCK + 1
    kpad = jnp.pad(k, ((0, 0), (WINDOW, 0), (0, 0), (0, 0)))
    vpad = jnp.pad(v, ((0, 0), (WINDOW, 0), (0, 0), (0, 0)))
    k_band = jnp.concatenate([kpad[:, j * WIN_QBLOCK: j * WIN_QBLOCK + S].reshape(B, nqb, WIN_QBLOCK, G, Dh)
                              for j in range(n_off)], axis=2)
    v_band = jnp.concatenate([vpad[:, j * WIN_QBLOCK: j * WIN_QBLOCK + S].reshape(B, nqb, WIN_QBLOCK, G, Dh)
                              for j in range(n_off)], axis=2)
    kb_len = n_off * WIN_QBLOCK
    qpos = np.arange(nqb)[:, None, None] * WIN_QBLOCK + np.arange(WIN_QBLOCK)[None, :, None]
    kpos = np.arange(nqb)[:, None, None] * WIN_QBLOCK + np.arange(kb_len)[None, None, :] - WINDOW
    mask = jnp.asarray((kpos >= 0) & (kpos <= qpos) & (kpos > qpos - WINDOW))
    qb = q.reshape(B, nqb, WIN_QBLOCK, G, h, Dh)
    s = jnp.einsum('bnqghd,bnkgd->bnghqk', qb, k_band) * SCALE
    p = masked_softmax(s, mask[None, :, None, None]).astype(v.dtype)
    o = jnp.einsum('bnghqk,bnkgd->bnqghd', p, v_band)
    return o.reshape(B, S, G, h, Dh)


def short_conv(u, c_b, c_c, conv_w, conv_b):
    v = c_c * u
    y = lax.conv_general_dilated(v, conv_w[:, None, :].astype(v.dtype), window_strides=(1,),
                                 padding=[(CONV_K - 1, 0)],
                                 dimension_numbers=('NWC', 'WIO', 'NWC'),
                                 feature_group_count=CONV_WIDTH)
    return c_b * (y + conv_b)


def hybrid_layer(x, norm_g, w_in, b_in, pe_k, w1_k, w2_k, pe_v, w1_v, w2_v,
                 conv_w, conv_b, p_attn, p_conv, w_o):
    B, S, D = x.shape
    G, h, Dh = N_KV_GROUPS, HEADS_PER_GROUP, HEAD_DIM
    hn = rmsnorm(x, norm_g)
    proj = hn @ w_in + b_in
    cuts = np.cumsum(IN_SPLIT_SIZES)[:-1].tolist()
    (q, kc, vc, ks, vs, kw, vw, g_nsa, z_attn,
     u, c_c, c_b, z_conv, g_merge) = jnp.split(proj, cuts, axis=-1)
    q = q.reshape(B, S, G, h, Dh)
    kv = lambda a: a.reshape(B, S, G, Dh)
    kcb = compress_blocks(kv(kc), pe_k, w1_k, w2_k)
    vcb = compress_blocks(kv(vc), pe_v, w1_v, w2_v)
    o_cmp, p_cmp = compressed_attention(q, kcb, vcb)
    idx = select_blocks(p_cmp, S)
    o_sel = selected_attention(q, kv(ks), kv(vs), idx)
    o_win = window_attention(q, kv(kw), kv(vw))
    gb = jax.nn.sigmoid(g_nsa).reshape(B, S, 3, G, h, 1)
    o_attn = (gb[:, :, 0] * o_cmp + gb[:, :, 1] * o_sel + gb[:, :, 2] * o_win).reshape(B, S, ATTN_WIDTH)
    y_attn = (o_attn * jax.nn.silu(z_attn)) @ p_attn
    y_conv = (short_conv(u, c_b, c_c, conv_w, conv_b) * jax.nn.silu(z_conv)) @ p_conv
    gm = jax.nn.sigmoid(g_merge).reshape(B, S, N_BRANCHES, D)
    y = gm[:, :, 0] * y_attn + gm[:, :, 1] * y_conv
    return x + y @ w_o


def setup_inputs(seed: int = 0) -> dict:
    key = jax.random.key(seed)
    ks = jax.random.split(key, 17)
    f32 = jnp.float32
    L = DEPTH
    nrm = lambda k, shape, fan: jax.random.normal(k, shape, f32) * (fan ** -0.5)
    return {
        "x": jax.random.normal(ks[0], (BATCH, SEQ, D_MODEL), f32),
        "norm_g": 1.0 + 0.02 * jax.random.normal(ks[1], (L, D_MODEL), f32),
        "w_in": nrm(ks[2], (L, D_MODEL, N_IN), D_MODEL),
        "b_in": 0.02 * jax.random.normal(ks[3], (L, N_IN), f32),
        "cmp_pe_k": 0.1 * jax.random.normal(ks[4], (L, CMP_BLOCK, HEAD_DIM), f32),
        "cmp_w1_k": nrm(ks[5], (L, CMP_BLOCK * HEAD_DIM, CMP_HIDDEN), CMP_BLOCK * HEAD_DIM),
        "cmp_w2_k": nrm(ks[6], (L, CMP_HIDDEN, HEAD_DIM), CMP_HIDDEN),
        "cmp_pe_v": 0.1 * jax.random.normal(ks[7], (L, CMP_BLOCK, HEAD_DIM), f32),
        "cmp_w1_v": nrm(ks[8], (L, CMP_BLOCK * HEAD_DIM, CMP_HIDDEN), CMP_BLOCK * HEAD_DIM),
        "cmp_w2_v": nrm(ks[9], (L, CMP_HIDDEN, HEAD_DIM), CMP_HIDDEN),
        "conv_w": nrm(ks[10], (L, CONV_K, CONV_WIDTH), CONV_K),
        "conv_b": 0.02 * jax.random.normal(ks[11], (L, CONV_WIDTH), f32),
        "p_attn": nrm(ks[12], (L, ATTN_WIDTH, D_MODEL), ATTN_WIDTH),
        "p_conv": nrm(ks[13], (L, CONV_WIDTH, D_MODEL), CONV_WIDTH),
        "w_o": nrm(ks[14], (L, D_MODEL, D_MODEL), D_MODEL),
        "final_g": 1.0 + 0.02 * jax.random.normal(ks[15], (D_MODEL,), f32),
    }


def reference(x, norm_g, w_in, b_in, cmp_pe_k, cmp_w1_k, cmp_w2_k, cmp_pe_v, cmp_w1_v,
              cmp_w2_v, conv_w, conv_b, p_attn, p_conv, w_o, final_g):
    for l in range(DEPTH):
        x = hybrid_layer(x, norm_g[l], w_in[l], b_in[l], cmp_pe_k[l], cmp_w1_k[l], cmp_w2_k[l],
                         cmp_pe_v[l], cmp_w1_v[l], cmp_w2_v[l], conv_w[l], conv_b[l],
                         p_attn[l], p_conv[l], w_o[l])
    return rmsnorm(x, final_g)
```

```python
import functools

import numpy as np
import jax
import jax.numpy as jnp
from jax import lax
from jax.experimental import pallas as pl
from jax.experimental.pallas import tpu as pltpu

D_MODEL = 2048
N_HEADS = 16
HEAD_DIM = 128
N_KV_GROUPS = 4
HEADS_PER_GROUP = N_HEADS // N_KV_GROUPS
ATTN_WIDTH = N_HEADS * HEAD_DIM
KV_WIDTH = N_KV_GROUPS * HEAD_DIM
GROUP_WIDTH = HEADS_PER_GROUP * HEAD_DIM
CMP_BLOCK = 32
CMP_STRIDE = 16
CMP_HIDDEN = 256
SEL_BLOCK = 64
SEL_TOP_N = 8
WINDOW = 512
SCALE = HEAD_DIM ** -0.5
CONV_WIDTH = D_MODEL
CONV_K = 3
RMS_EPS = 1e-6
FORCED_SCORE = 1e4
MASKED = -(2.0 ** 100)

LANES = 128
Q_TILE = 128
SEL_CHUNK = 256
VMEM_LIMIT = 56 * 1024 * 1024

_NT = (((1,), (1,)), ((), ()))


def _params(*sem):
    return pltpu.CompilerParams(dimension_semantics=sem, vmem_limit_bytes=VMEM_LIMIT)


def _sigmoid(x):
    return 1.0 / (1.0 + jnp.exp(-x))


def _silu(x):
    return x * _sigmoid(x)


def _rmsnorm_kernel(x_ref, g_ref, o_ref):
    x = x_ref[...]
    ms = jnp.mean(x * x, axis=-1, keepdims=True)
    o_ref[...] = (x * lax.rsqrt(ms + RMS_EPS) * g_ref[...]).astype(o_ref.dtype)


def _rmsnorm(x2d, g, out_dtype, tm=512):
    m, d = x2d.shape
    return pl.pallas_call(
        _rmsnorm_kernel,
        grid=(m // tm,),
        in_specs=[pl.BlockSpec((tm, d), lambda i: (i, 0)),
                  pl.BlockSpec((1, d), lambda i: (0, 0))],
        out_specs=pl.BlockSpec((tm, d), lambda i: (i, 0)),
        out_shape=jax.ShapeDtypeStruct((m, d), out_dtype),
        compiler_params=_params("parallel"),
        name="rmsnorm",
    )(x2d, g.reshape(1, d))


def _proj_kernel(a_ref, w_ref, b_ref, o_ref, *, act):
    r = jnp.dot(a_ref[...], w_ref[...], preferred_element_type=jnp.float32) + b_ref[...]
    if act == "sigmoid":
        r = _sigmoid(r)
    o_ref[...] = r.astype(o_ref.dtype)


def _proj(a, w, b, out_dtype, tm, tn, act=None, name="proj"):
    m, k = a.shape
    n = w.shape[1]
    return pl.pallas_call(
        functools.partial(_proj_kernel, act=act),
        grid=(m // tm, n // tn),
        in_specs=[pl.BlockSpec((tm, k), lambda i, j: (i, 0)),
                  pl.BlockSpec((k, tn), lambda i, j: (0, j)),
                  pl.BlockSpec((1, tn), lambda i, j: (0, j))],
        out_specs=pl.BlockSpec((tm, tn), lambda i, j: (i, j)),
        out_shape=jax.ShapeDtypeStruct((m, n), out_dtype),
        compiler_params=_params("parallel", "arbitrary"),
        name=name,
    )(a, w, b.reshape(1, n))


def _compress_kernel(x_ref, pe_ref, w1_ref, w2_ref, o_ref):
    half = CMP_BLOCK // 2
    n_chunks = x_ref.shape[1] // CMP_STRIDE
    top = jnp.zeros((n_chunks, CMP_HIDDEN), jnp.float32)
    bot = jnp.zeros((n_chunks, CMP_HIDDEN), jnp.float32)
    for l in range(half):
        xl = x_ref[0, pl.ds(l, n_chunks, stride=CMP_STRIDE), :]
        a = (xl + pe_ref[l:l + 1, :]).astype(jnp.bfloat16)
        c = (xl + pe_ref[half + l:half + l + 1, :]).astype(jnp.bfloat16)
        top = top + jnp.dot(a, w1_ref[l], preferred_element_type=jnp.float32)
        bot = bot + jnp.dot(c, w1_ref[half + l], preferred_element_type=jnp.float32)
    h = top + pltpu.roll(bot, n_chunks - 1, axis=0)
    h = _silu(h).astype(jnp.bfloat16)
    o_ref[0, 0] = jnp.dot(h, w2_ref[...], preferred_element_type=jnp.float32).astype(o_ref.dtype)


def _compress(kv, col_block0, pe, w1, w2):
    b, s, _ = kv.shape
    return pl.pallas_call(
        _compress_kernel,
        grid=(b, N_KV_GROUPS),
        in_specs=[pl.BlockSpec((1, s, HEAD_DIM), lambda bi, g: (bi, 0, col_block0 + g)),
                  pl.BlockSpec((CMP_BLOCK, HEAD_DIM), lambda bi, g: (0, 0)),
                  pl.BlockSpec((CMP_BLOCK, HEAD_DIM, CMP_HIDDEN), lambda bi, g: (0, 0, 0)),
                  pl.BlockSpec((CMP_HIDDEN, HEAD_DIM), lambda bi, g: (0, 0))],
        out_specs=pl.BlockSpec((1, 1, s // CMP_STRIDE, HEAD_DIM), lambda bi, g: (bi, g, 0, 0)),
        out_shape=jax.ShapeDtypeStruct((b, N_KV_GROUPS, s // CMP_STRIDE, HEAD_DIM), jnp.bfloat16),
        compiler_params=_params("parallel", "parallel"),
        name="compress",
    )(kv, pe, w1, w2)


def _attn_kernel(q_ref, ks_ref, vs_ref, kw_ref, vw_ref, kc_ref, vc_ref, gate_ref, z_ref,
                 agg_ref, expand_ref, o_ref, s_scr):
    tq = Q_TILE
    hpg = HEADS_PER_GROUP
    rows = hpg * tq
    i = pl.program_id(2)
    t0 = i * tq

    q = q_ref[0]
    qs = jnp.concatenate([q[:, h * HEAD_DIM:(h + 1) * HEAD_DIM] for h in range(hpg)], axis=0)
    t_q = t0 + lax.broadcasted_iota(jnp.int32, (tq, 1), 0)
    t_s = jnp.concatenate([t_q] * hpg, axis=0)
    lane = lax.broadcasted_iota(jnp.int32, (1, LANES), 1)

    sc = lax.dot_general(qs, kc_ref[0, 0], _NT, preferred_element_type=jnp.float32) * SCALE
    n_cmp = (kw_ref.shape[1] - CMP_BLOCK) // CMP_STRIDE + 1
    vis_c = (lane * CMP_STRIDE + (CMP_BLOCK - 1) <= t_s) & (lane < n_cmp)
    sc = jnp.where(vis_c, sc, MASKED)
    m_c = jnp.max(sc, axis=-1, keepdims=True)
    p_c = jnp.where(vis_c, jnp.exp(sc - m_c), 0.0)
    l_c = jnp.sum(p_c, axis=-1, keepdims=True)
    p_c = p_c * jnp.where(l_c > 0.0, 1.0 / l_c, 0.0)
    o_cmp = jnp.dot(p_c.astype(jnp.bfloat16), vc_ref[0, 0], preferred_element_type=jnp.float32)

    p_sum = p_c[0:tq]
    for h in range(1, hpg):
        p_sum = p_sum + p_c[h * tq:(h + 1) * tq]
    agg = agg_ref[...]
    hi = p_sum.astype(jnp.bfloat16)
    r1 = p_sum - hi.astype(jnp.float32)
    mid = r1.astype(jnp.bfloat16)
    lo = (r1 - mid.astype(jnp.float32)).astype(jnp.bfloat16)
    imp = (jnp.dot(hi, agg, preferred_element_type=jnp.float32)
           + jnp.dot(mid, agg, preferred_element_type=jnp.float32)
           + jnp.dot(lo, agg, preferred_element_type=jnp.float32))
    n_slc = kw_ref.shape[1] // SEL_BLOCK
    j = lane & (n_slc - 1)
    cur = t_q >> 6
    causal = j * SEL_BLOCK <= t_q
    forced = ((j == 0) | (j == cur) | (j == cur - 1)) & causal
    imp = jnp.where(forced, FORCED_SCORE, imp)
    imp = jnp.where(causal, imp, -1.0)
    rank = jnp.zeros((tq, LANES), jnp.float32)
    for r in range(1, n_slc):
        other = pltpu.roll(imp, r, axis=1)
        tie = jnp.where(j >= r, 1.0, 0.0)
        rank = rank + jnp.where(other > imp, 1.0, jnp.where(other == imp, tie, 0.0))
    sel_bias = jnp.where(rank < float(SEL_TOP_N), 0.0, MASKED).astype(jnp.bfloat16)

    n_chunks = (t0 + tq - 1) // SEL_CHUNK + 1
    kpos0 = lax.broadcasted_iota(jnp.int32, (1, SEL_CHUNK), 1)

    def score_chunk(c, m_run):
        base = pl.multiple_of(c * SEL_CHUNK, SEL_CHUNK)
        k = ks_ref[0, pl.ds(base, SEL_CHUNK), :]
        s = lax.dot_general(qs, k, _NT, preferred_element_type=jnp.float32)
        bias = jnp.dot(sel_bias, expand_ref[:, pl.ds(base, SEL_CHUNK)],
                       preferred_element_type=jnp.float32)
        s = s + jnp.concatenate([bias] * hpg, axis=0)
        s = jnp.where(base + kpos0 <= t_s, s, MASKED)
        s_scr[:, pl.ds(base, SEL_CHUNK)] = s
        return jnp.maximum(m_run, jnp.max(s, axis=-1, keepdims=True))

    m_s = lax.fori_loop(0, n_chunks, score_chunk, jnp.full((rows, 1), MASKED, jnp.float32))

    def value_chunk(c, carry):
        l_run, acc = carry
        base = pl.multiple_of(c * SEL_CHUNK, SEL_CHUNK)
        p = jnp.exp((s_scr[:, pl.ds(base, SEL_CHUNK)] - m_s) * SCALE)
        v = vs_ref[0, pl.ds(base, SEL_CHUNK), :]
        acc = acc + jnp.dot(p.astype(jnp.bfloat16), v, preferred_element_type=jnp.float32)
        return l_run + jnp.sum(p, axis=-1, keepdims=True), acc

    l_s, acc_s = lax.fori_loop(0, n_chunks, value_chunk,
                               (jnp.zeros((rows, 1), jnp.float32),
                                jnp.zeros((rows, HEAD_DIM), jnp.float32)))
    o_sel = acc_s * (1.0 / l_s)

    n_win = WINDOW + tq
    start = pl.multiple_of(jnp.maximum(t0 - WINDOW, 0), tq)
    kpos = start + lax.broadcasted_iota(jnp.int32, (1, n_win), 1)
    vis_w = (kpos <= t_s) & (kpos > t_s - WINDOW)
    sw = lax.dot_general(qs, kw_ref[0, pl.ds(start, n_win), :], _NT,
                         preferred_element_type=jnp.float32) * SCALE
    sw = jnp.where(vis_w, sw, MASKED)
    m_w = jnp.max(sw, axis=-1, keepdims=True)
    p_w = jnp.exp(sw - m_w)
    l_w = jnp.sum(p_w, axis=-1, keepdims=True)
    o_win = jnp.dot(p_w.astype(jnp.bfloat16), vw_ref[0, pl.ds(start, n_win), :],
                    preferred_element_type=jnp.float32) * (1.0 / l_w)

    gate = gate_ref[0, 0]
    outs = []
    for h in range(hpg):
        r0 = h * tq
        outs.append(gate[:, h:h + 1] * o_cmp[r0:r0 + tq]
                    + gate[:, hpg + h:hpg + h + 1] * o_sel[r0:r0 + tq]
                    + gate[:, 2 * hpg + h:2 * hpg + h + 1] * o_win[r0:r0 + tq])
    o = jnp.concatenate(outs, axis=1)
    o_ref[0] = (o * _silu(z_ref[0])).astype(o_ref.dtype)


def _attention(qkv, kcb, vcb, gates, zu, agg, expand):
    b, s, _ = qkv.shape
    qb = ATTN_WIDTH // HEAD_DIM
    kvb = KV_WIDTH // HEAD_DIM
    n_c = s // CMP_STRIDE
    grid = (b, N_KV_GROUPS, s // Q_TILE)
    kv_spec = lambda off: pl.BlockSpec((1, s, HEAD_DIM), lambda bi, g, i: (bi, 0, off + g))
    return pl.pallas_call(
        _attn_kernel,
        grid=grid,
        in_specs=[pl.BlockSpec((1, Q_TILE, GROUP_WIDTH), lambda bi, g, i: (bi, i, g)),
                  kv_spec(qb), kv_spec(qb + kvb), kv_spec(qb + 2 * kvb), kv_spec(qb + 3 * kvb),
                  pl.BlockSpec((1, 1, n_c, HEAD_DIM), lambda bi, g, i: (bi, g, 0, 0)),
                  pl.BlockSpec((1, 1, n_c, HEAD_DIM), lambda bi, g, i: (bi, g, 0, 0)),
                  pl.BlockSpec((1, 1, Q_TILE, 3 * HEADS_PER_GROUP), lambda bi, g, i: (bi, g, i, 0)),
                  pl.BlockSpec((1, Q_TILE, GROUP_WIDTH), lambda bi, g, i: (bi, i, g)),
                  pl.BlockSpec((LANES, LANES), lambda bi, g, i: (0, 0)),
                  pl.BlockSpec((LANES, s), lambda bi, g, i: (0, 0))],
        out_specs=pl.BlockSpec((1, Q_TILE, GROUP_WIDTH), lambda bi, g, i: (bi, i, g)),
        out_shape=jax.ShapeDtypeStruct((b, s, ATTN_WIDTH), jnp.bfloat16),
        scratch_shapes=[pltpu.VMEM((HEADS_PER_GROUP * Q_TILE, s), jnp.float32)],
        compiler_params=_params("parallel", "parallel", "arbitrary"),
        name="nsa_attention",
    )(qkv, qkv, qkv, qkv, qkv, kcb, vcb, gates, zu, agg, expand)


def _conv_kernel(u_ref, cc_ref, cb_ref, z_ref, w_ref, b_ref, o_ref):
    v = cc_ref[0] * u_ref[0]
    row = lax.broadcasted_iota(jnp.int32, (v.shape[0], 1), 0)
    y = w_ref[CONV_K - 1:CONV_K, :] * v
    for d in range(1, CONV_K):
        shifted = jnp.where(row >= d, pltpu.roll(v, d, axis=0), 0.0)
        y = y + w_ref[CONV_K - 1 - d:CONV_K - d, :] * shifted
    o_ref[0] = (cb_ref[0] * (y + b_ref[...]) * _silu(z_ref[0])).astype(o_ref.dtype)


def _short_conv(zu, col0, conv_w, conv_b, tc=256):
    b, s, _ = zu.shape
    nb = CONV_WIDTH // tc
    c0 = col0 // tc
    spec = lambda k: pl.BlockSpec((1, s, tc), lambda bi, j: (bi, 0, c0 + k * nb + j))
    return pl.pallas_call(
        _conv_kernel,
        grid=(b, nb),
        in_specs=[spec(0), spec(1), spec(2), spec(3),
                  pl.BlockSpec((CONV_K, tc), lambda bi, j: (0, j)),
                  pl.BlockSpec((1, tc), lambda bi, j: (0, j))],
        out_specs=pl.BlockSpec((1, s, tc), lambda bi, j: (bi, 0, j)),
        out_shape=jax.ShapeDtypeStruct((b, s, CONV_WIDTH), jnp.bfloat16),
        compiler_params=_params("parallel", "parallel"),
        name="short_conv",
    )(zu, zu, zu, zu, conv_w, conv_b.reshape(1, CONV_WIDTH))


def _merge_kernel(a_ref, c_ref, wa_ref, wc_ref, g0_ref, g1_ref, o_ref):
    ya = jnp.dot(a_ref[...], wa_ref[...], preferred_element_type=jnp.float32)
    yc = jnp.dot(c_ref[...], wc_ref[...], preferred_element_type=jnp.float32)
    o_ref[...] = (_sigmoid(g0_ref[...]) * ya + _sigmoid(g1_ref[...]) * yc).astype(o_ref.dtype)


def _merge(a, c, wa, wc, zu, gcol0, tm=512, tn=1024):
    m, k = a.shape
    n = wa.shape[1]
    g0 = gcol0 // tn
    g1 = (gcol0 + n) // tn
    return pl.pallas_call(
        _merge_kernel,
        grid=(m // tm, n // tn),
        in_specs=[pl.BlockSpec((tm, k), lambda i, j: (i, 0)),
                  pl.BlockSpec((tm, k), lambda i, j: (i, 0)),
                  pl.BlockSpec((k, tn), lambda i, j: (0, j)),
                  pl.BlockSpec((k, tn), lambda i, j: (0, j)),
                  pl.BlockSpec((tm, tn), lambda i, j: (i, g0 + j)),
                  pl.BlockSpec((tm, tn), lambda i, j: (i, g1 + j))],
        out_specs=pl.BlockSpec((tm, tn), lambda i, j: (i, j)),
        out_shape=jax.ShapeDtypeStruct((m, n), jnp.bfloat16),
        compiler_params=_params("parallel", "arbitrary"),
        name="merge",
    )(a, c, wa, wc, zu, zu)


def _out_kernel(y_ref, w_ref, x_ref, g_ref, o_ref):
    r = x_ref[...] + jnp.dot(y_ref[...], w_ref[...], preferred_element_type=jnp.float32)
    ms = jnp.mean(r * r, axis=-1, keepdims=True)
    o_ref[...] = r * lax.rsqrt(ms + RMS_EPS) * g_ref[...]


def _out_proj(y, w, x2d, g, tm=512):
    m, k = y.shape
    n = w.shape[1]
    return pl.pallas_call(
        _out_kernel,
        grid=(m // tm,),
        in_specs=[pl.BlockSpec((tm, k), lambda i: (i, 0)),
                  pl.BlockSpec((k, n), lambda i: (0, 0)),
                  pl.BlockSpec((tm, n), lambda i: (i, 0)),
                  pl.BlockSpec((1, n), lambda i: (0, 0))],
        out_specs=pl.BlockSpec((tm, n), lambda i: (i, 0)),
        out_shape=jax.ShapeDtypeStruct((m, n), jnp.float32),
        compiler_params=_params("parallel"),
        name="out_proj",
    )(y, w, x2d, g.reshape(1, n))


def _selection_constants(s):
    n_slc = s // SEL_BLOCK
    n_cmp = (s - CMP_BLOCK) // CMP_STRIDE + 1
    c0 = np.arange(LANES)[:, None] * CMP_STRIDE
    s0 = (np.arange(LANES)[None, :] % n_slc) * SEL_BLOCK
    overlap = np.maximum(0, np.minimum(c0 + CMP_BLOCK, s0 + SEL_BLOCK) - np.maximum(c0, s0))
    agg = (overlap / CMP_BLOCK) * (np.arange(LANES)[:, None] < n_cmp)
    expand = (np.arange(LANES)[:, None] == (np.arange(s)[None, :] // SEL_BLOCK)).astype(np.float32)
    return jnp.asarray(agg, jnp.bfloat16), jnp.asarray(expand, jnp.bfloat16)


def _layer(x, norm_g, w_in, b_in, pe_k, w1_k, w2_k, pe_v, w1_v, w2_v,
           conv_w, conv_b, p_attn, p_conv, w_o, out_g):
    b, s, d = x.shape
    m = b * s
    bf = jnp.bfloat16
    x2d = x.reshape(m, d)

    c_q = ATTN_WIDTH
    c_cmp = c_q + 2 * KV_WIDTH
    c_kv = c_cmp + 4 * KV_WIDTH
    c_gate = c_kv + 3 * N_HEADS
    w_qkv = jnp.concatenate([w_in[:, :c_q], w_in[:, c_cmp:c_kv]], axis=1).astype(bf)
    b_qkv = jnp.concatenate([b_in[:c_q], b_in[c_cmp:c_kv]])
    w_cmp = w_in[:, c_q:c_cmp].astype(bf)
    b_cmp = b_in[c_q:c_cmp]
    n_gate = 3 * N_HEADS
    w_gate = jnp.pad(w_in[:, c_kv:c_gate], ((0, 0), (0, LANES - n_gate))).astype(bf)
    b_gate = jnp.pad(b_in[c_kv:c_gate], (0, LANES - n_gate))
    w_rest = w_in[:, c_gate:].astype(bf)
    b_rest = b_in[c_gate:]

    hn = _rmsnorm(x2d, norm_g, bf)
    qkv = _proj(hn, w_qkv, b_qkv, bf, 1024, 1024, name="proj_qkv").reshape(b, s, -1)
    kvc = _proj(hn, w_cmp, b_cmp, jnp.float32, 1024, 1024, name="proj_cmp").reshape(b, s, -1)
    gate = _proj(hn, w_gate, b_gate, jnp.float32, 1024, LANES, act="sigmoid", name="proj_gate")
    zu = _proj(hn, w_rest, b_rest, jnp.float32, 1024, 1024, name="proj_rest").reshape(b, s, -1)

    kcb = _compress(kvc, 0, pe_k, w1_k.astype(bf).reshape(CMP_BLOCK, HEAD_DIM, CMP_HIDDEN), w2_k.astype(bf))
    vcb = _compress(kvc, N_KV_GROUPS, pe_v, w1_v.astype(bf).reshape(CMP_BLOCK, HEAD_DIM, CMP_HIDDEN),
                    w2_v.astype(bf))

    gates = gate[:, :n_gate].reshape(b, s, 3, N_KV_GROUPS, HEADS_PER_GROUP)
    gates = gates.transpose(0, 3, 1, 2, 4).reshape(b, N_KV_GROUPS, s, 3 * HEADS_PER_GROUP)

    agg, expand = _selection_constants(s)
    a = _attention(qkv, kcb, vcb, gates, zu, agg, expand)
    c = _short_conv(zu, ATTN_WIDTH, conv_w, conv_b)

    y = _merge(a.reshape(m, -1), c.reshape(m, -1), p_attn.astype(bf), p_conv.astype(bf),
               zu.reshape(m, -1), ATTN_WIDTH + 4 * CONV_WIDTH)
    return _out_proj(y, w_o.astype(bf), x2d, out_g).reshape(b, s, d)


def kernel(x, norm_g, w_in, b_in, cmp_pe_k, cmp_w1_k, cmp_w2_k, cmp_pe_v, cmp_w1_v, cmp_w2_v,
           conv_w, conv_b, p_attn, p_conv, w_o, final_g):
    assert norm_g.shape[0] == 1, "single-layer block"
    return _layer(x, norm_g[0], w_in[0], b_in[0], cmp_pe_k[0], cmp_w1_k[0], cmp_w2_k[0],
                  cmp_pe_v[0], cmp_w1_v[0], cmp_w2_v[0], conv_w[0], conv_b[0],
                  p_attn[0], p_conv[0], w_o[0], final_g)
```

```python
import functools
import math

import numpy as np
import jax
import jax.numpy as jnp
from jax import lax
from jax.experimental import pallas as pl
from jax.experimental.pallas import tpu as pltpu

D_MODEL = 2048
N_HEADS = 16
HEAD_DIM = 128
N_KV_GROUPS = 4
HEADS_PER_GROUP = N_HEADS // N_KV_GROUPS
ATTN_WIDTH = N_HEADS * HEAD_DIM
KV_WIDTH = N_KV_GROUPS * HEAD_DIM
GROUP_WIDTH = HEADS_PER_GROUP * HEAD_DIM
CMP_BLOCK = 32
CMP_STRIDE = 16
CMP_HIDDEN = 256
SEL_BLOCK = 64
SEL_TOP_N = 8
WINDOW = 512
SCALE = HEAD_DIM ** -0.5
CONV_WIDTH = D_MODEL
CONV_K = 3
RMS_EPS = 1e-6
FORCED_SCORE = 1e4
MASKED = -(2.0 ** 100)

LANES = 128
SUBLANES = 8
Q_TILE = 256
SEL_CHUNK = 256
VMEM_LIMIT = 56 * 1024 * 1024

_NT = (((1,), (1,)), ((), ()))


def _params(*sem):
    return pltpu.CompilerParams(dimension_semantics=sem, vmem_limit_bytes=VMEM_LIMIT)


def _sigmoid(x):
    return 1.0 / (1.0 + jnp.exp(-x))


def _silu(x):
    return x * _sigmoid(x)


def _rmsnorm_kernel(x_ref, g_ref, o_ref):
    x = x_ref[...]
    ms = jnp.mean(x * x, axis=-1, keepdims=True)
    o_ref[...] = (x * lax.rsqrt(ms + RMS_EPS) * g_ref[...]).astype(o_ref.dtype)


def _rmsnorm(x2d, g, out_dtype, tm=512):
    m, d = x2d.shape
    return pl.pallas_call(
        _rmsnorm_kernel,
        grid=(m // tm,),
        in_specs=[pl.BlockSpec((tm, d), lambda i: (i, 0)),
                  pl.BlockSpec((1, d), lambda i: (0, 0))],
        out_specs=pl.BlockSpec((tm, d), lambda i: (i, 0)),
        out_shape=jax.ShapeDtypeStruct((m, d), out_dtype),
        compiler_params=_params("parallel"),
        name="rmsnorm",
    )(x2d, g.reshape(1, d))


def _proj_kernel(a_ref, w_ref, b_ref, s_ref, o_ref, *, act):
    r = jnp.dot(a_ref[...], w_ref[...], preferred_element_type=jnp.float32) + b_ref[...]
    r = r * s_ref[...]
    if act == "sigmoid":
        r = _sigmoid(r)
    o_ref[...] = r.astype(o_ref.dtype)


def _proj(a, w, b, out_dtype, tm, tn, col_scale=None, act=None, name="proj"):
    m, k = a.shape
    n = w.shape[1]
    if col_scale is None:
        col_scale = jnp.ones((n,), jnp.float32)
    return pl.pallas_call(
        functools.partial(_proj_kernel, act=act),
        grid=(m // tm, n // tn),
        in_specs=[pl.BlockSpec((tm, k), lambda i, j: (i, 0)),
                  pl.BlockSpec((k, tn), lambda i, j: (0, j)),
                  pl.BlockSpec((1, tn), lambda i, j: (0, j)),
                  pl.BlockSpec((1, tn), lambda i, j: (0, j))],
        out_specs=pl.BlockSpec((tm, tn), lambda i, j: (i, j)),
        out_shape=jax.ShapeDtypeStruct((m, n), out_dtype),
        compiler_params=_params("parallel", "arbitrary"),
        name=name,
    )(a, w, b.reshape(1, n), col_scale.reshape(1, n))


def _compress_kernel(x_ref, pe_ref, w1_ref, w2_ref, o_ref):
    half = CMP_BLOCK // 2
    n_chunks = x_ref.shape[1] // CMP_STRIDE
    top = jnp.zeros((n_chunks, CMP_HIDDEN), jnp.float32)
    bot = jnp.zeros((n_chunks, CMP_HIDDEN), jnp.float32)
    for l in range(half):
        xl = x_ref[0, pl.ds(l, n_chunks, stride=CMP_STRIDE), :]
        a = (xl + pe_ref[l:l + 1, :]).astype(jnp.bfloat16)
        c = (xl + pe_ref[half + l:half + l + 1, :]).astype(jnp.bfloat16)
        top = top + jnp.dot(a, w1_ref[l], preferred_element_type=jnp.float32)
        bot = bot + jnp.dot(c, w1_ref[half + l], preferred_element_type=jnp.float32)
    h = top + pltpu.roll(bot, n_chunks - 1, axis=0)
    h = _silu(h).astype(jnp.bfloat16)
    o_ref[0, 0] = jnp.dot(h, w2_ref[...], preferred_element_type=jnp.float32).astype(o_ref.dtype)


def _compress(kv, col_block0, pe, w1, w2):
    b, s, _ = kv.shape
    return pl.pallas_call(
        _compress_kernel,
        grid=(b, N_KV_GROUPS),
        in_specs=[pl.BlockSpec((1, s, HEAD_DIM), lambda bi, g: (bi, 0, col_block0 + g)),
                  pl.BlockSpec((CMP_BLOCK, HEAD_DIM), lambda bi, g: (0, 0)),
                  pl.BlockSpec((CMP_BLOCK, HEAD_DIM, CMP_HIDDEN), lambda bi, g: (0, 0, 0)),
                  pl.BlockSpec((CMP_HIDDEN, HEAD_DIM), lambda bi, g: (0, 0))],
        out_specs=pl.BlockSpec((1, 1, s // CMP_STRIDE, HEAD_DIM), lambda bi, g: (bi, g, 0, 0)),
        out_shape=jax.ShapeDtypeStruct((b, N_KV_GROUPS, s // CMP_STRIDE, HEAD_DIM), jnp.bfloat16),
        compiler_params=_params("parallel", "parallel"),
        name="compress",
    )(kv, pe, w1, w2)


def _attn_kernel(q_ref, ks_ref, vs_ref, kw_ref, vw_ref, kc_ref, vc_ref, gate_ref, z_ref,
                 aggt_ref, blockid_ref, wbias_ref, o_ref,
                 kaug_scr, s_scr, m_scr, l_scr, acc_scr):
    tq = Q_TILE
    hpg = HEADS_PER_GROUP
    rows = hpg * tq
    seq = kw_ref.shape[1]
    i = pl.program_id(2)
    t0 = i * tq

    @pl.when(i == 0)
    def _():
        kaug_scr[:, :HEAD_DIM] = ks_ref[0]
        kaug_scr[:, HEAD_DIM:] = blockid_ref[...]

    q = q_ref[0]
    qs = jnp.concatenate([q[:, h * HEAD_DIM:(h + 1) * HEAD_DIM] for h in range(hpg)], axis=0)
    t_q = t0 + lax.broadcasted_iota(jnp.int32, (tq, 1), 0)
    t_s = jnp.concatenate([t_q] * hpg, axis=0)
    lane = lax.broadcasted_iota(jnp.int32, (1, LANES), 1)

    n_win = WINDOW + tq
    start = pl.multiple_of(jnp.maximum(t0 - WINDOW, 0), tq)
    sw = lax.dot_general(qs, kw_ref[0, pl.ds(start, n_win), :], _NT,
                         preferred_element_type=jnp.float32)
    sw = sw + jnp.concatenate([wbias_ref[0]] * hpg, axis=0)
    m_w = jnp.max(sw, axis=-1, keepdims=True)
    p_w = jnp.exp2(sw - m_w)
    l_w = jnp.sum(p_w, axis=-1, keepdims=True)
    o_win = jnp.dot(p_w.astype(jnp.bfloat16), vw_ref[0, pl.ds(start, n_win), :],
                    preferred_element_type=jnp.float32) * (1.0 / l_w)

    n_cmp = (seq - CMP_BLOCK) // CMP_STRIDE + 1
    sc = lax.dot_general(qs, kc_ref[0, 0], _NT, preferred_element_type=jnp.float32)
    vis_c = (lane * CMP_STRIDE + (CMP_BLOCK - 1) <= t_s) & (lane < n_cmp)
    sc = jnp.where(vis_c, sc, MASKED)
    m_c = jnp.max(sc, axis=-1, keepdims=True)
    p_c = jnp.exp2(sc - m_c)
    l_c = jnp.sum(p_c, axis=-1, keepdims=True)
    p_c = p_c * jnp.where(t_s >= CMP_BLOCK - 1, 1.0 / l_c, 0.0)
    o_cmp = jnp.dot(p_c.astype(jnp.bfloat16), vc_ref[0, 0], preferred_element_type=jnp.float32)

    p_sum = p_c[0:tq]
    for h in range(1, hpg):
        p_sum = p_sum + p_c[h * tq:(h + 1) * tq]
    aggt = aggt_ref[...]
    hi = p_sum.astype(jnp.bfloat16)
    r1 = p_sum - hi.astype(jnp.float32)
    mid = r1.astype(jnp.bfloat16)
    lo = (r1 - mid.astype(jnp.float32)).astype(jnp.bfloat16)
    imp = (lax.dot_general(aggt, hi, _NT, preferred_element_type=jnp.float32)
           + lax.dot_general(aggt, mid, _NT, preferred_element_type=jnp.float32)
           + lax.dot_general(aggt, lo, _NT, preferred_element_type=jnp.float32))
    n_slc = seq // SEL_BLOCK
    blk = lax.broadcasted_iota(jnp.int32, (n_slc, 1), 0)
    t_l = t0 + lax.broadcasted_iota(jnp.int32, (1, tq), 1)
    cur = t_l >> int(math.log2(SEL_BLOCK))
    causal = blk * SEL_BLOCK <= t_l
    forced = ((blk == 0) | (blk == cur) | (blk == cur - 1)) & causal
    imp = jnp.where(forced, FORCED_SCORE, imp)
    imp = jnp.where(causal, imp, -1.0)

    bias_slabs = []
    for v in range(n_slc // SUBLANES):
        lo_row = v * SUBLANES
        slab = imp[lo_row:lo_row + SUBLANES]
        sub = lo_row + lax.broadcasted_iota(jnp.int32, (SUBLANES, 1), 0)
        rank = jnp.zeros((SUBLANES, tq), jnp.float32)
        for jp in range(n_slc):
            other = jnp.broadcast_to(imp[jp:jp + 1], (SUBLANES, tq))
            if jp < lo_row:
                ahead = jnp.where(other >= slab, 1.0, 0.0)
            elif jp >= lo_row + SUBLANES:
                ahead = jnp.where(other > slab, 1.0, 0.0)
            else:
                tie = jnp.where(sub > jp, 1.0, 0.0)
                ahead = jnp.where(other > slab, 1.0, jnp.where(other == slab, tie, 0.0))
            rank = rank + ahead
        bias_slabs.append(jnp.where(rank < float(SEL_TOP_N), 0.0, MASKED))
    bias_t = jnp.concatenate(bias_slabs + [jnp.zeros((LANES - n_slc, tq), jnp.float32)], axis=0)
    sel_bias = bias_t.T.astype(jnp.bfloat16)
    qa = jnp.concatenate([qs, jnp.concatenate([sel_bias] * hpg, axis=0)], axis=1)

    n_full = t0 // SEL_CHUNK
    groups = [g for g in (4, 2, 1) if g <= max((seq - tq) // SEL_CHUNK, 1)]

    def scores(c):
        base = pl.multiple_of(c * SEL_CHUNK, SEL_CHUNK)
        return base, lax.dot_general(qa, kaug_scr[pl.ds(base, SEL_CHUNK), :], _NT,
                                     preferred_element_type=jnp.float32)

    def keep(base, s, m):
        s_scr[:, pl.ds(base, SEL_CHUNK)] = s
        for k in range(SEL_CHUNK // LANES):
            m = jnp.maximum(m, s[:, k * LANES:(k + 1) * LANES])
        return m

    base_d, s_d = scores(n_full)
    kpos_d = base_d + lax.broadcasted_iota(jnp.int32, (1, SEL_CHUNK), 1)
    m_scr[...] = keep(base_d, jnp.where(kpos_d <= t_s, s_d, MASKED),
                      jnp.full(m_scr.shape, MASKED, jnp.float32))
    done = 0
    for g in groups:
        @pl.when((n_full & g) != 0)
        def _(done=done, g=g):
            m = m_scr[...]
            for u in range(g):
                m = keep(*scores(done + u), m)
            m_scr[...] = m
        done = done + (n_full & g)
    m_s = jnp.max(m_scr[...], axis=-1, keepdims=True)
    m_scr[...] = jnp.broadcast_to(m_s, m_scr.shape)

    def values(c, m, l, acc):
        base = pl.multiple_of(c * SEL_CHUNK, SEL_CHUNK)
        ps = []
        for k in range(SEL_CHUNK // LANES):
            p = jnp.exp2(s_scr[:, pl.ds(base + k * LANES, LANES)] - m)
            l = l + p
            ps.append(p.astype(jnp.bfloat16))
        acc = acc + jnp.dot(jnp.concatenate(ps, axis=1), vs_ref[0, pl.ds(base, SEL_CHUNK), :],
                            preferred_element_type=jnp.float32)
        return l, acc

    l0, acc0 = values(n_full, m_scr[...], jnp.zeros(l_scr.shape, jnp.float32),
                      jnp.zeros(acc_scr.shape, jnp.float32))
    l_scr[...] = l0
    acc_scr[...] = acc0
    done = 0
    for g in groups:
        @pl.when((n_full & g) != 0)
        def _(done=done, g=g):
            m, l, acc = m_scr[...], l_scr[...], acc_scr[...]
            for u in range(g):
                l, acc = values(done + u, m, l, acc)
            l_scr[...] = l
            acc_scr[...] = acc
        done = done + (n_full & g)
    o_sel = acc_scr[...] * (1.0 / jnp.sum(l_scr[...], axis=-1, keepdims=True))

    gate = gate_ref[0, 0]
    outs = []
    for h in range(hpg):
        r0 = h * tq
        outs.append(gate[:, h:h + 1] * o_cmp[r0:r0 + tq]
                    + gate[:, hpg + h:hpg + h + 1] * o_sel[r0:r0 + tq]
                    + gate[:, 2 * hpg + h:2 * hpg + h + 1] * o_win[r0:r0 + tq])
    o = jnp.concatenate(outs, axis=1)
    o_ref[0] = (o * _silu(z_ref[0])).astype(o_ref.dtype)


def _attention(qkv, kcb, vcb, gates, zu, aggt, blockid, wbias):
    b, s, _ = qkv.shape
    assert SEL_CHUNK % Q_TILE == 0 and WINDOW % Q_TILE == 0 and s % SEL_CHUNK == 0
    qb = ATTN_WIDTH // HEAD_DIM
    kvb = KV_WIDTH // HEAD_DIM
    n_c = s // CMP_STRIDE
    rows = HEADS_PER_GROUP * Q_TILE
    n_pat = WINDOW // Q_TILE
    grid = (b, N_KV_GROUPS, s // Q_TILE)
    kv_spec = lambda off: pl.BlockSpec((1, s, HEAD_DIM), lambda bi, g, i: (bi, 0, off + g))
    return pl.pallas_call(
        _attn_kernel,
        grid=grid,
        in_specs=[pl.BlockSpec((1, Q_TILE, GROUP_WIDTH), lambda bi, g, i: (bi, i, g)),
                  kv_spec(qb), kv_spec(qb + kvb), kv_spec(qb + 2 * kvb), kv_spec(qb + 3 * kvb),
                  pl.BlockSpec((1, 1, n_c, HEAD_DIM), lambda bi, g, i: (bi, g, 0, 0)),
                  pl.BlockSpec((1, 1, n_c, HEAD_DIM), lambda bi, g, i: (bi, g, 0, 0)),
                  pl.BlockSpec((1, 1, Q_TILE, 3 * HEADS_PER_GROUP), lambda bi, g, i: (bi, g, i, 0)),
                  pl.BlockSpec((1, Q_TILE, GROUP_WIDTH), lambda bi, g, i: (bi, i, g)),
                  pl.BlockSpec(aggt.shape, lambda bi, g, i: (0, 0)),
                  pl.BlockSpec(blockid.shape, lambda bi, g, i: (0, 0)),
                  pl.BlockSpec((1, Q_TILE, WINDOW + Q_TILE),
                               lambda bi, g, i: (jnp.minimum(i, n_pat), 0, 0))],
        out_specs=pl.BlockSpec((1, Q_TILE, GROUP_WIDTH), lambda bi, g, i: (bi, i, g)),
        out_shape=jax.ShapeDtypeStruct((b, s, ATTN_WIDTH), jnp.bfloat16),
        scratch_shapes=[pltpu.VMEM((s, HEAD_DIM + LANES), jnp.bfloat16),
                        pltpu.VMEM((rows, s), jnp.float32),
                        pltpu.VMEM((rows, LANES), jnp.float32),
                        pltpu.VMEM((rows, LANES), jnp.float32),
                        pltpu.VMEM((rows, HEAD_DIM), jnp.float32)],
        compiler_params=_params("arbitrary", "arbitrary", "arbitrary"),
        name="nsa_attention",
    )(qkv, qkv, qkv, qkv, qkv, kcb, vcb, gates, zu, aggt, blockid, wbias)


def _conv_kernel(u_ref, cc_ref, cb_ref, z_ref, w_ref, b_ref, o_ref):
    v = cc_ref[0] * u_ref[0]
    row = lax.broadcasted_iota(jnp.int32, (v.shape[0], 1), 0)
    y = w_ref[CONV_K - 1:CONV_K, :] * v
    for d in range(1, CONV_K):
        shifted = jnp.where(row >= d, pltpu.roll(v, d, axis=0), 0.0)
        y = y + w_ref[CONV_K - 1 - d:CONV_K - d, :] * shifted
    o_ref[0] = (cb_ref[0] * (y + b_ref[...]) * _silu(z_ref[0])).astype(o_ref.dtype)


def _short_conv(zu, col0, conv_w, conv_b, tc=256):
    b, s, _ = zu.shape
    nb = CONV_WIDTH // tc
    c0 = col0 // tc
    spec = lambda k: pl.BlockSpec((1, s, tc), lambda bi, j: (bi, 0, c0 + k * nb + j))
    return pl.pallas_call(
        _conv_kernel,
        grid=(b, nb),
        in_specs=[spec(0), spec(1), spec(2), spec(3),
                  pl.BlockSpec((CONV_K, tc), lambda bi, j: (0, j)),
                  pl.BlockSpec((1, tc), lambda bi, j: (0, j))],
        out_specs=pl.BlockSpec((1, s, tc), lambda bi, j: (bi, 0, j)),
        out_shape=jax.ShapeDtypeStruct((b, s, CONV_WIDTH), jnp.bfloat16),
        compiler_params=_params("parallel", "parallel"),
        name="short_conv",
    )(zu, zu, zu, zu, conv_w, conv_b.reshape(1, CONV_WIDTH))


def _merge_kernel(a_ref, c_ref, wa_ref, wc_ref, g0_ref, g1_ref, o_ref):
    ya = jnp.dot(a_ref[...], wa_ref[...], preferred_element_type=jnp.float32)
    yc = jnp.dot(c_ref[...], wc_ref[...], preferred_element_type=jnp.float32)
    o_ref[...] = (_sigmoid(g0_ref[...]) * ya + _sigmoid(g1_ref[...]) * yc).astype(o_ref.dtype)


def _merge(a, c, wa, wc, zu, gcol0, tm=512, tn=1024):
    m, k = a.shape
    n = wa.shape[1]
    g0 = gcol0 // tn
    g1 = (gcol0 + n) // tn
    return pl.pallas_call(
        _merge_kernel,
        grid=(m // tm, n // tn),
        in_specs=[pl.BlockSpec((tm, k), lambda i, j: (i, 0)),
                  pl.BlockSpec((tm, k), lambda i, j: (i, 0)),
                  pl.BlockSpec((k, tn), lambda i, j: (0, j)),
                  pl.BlockSpec((k, tn), lambda i, j: (0, j)),
                  pl.BlockSpec((tm, tn), lambda i, j: (i, g0 + j)),
                  pl.BlockSpec((tm, tn), lambda i, j: (i, g1 + j))],
        out_specs=pl.BlockSpec((tm, tn), lambda i, j: (i, j)),
        out_shape=jax.ShapeDtypeStruct((m, n), jnp.bfloat16),
        compiler_params=_params("parallel", "arbitrary"),
        name="merge",
    )(a, c, wa, wc, zu, zu)


def _out_kernel(y_ref, w_ref, x_ref, g_ref, o_ref):
    r = x_ref[...] + jnp.dot(y_ref[...], w_ref[...], preferred_element_type=jnp.float32)
    ms = jnp.mean(r * r, axis=-1, keepdims=True)
    o_ref[...] = r * lax.rsqrt(ms + RMS_EPS) * g_ref[...]


def _out_proj(y, w, x2d, g, tm=512):
    m, k = y.shape
    n = w.shape[1]
    return pl.pallas_call(
        _out_kernel,
        grid=(m // tm,),
        in_specs=[pl.BlockSpec((tm, k), lambda i: (i, 0)),
                  pl.BlockSpec((k, n), lambda i: (0, 0)),
                  pl.BlockSpec((tm, n), lambda i: (i, 0)),
                  pl.BlockSpec((1, n), lambda i: (0, 0))],
        out_specs=pl.BlockSpec((tm, n), lambda i: (i, 0)),
        out_shape=jax.ShapeDtypeStruct((m, n), jnp.float32),
        compiler_params=_params("parallel"),
        name="out_proj",
    )(y, w, x2d, g.reshape(1, n))


def _attention_constants(s):
    n_slc = s // SEL_BLOCK
    n_cmp = (s - CMP_BLOCK) // CMP_STRIDE + 1
    c0 = np.arange(s // CMP_STRIDE)[None, :] * CMP_STRIDE
    s0 = np.arange(n_slc)[:, None] * SEL_BLOCK
    overlap = np.maximum(0, np.minimum(c0 + CMP_BLOCK, s0 + SEL_BLOCK) - np.maximum(c0, s0))
    aggt = (overlap / CMP_BLOCK) * (np.arange(s // CMP_STRIDE)[None, :] < n_cmp)
    blockid = (np.arange(s)[:, None] // SEL_BLOCK == np.arange(LANES)[None, :]).astype(np.float32)
    n_pat = WINDOW // Q_TILE
    pats = []
    for p in range(n_pat + 1):
        t = p * Q_TILE + np.arange(Q_TILE)[:, None]
        kpos = max(p * Q_TILE - WINDOW, 0) + np.arange(WINDOW + Q_TILE)[None, :]
        pats.append(np.where((kpos <= t) & (kpos > t - WINDOW), 0.0, MASKED))
    return (jnp.asarray(aggt, jnp.bfloat16), jnp.asarray(blockid, jnp.bfloat16),
            jnp.asarray(np.stack(pats), jnp.float32))


def _layer(x, norm_g, w_in, b_in, pe_k, w1_k, w2_k, pe_v, w1_v, w2_v,
           conv_w, conv_b, p_attn, p_conv, w_o, out_g):
    b, s, d = x.shape
    m = b * s
    bf = jnp.bfloat16
    x2d = x.reshape(m, d)

    c_q = ATTN_WIDTH
    c_cmp = c_q + 2 * KV_WIDTH
    c_kv = c_cmp + 4 * KV_WIDTH
    c_gate = c_kv + 3 * N_HEADS
    w_qkv = jnp.concatenate([w_in[:, :c_q], w_in[:, c_cmp:c_kv]], axis=1).astype(bf)
    b_qkv = jnp.concatenate([b_in[:c_q], b_in[c_cmp:c_kv]])
    s_qkv = jnp.concatenate([jnp.full((c_q,), SCALE * math.log2(math.e), jnp.float32),
                             jnp.ones((c_kv - c_cmp,), jnp.float32)])
    w_cmp = w_in[:, c_q:c_cmp].astype(bf)
    b_cmp = b_in[c_q:c_cmp]
    n_gate = 3 * N_HEADS
    w_gate = jnp.pad(w_in[:, c_kv:c_gate], ((0, 0), (0, LANES - n_gate))).astype(bf)
    b_gate = jnp.pad(b_in[c_kv:c_gate], (0, LANES - n_gate))
    w_rest = w_in[:, c_gate:].astype(bf)
    b_rest = b_in[c_gate:]

    hn = _rmsnorm(x2d, norm_g, bf)
    qkv = _proj(hn, w_qkv, b_qkv, bf, 1024, 1024, col_scale=s_qkv, name="proj_qkv").reshape(b, s, -1)
    kvc = _proj(hn, w_cmp, b_cmp, jnp.float32, 1024, 1024, name="proj_cmp").reshape(b, s, -1)
    gate = _proj(hn, w_gate, b_gate, jnp.float32, 1024, LANES, act="sigmoid", name="proj_gate")
    zu = _proj(hn, w_rest, b_rest, jnp.float32, 1024, 1024, name="proj_rest").reshape(b, s, -1)

    kcb = _compress(kvc, 0, pe_k, w1_k.astype(bf).reshape(CMP_BLOCK, HEAD_DIM, CMP_HIDDEN), w2_k.astype(bf))
    vcb = _compress(kvc, N_KV_GROUPS, pe_v, w1_v.astype(bf).reshape(CMP_BLOCK, HEAD_DIM, CMP_HIDDEN),
                    w2_v.astype(bf))

    gates = gate[:, :n_gate].reshape(b, s, 3, N_KV_GROUPS, HEADS_PER_GROUP)
    gates = gates.transpose(0, 3, 1, 2, 4).reshape(b, N_KV_GROUPS, s, 3 * HEADS_PER_GROUP)

    aggt, blockid, wbias = _attention_constants(s)
    a = _attention(qkv, kcb, vcb, gates, zu, aggt, blockid, wbias)
    c = _short_conv(zu, ATTN_WIDTH, conv_w, conv_b)

    y = _merge(a.reshape(m, -1), c.reshape(m, -1), p_attn.astype(bf), p_conv.astype(bf),
               zu.reshape(m, -1), ATTN_WIDTH + 4 * CONV_WIDTH)
    return _out_proj(y, w_o.astype(bf), x2d, out_g).reshape(b, s, d)


def kernel(x, norm_g, w_in, b_in, cmp_pe_k, cmp_w1_k, cmp_w2_k, cmp_pe_v, cmp_w1_v, cmp_w2_v,
           conv_w, conv_b, p_attn, p_conv, w_o, final_g):
    assert norm_g.shape[0] == 1, "single-layer block"
    return _layer(x, norm_g[0], w_in[0], b_in[0], cmp_pe_k[0], cmp_w1_k[0], cmp_w2_k[0],
                  cmp_pe_v[0], cmp_w1_v[0], cmp_w2_v[0], conv_w[0], conv_b[0],
                  p_attn[0], p_conv[0], w_o[0], final_g)
```

```python
import functools
import math

import numpy as np
import jax
import jax.numpy as jnp
from jax import lax
from jax.experimental import pallas as pl
from jax.experimental.pallas import tpu as pltpu

D_MODEL = 2048
N_HEADS = 16
HEAD_DIM = 128
N_KV_GROUPS = 4
HEADS_PER_GROUP = N_HEADS // N_KV_GROUPS
ATTN_WIDTH = N_HEADS * HEAD_DIM
KV_WIDTH = N_KV_GROUPS * HEAD_DIM
GROUP_WIDTH = HEADS_PER_GROUP * HEAD_DIM
CMP_BLOCK = 32
CMP_STRIDE = 16
CMP_HIDDEN = 256
SEL_BLOCK = 64
SEL_TOP_N = 8
WINDOW = 512
SCALE = HEAD_DIM ** -0.5
CONV_WIDTH = D_MODEL
CONV_K = 3
RMS_EPS = 1e-6
FORCED_SCORE = 1e4
MASKED = -(2.0 ** 100)

LANES = 128
SUBLANES = 8
Q_TILE = 256
SEL_CHUNK = 256
W_ROWS = 512
VMEM_LIMIT = 56 * 1024 * 1024

_NT = (((1,), (1,)), ((), ()))


def _params(*sem):
    return pltpu.CompilerParams(dimension_semantics=sem, vmem_limit_bytes=VMEM_LIMIT)


def _sigmoid(x):
    return 1.0 / (1.0 + jnp.exp(-x))


def _silu(x):
    return x * _sigmoid(x)


def _rmsnorm_kernel(x_ref, g_ref, o_ref):
    x = x_ref[...]
    ms = jnp.mean(x * x, axis=-1, keepdims=True)
    o_ref[...] = (x * lax.rsqrt(ms + RMS_EPS) * g_ref[...]).astype(o_ref.dtype)


def _rmsnorm(x2d, g, out_dtype, tm=512):
    m, d = x2d.shape
    return pl.pallas_call(
        _rmsnorm_kernel,
        grid=(m // tm,),
        in_specs=[pl.BlockSpec((tm, d), lambda i: (i, 0)),
                  pl.BlockSpec((1, d), lambda i: (0, 0))],
        out_specs=pl.BlockSpec((tm, d), lambda i: (i, 0)),
        out_shape=jax.ShapeDtypeStruct((m, d), out_dtype),
        compiler_params=_params("parallel"),
        name="rmsnorm",
    )(x2d, g.reshape(1, d))


def _fill_weight_tile(w_scr, mains, nexts, shift):
    k = w_scr.shape[0]
    col = 0
    for p, main in enumerate(mains):
        pw = main.shape[1]
        for r in range(0, k, W_ROWS):
            w = main[r:r + W_ROWS, :]
            if shift:
                w = jnp.concatenate([w, nexts[p][r:r + W_ROWS, :]], axis=1)
                w = pltpu.roll(w, pw + LANES - shift, axis=1)[:, :pw]
            w_scr[r:r + W_ROWS, col:col + pw] = w.astype(w_scr.dtype)
        col += pw


def _weight_specs(k, piece_w, block_fns, shift):
    mains = [pl.BlockSpec((k, piece_w), lambda j, i, f=f: (0, f(j))) for f in block_fns]
    nexts = [pl.BlockSpec((k, LANES), lambda j, i, f=f: (0, (f(j) + 1) * (piece_w // LANES)))
             for f in block_fns] if shift else []
    return mains + nexts


def _proj_kernel(a_ref, b_ref, *rest, shift, act, scaled_tiles, scale):
    n_w = 2 if shift else 1
    mains, nexts = rest[:1], rest[1:n_w]
    o_ref, w_scr = rest[n_w], rest[n_w + 1]

    @pl.when(pl.program_id(1) == 0)
    def _():
        _fill_weight_tile(w_scr, mains, nexts, shift)

    r = jnp.dot(a_ref[...], w_scr[...], preferred_element_type=jnp.float32) + b_ref[...]
    if scaled_tiles:
        r = r * jnp.where(pl.program_id(0) < scaled_tiles, scale, 1.0)
    if act == "sigmoid":
        r = _sigmoid(r)
    o_ref[...] = r.astype(o_ref.dtype)


def _proj(a, w, bias, block_fn, n_tiles, tn, shift, out_dtype, tm=1024, act=None,
          scaled_tiles=0, scale=1.0, name="proj"):
    m, k = a.shape
    specs = _weight_specs(k, tn, [block_fn], shift)
    return pl.pallas_call(
        functools.partial(_proj_kernel, shift=shift, act=act, scaled_tiles=scaled_tiles, scale=scale),
        grid=(n_tiles, m // tm),
        in_specs=[pl.BlockSpec((tm, k), lambda j, i: (i, 0)),
                  pl.BlockSpec((1, tn), lambda j, i: (0, j))] + specs,
        out_specs=pl.BlockSpec((tm, tn), lambda j, i: (i, j)),
        out_shape=jax.ShapeDtypeStruct((m, n_tiles * tn), out_dtype),
        scratch_shapes=[pltpu.VMEM((k, tn), jnp.bfloat16)],
        compiler_params=_params("arbitrary", "arbitrary"),
        name=name,
    )(a, bias.reshape(1, n_tiles * tn), *([w] * len(specs)))


def _proj_conv_kernel(a_ref, b_ref, cw_ref, cb_ref, *rest, shift, seq):
    n_p = 4
    mains, nexts = rest[:n_p], rest[n_p:2 * n_p] if shift else ()
    o_ref, w_scr, halo_scr = rest[-3], rest[-2], rest[-1]
    tm, cw = o_ref.shape
    i = pl.program_id(1)

    @pl.when(i == 0)
    def _():
        _fill_weight_tile(w_scr, mains, nexts, shift)

    @pl.when((i * tm) % seq == 0)
    def _():
        halo_scr[...] = jnp.zeros(halo_scr.shape, jnp.float32)

    r = jnp.dot(a_ref[...], w_scr[...], preferred_element_type=jnp.float32) + b_ref[...]
    u, cc, cb, z = (r[:, p * cw:(p + 1) * cw] for p in range(n_p))
    v = cc * u
    prev = halo_scr[...]
    halo_scr[...] = v[tm - SUBLANES:, :]
    row = lax.broadcasted_iota(jnp.int32, (tm, 1), 0)
    y = cw_ref[CONV_K - 1:CONV_K, :] * v
    for d in range(1, CONV_K):
        shifted = pltpu.roll(v, d, axis=0)
        for t in range(d):
            shifted = jnp.where(row == t, prev[SUBLANES - d + t:SUBLANES - d + t + 1, :], shifted)
        y = y + cw_ref[CONV_K - 1 - d:CONV_K - d, :] * shifted
    o_ref[...] = (cb * (y + cb_ref[...]) * _silu(z)).astype(o_ref.dtype)


def _proj_conv(a, w, bias_tiles, conv_w, conv_b, block_fns, shift, seq, tm=1024, cw=256):
    m, k = a.shape
    n_tiles = CONV_WIDTH // cw
    assert seq % tm == 0 and CONV_K - 1 <= SUBLANES
    specs = _weight_specs(k, cw, block_fns, shift)
    return pl.pallas_call(
        functools.partial(_proj_conv_kernel, shift=shift, seq=seq),
        grid=(n_tiles, m // tm),
        in_specs=[pl.BlockSpec((tm, k), lambda j, i: (i, 0)),
                  pl.BlockSpec((1, 4 * cw), lambda j, i: (0, j)),
                  pl.BlockSpec((CONV_K, cw), lambda j, i: (0, j)),
                  pl.BlockSpec((1, cw), lambda j, i: (0, j))] + specs,
        out_specs=pl.BlockSpec((tm, cw), lambda j, i: (i, j)),
        out_shape=jax.ShapeDtypeStruct((m, CONV_WIDTH), jnp.bfloat16),
        scratch_shapes=[pltpu.VMEM((k, 4 * cw), jnp.bfloat16),
                        pltpu.VMEM((SUBLANES, cw), jnp.float32)],
        compiler_params=_params("arbitrary", "arbitrary"),
        name="proj_conv",
    )(a, bias_tiles, conv_w, conv_b.reshape(1, CONV_WIDTH), *([w] * len(specs)))


def _compress_kernel(x_ref, pe_ref, w1_ref, w2_ref, o_ref):
    half = CMP_BLOCK // 2
    n_chunks = x_ref.shape[1] // CMP_STRIDE
    top = jnp.zeros((n_chunks, CMP_HIDDEN), jnp.float32)
    bot = jnp.zeros((n_chunks, CMP_HIDDEN), jnp.float32)
    for l in range(half):
        xl = x_ref[0, pl.ds(l, n_chunks, stride=CMP_STRIDE), :]
        a = (xl + pe_ref[l:l + 1, :]).astype(jnp.bfloat16)
        c = (xl + pe_ref[half + l:half + l + 1, :]).astype(jnp.bfloat16)
        top = top + jnp.dot(a, w1_ref[l], preferred_element_type=jnp.float32)
        bot = bot + jnp.dot(c, w1_ref[half + l], preferred_element_type=jnp.float32)
    h = top + pltpu.roll(bot, n_chunks - 1, axis=0)
    h = _silu(h).astype(jnp.bfloat16)
    o_ref[0, 0] = jnp.dot(h, w2_ref[...], preferred_element_type=jnp.float32).astype(o_ref.dtype)


def _compress(kv, col_block0, pe, w1, w2):
    b, s, _ = kv.shape
    return pl.pallas_call(
        _compress_kernel,
        grid=(b, N_KV_GROUPS),
        in_specs=[pl.BlockSpec((1, s, HEAD_DIM), lambda bi, g: (bi, 0, col_block0 + g)),
                  pl.BlockSpec((CMP_BLOCK, HEAD_DIM), lambda bi, g: (0, 0)),
                  pl.BlockSpec((CMP_BLOCK, HEAD_DIM, CMP_HIDDEN), lambda bi, g: (0, 0, 0)),
                  pl.BlockSpec((CMP_HIDDEN, HEAD_DIM), lambda bi, g: (0, 0))],
        out_specs=pl.BlockSpec((1, 1, s // CMP_STRIDE, HEAD_DIM), lambda bi, g: (bi, g, 0, 0)),
        out_shape=jax.ShapeDtypeStruct((b, N_KV_GROUPS, s // CMP_STRIDE, HEAD_DIM), jnp.bfloat16),
        compiler_params=_params("parallel", "parallel"),
        name="compress",
    )(kv, pe, w1, w2)


def _attn_kernel(q_ref, ks_ref, vs_ref, kw_ref, vw_ref, kc_ref, vc_ref, gate_ref, z_ref,
                 aggt_ref, blockid_ref, wbias_ref, o_ref,
                 kaug_scr, s_scr, m_scr, l_scr, acc_scr):
    tq = Q_TILE
    hpg = HEADS_PER_GROUP
    rows = hpg * tq
    seq = kw_ref.shape[1]
    i = pl.program_id(2)
    t0 = i * tq

    @pl.when(i == 0)
    def _():
        kaug_scr[:, :HEAD_DIM] = ks_ref[0]
        kaug_scr[:, HEAD_DIM:] = blockid_ref[...]

    q = q_ref[0]
    qs = jnp.concatenate([q[:, h * HEAD_DIM:(h + 1) * HEAD_DIM] for h in range(hpg)], axis=0)
    t_q = t0 + lax.broadcasted_iota(jnp.int32, (tq, 1), 0)
    t_s = jnp.concatenate([t_q] * hpg, axis=0)
    lane = lax.broadcasted_iota(jnp.int32, (1, LANES), 1)

    n_win = WINDOW + tq
    start = pl.multiple_of(jnp.maximum(t0 - WINDOW, 0), tq)
    sw = lax.dot_general(qs, kw_ref[0, pl.ds(start, n_win), :], _NT,
                         preferred_element_type=jnp.float32)
    sw = sw + jnp.concatenate([wbias_ref[0]] * hpg, axis=0)
    m_w = jnp.max(sw, axis=-1, keepdims=True)
    p_w = jnp.exp2(sw - m_w)
    l_w = jnp.sum(p_w, axis=-1, keepdims=True)
    o_win = jnp.dot(p_w.astype(jnp.bfloat16), vw_ref[0, pl.ds(start, n_win), :],
                    preferred_element_type=jnp.float32) * (1.0 / l_w)

    n_cmp = (seq - CMP_BLOCK) // CMP_STRIDE + 1
    sc = lax.dot_general(qs, kc_ref[0, 0], _NT, preferred_element_type=jnp.float32)
    vis_c = (lane * CMP_STRIDE + (CMP_BLOCK - 1) <= t_s) & (lane < n_cmp)
    sc = jnp.where(vis_c, sc, MASKED)
    m_c = jnp.max(sc, axis=-1, keepdims=True)
    p_c = jnp.exp2(sc - m_c)
    l_c = jnp.sum(p_c, axis=-1, keepdims=True)
    p_c = p_c * jnp.where(t_s >= CMP_BLOCK - 1, 1.0 / l_c, 0.0)
    o_cmp = jnp.dot(p_c.astype(jnp.bfloat16), vc_ref[0, 0], preferred_element_type=jnp.float32)

    p_sum = p_c[0:tq]
    for h in range(1, hpg):
        p_sum = p_sum + p_c[h * tq:(h + 1) * tq]
    aggt = aggt_ref[...]
    hi = p_sum.astype(jnp.bfloat16)
    r1 = p_sum - hi.astype(jnp.float32)
    mid = r1.astype(jnp.bfloat16)
    lo = (r1 - mid.astype(jnp.float32)).astype(jnp.bfloat16)
    imp = (lax.dot_general(aggt, hi, _NT, preferred_element_type=jnp.float32)
           + lax.dot_general(aggt, mid, _NT, preferred_element_type=jnp.float32)
           + lax.dot_general(aggt, lo, _NT, preferred_element_type=jnp.float32))
    n_slc = seq // SEL_BLOCK
    blk = lax.broadcasted_iota(jnp.int32, (n_slc, 1), 0)
    t_l = t0 + lax.broadcasted_iota(jnp.int32, (1, tq), 1)
    cur = t_l >> int(math.log2(SEL_BLOCK))
    causal = blk * SEL_BLOCK <= t_l
    forced = ((blk == 0) | (blk == cur) | (blk == cur - 1)) & causal
    imp = jnp.where(forced, FORCED_SCORE, imp)
    imp = jnp.where(causal, imp, -1.0)

    bias_slabs = []
    for v in range(n_slc // SUBLANES):
        lo_row = v * SUBLANES
        slab = imp[lo_row:lo_row + SUBLANES]
        sub = lo_row + lax.broadcasted_iota(jnp.int32, (SUBLANES, 1), 0)
        rank = jnp.zeros((SUBLANES, tq), jnp.float32)
        for jp in range(n_slc):
            other = jnp.broadcast_to(imp[jp:jp + 1], (SUBLANES, tq))
            if jp < lo_row:
                ahead = jnp.where(other >= slab, 1.0, 0.0)
            elif jp >= lo_row + SUBLANES:
                ahead = jnp.where(other > slab, 1.0, 0.0)
            else:
                tie = jnp.where(sub > jp, 1.0, 0.0)
                ahead = jnp.where(other > slab, 1.0, jnp.where(other == slab, tie, 0.0))
            rank = rank + ahead
        bias_slabs.append(jnp.where(rank < float(SEL_TOP_N), 0.0, MASKED))
    bias_t = jnp.concatenate(bias_slabs + [jnp.zeros((LANES - n_slc, tq), jnp.float32)], axis=0)
    sel_bias = bias_t.T.astype(jnp.bfloat16)
    qa = jnp.concatenate([qs, jnp.concatenate([sel_bias] * hpg, axis=0)], axis=1)

    n_full = t0 // SEL_CHUNK
    groups = [g for g in (4, 2, 1) if g <= max((seq - tq) // SEL_CHUNK, 1)]

    def scores(c):
        base = pl.multiple_of(c * SEL_CHUNK, SEL_CHUNK)
        return base, lax.dot_general(qa, kaug_scr[pl.ds(base, SEL_CHUNK), :], _NT,
                                     preferred_element_type=jnp.float32)

    def keep(base, s, m):
        s_scr[:, pl.ds(base, SEL_CHUNK)] = s
        for k in range(SEL_CHUNK // LANES):
            m = jnp.maximum(m, s[:, k * LANES:(k + 1) * LANES])
        return m

    base_d, s_d = scores(n_full)
    kpos_d = base_d + lax.broadcasted_iota(jnp.int32, (1, SEL_CHUNK), 1)
    m_scr[...] = keep(base_d, jnp.where(kpos_d <= t_s, s_d, MASKED),
                      jnp.full(m_scr.shape, MASKED, jnp.float32))
    done = 0
    for g in groups:
        @pl.when((n_full & g) != 0)
        def _(done=done, g=g):
            m = m_scr[...]
            for u in range(g):
                m = keep(*scores(done + u), m)
            m_scr[...] = m
        done = done + (n_full & g)
    m_s = jnp.max(m_scr[...], axis=-1, keepdims=True)
    m_scr[...] = jnp.broadcast_to(m_s, m_scr.shape)

    def values(c, m, l, acc):
        base = pl.multiple_of(c * SEL_CHUNK, SEL_CHUNK)
        ps = []
        for k in range(SEL_CHUNK // LANES):
            p = jnp.exp2(s_scr[:, pl.ds(base + k * LANES, LANES)] - m)
            l = l + p
            ps.append(p.astype(jnp.bfloat16))
        acc = acc + jnp.dot(jnp.concatenate(ps, axis=1), vs_ref[0, pl.ds(base, SEL_CHUNK), :],
                            preferred_element_type=jnp.float32)
        return l, acc

    l0, acc0 = values(n_full, m_scr[...], jnp.zeros(l_scr.shape, jnp.float32),
                      jnp.zeros(acc_scr.shape, jnp.float32))
    l_scr[...] = l0
    acc_scr[...] = acc0
    done = 0
    for g in groups:
        @pl.when((n_full & g) != 0)
        def _(done=done, g=g):
            m, l, acc = m_scr[...], l_scr[...], acc_scr[...]
            for u in range(g):
                l, acc = values(done + u, m, l, acc)
            l_scr[...] = l
            acc_scr[...] = acc
        done = done + (n_full & g)
    o_sel = acc_scr[...] * (1.0 / jnp.sum(l_scr[...], axis=-1, keepdims=True))

    gate = gate_ref[0, 0]
    outs = []
    for h in range(hpg):
        r0 = h * tq
        outs.append(gate[:, h:h + 1] * o_cmp[r0:r0 + tq]
                    + gate[:, hpg + h:hpg + h + 1] * o_sel[r0:r0 + tq]
                    + gate[:, 2 * hpg + h:2 * hpg + h + 1] * o_win[r0:r0 + tq])
    o = jnp.concatenate(outs, axis=1)
    o_ref[0] = (o * _silu(z_ref[0])).astype(o_ref.dtype)


def _attention(qkv, kcb, vcb, gates, zu, aggt, blockid, wbias):
    b, s, _ = qkv.shape
    assert SEL_CHUNK % Q_TILE == 0 and WINDOW % Q_TILE == 0 and s % SEL_CHUNK == 0
    qb = ATTN_WIDTH // HEAD_DIM
    kvb = KV_WIDTH // HEAD_DIM
    n_c = s // CMP_STRIDE
    rows = HEADS_PER_GROUP * Q_TILE
    n_pat = WINDOW // Q_TILE
    grid = (b, N_KV_GROUPS, s // Q_TILE)
    kv_spec = lambda off: pl.BlockSpec((1, s, HEAD_DIM), lambda bi, g, i: (bi, 0, off + g))
    return pl.pallas_call(
        _attn_kernel,
        grid=grid,
        in_specs=[pl.BlockSpec((1, Q_TILE, GROUP_WIDTH), lambda bi, g, i: (bi, i, g)),
                  kv_spec(qb), kv_spec(qb + kvb), kv_spec(qb + 2 * kvb), kv_spec(qb + 3 * kvb),
                  pl.BlockSpec((1, 1, n_c, HEAD_DIM), lambda bi, g, i: (bi, g, 0, 0)),
                  pl.BlockSpec((1, 1, n_c, HEAD_DIM), lambda bi, g, i: (bi, g, 0, 0)),
                  pl.BlockSpec((1, 1, Q_TILE, 3 * HEADS_PER_GROUP), lambda bi, g, i: (bi, g, i, 0)),
                  pl.BlockSpec((1, Q_TILE, GROUP_WIDTH), lambda bi, g, i: (bi, i, g)),
                  pl.BlockSpec(aggt.shape, lambda bi, g, i: (0, 0)),
                  pl.BlockSpec(blockid.shape, lambda bi, g, i: (0, 0)),
                  pl.BlockSpec((1, Q_TILE, WINDOW + Q_TILE),
                               lambda bi, g, i: (jnp.minimum(i, n_pat), 0, 0))],
        out_specs=pl.BlockSpec((1, Q_TILE, GROUP_WIDTH), lambda bi, g, i: (bi, i, g)),
        out_shape=jax.ShapeDtypeStruct((b, s, ATTN_WIDTH), jnp.bfloat16),
        scratch_shapes=[pltpu.VMEM((s, HEAD_DIM + LANES), jnp.bfloat16),
                        pltpu.VMEM((rows, s), jnp.float32),
                        pltpu.VMEM((rows, LANES), jnp.float32),
                        pltpu.VMEM((rows, LANES), jnp.float32),
                        pltpu.VMEM((rows, HEAD_DIM), jnp.float32)],
        compiler_params=_params("arbitrary", "arbitrary", "arbitrary"),
        name="nsa_attention",
    )(qkv, qkv, qkv, qkv, qkv, kcb, vcb, gates, zu, aggt, blockid, wbias)


def _merge_kernel(a_ref, c_ref, wa_ref, wc_ref, g0_ref, g1_ref, o_ref, wa_scr, wc_scr):
    @pl.when(pl.program_id(1) == 0)
    def _():
        _fill_weight_tile(wa_scr, [wa_ref], [], 0)
        _fill_weight_tile(wc_scr, [wc_ref], [], 0)

    ya = jnp.dot(a_ref[...], wa_scr[...], preferred_element_type=jnp.float32)
    yc = jnp.dot(c_ref[...], wc_scr[...], preferred_element_type=jnp.float32)
    o_ref[...] = (_sigmoid(g0_ref[...]) * ya + _sigmoid(g1_ref[...]) * yc).astype(o_ref.dtype)


def _merge(a, c, wa, wc, zg, gcol0, tm=512, tn=512):
    m, k = a.shape
    n = wa.shape[1]
    g0 = gcol0 // tn
    g1 = (gcol0 + n) // tn
    return pl.pallas_call(
        _merge_kernel,
        grid=(n // tn, m // tm),
        in_specs=[pl.BlockSpec((tm, k), lambda j, i: (i, 0)),
                  pl.BlockSpec((tm, k), lambda j, i: (i, 0)),
                  pl.BlockSpec((k, tn), lambda j, i: (0, j)),
                  pl.BlockSpec((k, tn), lambda j, i: (0, j)),
                  pl.BlockSpec((tm, tn), lambda j, i: (i, g0 + j)),
                  pl.BlockSpec((tm, tn), lambda j, i: (i, g1 + j))],
        out_specs=pl.BlockSpec((tm, tn), lambda j, i: (i, j)),
        out_shape=jax.ShapeDtypeStruct((m, n), jnp.bfloat16),
        scratch_shapes=[pltpu.VMEM((k, tn), jnp.bfloat16), pltpu.VMEM((k, tn), jnp.bfloat16)],
        compiler_params=_params("arbitrary", "arbitrary"),
        name="merge",
    )(a, c, wa, wc, zg, zg)


def _out_kernel(y_ref, w_ref, x_ref, g_ref, o_ref, w_scr):
    @pl.when(pl.program_id(0) == 0)
    def _():
        _fill_weight_tile(w_scr, [w_ref], [], 0)

    r = x_ref[...] + jnp.dot(y_ref[...], w_scr[...], preferred_element_type=jnp.float32)
    ms = jnp.mean(r * r, axis=-1, keepdims=True)
    o_ref[...] = r * lax.rsqrt(ms + RMS_EPS) * g_ref[...]


def _out_proj(y, w, x2d, g, tm=512):
    m, k = y.shape
    n = w.shape[1]
    return pl.pallas_call(
        _out_kernel,
        grid=(m // tm,),
        in_specs=[pl.BlockSpec((tm, k), lambda i: (i, 0)),
                  pl.BlockSpec((k, n), lambda i: (0, 0), pipeline_mode=pl.Buffered(1)),
                  pl.BlockSpec((tm, n), lambda i: (i, 0)),
                  pl.BlockSpec((1, n), lambda i: (0, 0))],
        out_specs=pl.BlockSpec((tm, n), lambda i: (i, 0)),
        out_shape=jax.ShapeDtypeStruct((m, n), jnp.float32),
        scratch_shapes=[pltpu.VMEM((k, n), jnp.bfloat16)],
        compiler_params=_params("arbitrary"),
        name="out_proj",
    )(y, w, x2d, g.reshape(1, n))


def _attention_constants(s):
    n_slc = s // SEL_BLOCK
    n_cmp = (s - CMP_BLOCK) // CMP_STRIDE + 1
    c0 = np.arange(s // CMP_STRIDE)[None, :] * CMP_STRIDE
    s0 = np.arange(n_slc)[:, None] * SEL_BLOCK
    overlap = np.maximum(0, np.minimum(c0 + CMP_BLOCK, s0 + SEL_BLOCK) - np.maximum(c0, s0))
    aggt = (overlap / CMP_BLOCK) * (np.arange(s // CMP_STRIDE)[None, :] < n_cmp)
    blockid = (np.arange(s)[:, None] // SEL_BLOCK == np.arange(LANES)[None, :]).astype(np.float32)
    n_pat = WINDOW // Q_TILE
    pats = []
    for p in range(n_pat + 1):
        t = p * Q_TILE + np.arange(Q_TILE)[:, None]
        kpos = max(p * Q_TILE - WINDOW, 0) + np.arange(WINDOW + Q_TILE)[None, :]
        pats.append(np.where((kpos <= t) & (kpos > t - WINDOW), 0.0, MASKED))
    return (jnp.asarray(aggt, jnp.bfloat16), jnp.asarray(blockid, jnp.bfloat16),
            jnp.asarray(np.stack(pats), jnp.float32))


def _layer(x, norm_g, w_in, b_in, pe_k, w1_k, w2_k, pe_v, w1_v, w2_v,
           conv_w, conv_b, p_attn, p_conv, w_o, out_g):
    b, s, d = x.shape
    m = b * s
    bf = jnp.bfloat16
    x2d = x.reshape(m, d)
    tn = 1024

    c_q = ATTN_WIDTH
    c_cmp = c_q + 2 * KV_WIDTH
    c_kv = c_cmp + 4 * KV_WIDTH
    n_gate = 3 * N_HEADS
    c_gate = c_kv + n_gate
    c_z = c_gate
    c_conv = c_z + ATTN_WIDTH
    c_merge = c_conv + 4 * CONV_WIDTH
    shift = c_gate % LANES
    assert c_cmp - c_q == tn and c_q % tn == 0 and c_kv % tn == 0 and (c_z - shift) % tn == 0

    hn = _rmsnorm(x2d, norm_g, bf)
    qkv = _proj(hn, w_in, jnp.concatenate([b_in[:c_q], b_in[c_cmp:c_kv]]),
                lambda j: jnp.where(j < c_q // tn, j, j + 1), (c_q + c_kv - c_cmp) // tn, tn, 0, bf,
                scaled_tiles=c_q // tn, scale=SCALE * math.log2(math.e), name="proj_qkv").reshape(b, s, -1)
    kvc = _proj(hn, w_in, b_in[c_q:c_cmp], lambda j: c_q // tn + j, 1, tn, 0, jnp.float32,
                name="proj_cmp").reshape(b, s, -1)
    gate = _proj(hn, w_in, jnp.pad(b_in[c_kv:c_gate], (0, LANES - n_gate)), lambda j: c_kv // LANES + j,
                 1, LANES, 0, jnp.float32, act="sigmoid", name="proj_gate")
    z_tiles = ATTN_WIDTH // tn
    zg = _proj(hn, w_in, jnp.concatenate([b_in[c_z:c_conv], b_in[c_merge:]]),
               lambda j: jnp.where(j < z_tiles, (c_z - shift) // tn + j,
                                   (c_merge - shift) // tn + j - z_tiles),
               z_tiles + 2 * D_MODEL // tn, tn, shift, jnp.float32, name="proj_zg").reshape(b, s, -1)
    cw = 256
    b_conv = b_in[c_conv:c_merge].reshape(4, CONV_WIDTH // cw, cw).transpose(1, 0, 2).reshape(1, -1)
    conv_blocks = [lambda j, p=p: (c_conv - shift + p * CONV_WIDTH) // cw + j for p in range(4)]
    c = _proj_conv(hn, w_in, b_conv, conv_w, conv_b, conv_blocks, shift, s, cw=cw)

    kcb = _compress(kvc, 0, pe_k, w1_k.astype(bf).reshape(CMP_BLOCK, HEAD_DIM, CMP_HIDDEN), w2_k.astype(bf))
    vcb = _compress(kvc, N_KV_GROUPS, pe_v, w1_v.astype(bf).reshape(CMP_BLOCK, HEAD_DIM, CMP_HIDDEN),
                    w2_v.astype(bf))

    gates = gate[:, :n_gate].reshape(b, s, 3, N_KV_GROUPS, HEADS_PER_GROUP)
    gates = gates.transpose(0, 3, 1, 2, 4).reshape(b, N_KV_GROUPS, s, 3 * HEADS_PER_GROUP)

    aggt, blockid, wbias = _attention_constants(s)
    a = _attention(qkv, kcb, vcb, gates, zg, aggt, blockid, wbias)

    y = _merge(a.reshape(m, -1), c, p_attn, p_conv, zg.reshape(m, -1), ATTN_WIDTH)
    return _out_proj(y, w_o, x2d, out_g).reshape(b, s, d)


def kernel(x, norm_g, w_in, b_in, cmp_pe_k, cmp_w1_k, cmp_w2_k, cmp_pe_v, cmp_w1_v, cmp_w2_v,
           conv_w, conv_b, p_attn, p_conv, w_o, final_g):
    assert norm_g.shape[0] == 1, "single-layer block"
    return _layer(x, norm_g[0], w_in[0], b_in[0], cmp_pe_k[0], cmp_w1_k[0], cmp_w2_k[0],
                  cmp_pe_v[0], cmp_w1_v[0], cmp_w2_v[0], conv_w[0], conv_b[0],
                  p_attn[0], p_conv[0], w_o[0], final_g)
```

```python
import functools
import math

import numpy as np
import jax
import jax.numpy as jnp
from jax import lax
from jax.experimental import pallas as pl
from jax.experimental.pallas import tpu as pltpu

D_MODEL = 2048
N_HEADS = 16
HEAD_DIM = 128
N_KV_GROUPS = 4
HEADS_PER_GROUP = N_HEADS // N_KV_GROUPS
ATTN_WIDTH = N_HEADS * HEAD_DIM
KV_WIDTH = N_KV_GROUPS * HEAD_DIM
GROUP_WIDTH = HEADS_PER_GROUP * HEAD_DIM
CMP_BLOCK = 32
CMP_STRIDE = 16
CMP_HIDDEN = 256
SEL_BLOCK = 64
SEL_TOP_N = 8
WINDOW = 512
SCALE = HEAD_DIM ** -0.5
CONV_WIDTH = D_MODEL
CONV_K = 3
RMS_EPS = 1e-6
FORCED_SCORE = 1e4
MASKED = -(2.0 ** 100)

LANES = 128
SUBLANES = 8
Q_TILE = 256
SEL_CHUNK = 256
W_ROWS = 512
VMEM_LIMIT = 56 * 1024 * 1024

_NT = (((1,), (1,)), ((), ()))


def _params(*sem):
    return pltpu.CompilerParams(dimension_semantics=sem, vmem_limit_bytes=VMEM_LIMIT)


def _sigmoid(x):
    return 1.0 / (1.0 + jnp.exp(-x))


def _silu(x):
    return x * _sigmoid(x)


def _rmsnorm_kernel(x_ref, g_ref, o_ref):
    x = x_ref[...]
    ms = jnp.mean(x * x, axis=-1, keepdims=True)
    o_ref[...] = (x * lax.rsqrt(ms + RMS_EPS) * g_ref[...]).astype(o_ref.dtype)


def _rmsnorm(x2d, g, out_dtype, tm=512):
    m, d = x2d.shape
    return pl.pallas_call(
        _rmsnorm_kernel,
        grid=(m // tm,),
        in_specs=[pl.BlockSpec((tm, d), lambda i: (i, 0)),
                  pl.BlockSpec((1, d), lambda i: (0, 0))],
        out_specs=pl.BlockSpec((tm, d), lambda i: (i, 0)),
        out_shape=jax.ShapeDtypeStruct((m, d), out_dtype),
        compiler_params=_params("parallel"),
        name="rmsnorm",
    )(x2d, g.reshape(1, d))


def _fill_weight_tile(w_scr, mains, nexts, shift):
    k = w_scr.shape[0]
    col = 0
    for p, main in enumerate(mains):
        pw = main.shape[1]
        for r in range(0, k, W_ROWS):
            w = main[r:r + W_ROWS, :]
            if shift:
                w = jnp.concatenate([w, nexts[p][r:r + W_ROWS, :]], axis=1)
                w = pltpu.roll(w, pw + LANES - shift, axis=1)[:, :pw]
            w_scr[r:r + W_ROWS, col:col + pw] = w.astype(w_scr.dtype)
        col += pw


def _weight_specs(k, piece_w, block_fns, shift):
    mains = [pl.BlockSpec((None, k, piece_w), lambda j, i, f=f: (0, 0, f(j))) for f in block_fns]
    nexts = [pl.BlockSpec((None, k, LANES), lambda j, i, f=f: (0, 0, (f(j) + 1) * (piece_w // LANES)))
             for f in block_fns] if shift else []
    return mains + nexts


def _proj_kernel(a_ref, b_ref, *rest, shift, act, scaled_tiles, scale):
    n_w = 2 if shift else 1
    mains, nexts = rest[:1], rest[1:n_w]
    o_ref, w_scr = rest[n_w], rest[n_w + 1]

    @pl.when(pl.program_id(1) == 0)
    def _():
        _fill_weight_tile(w_scr, mains, nexts, shift)

    r = jnp.dot(a_ref[...], w_scr[...], preferred_element_type=jnp.float32) + b_ref[...]
    if scaled_tiles:
        r = r * jnp.where(pl.program_id(0) < scaled_tiles, scale, 1.0)
    if act == "sigmoid":
        r = _sigmoid(r)
    o_ref[...] = r.astype(o_ref.dtype)


def _proj(a, w, bias, block_fn, n_tiles, tn, shift, out_dtype, tm=1024, act=None,
          scaled_tiles=0, scale=1.0, name="proj"):
    m, k = a.shape
    specs = _weight_specs(k, tn, [block_fn], shift)
    return pl.pallas_call(
        functools.partial(_proj_kernel, shift=shift, act=act, scaled_tiles=scaled_tiles, scale=scale),
        grid=(n_tiles, m // tm),
        in_specs=[pl.BlockSpec((tm, k), lambda j, i: (i, 0)),
                  pl.BlockSpec((1, tn), lambda j, i: (0, j))] + specs,
        out_specs=pl.BlockSpec((tm, tn), lambda j, i: (i, j)),
        out_shape=jax.ShapeDtypeStruct((m, n_tiles * tn), out_dtype),
        scratch_shapes=[pltpu.VMEM((k, tn), jnp.bfloat16)],
        compiler_params=_params("arbitrary", "arbitrary"),
        name=name,
    )(a, bias.reshape(1, n_tiles * tn), *([w] * len(specs)))


def _proj_conv_kernel(a_ref, b_ref, cw_ref, cb_ref, *rest, shift, seq):
    n_p = 4
    mains, nexts = rest[:n_p], rest[n_p:2 * n_p] if shift else ()
    o_ref, w_scr, halo_scr = rest[-3], rest[-2], rest[-1]
    tm, cw = o_ref.shape
    i = pl.program_id(1)

    @pl.when(i == 0)
    def _():
        _fill_weight_tile(w_scr, mains, nexts, shift)

    @pl.when((i * tm) % seq == 0)
    def _():
        halo_scr[...] = jnp.zeros(halo_scr.shape, jnp.float32)

    r = jnp.dot(a_ref[...], w_scr[...], preferred_element_type=jnp.float32) + b_ref[...]
    u, cc, cb, z = (r[:, p * cw:(p + 1) * cw] for p in range(n_p))
    v = cc * u
    prev = halo_scr[...]
    halo_scr[...] = v[tm - SUBLANES:, :]
    row = lax.broadcasted_iota(jnp.int32, (tm, 1), 0)
    y = cw_ref[CONV_K - 1:CONV_K, :] * v
    for d in range(1, CONV_K):
        shifted = pltpu.roll(v, d, axis=0)
        for t in range(d):
            shifted = jnp.where(row == t, prev[SUBLANES - d + t:SUBLANES - d + t + 1, :], shifted)
        y = y + cw_ref[CONV_K - 1 - d:CONV_K - d, :] * shifted
    o_ref[...] = (cb * (y + cb_ref[...]) * _silu(z)).astype(o_ref.dtype)


def _proj_conv(a, w, bias_tiles, conv_w, conv_b, block_fns, shift, seq, tm=1024, cw=256):
    m, k = a.shape
    n_tiles = CONV_WIDTH // cw
    assert seq % tm == 0 and CONV_K - 1 <= SUBLANES
    specs = _weight_specs(k, cw, block_fns, shift)
    return pl.pallas_call(
        functools.partial(_proj_conv_kernel, shift=shift, seq=seq),
        grid=(n_tiles, m // tm),
        in_specs=[pl.BlockSpec((tm, k), lambda j, i: (i, 0)),
                  pl.BlockSpec((1, 4 * cw), lambda j, i: (0, j)),
                  pl.BlockSpec((CONV_K, cw), lambda j, i: (0, j)),
                  pl.BlockSpec((1, cw), lambda j, i: (0, j))] + specs,
        out_specs=pl.BlockSpec((tm, cw), lambda j, i: (i, j)),
        out_shape=jax.ShapeDtypeStruct((m, CONV_WIDTH), jnp.bfloat16),
        scratch_shapes=[pltpu.VMEM((k, 4 * cw), jnp.bfloat16),
                        pltpu.VMEM((SUBLANES, cw), jnp.float32)],
        compiler_params=_params("arbitrary", "arbitrary"),
        name="proj_conv",
    )(a, bias_tiles, conv_w, conv_b.reshape(1, CONV_WIDTH), *([w] * len(specs)))


def _compress_kernel(x_ref, pe_ref, w1_ref, w2_ref, o_ref):
    half = CMP_BLOCK // 2
    n_chunks = x_ref.shape[1] // CMP_STRIDE
    top = jnp.zeros((n_chunks, CMP_HIDDEN), jnp.float32)
    bot = jnp.zeros((n_chunks, CMP_HIDDEN), jnp.float32)
    for l in range(half):
        xl = x_ref[0, pl.ds(l, n_chunks, stride=CMP_STRIDE), :]
        a = (xl + pe_ref[l:l + 1, :]).astype(jnp.bfloat16)
        c = (xl + pe_ref[half + l:half + l + 1, :]).astype(jnp.bfloat16)
        top = top + jnp.dot(a, w1_ref[l], preferred_element_type=jnp.float32)
        bot = bot + jnp.dot(c, w1_ref[half + l], preferred_element_type=jnp.float32)
    h = top + pltpu.roll(bot, n_chunks - 1, axis=0)
    h = _silu(h).astype(jnp.bfloat16)
    o_ref[0, 0] = jnp.dot(h, w2_ref[...], preferred_element_type=jnp.float32).astype(o_ref.dtype)


def _compress(kv, col_block0, pe, w1, w2):
    b, s, _ = kv.shape
    return pl.pallas_call(
        _compress_kernel,
        grid=(b, N_KV_GROUPS),
        in_specs=[pl.BlockSpec((1, s, HEAD_DIM), lambda bi, g: (bi, 0, col_block0 + g)),
                  pl.BlockSpec((CMP_BLOCK, HEAD_DIM), lambda bi, g: (0, 0)),
                  pl.BlockSpec((CMP_BLOCK, HEAD_DIM, CMP_HIDDEN), lambda bi, g: (0, 0, 0)),
                  pl.BlockSpec((CMP_HIDDEN, HEAD_DIM), lambda bi, g: (0, 0))],
        out_specs=pl.BlockSpec((1, 1, s // CMP_STRIDE, HEAD_DIM), lambda bi, g: (bi, g, 0, 0)),
        out_shape=jax.ShapeDtypeStruct((b, N_KV_GROUPS, s // CMP_STRIDE, HEAD_DIM), jnp.bfloat16),
        compiler_params=_params("parallel", "parallel"),
        name="compress",
    )(kv, pe, w1, w2)


def _attn_kernel(q_ref, ks_ref, vs_ref, kw_ref, vw_ref, kc_ref, vc_ref, gate_ref, z_ref,
                 aggt_ref, blockid_ref, wbias_ref, o_ref,
                 kaug_scr, s_scr, m_scr, l_scr, acc_scr):
    tq = Q_TILE
    hpg = HEADS_PER_GROUP
    rows = hpg * tq
    seq = kw_ref.shape[1]
    i = pl.program_id(2)
    t0 = i * tq

    @pl.when(i == 0)
    def _():
        kaug_scr[:, :HEAD_DIM] = ks_ref[0]
        kaug_scr[:, HEAD_DIM:] = blockid_ref[...]

    q = q_ref[0]
    qs = jnp.concatenate([q[:, h * HEAD_DIM:(h + 1) * HEAD_DIM] for h in range(hpg)], axis=0)
    t_q = t0 + lax.broadcasted_iota(jnp.int32, (tq, 1), 0)
    t_s = jnp.concatenate([t_q] * hpg, axis=0)
    lane = lax.broadcasted_iota(jnp.int32, (1, LANES), 1)

    n_win = WINDOW + tq
    start = pl.multiple_of(jnp.maximum(t0 - WINDOW, 0), tq)
    sw = lax.dot_general(qs, kw_ref[0, pl.ds(start, n_win), :], _NT,
                         preferred_element_type=jnp.float32)
    sw = sw + jnp.concatenate([wbias_ref[0]] * hpg, axis=0)
    m_w = jnp.max(sw, axis=-1, keepdims=True)
    p_w = jnp.exp2(sw - m_w)
    l_w = jnp.sum(p_w, axis=-1, keepdims=True)
    o_win = jnp.dot(p_w.astype(jnp.bfloat16), vw_ref[0, pl.ds(start, n_win), :],
                    preferred_element_type=jnp.float32) * (1.0 / l_w)

    n_cmp = (seq - CMP_BLOCK) // CMP_STRIDE + 1
    sc = lax.dot_general(qs, kc_ref[0, 0], _NT, preferred_element_type=jnp.float32)
    vis_c = (lane * CMP_STRIDE + (CMP_BLOCK - 1) <= t_s) & (lane < n_cmp)
    sc = jnp.where(vis_c, sc, MASKED)
    m_c = jnp.max(sc, axis=-1, keepdims=True)
    p_c = jnp.exp2(sc - m_c)
    l_c = jnp.sum(p_c, axis=-1, keepdims=True)
    p_c = p_c * jnp.where(t_s >= CMP_BLOCK - 1, 1.0 / l_c, 0.0)
    o_cmp = jnp.dot(p_c.astype(jnp.bfloat16), vc_ref[0, 0], preferred_element_type=jnp.float32)

    p_sum = p_c[0:tq]
    for h in range(1, hpg):
        p_sum = p_sum + p_c[h * tq:(h + 1) * tq]
    aggt = aggt_ref[...]
    hi = p_sum.astype(jnp.bfloat16)
    r1 = p_sum - hi.astype(jnp.float32)
    mid = r1.astype(jnp.bfloat16)
    lo = (r1 - mid.astype(jnp.float32)).astype(jnp.bfloat16)
    imp = (lax.dot_general(aggt, hi, _NT, preferred_element_type=jnp.float32)
           + lax.dot_general(aggt, mid, _NT, preferred_element_type=jnp.float32)
           + lax.dot_general(aggt, lo, _NT, preferred_element_type=jnp.float32))
    n_slc = seq // SEL_BLOCK
    blk = lax.broadcasted_iota(jnp.int32, (n_slc, 1), 0)
    t_l = t0 + lax.broadcasted_iota(jnp.int32, (1, tq), 1)
    cur = t_l >> int(math.log2(SEL_BLOCK))
    causal = blk * SEL_BLOCK <= t_l
    forced = ((blk == 0) | (blk == cur) | (blk == cur - 1)) & causal
    imp = jnp.where(forced, FORCED_SCORE, imp)
    imp = jnp.where(causal, imp, -1.0)

    bias_slabs = []
    for v in range(n_slc // SUBLANES):
        lo_row = v * SUBLANES
        slab = imp[lo_row:lo_row + SUBLANES]
        sub = lo_row + lax.broadcasted_iota(jnp.int32, (SUBLANES, 1), 0)
        rank = jnp.zeros((SUBLANES, tq), jnp.float32)
        for jp in range(n_slc):
            other = jnp.broadcast_to(imp[jp:jp + 1], (SUBLANES, tq))
            if jp < lo_row:
                ahead = jnp.where(other >= slab, 1.0, 0.0)
            elif jp >= lo_row + SUBLANES:
                ahead = jnp.where(other > slab, 1.0, 0.0)
            else:
                tie = jnp.where(sub > jp, 1.0, 0.0)
                ahead = jnp.where(other > slab, 1.0, jnp.where(other == slab, tie, 0.0))
            rank = rank + ahead
        bias_slabs.append(jnp.where(rank < float(SEL_TOP_N), 0.0, MASKED))
    bias_t = jnp.concatenate(bias_slabs + [jnp.zeros((LANES - n_slc, tq), jnp.float32)], axis=0)
    sel_bias = bias_t.T.astype(jnp.bfloat16)
    qa = jnp.concatenate([qs, jnp.concatenate([sel_bias] * hpg, axis=0)], axis=1)

    n_full = t0 // SEL_CHUNK
    groups = [g for g in (4, 2, 1) if g <= max((seq - tq) // SEL_CHUNK, 1)]

    def scores(c):
        base = pl.multiple_of(c * SEL_CHUNK, SEL_CHUNK)
        return base, lax.dot_general(qa, kaug_scr[pl.ds(base, SEL_CHUNK), :], _NT,
                                     preferred_element_type=jnp.float32)

    def keep(base, s, m):
        s_scr[:, pl.ds(base, SEL_CHUNK)] = s
        for k in range(SEL_CHUNK // LANES):
            m = jnp.maximum(m, s[:, k * LANES:(k + 1) * LANES])
        return m

    base_d, s_d = scores(n_full)
    kpos_d = base_d + lax.broadcasted_iota(jnp.int32, (1, SEL_CHUNK), 1)
    m_scr[...] = keep(base_d, jnp.where(kpos_d <= t_s, s_d, MASKED),
                      jnp.full(m_scr.shape, MASKED, jnp.float32))
    done = 0
    for g in groups:
        @pl.when((n_full & g) != 0)
        def _(done=done, g=g):
            m = m_scr[...]
            for u in range(g):
                m = keep(*scores(done + u), m)
            m_scr[...] = m
        done = done + (n_full & g)
    m_s = jnp.max(m_scr[...], axis=-1, keepdims=True)
    m_scr[...] = jnp.broadcast_to(m_s, m_scr.shape)

    def values(c, m, l, acc):
        base = pl.multiple_of(c * SEL_CHUNK, SEL_CHUNK)
        ps = []
        for k in range(SEL_CHUNK // LANES):
            p = jnp.exp2(s_scr[:, pl.ds(base + k * LANES, LANES)] - m)
            l = l + p
            ps.append(p.astype(jnp.bfloat16))
        acc = acc + jnp.dot(jnp.concatenate(ps, axis=1), vs_ref[0, pl.ds(base, SEL_CHUNK), :],
                            preferred_element_type=jnp.float32)
        return l, acc

    l0, acc0 = values(n_full, m_scr[...], jnp.zeros(l_scr.shape, jnp.float32),
                      jnp.zeros(acc_scr.shape, jnp.float32))
    l_scr[...] = l0
    acc_scr[...] = acc0
    done = 0
    for g in groups:
        @pl.when((n_full & g) != 0)
        def _(done=done, g=g):
            m, l, acc = m_scr[...], l_scr[...], acc_scr[...]
            for u in range(g):
                l, acc = values(done + u, m, l, acc)
            l_scr[...] = l
            acc_scr[...] = acc
        done = done + (n_full & g)
    o_sel = acc_scr[...] * (1.0 / jnp.sum(l_scr[...], axis=-1, keepdims=True))

    gate = gate_ref[0, 0]
    outs = []
    for h in range(hpg):
        r0 = h * tq
        outs.append(gate[:, h:h + 1] * o_cmp[r0:r0 + tq]
                    + gate[:, hpg + h:hpg + h + 1] * o_sel[r0:r0 + tq]
                    + gate[:, 2 * hpg + h:2 * hpg + h + 1] * o_win[r0:r0 + tq])
    o = jnp.concatenate(outs, axis=1)
    o_ref[0] = (o * _silu(z_ref[0])).astype(o_ref.dtype)


def _attention(qkv, kcb, vcb, gates, zu, aggt, blockid, wbias):
    b, s, _ = qkv.shape
    assert SEL_CHUNK % Q_TILE == 0 and WINDOW % Q_TILE == 0 and s % SEL_CHUNK == 0
    qb = ATTN_WIDTH // HEAD_DIM
    kvb = KV_WIDTH // HEAD_DIM
    n_c = s // CMP_STRIDE
    rows = HEADS_PER_GROUP * Q_TILE
    n_pat = WINDOW // Q_TILE
    grid = (b, N_KV_GROUPS, s // Q_TILE)
    kv_spec = lambda off: pl.BlockSpec((1, s, HEAD_DIM), lambda bi, g, i: (bi, 0, off + g))
    return pl.pallas_call(
        _attn_kernel,
        grid=grid,
        in_specs=[pl.BlockSpec((1, Q_TILE, GROUP_WIDTH), lambda bi, g, i: (bi, i, g)),
                  kv_spec(qb), kv_spec(qb + kvb), kv_spec(qb + 2 * kvb), kv_spec(qb + 3 * kvb),
                  pl.BlockSpec((1, 1, n_c, HEAD_DIM), lambda bi, g, i: (bi, g, 0, 0)),
                  pl.BlockSpec((1, 1, n_c, HEAD_DIM), lambda bi, g, i: (bi, g, 0, 0)),
                  pl.BlockSpec((1, 1, Q_TILE, 3 * HEADS_PER_GROUP), lambda bi, g, i: (bi, g, i, 0)),
                  pl.BlockSpec((1, Q_TILE, GROUP_WIDTH), lambda bi, g, i: (bi, i, g)),
                  pl.BlockSpec(aggt.shape, lambda bi, g, i: (0, 0)),
                  pl.BlockSpec(blockid.shape, lambda bi, g, i: (0, 0)),
                  pl.BlockSpec((1, Q_TILE, WINDOW + Q_TILE),
                               lambda bi, g, i: (jnp.minimum(i, n_pat), 0, 0))],
        out_specs=pl.BlockSpec((1, Q_TILE, GROUP_WIDTH), lambda bi, g, i: (bi, i, g)),
        out_shape=jax.ShapeDtypeStruct((b, s, ATTN_WIDTH), jnp.bfloat16),
        scratch_shapes=[pltpu.VMEM((s, HEAD_DIM + LANES), jnp.bfloat16),
                        pltpu.VMEM((rows, s), jnp.float32),
                        pltpu.VMEM((rows, LANES), jnp.float32),
                        pltpu.VMEM((rows, LANES), jnp.float32),
                        pltpu.VMEM((rows, HEAD_DIM), jnp.float32)],
        compiler_params=_params("arbitrary", "arbitrary", "arbitrary"),
        name="nsa_attention",
    )(qkv, qkv, qkv, qkv, qkv, kcb, vcb, gates, zu, aggt, blockid, wbias)


def _merge_kernel(a_ref, c_ref, wa_ref, wc_ref, g0_ref, g1_ref, o_ref, wa_scr, wc_scr):
    @pl.when(pl.program_id(1) == 0)
    def _():
        _fill_weight_tile(wa_scr, [wa_ref], [], 0)
        _fill_weight_tile(wc_scr, [wc_ref], [], 0)

    ya = jnp.dot(a_ref[...], wa_scr[...], preferred_element_type=jnp.float32)
    yc = jnp.dot(c_ref[...], wc_scr[...], preferred_element_type=jnp.float32)
    o_ref[...] = (_sigmoid(g0_ref[...]) * ya + _sigmoid(g1_ref[...]) * yc).astype(o_ref.dtype)


def _merge(a, c, wa, wc, zg, gcol0, tm=512, tn=512):
    m, k = a.shape
    n = wa.shape[-1]
    g0 = gcol0 // tn
    g1 = (gcol0 + n) // tn
    return pl.pallas_call(
        _merge_kernel,
        grid=(n // tn, m // tm),
        in_specs=[pl.BlockSpec((tm, k), lambda j, i: (i, 0)),
                  pl.BlockSpec((tm, k), lambda j, i: (i, 0)),
                  pl.BlockSpec((None, k, tn), lambda j, i: (0, 0, j)),
                  pl.BlockSpec((None, k, tn), lambda j, i: (0, 0, j)),
                  pl.BlockSpec((tm, tn), lambda j, i: (i, g0 + j)),
                  pl.BlockSpec((tm, tn), lambda j, i: (i, g1 + j))],
        out_specs=pl.BlockSpec((tm, tn), lambda j, i: (i, j)),
        out_shape=jax.ShapeDtypeStruct((m, n), jnp.bfloat16),
        scratch_shapes=[pltpu.VMEM((k, tn), jnp.bfloat16), pltpu.VMEM((k, tn), jnp.bfloat16)],
        compiler_params=_params("arbitrary", "arbitrary"),
        name="merge",
    )(a, c, wa, wc, zg, zg)


def _out_kernel(y_ref, w_ref, x_ref, g_ref, o_ref, w_scr):
    @pl.when(pl.program_id(0) == 0)
    def _():
        _fill_weight_tile(w_scr, [w_ref], [], 0)

    r = x_ref[...] + jnp.dot(y_ref[...], w_scr[...], preferred_element_type=jnp.float32)
    ms = jnp.mean(r * r, axis=-1, keepdims=True)
    o_ref[...] = r * lax.rsqrt(ms + RMS_EPS) * g_ref[...]


def _out_proj(y, w, x2d, g, tm=512):
    m, k = y.shape
    n = w.shape[-1]
    return pl.pallas_call(
        _out_kernel,
        grid=(m // tm,),
        in_specs=[pl.BlockSpec((tm, k), lambda i: (i, 0)),
                  pl.BlockSpec((None, k, n), lambda i: (0, 0, 0), pipeline_mode=pl.Buffered(1)),
                  pl.BlockSpec((tm, n), lambda i: (i, 0)),
                  pl.BlockSpec((1, n), lambda i: (0, 0))],
        out_specs=pl.BlockSpec((tm, n), lambda i: (i, 0)),
        out_shape=jax.ShapeDtypeStruct((m, n), jnp.float32),
        scratch_shapes=[pltpu.VMEM((k, n), jnp.bfloat16)],
        compiler_params=_params("arbitrary"),
        name="out_proj",
    )(y, w, x2d, g.reshape(1, n))


def _attention_constants(s):
    n_slc = s // SEL_BLOCK
    n_cmp = (s - CMP_BLOCK) // CMP_STRIDE + 1
    c0 = np.arange(s // CMP_STRIDE)[None, :] * CMP_STRIDE
    s0 = np.arange(n_slc)[:, None] * SEL_BLOCK
    overlap = np.maximum(0, np.minimum(c0 + CMP_BLOCK, s0 + SEL_BLOCK) - np.maximum(c0, s0))
    aggt = (overlap / CMP_BLOCK) * (np.arange(s // CMP_STRIDE)[None, :] < n_cmp)
    blockid = (np.arange(s)[:, None] // SEL_BLOCK == np.arange(LANES)[None, :]).astype(np.float32)
    n_pat = WINDOW // Q_TILE
    pats = []
    for p in range(n_pat + 1):
        t = p * Q_TILE + np.arange(Q_TILE)[:, None]
        kpos = max(p * Q_TILE - WINDOW, 0) + np.arange(WINDOW + Q_TILE)[None, :]
        pats.append(np.where((kpos <= t) & (kpos > t - WINDOW), 0.0, MASKED))
    return (jnp.asarray(aggt, jnp.bfloat16), jnp.asarray(blockid, jnp.bfloat16),
            jnp.asarray(np.stack(pats), jnp.float32))


def _layer(x, norm_g, w_in, b_in, pe_k, w1_k, w2_k, pe_v, w1_v, w2_v,
           conv_w, conv_b, p_attn, p_conv, w_o, out_g):
    b, s, d = x.shape
    m = b * s
    bf = jnp.bfloat16
    x2d = x.reshape(m, d)
    tn = 1024

    c_q = ATTN_WIDTH
    c_cmp = c_q + 2 * KV_WIDTH
    c_kv = c_cmp + 4 * KV_WIDTH
    n_gate = 3 * N_HEADS
    c_gate = c_kv + n_gate
    c_z = c_gate
    c_conv = c_z + ATTN_WIDTH
    c_merge = c_conv + 4 * CONV_WIDTH
    shift = c_gate % LANES
    assert c_cmp - c_q == tn and c_q % tn == 0 and c_kv % tn == 0 and (c_z - shift) % tn == 0

    hn = _rmsnorm(x2d, norm_g, bf)
    qkv = _proj(hn, w_in, jnp.concatenate([b_in[:c_q], b_in[c_cmp:c_kv]]),
                lambda j: jnp.where(j < c_q // tn, j, j + 1), (c_q + c_kv - c_cmp) // tn, tn, 0, bf,
                scaled_tiles=c_q // tn, scale=SCALE * math.log2(math.e), name="proj_qkv").reshape(b, s, -1)
    kvc = _proj(hn, w_in, b_in[c_q:c_cmp], lambda j: c_q // tn + j, 1, tn, 0, jnp.float32,
                name="proj_cmp").reshape(b, s, -1)
    gate = _proj(hn, w_in, jnp.pad(b_in[c_kv:c_gate], (0, LANES - n_gate)), lambda j: c_kv // LANES + j,
                 1, LANES, 0, jnp.float32, act="sigmoid", name="proj_gate")
    z_tiles = ATTN_WIDTH // tn
    zg = _proj(hn, w_in, jnp.concatenate([b_in[c_z:c_conv], b_in[c_merge:]]),
               lambda j: jnp.where(j < z_tiles, (c_z - shift) // tn + j,
                                   (c_merge - shift) // tn + j - z_tiles),
               z_tiles + 2 * D_MODEL // tn, tn, shift, jnp.float32, name="proj_zg").reshape(b, s, -1)
    cw = 256
    b_conv = b_in[c_conv:c_merge].reshape(4, CONV_WIDTH // cw, cw).transpose(1, 0, 2).reshape(1, -1)
    conv_blocks = [lambda j, p=p: (c_conv - shift + p * CONV_WIDTH) // cw + j for p in range(4)]
    c = _proj_conv(hn, w_in, b_conv, conv_w, conv_b, conv_blocks, shift, s, cw=cw)

    kcb = _compress(kvc, 0, pe_k, w1_k.astype(bf).reshape(CMP_BLOCK, HEAD_DIM, CMP_HIDDEN), w2_k.astype(bf))
    vcb = _compress(kvc, N_KV_GROUPS, pe_v, w1_v.astype(bf).reshape(CMP_BLOCK, HEAD_DIM, CMP_HIDDEN),
                    w2_v.astype(bf))

    gates = gate[:, :n_gate].reshape(b, s, 3, N_KV_GROUPS, HEADS_PER_GROUP)
    gates = gates.transpose(0, 3, 1, 2, 4).reshape(b, N_KV_GROUPS, s, 3 * HEADS_PER_GROUP)

    aggt, blockid, wbias = _attention_constants(s)
    a = _attention(qkv, kcb, vcb, gates, zg, aggt, blockid, wbias)

    y = _merge(a.reshape(m, -1), c, p_attn, p_conv, zg.reshape(m, -1), ATTN_WIDTH)
    return _out_proj(y, w_o, x2d, out_g).reshape(b, s, d)


def kernel(x, norm_g, w_in, b_in, cmp_pe_k, cmp_w1_k, cmp_w2_k, cmp_pe_v, cmp_w1_v, cmp_w2_v,
           conv_w, conv_b, p_attn, p_conv, w_o, final_g):
    assert norm_g.shape[0] == 1, "single-layer block"
    return _layer(x, norm_g[0], w_in, b_in[0], cmp_pe_k[0], cmp_w1_k[0], cmp_w2_k[0],
                  cmp_pe_v[0], cmp_w1_v[0], cmp_w2_v[0], conv_w[0], conv_b[0],
                  p_attn, p_conv, w_o, final_g)
```

```python
import functools
import math

import numpy as np
import jax
import jax.numpy as jnp
from jax import lax
from jax.experimental import pallas as pl
from jax.experimental.pallas import tpu as pltpu

D_MODEL = 2048
N_HEADS = 16
HEAD_DIM = 128
N_KV_GROUPS = 4
HEADS_PER_GROUP = N_HEADS // N_KV_GROUPS
ATTN_WIDTH = N_HEADS * HEAD_DIM
KV_WIDTH = N_KV_GROUPS * HEAD_DIM
GROUP_WIDTH = HEADS_PER_GROUP * HEAD_DIM
CMP_BLOCK = 32
CMP_STRIDE = 16
CMP_HIDDEN = 256
SEL_BLOCK = 64
SEL_TOP_N = 8
WINDOW = 512
SCALE = HEAD_DIM ** -0.5
CONV_WIDTH = D_MODEL
CONV_K = 3
RMS_EPS = 1e-6
FORCED_SCORE = 1e4
MASKED = -(2.0 ** 100)

LANES = 128
SUBLANES = 8
Q_TILE = 256
SEL_CHUNK = 256
W_ROWS = 512
W_ALIGN = 16
VMEM_LIMIT = 56 * 1024 * 1024

_NT = (((1,), (1,)), ((), ()))


def _params(*sem):
    return pltpu.CompilerParams(dimension_semantics=sem, vmem_limit_bytes=VMEM_LIMIT)


def _sigmoid(x):
    return 1.0 / (1.0 + jnp.exp(-x))


def _silu(x):
    return x * _sigmoid(x)


def _rmsnorm_kernel(x_ref, g_ref, o_ref):
    x = x_ref[...]
    ms = jnp.mean(x * x, axis=-1, keepdims=True)
    o_ref[...] = (x * lax.rsqrt(ms + RMS_EPS) * g_ref[...]).astype(o_ref.dtype)


def _rmsnorm(x2d, g, out_dtype, tm=512):
    m, d = x2d.shape
    return pl.pallas_call(
        _rmsnorm_kernel,
        grid=(m // tm,),
        in_specs=[pl.BlockSpec((tm, d), lambda i: (i, 0)),
                  pl.BlockSpec((1, d), lambda i: (0, 0))],
        out_specs=pl.BlockSpec((tm, d), lambda i: (i, 0)),
        out_shape=jax.ShapeDtypeStruct((m, d), out_dtype),
        compiler_params=_params("parallel"),
        name="rmsnorm",
    )(x2d, g.reshape(1, d))


def _cast_weight(w_scr, w_ref, row0=0):
    rows = w_ref.shape[0]
    for r in range(0, rows, W_ROWS):
        n = min(W_ROWS, rows - r)
        w_scr[row0 + r:row0 + r + n, :] = w_ref[r:r + n, :].astype(w_scr.dtype)


def _weight_specs(k, piece_rows, start_fns):
    return [pl.BlockSpec((pl.Element(piece_rows), pl.Element(k)),
                         lambda j, i, f=f: (pl.multiple_of(f(j), W_ALIGN), 0))
            for f in start_fns]


def _proj_kernel(a_ref, b_ref, w_ref, o_ref, w_scr, *, act, scaled_tiles, scale):
    @pl.when(pl.program_id(1) == 0)
    def _():
        _cast_weight(w_scr, w_ref)

    r = lax.dot_general(a_ref[...], w_scr[...], _NT, preferred_element_type=jnp.float32) + b_ref[...]
    if scaled_tiles:
        r = r * jnp.where(pl.program_id(0) < scaled_tiles, scale, 1.0)
    if act == "sigmoid":
        r = _sigmoid(r)
    o_ref[...] = r.astype(o_ref.dtype)


def _proj(a, wt, bias, start_fn, n_tiles, tn, out_dtype, tm=1024, act=None,
          scaled_tiles=0, scale=1.0, name="proj"):
    m, k = a.shape
    return pl.pallas_call(
        functools.partial(_proj_kernel, act=act, scaled_tiles=scaled_tiles, scale=scale),
        grid=(n_tiles, m // tm),
        in_specs=[pl.BlockSpec((tm, k), lambda j, i: (i, 0)),
                  pl.BlockSpec((1, tn), lambda j, i: (0, j))] + _weight_specs(k, tn, [start_fn]),
        out_specs=pl.BlockSpec((tm, tn), lambda j, i: (i, j)),
        out_shape=jax.ShapeDtypeStruct((m, n_tiles * tn), out_dtype),
        scratch_shapes=[pltpu.VMEM((tn, k), jnp.bfloat16)],
        compiler_params=_params("arbitrary", "arbitrary"),
        name=name,
    )(a, bias.reshape(1, n_tiles * tn), wt)


def _proj_conv_kernel(a_ref, b_ref, cw_ref, cb_ref, wu_ref, wcc_ref, wcb_ref, wz_ref,
                      o_ref, w_scr, halo_scr, *, seq):
    tm, cw = o_ref.shape
    i = pl.program_id(1)

    @pl.when(i == 0)
    def _():
        for p, w_ref in enumerate((wu_ref, wcc_ref, wcb_ref, wz_ref)):
            _cast_weight(w_scr, w_ref, p * cw)

    @pl.when((i * tm) % seq == 0)
    def _():
        halo_scr[...] = jnp.zeros(halo_scr.shape, jnp.float32)

    r = lax.dot_general(a_ref[...], w_scr[...], _NT, preferred_element_type=jnp.float32) + b_ref[...]
    u, cc, cb, z = (r[:, p * cw:(p + 1) * cw] for p in range(4))
    v = cc * u
    prev = halo_scr[...]
    halo_scr[...] = v[tm - SUBLANES:, :]
    row = lax.broadcasted_iota(jnp.int32, (tm, 1), 0)
    y = cw_ref[CONV_K - 1:CONV_K, :] * v
    for d in range(1, CONV_K):
        shifted = pltpu.roll(v, d, axis=0)
        for t in range(d):
            shifted = jnp.where(row == t, prev[SUBLANES - d + t:SUBLANES - d + t + 1, :], shifted)
        y = y + cw_ref[CONV_K - 1 - d:CONV_K - d, :] * shifted
    o_ref[...] = (cb * (y + cb_ref[...]) * _silu(z)).astype(o_ref.dtype)


def _proj_conv(a, wt, bias_tiles, conv_w, conv_b, start_fns, seq, tm=1024, cw=256):
    m, k = a.shape
    n_tiles = CONV_WIDTH // cw
    assert seq % tm == 0 and CONV_K - 1 <= SUBLANES and len(start_fns) == 4
    return pl.pallas_call(
        functools.partial(_proj_conv_kernel, seq=seq),
        grid=(n_tiles, m // tm),
        in_specs=[pl.BlockSpec((tm, k), lambda j, i: (i, 0)),
                  pl.BlockSpec((1, 4 * cw), lambda j, i: (0, j)),
                  pl.BlockSpec((CONV_K, cw), lambda j, i: (0, j)),
                  pl.BlockSpec((1, cw), lambda j, i: (0, j))] + _weight_specs(k, cw, start_fns),
        out_specs=pl.BlockSpec((tm, cw), lambda j, i: (i, j)),
        out_shape=jax.ShapeDtypeStruct((m, CONV_WIDTH), jnp.bfloat16),
        scratch_shapes=[pltpu.VMEM((4 * cw, k), jnp.bfloat16),
                        pltpu.VMEM((SUBLANES, cw), jnp.float32)],
        compiler_params=_params("arbitrary", "arbitrary"),
        name="proj_conv",
    )(a, bias_tiles, conv_w, conv_b.reshape(1, CONV_WIDTH), wt, wt, wt, wt)


def _compress_kernel(x_ref, pe_ref, w1_ref, w2_ref, o_ref):
    half = CMP_BLOCK // 2
    n_chunks = x_ref.shape[1] // CMP_STRIDE
    top = jnp.zeros((n_chunks, CMP_HIDDEN), jnp.float32)
    bot = jnp.zeros((n_chunks, CMP_HIDDEN), jnp.float32)
    for l in range(half):
        xl = x_ref[0, pl.ds(l, n_chunks, stride=CMP_STRIDE), :]
        a = (xl + pe_ref[l:l + 1, :]).astype(jnp.bfloat16)
        c = (xl + pe_ref[half + l:half + l + 1, :]).astype(jnp.bfloat16)
        top = top + jnp.dot(a, w1_ref[l], preferred_element_type=jnp.float32)
        bot = bot + jnp.dot(c, w1_ref[half + l], preferred_element_type=jnp.float32)
    h = top + pltpu.roll(bot, n_chunks - 1, axis=0)
    h = _silu(h).astype(jnp.bfloat16)
    o_ref[0, 0] = jnp.dot(h, w2_ref[...], preferred_element_type=jnp.float32).astype(o_ref.dtype)


def _compress(kv, col_block0, pe, w1, w2):
    b, s, _ = kv.shape
    return pl.pallas_call(
        _compress_kernel,
        grid=(b, N_KV_GROUPS),
        in_specs=[pl.BlockSpec((1, s, HEAD_DIM), lambda bi, g: (bi, 0, col_block0 + g)),
                  pl.BlockSpec((CMP_BLOCK, HEAD_DIM), lambda bi, g: (0, 0)),
                  pl.BlockSpec((CMP_BLOCK, HEAD_DIM, CMP_HIDDEN), lambda bi, g: (0, 0, 0)),
                  pl.BlockSpec((CMP_HIDDEN, HEAD_DIM), lambda bi, g: (0, 0))],
        out_specs=pl.BlockSpec((1, 1, s // CMP_STRIDE, HEAD_DIM), lambda bi, g: (bi, g, 0, 0)),
        out_shape=jax.ShapeDtypeStruct((b, N_KV_GROUPS, s // CMP_STRIDE, HEAD_DIM), jnp.bfloat16),
        compiler_params=_params("parallel", "parallel"),
        name="compress",
    )(kv, pe, w1, w2)


def _attn_kernel(q_ref, ks_ref, vs_ref, kw_ref, vw_ref, kc_ref, vc_ref, gate_ref, z_ref,
                 aggt_ref, blockid_ref, wbias_ref, o_ref,
                 kaug_scr, s_scr, m_scr, l_scr, acc_scr):
    tq = Q_TILE
    hpg = HEADS_PER_GROUP
    rows = hpg * tq
    seq = kw_ref.shape[1]
    i = pl.program_id(2)
    t0 = i * tq

    @pl.when(i == 0)
    def _():
        kaug_scr[:, :HEAD_DIM] = ks_ref[0]
        kaug_scr[:, HEAD_DIM:] = blockid_ref[...]

    q = q_ref[0]
    qs = jnp.concatenate([q[:, h * HEAD_DIM:(h + 1) * HEAD_DIM] for h in range(hpg)], axis=0)
    t_q = t0 + lax.broadcasted_iota(jnp.int32, (tq, 1), 0)
    t_s = jnp.concatenate([t_q] * hpg, axis=0)
    lane = lax.broadcasted_iota(jnp.int32, (1, LANES), 1)

    n_win = WINDOW + tq
    start = pl.multiple_of(jnp.maximum(t0 - WINDOW, 0), tq)
    sw = lax.dot_general(qs, kw_ref[0, pl.ds(start, n_win), :], _NT,
                         preferred_element_type=jnp.float32)
    sw = sw + jnp.concatenate([wbias_ref[0]] * hpg, axis=0)
    m_w = jnp.max(sw, axis=-1, keepdims=True)
    p_w = jnp.exp2(sw - m_w)
    l_w = jnp.sum(p_w, axis=-1, keepdims=True)
    o_win = jnp.dot(p_w.astype(jnp.bfloat16), vw_ref[0, pl.ds(start, n_win), :],
                    preferred_element_type=jnp.float32) * (1.0 / l_w)

    n_cmp = (seq - CMP_BLOCK) // CMP_STRIDE + 1
    sc = lax.dot_general(qs, kc_ref[0, 0], _NT, preferred_element_type=jnp.float32)
    vis_c = (lane * CMP_STRIDE + (CMP_BLOCK - 1) <= t_s) & (lane < n_cmp)
    sc = jnp.where(vis_c, sc, MASKED)
    m_c = jnp.max(sc, axis=-1, keepdims=True)
    p_c = jnp.exp2(sc - m_c)
    l_c = jnp.sum(p_c, axis=-1, keepdims=True)
    p_c = p_c * jnp.where(t_s >= CMP_BLOCK - 1, 1.0 / l_c, 0.0)
    o_cmp = jnp.dot(p_c.astype(jnp.bfloat16), vc_ref[0, 0], preferred_element_type=jnp.float32)

    p_sum = p_c[0:tq]
    for h in range(1, hpg):
        p_sum = p_sum + p_c[h * tq:(h + 1) * tq]
    aggt = aggt_ref[...]
    hi = p_sum.astype(jnp.bfloat16)
    r1 = p_sum - hi.astype(jnp.float32)
    mid = r1.astype(jnp.bfloat16)
    lo = (r1 - mid.astype(jnp.float32)).astype(jnp.bfloat16)
    imp = (lax.dot_general(aggt, hi, _NT, preferred_element_type=jnp.float32)
           + lax.dot_general(aggt, mid, _NT, preferred_element_type=jnp.float32)
           + lax.dot_general(aggt, lo, _NT, preferred_element_type=jnp.float32))
    n_slc = seq // SEL_BLOCK
    blk = lax.broadcasted_iota(jnp.int32, (n_slc, 1), 0)
    t_l = t0 + lax.broadcasted_iota(jnp.int32, (1, tq), 1)
    cur = t_l >> int(math.log2(SEL_BLOCK))
    causal = blk * SEL_BLOCK <= t_l
    forced = ((blk == 0) | (blk == cur) | (blk == cur - 1)) & causal
    imp = jnp.where(forced, FORCED_SCORE, imp)
    imp = jnp.where(causal, imp, -1.0)

    bias_slabs = []
    for v in range(n_slc // SUBLANES):
        lo_row = v * SUBLANES
        slab = imp[lo_row:lo_row + SUBLANES]
        sub = lo_row + lax.broadcasted_iota(jnp.int32, (SUBLANES, 1), 0)
        rank = jnp.zeros((SUBLANES, tq), jnp.float32)
        for jp in range(n_slc):
            other = jnp.broadcast_to(imp[jp:jp + 1], (SUBLANES, tq))
            if jp < lo_row:
                ahead = jnp.where(other >= slab, 1.0, 0.0)
            elif jp >= lo_row + SUBLANES:
                ahead = jnp.where(other > slab, 1.0, 0.0)
            else:
                tie = jnp.where(sub > jp, 1.0, 0.0)
                ahead = jnp.where(other > slab, 1.0, jnp.where(other == slab, tie, 0.0))
            rank = rank + ahead
        bias_slabs.append(jnp.where(rank < float(SEL_TOP_N), 0.0, MASKED))
    bias_t = jnp.concatenate(bias_slabs + [jnp.zeros((LANES - n_slc, tq), jnp.float32)], axis=0)
    sel_bias = bias_t.T.astype(jnp.bfloat16)
    qa = jnp.concatenate([qs, jnp.concatenate([sel_bias] * hpg, axis=0)], axis=1)

    n_full = t0 // SEL_CHUNK
    groups = [g for g in (4, 2, 1) if g <= max((seq - tq) // SEL_CHUNK, 1)]

    def scores(c):
        base = pl.multiple_of(c * SEL_CHUNK, SEL_CHUNK)
        return base, lax.dot_general(qa, kaug_scr[pl.ds(base, SEL_CHUNK), :], _NT,
                                     preferred_element_type=jnp.float32)

    def keep(base, s, m):
        s_scr[:, pl.ds(base, SEL_CHUNK)] = s
        for k in range(SEL_CHUNK // LANES):
            m = jnp.maximum(m, s[:, k * LANES:(k + 1) * LANES])
        return m

    base_d, s_d = scores(n_full)
    kpos_d = base_d + lax.broadcasted_iota(jnp.int32, (1, SEL_CHUNK), 1)
    m_scr[...] = keep(base_d, jnp.where(kpos_d <= t_s, s_d, MASKED),
                      jnp.full(m_scr.shape, MASKED, jnp.float32))
    done = 0
    for g in groups:
        @pl.when((n_full & g) != 0)
        def _(done=done, g=g):
            m = m_scr[...]
            for u in range(g):
                m = keep(*scores(done + u), m)
            m_scr[...] = m
        done = done + (n_full & g)
    m_s = jnp.max(m_scr[...], axis=-1, keepdims=True)
    m_scr[...] = jnp.broadcast_to(m_s, m_scr.shape)

    def values(c, m, l, acc):
        base = pl.multiple_of(c * SEL_CHUNK, SEL_CHUNK)
        ps = []
        for k in range(SEL_CHUNK // LANES):
            p = jnp.exp2(s_scr[:, pl.ds(base + k * LANES, LANES)] - m)
            l = l + p
            ps.append(p.astype(jnp.bfloat16))
        acc = acc + jnp.dot(jnp.concatenate(ps, axis=1), vs_ref[0, pl.ds(base, SEL_CHUNK), :],
                            preferred_element_type=jnp.float32)
        return l, acc

    l0, acc0 = values(n_full, m_scr[...], jnp.zeros(l_scr.shape, jnp.float32),
                      jnp.zeros(acc_scr.shape, jnp.float32))
    l_scr[...] = l0
    acc_scr[...] = acc0
    done = 0
    for g in groups:
        @pl.when((n_full & g) != 0)
        def _(done=done, g=g):
            m, l, acc = m_scr[...], l_scr[...], acc_scr[...]
            for u in range(g):
                l, acc = values(done + u, m, l, acc)
            l_scr[...] = l
            acc_scr[...] = acc
        done = done + (n_full & g)
    o_sel = acc_scr[...] * (1.0 / jnp.sum(l_scr[...], axis=-1, keepdims=True))

    gate = gate_ref[0, 0]
    outs = []
    for h in range(hpg):
        r0 = h * tq
        outs.append(gate[:, h:h + 1] * o_cmp[r0:r0 + tq]
                    + gate[:, hpg + h:hpg + h + 1] * o_sel[r0:r0 + tq]
                    + gate[:, 2 * hpg + h:2 * hpg + h + 1] * o_win[r0:r0 + tq])
    o = jnp.concatenate(outs, axis=1)
    o_ref[0] = (o * _silu(z_ref[0])).astype(o_ref.dtype)


def _attention(qkv, kcb, vcb, gates, zu, aggt, blockid, wbias):
    b, s, _ = qkv.shape
    assert SEL_CHUNK % Q_TILE == 0 and WINDOW % Q_TILE == 0 and s % SEL_CHUNK == 0
    qb = ATTN_WIDTH // HEAD_DIM
    kvb = KV_WIDTH // HEAD_DIM
    n_c = s // CMP_STRIDE
    rows = HEADS_PER_GROUP * Q_TILE
    n_pat = WINDOW // Q_TILE
    grid = (b, N_KV_GROUPS, s // Q_TILE)
    kv_spec = lambda off: pl.BlockSpec((1, s, HEAD_DIM), lambda bi, g, i: (bi, 0, off + g))
    return pl.pallas_call(
        _attn_kernel,
        grid=grid,
        in_specs=[pl.BlockSpec((1, Q_TILE, GROUP_WIDTH), lambda bi, g, i: (bi, i, g)),
                  kv_spec(qb), kv_spec(qb + kvb), kv_spec(qb + 2 * kvb), kv_spec(qb + 3 * kvb),
                  pl.BlockSpec((1, 1, n_c, HEAD_DIM), lambda bi, g, i: (bi, g, 0, 0)),
                  pl.BlockSpec((1, 1, n_c, HEAD_DIM), lambda bi, g, i: (bi, g, 0, 0)),
                  pl.BlockSpec((1, 1, Q_TILE, 3 * HEADS_PER_GROUP), lambda bi, g, i: (bi, g, i, 0)),
                  pl.BlockSpec((1, Q_TILE, GROUP_WIDTH), lambda bi, g, i: (bi, i, g)),
                  pl.BlockSpec(aggt.shape, lambda bi, g, i: (0, 0)),
                  pl.BlockSpec(blockid.shape, lambda bi, g, i: (0, 0)),
                  pl.BlockSpec((1, Q_TILE, WINDOW + Q_TILE),
                               lambda bi, g, i: (jnp.minimum(i, n_pat), 0, 0))],
        out_specs=pl.BlockSpec((1, Q_TILE, GROUP_WIDTH), lambda bi, g, i: (bi, i, g)),
        out_shape=jax.ShapeDtypeStruct((b, s, ATTN_WIDTH), jnp.bfloat16),
        scratch_shapes=[pltpu.VMEM((s, HEAD_DIM + LANES), jnp.bfloat16),
                        pltpu.VMEM((rows, s), jnp.float32),
                        pltpu.VMEM((rows, LANES), jnp.float32),
                        pltpu.VMEM((rows, LANES), jnp.float32),
                        pltpu.VMEM((rows, HEAD_DIM), jnp.float32)],
        compiler_params=_params("arbitrary", "arbitrary", "arbitrary"),
        name="nsa_attention",
    )(qkv, qkv, qkv, qkv, qkv, kcb, vcb, gates, zu, aggt, blockid, wbias)


def _merge_kernel(a_ref, c_ref, wa_ref, wc_ref, g0_ref, g1_ref, o_ref, wa_scr, wc_scr):
    @pl.when(pl.program_id(1) == 0)
    def _():
        _cast_weight(wa_scr, wa_ref)
        _cast_weight(wc_scr, wc_ref)

    ya = jnp.dot(a_ref[...], wa_scr[...], preferred_element_type=jnp.float32)
    yc = jnp.dot(c_ref[...], wc_scr[...], preferred_element_type=jnp.float32)
    o_ref[...] = (_sigmoid(g0_ref[...]) * ya + _sigmoid(g1_ref[...]) * yc).astype(o_ref.dtype)


def _merge(a, c, wa, wc, zg, gcol0, tm=512, tn=512):
    m, k = a.shape
    n = wa.shape[-1]
    g0 = gcol0 // tn
    g1 = (gcol0 + n) // tn
    return pl.pallas_call(
        _merge_kernel,
        grid=(n // tn, m // tm),
        in_specs=[pl.BlockSpec((tm, k), lambda j, i: (i, 0)),
                  pl.BlockSpec((tm, k), lambda j, i: (i, 0)),
                  pl.BlockSpec((None, k, tn), lambda j, i: (0, 0, j)),
                  pl.BlockSpec((None, k, tn), lambda j, i: (0, 0, j)),
                  pl.BlockSpec((tm, tn), lambda j, i: (i, g0 + j)),
                  pl.BlockSpec((tm, tn), lambda j, i: (i, g1 + j))],
        out_specs=pl.BlockSpec((tm, tn), lambda j, i: (i, j)),
        out_shape=jax.ShapeDtypeStruct((m, n), jnp.bfloat16),
        scratch_shapes=[pltpu.VMEM((k, tn), jnp.bfloat16), pltpu.VMEM((k, tn), jnp.bfloat16)],
        compiler_params=_params("arbitrary", "arbitrary"),
        name="merge",
    )(a, c, wa, wc, zg, zg)


def _out_kernel(y_ref, w_ref, x_ref, g_ref, o_ref, w_scr):
    @pl.when(pl.program_id(0) == 0)
    def _():
        _cast_weight(w_scr, w_ref)

    r = x_ref[...] + jnp.dot(y_ref[...], w_scr[...], preferred_element_type=jnp.float32)
    ms = jnp.mean(r * r, axis=-1, keepdims=True)
    o_ref[...] = r * lax.rsqrt(ms + RMS_EPS) * g_ref[...]


def _out_proj(y, w, x2d, g, tm=512):
    m, k = y.shape
    n = w.shape[-1]
    return pl.pallas_call(
        _out_kernel,
        grid=(m // tm,),
        in_specs=[pl.BlockSpec((tm, k), lambda i: (i, 0)),
                  pl.BlockSpec((None, k, n), lambda i: (0, 0, 0), pipeline_mode=pl.Buffered(1)),
                  pl.BlockSpec((tm, n), lambda i: (i, 0)),
                  pl.BlockSpec((1, n), lambda i: (0, 0))],
        out_specs=pl.BlockSpec((tm, n), lambda i: (i, 0)),
        out_shape=jax.ShapeDtypeStruct((m, n), jnp.float32),
        scratch_shapes=[pltpu.VMEM((k, n), jnp.bfloat16)],
        compiler_params=_params("arbitrary"),
        name="out_proj",
    )(y, w, x2d, g.reshape(1, n))


def _attention_constants(s):
    n_slc = s // SEL_BLOCK
    n_cmp = (s - CMP_BLOCK) // CMP_STRIDE + 1
    c0 = np.arange(s // CMP_STRIDE)[None, :] * CMP_STRIDE
    s0 = np.arange(n_slc)[:, None] * SEL_BLOCK
    overlap = np.maximum(0, np.minimum(c0 + CMP_BLOCK, s0 + SEL_BLOCK) - np.maximum(c0, s0))
    aggt = (overlap / CMP_BLOCK) * (np.arange(s // CMP_STRIDE)[None, :] < n_cmp)
    blockid = (np.arange(s)[:, None] // SEL_BLOCK == np.arange(LANES)[None, :]).astype(np.float32)
    n_pat = WINDOW // Q_TILE
    pats = []
    for p in range(n_pat + 1):
        t = p * Q_TILE + np.arange(Q_TILE)[:, None]
        kpos = max(p * Q_TILE - WINDOW, 0) + np.arange(WINDOW + Q_TILE)[None, :]
        pats.append(np.where((kpos <= t) & (kpos > t - WINDOW), 0.0, MASKED))
    return (jnp.asarray(aggt, jnp.bfloat16), jnp.asarray(blockid, jnp.bfloat16),
            jnp.asarray(np.stack(pats), jnp.float32))


def _layer(x, norm_g, w_in, b_in, pe_k, w1_k, w2_k, pe_v, w1_v, w2_v,
           conv_w, conv_b, p_attn, p_conv, w_o, out_g):
    b, s, d = x.shape
    m = b * s
    bf = jnp.bfloat16
    x2d = x.reshape(m, d)
    tn = 1024

    c_q = ATTN_WIDTH
    c_cmp = c_q + 2 * KV_WIDTH
    c_kv = c_cmp + 4 * KV_WIDTH
    n_gate = 3 * N_HEADS
    c_z = c_kv + n_gate
    c_conv = c_z + ATTN_WIDTH
    c_merge = c_conv + 4 * CONV_WIDTH
    assert c_cmp - c_q == tn and c_q % tn == 0 and (c_kv - c_cmp) % tn == 0
    wt = jnp.swapaxes(w_in, 1, 2).reshape(w_in.shape[2], w_in.shape[1])

    hn = _rmsnorm(x2d, norm_g, bf)
    q_tiles = c_q // tn
    qkv = _proj(hn, wt, jnp.concatenate([b_in[:c_q], b_in[c_cmp:c_kv]]),
                lambda j: jnp.where(j < q_tiles, j * tn, c_cmp + (j - q_tiles) * tn),
                q_tiles + (c_kv - c_cmp) // tn, tn, bf,
                scaled_tiles=q_tiles, scale=SCALE * math.log2(math.e), name="proj_qkv").reshape(b, s, -1)
    kvc = _proj(hn, wt, b_in[c_q:c_cmp], lambda j: c_q + j * tn, 1, tn, jnp.float32,
                name="proj_cmp").reshape(b, s, -1)
    gate = _proj(hn, wt, b_in[c_kv:c_kv + LANES], lambda j: c_kv + j * LANES, 1, LANES, jnp.float32,
                 act="sigmoid", name="proj_gate")
    z_tiles = ATTN_WIDTH // tn
    zg = _proj(hn, wt, jnp.concatenate([b_in[c_z:c_conv], b_in[c_merge:]]),
               lambda j: jnp.where(j < z_tiles, c_z + j * tn, c_merge + (j - z_tiles) * tn),
               z_tiles + 2 * D_MODEL // tn, tn, jnp.float32, name="proj_zg").reshape(b, s, -1)
    cw = 256
    b_conv = b_in[c_conv:c_merge].reshape(4, CONV_WIDTH // cw, cw).transpose(1, 0, 2).reshape(1, -1)
    conv_starts = [lambda j, p=p: c_conv + p * CONV_WIDTH + j * cw for p in range(4)]
    c = _proj_conv(hn, wt, b_conv, conv_w, conv_b, conv_starts, s, cw=cw)

    kcb = _compress(kvc, 0, pe_k, w1_k.astype(bf).reshape(CMP_BLOCK, HEAD_DIM, CMP_HIDDEN), w2_k.astype(bf))
    vcb = _compress(kvc, N_KV_GROUPS, pe_v, w1_v.astype(bf).reshape(CMP_BLOCK, HEAD_DIM, CMP_HIDDEN),
                    w2_v.astype(bf))

    gates = gate[:, :n_gate].reshape(b, s, 3, N_KV_GROUPS, HEADS_PER_GROUP)
    gates = gates.transpose(0, 3, 1, 2, 4).reshape(b, N_KV_GROUPS, s, 3 * HEADS_PER_GROUP)

    aggt, blockid, wbias = _attention_constants(s)
    a = _attention(qkv, kcb, vcb, gates, zg, aggt, blockid, wbias)

    y = _merge(a.reshape(m, -1), c, p_attn, p_conv, zg.reshape(m, -1), ATTN_WIDTH)
    return _out_proj(y, w_o, x2d, out_g).reshape(b, s, d)


def kernel(x, norm_g, w_in, b_in, cmp_pe_k, cmp_w1_k, cmp_w2_k, cmp_pe_v, cmp_w1_v, cmp_w2_v,
           conv_w, conv_b, p_attn, p_conv, w_o, final_g):
    assert norm_g.shape[0] == 1, "single-layer block"
    return _layer(x, norm_g[0], w_in, b_in[0], cmp_pe_k[0], cmp_w1_k[0], cmp_w2_k[0],
                  cmp_pe_v[0], cmp_w1_v[0], cmp_w2_v[0], conv_w[0], conv_b[0],
                  p_attn, p_conv, w_o, final_g)
```

```python
import functools
import math

import numpy as np
import jax
import jax.numpy as jnp
from jax import lax
from jax.experimental import pallas as pl
from jax.experimental.pallas import tpu as pltpu

D_MODEL = 2048
N_HEADS = 16
HEAD_DIM = 128
N_KV_GROUPS = 4
HEADS_PER_GROUP = N_HEADS // N_KV_GROUPS
ATTN_WIDTH = N_HEADS * HEAD_DIM
KV_WIDTH = N_KV_GROUPS * HEAD_DIM
GROUP_WIDTH = HEADS_PER_GROUP * HEAD_DIM
CMP_BLOCK = 32
CMP_STRIDE = 16
CMP_HIDDEN = 256
SEL_BLOCK = 64
SEL_TOP_N = 8
WINDOW = 512
SCALE = HEAD_DIM ** -0.5
CONV_WIDTH = D_MODEL
CONV_K = 3
RMS_EPS = 1e-6
FORCED_SCORE = 1e4
MASKED = -(2.0 ** 100)

LANES = 128
SUBLANES = 8
Q_TILE = 256
SEL_CHUNK = 256
W_ROWS = 512
W_ALIGN = 16
VMEM_LIMIT = 56 * 1024 * 1024

_NT = (((1,), (1,)), ((), ()))


def _params(*sem):
    return pltpu.CompilerParams(dimension_semantics=sem, vmem_limit_bytes=VMEM_LIMIT)


def _sigmoid(x):
    return 1.0 / (1.0 + jnp.exp(-x))


def _silu(x):
    return x * _sigmoid(x)


def _rmsnorm_kernel(x_ref, g_ref, o_ref):
    x = x_ref[...]
    ms = jnp.mean(x * x, axis=-1, keepdims=True)
    o_ref[...] = (x * lax.rsqrt(ms + RMS_EPS) * g_ref[...]).astype(o_ref.dtype)


def _rmsnorm(x2d, g, out_dtype, tm=512):
    m, d = x2d.shape
    return pl.pallas_call(
        _rmsnorm_kernel,
        grid=(m // tm,),
        in_specs=[pl.BlockSpec((tm, d), lambda i: (i, 0)),
                  pl.BlockSpec((1, d), lambda i: (0, 0))],
        out_specs=pl.BlockSpec((tm, d), lambda i: (i, 0)),
        out_shape=jax.ShapeDtypeStruct((m, d), out_dtype),
        compiler_params=_params("parallel"),
        name="rmsnorm",
    )(x2d, g.reshape(1, d))


def _cast_weight(w_scr, w_ref, row0=0):
    rows = w_ref.shape[0]
    for r in range(0, rows, W_ROWS):
        n = min(W_ROWS, rows - r)
        w_scr[row0 + r:row0 + r + n, :] = w_ref[r:r + n, :].astype(w_scr.dtype)


def _weight_specs(k, piece_rows, start_fns):
    return [pl.BlockSpec((pl.Element(piece_rows), pl.Element(k)),
                         lambda j, i, f=f: (pl.multiple_of(f(j), W_ALIGN), 0))
            for f in start_fns]


def _proj_kernel(a_ref, b_ref, w_ref, o_ref, w_scr, *, act, scaled_tiles, scale):
    @pl.when(pl.program_id(1) == 0)
    def _():
        _cast_weight(w_scr, w_ref)

    r = lax.dot_general(a_ref[...], w_scr[...], _NT, preferred_element_type=jnp.float32) + b_ref[...]
    if scaled_tiles:
        r = r * jnp.where(pl.program_id(0) < scaled_tiles, scale, 1.0)
    if act == "sigmoid":
        r = _sigmoid(r)
    o_ref[...] = r.astype(o_ref.dtype)


def _proj(a, wt, bias, start_fn, n_tiles, tn, out_dtype, tm=1024, act=None,
          scaled_tiles=0, scale=1.0, name="proj"):
    m, k = a.shape
    return pl.pallas_call(
        functools.partial(_proj_kernel, act=act, scaled_tiles=scaled_tiles, scale=scale),
        grid=(n_tiles, m // tm),
        in_specs=[pl.BlockSpec((tm, k), lambda j, i: (i, 0)),
                  pl.BlockSpec((1, tn), lambda j, i: (0, j))] + _weight_specs(k, tn, [start_fn]),
        out_specs=pl.BlockSpec((tm, tn), lambda j, i: (i, j)),
        out_shape=jax.ShapeDtypeStruct((m, n_tiles * tn), out_dtype),
        scratch_shapes=[pltpu.VMEM((tn, k), jnp.bfloat16)],
        compiler_params=_params("arbitrary", "arbitrary"),
        name=name,
    )(a, bias.reshape(1, n_tiles * tn), wt)


def _proj_conv_kernel(a_ref, b_ref, cw_ref, cb_ref, wu_ref, wcc_ref, wcb_ref, wz_ref,
                      o_ref, w_scr, halo_scr, *, seq):
    tm, cw = o_ref.shape
    i = pl.program_id(1)

    @pl.when(i == 0)
    def _():
        for p, w_ref in enumerate((wu_ref, wcc_ref, wcb_ref, wz_ref)):
            _cast_weight(w_scr, w_ref, p * cw)

    @pl.when((i * tm) % seq == 0)
    def _():
        halo_scr[...] = jnp.zeros(halo_scr.shape, jnp.float32)

    r = lax.dot_general(a_ref[...], w_scr[...], _NT, preferred_element_type=jnp.float32) + b_ref[...]
    u, cc, cb, z = (r[:, p * cw:(p + 1) * cw] for p in range(4))
    v = cc * u
    prev = halo_scr[...]
    halo_scr[...] = v[tm - SUBLANES:, :]
    row = lax.broadcasted_iota(jnp.int32, (tm, 1), 0)
    y = cw_ref[CONV_K - 1:CONV_K, :] * v
    for d in range(1, CONV_K):
        shifted = pltpu.roll(v, d, axis=0)
        for t in range(d):
            shifted = jnp.where(row == t, prev[SUBLANES - d + t:SUBLANES - d + t + 1, :], shifted)
        y = y + cw_ref[CONV_K - 1 - d:CONV_K - d, :] * shifted
    o_ref[...] = (cb * (y + cb_ref[...]) * _silu(z)).astype(o_ref.dtype)


def _proj_conv(a, wt, bias_tiles, conv_w, conv_b, start_fns, seq, tm=1024, cw=256):
    m, k = a.shape
    n_tiles = CONV_WIDTH // cw
    assert seq % tm == 0 and CONV_K - 1 <= SUBLANES and len(start_fns) == 4
    return pl.pallas_call(
        functools.partial(_proj_conv_kernel, seq=seq),
        grid=(n_tiles, m // tm),
        in_specs=[pl.BlockSpec((tm, k), lambda j, i: (i, 0)),
                  pl.BlockSpec((1, 4 * cw), lambda j, i: (0, j)),
                  pl.BlockSpec((CONV_K, cw), lambda j, i: (0, j)),
                  pl.BlockSpec((1, cw), lambda j, i: (0, j))] + _weight_specs(k, cw, start_fns),
        out_specs=pl.BlockSpec((tm, cw), lambda j, i: (i, j)),
        out_shape=jax.ShapeDtypeStruct((m, CONV_WIDTH), jnp.bfloat16),
        scratch_shapes=[pltpu.VMEM((4 * cw, k), jnp.bfloat16),
                        pltpu.VMEM((SUBLANES, cw), jnp.float32)],
        compiler_params=_params("arbitrary", "arbitrary"),
        name="proj_conv",
    )(a, bias_tiles, conv_w, conv_b.reshape(1, CONV_WIDTH), wt, wt, wt, wt)


def _compress_kernel(x_ref, pe_ref, w1_ref, w2_ref, o_ref):
    half = CMP_BLOCK // 2
    n_chunks = x_ref.shape[1] // CMP_STRIDE
    top = jnp.zeros((n_chunks, CMP_HIDDEN), jnp.float32)
    bot = jnp.zeros((n_chunks, CMP_HIDDEN), jnp.float32)
    for l in range(half):
        xl = x_ref[0, pl.ds(l, n_chunks, stride=CMP_STRIDE), :]
        a = (xl + pe_ref[l:l + 1, :]).astype(jnp.bfloat16)
        c = (xl + pe_ref[half + l:half + l + 1, :]).astype(jnp.bfloat16)
        top = top + jnp.dot(a, w1_ref[l], preferred_element_type=jnp.float32)
        bot = bot + jnp.dot(c, w1_ref[half + l], preferred_element_type=jnp.float32)
    h = top + pltpu.roll(bot, n_chunks - 1, axis=0)
    h = _silu(h).astype(jnp.bfloat16)
    o_ref[0, 0] = jnp.dot(h, w2_ref[...], preferred_element_type=jnp.float32).astype(o_ref.dtype)


def _compress(kv, col_block0, pe, w1, w2):
    b, s, _ = kv.shape
    return pl.pallas_call(
        _compress_kernel,
        grid=(b, N_KV_GROUPS),
        in_specs=[pl.BlockSpec((1, s, HEAD_DIM), lambda bi, g: (bi, 0, col_block0 + g)),
                  pl.BlockSpec((CMP_BLOCK, HEAD_DIM), lambda bi, g: (0, 0)),
                  pl.BlockSpec((CMP_BLOCK, HEAD_DIM, CMP_HIDDEN), lambda bi, g: (0, 0, 0)),
                  pl.BlockSpec((CMP_HIDDEN, HEAD_DIM), lambda bi, g: (0, 0))],
        out_specs=pl.BlockSpec((1, 1, s // CMP_STRIDE, HEAD_DIM), lambda bi, g: (bi, g, 0, 0)),
        out_shape=jax.ShapeDtypeStruct((b, N_KV_GROUPS, s // CMP_STRIDE, HEAD_DIM), jnp.bfloat16),
        compiler_params=_params("parallel", "parallel"),
        name="compress",
    )(kv, pe, w1, w2)


def _attn_kernel(q_ref, ks_ref, vs_ref, kw_ref, vw_ref, kc_ref, vc_ref, gate_ref, z_ref,
                 aggt_ref, blockid_ref, wbias_ref, o_ref,
                 kaug_scr, vsaug_scr, vwaug_scr, gexp_scr, s_scr, m_scr, acc_scr):
    tq = Q_TILE
    hpg = HEADS_PER_GROUP
    rows = hpg * tq
    seq = kw_ref.shape[1]
    i = pl.program_id(2)
    t0 = i * tq

    @pl.when(i == 0)
    def _():
        kaug_scr[:, :HEAD_DIM] = ks_ref[0]
        kaug_scr[:, HEAD_DIM:] = blockid_ref[...]
        ones = jnp.ones((seq, LANES), jnp.bfloat16)
        vsaug_scr[:, :HEAD_DIM] = vs_ref[0]
        vsaug_scr[:, HEAD_DIM:] = ones
        vwaug_scr[:, :HEAD_DIM] = vw_ref[0]
        vwaug_scr[:, HEAD_DIM:] = ones

    gate = gate_ref[0, 0]
    for c in range(3 * hpg):
        gexp_scr[c] = jnp.broadcast_to(gate[:, c:c + 1], (tq, LANES))

    q = q_ref[0]
    qs = jnp.concatenate([q[:, h * HEAD_DIM:(h + 1) * HEAD_DIM] for h in range(hpg)], axis=0)
    t_q = t0 + lax.broadcasted_iota(jnp.int32, (tq, 1), 0)
    t_s = jnp.concatenate([t_q] * hpg, axis=0)
    lane = lax.broadcasted_iota(jnp.int32, (1, LANES), 1)

    n_win = WINDOW + tq
    start = pl.multiple_of(jnp.maximum(t0 - WINDOW, 0), tq)
    sw = lax.dot_general(qs, kw_ref[0, pl.ds(start, n_win), :], _NT,
                         preferred_element_type=jnp.float32)
    sw = sw + jnp.concatenate([wbias_ref[0]] * hpg, axis=0)
    m_w = jnp.max(sw, axis=-1, keepdims=True)
    p_w = jnp.exp2(sw - m_w)
    pv_w = jnp.dot(p_w.astype(jnp.bfloat16), vwaug_scr[pl.ds(start, n_win), :],
                   preferred_element_type=jnp.float32)
    o_win = pv_w[:, :HEAD_DIM] * (1.0 / pv_w[:, HEAD_DIM:])

    n_cmp = (seq - CMP_BLOCK) // CMP_STRIDE + 1
    sc = lax.dot_general(qs, kc_ref[0, 0], _NT, preferred_element_type=jnp.float32)
    vis_c = (lane * CMP_STRIDE + (CMP_BLOCK - 1) <= t_s) & (lane < n_cmp)
    sc = jnp.where(vis_c, sc, MASKED)
    m_c = jnp.max(sc, axis=-1, keepdims=True)
    p_c = jnp.exp2(sc - m_c)
    l_c = jnp.sum(p_c, axis=-1, keepdims=True)
    p_c = p_c * jnp.where(t_s >= CMP_BLOCK - 1, 1.0 / l_c, 0.0)
    o_cmp = jnp.dot(p_c.astype(jnp.bfloat16), vc_ref[0, 0], preferred_element_type=jnp.float32)

    p_sum = p_c[0:tq]
    for h in range(1, hpg):
        p_sum = p_sum + p_c[h * tq:(h + 1) * tq]
    aggt = aggt_ref[...]
    hi = p_sum.astype(jnp.bfloat16)
    r1 = p_sum - hi.astype(jnp.float32)
    mid = r1.astype(jnp.bfloat16)
    lo = (r1 - mid.astype(jnp.float32)).astype(jnp.bfloat16)
    imp = (lax.dot_general(aggt, hi, _NT, preferred_element_type=jnp.float32)
           + lax.dot_general(aggt, mid, _NT, preferred_element_type=jnp.float32)
           + lax.dot_general(aggt, lo, _NT, preferred_element_type=jnp.float32))
    n_slc = seq // SEL_BLOCK
    blk = lax.broadcasted_iota(jnp.int32, (n_slc, 1), 0)
    t_l = t0 + lax.broadcasted_iota(jnp.int32, (1, tq), 1)
    cur = t_l >> int(math.log2(SEL_BLOCK))
    causal = blk * SEL_BLOCK <= t_l
    forced = ((blk == 0) | (blk == cur) | (blk == cur - 1)) & causal
    imp = jnp.where(forced, FORCED_SCORE, imp)
    imp = jnp.where(causal, imp, -1.0)

    bias_slabs = []
    for v in range(n_slc // SUBLANES):
        lo_row = v * SUBLANES
        slab = imp[lo_row:lo_row + SUBLANES]
        sub = lo_row + lax.broadcasted_iota(jnp.int32, (SUBLANES, 1), 0)
        rank = jnp.zeros((SUBLANES, tq), jnp.float32)
        for jp in range(n_slc):
            other = jnp.broadcast_to(imp[jp:jp + 1], (SUBLANES, tq))
            if jp < lo_row:
                ahead = jnp.where(other >= slab, 1.0, 0.0)
            elif jp >= lo_row + SUBLANES:
                ahead = jnp.where(other > slab, 1.0, 0.0)
            else:
                tie = jnp.where(sub > jp, 1.0, 0.0)
                ahead = jnp.where(other > slab, 1.0, jnp.where(other == slab, tie, 0.0))
            rank = rank + ahead
        bias_slabs.append(jnp.where(rank < float(SEL_TOP_N), 0.0, MASKED))
    bias_t = jnp.concatenate(bias_slabs + [jnp.zeros((LANES - n_slc, tq), jnp.float32)], axis=0)
    sel_bias = bias_t.T.astype(jnp.bfloat16)
    qa = jnp.concatenate([qs, jnp.concatenate([sel_bias] * hpg, axis=0)], axis=1)

    n_full = t0 // SEL_CHUNK
    groups = [g for g in (4, 2, 1) if g <= max((seq - tq) // SEL_CHUNK, 1)]

    def scores(c):
        base = pl.multiple_of(c * SEL_CHUNK, SEL_CHUNK)
        return base, lax.dot_general(qa, kaug_scr[pl.ds(base, SEL_CHUNK), :], _NT,
                                     preferred_element_type=jnp.float32)

    def keep(base, s, m):
        s_scr[:, pl.ds(base, SEL_CHUNK)] = s
        for k in range(SEL_CHUNK // LANES):
            m = jnp.maximum(m, s[:, k * LANES:(k + 1) * LANES])
        return m

    base_d, s_d = scores(n_full)
    kpos_d = base_d + lax.broadcasted_iota(jnp.int32, (1, SEL_CHUNK), 1)
    m_scr[...] = keep(base_d, jnp.where(kpos_d <= t_s, s_d, MASKED),
                      jnp.full(m_scr.shape, MASKED, jnp.float32))
    done = 0
    for g in groups:
        @pl.when((n_full & g) != 0)
        def _(done=done, g=g):
            m = m_scr[...]
            for u in range(g):
                m = keep(*scores(done + u), m)
            m_scr[...] = m
        done = done + (n_full & g)
    m_s = jnp.max(m_scr[...], axis=-1, keepdims=True)
    m_scr[...] = jnp.broadcast_to(m_s, m_scr.shape)

    def values(c, m):
        base = pl.multiple_of(c * SEL_CHUNK, SEL_CHUNK)
        ps = [jnp.exp2(s_scr[:, pl.ds(base + k * LANES, LANES)] - m).astype(jnp.bfloat16)
              for k in range(SEL_CHUNK // LANES)]
        return jnp.dot(jnp.concatenate(ps, axis=1), vsaug_scr[pl.ds(base, SEL_CHUNK), :],
                       preferred_element_type=jnp.float32)

    acc_scr[...] = values(n_full, m_scr[...])
    done = 0
    for g in groups:
        @pl.when((n_full & g) != 0)
        def _(done=done, g=g):
            m, acc = m_scr[...], acc_scr[...]
            for u in range(g):
                acc = acc + values(done + u, m)
            acc_scr[...] = acc
        done = done + (n_full & g)
    o_sel = acc_scr[:, :HEAD_DIM] * (1.0 / acc_scr[:, HEAD_DIM:])

    outs = []
    for h in range(hpg):
        r0 = h * tq
        outs.append(gexp_scr[h] * o_cmp[r0:r0 + tq]
                    + gexp_scr[hpg + h] * o_sel[r0:r0 + tq]
                    + gexp_scr[2 * hpg + h] * o_win[r0:r0 + tq])
    o = jnp.concatenate(outs, axis=1)
    o_ref[0] = (o * _silu(z_ref[0])).astype(o_ref.dtype)


def _attention(qkv, kcb, vcb, gates, zu, aggt, blockid, wbias):
    b, s, _ = qkv.shape
    assert SEL_CHUNK % Q_TILE == 0 and WINDOW % Q_TILE == 0 and s % SEL_CHUNK == 0
    qb = ATTN_WIDTH // HEAD_DIM
    kvb = KV_WIDTH // HEAD_DIM
    n_c = s // CMP_STRIDE
    rows = HEADS_PER_GROUP * Q_TILE
    n_pat = WINDOW // Q_TILE
    grid = (b, N_KV_GROUPS, s // Q_TILE)
    kv_spec = lambda off: pl.BlockSpec((1, s, HEAD_DIM), lambda bi, g, i: (bi, 0, off + g))
    return pl.pallas_call(
        _attn_kernel,
        grid=grid,
        in_specs=[pl.BlockSpec((1, Q_TILE, GROUP_WIDTH), lambda bi, g, i: (bi, i, g)),
                  kv_spec(qb), kv_spec(qb + kvb), kv_spec(qb + 2 * kvb), kv_spec(qb + 3 * kvb),
                  pl.BlockSpec((1, 1, n_c, HEAD_DIM), lambda bi, g, i: (bi, g, 0, 0)),
                  pl.BlockSpec((1, 1, n_c, HEAD_DIM), lambda bi, g, i: (bi, g, 0, 0)),
                  pl.BlockSpec((1, 1, Q_TILE, 3 * HEADS_PER_GROUP), lambda bi, g, i: (bi, g, i, 0)),
                  pl.BlockSpec((1, Q_TILE, GROUP_WIDTH), lambda bi, g, i: (bi, i, g)),
                  pl.BlockSpec(aggt.shape, lambda bi, g, i: (0, 0)),
                  pl.BlockSpec(blockid.shape, lambda bi, g, i: (0, 0)),
                  pl.BlockSpec((1, Q_TILE, WINDOW + Q_TILE),
                               lambda bi, g, i: (jnp.minimum(i, n_pat), 0, 0))],
        out_specs=pl.BlockSpec((1, Q_TILE, GROUP_WIDTH), lambda bi, g, i: (bi, i, g)),
        out_shape=jax.ShapeDtypeStruct((b, s, ATTN_WIDTH), jnp.bfloat16),
        scratch_shapes=[pltpu.VMEM((s, HEAD_DIM + LANES), jnp.bfloat16),
                        pltpu.VMEM((s, HEAD_DIM + LANES), jnp.bfloat16),
                        pltpu.VMEM((s, HEAD_DIM + LANES), jnp.bfloat16),
                        pltpu.VMEM((3 * HEADS_PER_GROUP, Q_TILE, LANES), jnp.float32),
                        pltpu.VMEM((rows, s), jnp.float32),
                        pltpu.VMEM((rows, LANES), jnp.float32),
                        pltpu.VMEM((rows, HEAD_DIM + LANES), jnp.float32)],
        compiler_params=_params("arbitrary", "arbitrary", "arbitrary"),
        name="nsa_attention",
    )(qkv, qkv, qkv, qkv, qkv, kcb, vcb, gates, zu, aggt, blockid, wbias)


def _merge_kernel(a_ref, c_ref, wa_ref, wc_ref, g0_ref, g1_ref, o_ref, wa_scr, wc_scr):
    @pl.when(pl.program_id(1) == 0)
    def _():
        _cast_weight(wa_scr, wa_ref)
        _cast_weight(wc_scr, wc_ref)

    ya = jnp.dot(a_ref[...], wa_scr[...], preferred_element_type=jnp.float32)
    yc = jnp.dot(c_ref[...], wc_scr[...], preferred_element_type=jnp.float32)
    o_ref[...] = (_sigmoid(g0_ref[...]) * ya + _sigmoid(g1_ref[...]) * yc).astype(o_ref.dtype)


def _merge(a, c, wa, wc, zg, gcol0, tm=1024, tn=512):
    m, k = a.shape
    n = wa.shape[-1]
    g0 = gcol0 // tn
    g1 = (gcol0 + n) // tn
    return pl.pallas_call(
        _merge_kernel,
        grid=(n // tn, m // tm),
        in_specs=[pl.BlockSpec((tm, k), lambda j, i: (i, 0)),
                  pl.BlockSpec((tm, k), lambda j, i: (i, 0)),
                  pl.BlockSpec((None, k, tn), lambda j, i: (0, 0, j)),
                  pl.BlockSpec((None, k, tn), lambda j, i: (0, 0, j)),
                  pl.BlockSpec((tm, tn), lambda j, i: (i, g0 + j)),
                  pl.BlockSpec((tm, tn), lambda j, i: (i, g1 + j))],
        out_specs=pl.BlockSpec((tm, tn), lambda j, i: (i, j)),
        out_shape=jax.ShapeDtypeStruct((m, n), jnp.bfloat16),
        scratch_shapes=[pltpu.VMEM((k, tn), jnp.bfloat16), pltpu.VMEM((k, tn), jnp.bfloat16)],
        compiler_params=_params("arbitrary", "arbitrary"),
        name="merge",
    )(a, c, wa, wc, zg, zg)


def _out_kernel(y_ref, w_ref, x_ref, g_ref, o_ref, w_scr):
    @pl.when(pl.program_id(0) == 0)
    def _():
        _cast_weight(w_scr, w_ref)

    r = x_ref[...] + jnp.dot(y_ref[...], w_scr[...], preferred_element_type=jnp.float32)
    ms = jnp.mean(r * r, axis=-1, keepdims=True)
    o_ref[...] = r * lax.rsqrt(ms + RMS_EPS) * g_ref[...]


def _out_proj(y, w, x2d, g, tm=512):
    m, k = y.shape
    n = w.shape[-1]
    return pl.pallas_call(
        _out_kernel,
        grid=(m // tm,),
        in_specs=[pl.BlockSpec((tm, k), lambda i: (i, 0)),
                  pl.BlockSpec((None, k, n), lambda i: (0, 0, 0), pipeline_mode=pl.Buffered(1)),
                  pl.BlockSpec((tm, n), lambda i: (i, 0)),
                  pl.BlockSpec((1, n), lambda i: (0, 0))],
        out_specs=pl.BlockSpec((tm, n), lambda i: (i, 0)),
        out_shape=jax.ShapeDtypeStruct((m, n), jnp.float32),
        scratch_shapes=[pltpu.VMEM((k, n), jnp.bfloat16)],
        compiler_params=_params("arbitrary"),
        name="out_proj",
    )(y, w, x2d, g.reshape(1, n))


def _attention_constants(s):
    n_slc = s // SEL_BLOCK
    n_cmp = (s - CMP_BLOCK) // CMP_STRIDE + 1
    c0 = np.arange(s // CMP_STRIDE)[None, :] * CMP_STRIDE
    s0 = np.arange(n_slc)[:, None] * SEL_BLOCK
    overlap = np.maximum(0, np.minimum(c0 + CMP_BLOCK, s0 + SEL_BLOCK) - np.maximum(c0, s0))
    aggt = (overlap / CMP_BLOCK) * (np.arange(s // CMP_STRIDE)[None, :] < n_cmp)
    blockid = (np.arange(s)[:, None] // SEL_BLOCK == np.arange(LANES)[None, :]).astype(np.float32)
    n_pat = WINDOW // Q_TILE
    pats = []
    for p in range(n_pat + 1):
        t = p * Q_TILE + np.arange(Q_TILE)[:, None]
        kpos = max(p * Q_TILE - WINDOW, 0) + np.arange(WINDOW + Q_TILE)[None, :]
        pats.append(np.where((kpos <= t) & (kpos > t - WINDOW), 0.0, MASKED))
    return (jnp.asarray(aggt, jnp.bfloat16), jnp.asarray(blockid, jnp.bfloat16),
            jnp.asarray(np.stack(pats), jnp.float32))


def _layer(x, norm_g, w_in, b_in, pe_k, w1_k, w2_k, pe_v, w1_v, w2_v,
           conv_w, conv_b, p_attn, p_conv, w_o, out_g):
    b, s, d = x.shape
    m = b * s
    bf = jnp.bfloat16
    x2d = x.reshape(m, d)
    tn = 1024

    c_q = ATTN_WIDTH
    c_cmp = c_q + 2 * KV_WIDTH
    c_kv = c_cmp + 4 * KV_WIDTH
    n_gate = 3 * N_HEADS
    c_z = c_kv + n_gate
    c_conv = c_z + ATTN_WIDTH
    c_merge = c_conv + 4 * CONV_WIDTH
    assert c_cmp - c_q == tn and c_q % tn == 0 and (c_kv - c_cmp) % tn == 0
    wt = jnp.swapaxes(w_in, 1, 2).reshape(w_in.shape[2], w_in.shape[1])

    hn = _rmsnorm(x2d, norm_g, bf)
    q_tiles = c_q // tn
    qkv = _proj(hn, wt, jnp.concatenate([b_in[:c_q], b_in[c_cmp:c_kv]]),
                lambda j: jnp.where(j < q_tiles, j * tn, c_cmp + (j - q_tiles) * tn),
                q_tiles + (c_kv - c_cmp) // tn, tn, bf,
                scaled_tiles=q_tiles, scale=SCALE * math.log2(math.e), name="proj_qkv").reshape(b, s, -1)
    kvc = _proj(hn, wt, b_in[c_q:c_cmp], lambda j: c_q + j * tn, 1, tn, jnp.float32,
                name="proj_cmp").reshape(b, s, -1)
    gate = _proj(hn, wt, b_in[c_kv:c_kv + LANES], lambda j: c_kv + j * LANES, 1, LANES, jnp.float32,
                 act="sigmoid", name="proj_gate")
    z_tiles = ATTN_WIDTH // tn
    zg = _proj(hn, wt, jnp.concatenate([b_in[c_z:c_conv], b_in[c_merge:]]),
               lambda j: jnp.where(j < z_tiles, c_z + j * tn, c_merge + (j - z_tiles) * tn),
               z_tiles + 2 * D_MODEL // tn, tn, jnp.float32, name="proj_zg").reshape(b, s, -1)
    cw = 256
    b_conv = b_in[c_conv:c_merge].reshape(4, CONV_WIDTH // cw, cw).transpose(1, 0, 2).reshape(1, -1)
    conv_starts = [lambda j, p=p: c_conv + p * CONV_WIDTH + j * cw for p in range(4)]
    c = _proj_conv(hn, wt, b_conv, conv_w, conv_b, conv_starts, s, cw=cw)

    kcb = _compress(kvc, 0, pe_k, w1_k.astype(bf).reshape(CMP_BLOCK, HEAD_DIM, CMP_HIDDEN), w2_k.astype(bf))
    vcb = _compress(kvc, N_KV_GROUPS, pe_v, w1_v.astype(bf).reshape(CMP_BLOCK, HEAD_DIM, CMP_HIDDEN),
                    w2_v.astype(bf))

    gates = gate[:, :n_gate].reshape(b, s, 3, N_KV_GROUPS, HEADS_PER_GROUP)
    gates = gates.transpose(0, 3, 1, 2, 4).reshape(b, N_KV_GROUPS, s, 3 * HEADS_PER_GROUP)

    aggt, blockid, wbias = _attention_constants(s)
    a = _attention(qkv, kcb, vcb, gates, zg, aggt, blockid, wbias)

    y = _merge(a.reshape(m, -1), c, p_attn, p_conv, zg.reshape(m, -1), ATTN_WIDTH)
    return _out_proj(y, w_o, x2d, out_g).reshape(b, s, d)


def kernel(x, norm_g, w_in, b_in, cmp_pe_k, cmp_w1_k, cmp_w2_k, cmp_pe_v, cmp_w1_v, cmp_w2_v,
           conv_w, conv_b, p_attn, p_conv, w_o, final_g):
    assert norm_g.shape[0] == 1, "single-layer block"
    return _layer(x, norm_g[0], w_in, b_in[0], cmp_pe_k[0], cmp_w1_k[0], cmp_w2_k[0],
                  cmp_pe_v[0], cmp_w1_v[0], cmp_w2_v[0], conv_w[0], conv_b[0],
                  p_attn, p_conv, w_o, final_g)
```

```python
import functools
import math

import numpy as np
import jax
import jax.numpy as jnp
from jax import lax
from jax.experimental import pallas as pl
from jax.experimental.pallas import tpu as pltpu

D_MODEL = 2048
N_HEADS = 16
HEAD_DIM = 128
N_KV_GROUPS = 4
HEADS_PER_GROUP = N_HEADS // N_KV_GROUPS
ATTN_WIDTH = N_HEADS * HEAD_DIM
KV_WIDTH = N_KV_GROUPS * HEAD_DIM
GROUP_WIDTH = HEADS_PER_GROUP * HEAD_DIM
CMP_BLOCK = 32
CMP_STRIDE = 16
CMP_HIDDEN = 256
SEL_BLOCK = 64
SEL_TOP_N = 8
WINDOW = 512
SCALE = HEAD_DIM ** -0.5
CONV_WIDTH = D_MODEL
CONV_K = 3
RMS_EPS = 1e-6
FORCED_SCORE = 1e4
MASKED = -(2.0 ** 100)

LANES = 128
SUBLANES = 8
Q_TILE = 256
SEL_CHUNK = 256
GROUPS_PER_STEP = 2
W_ROWS = 512
W_ALIGN = 16
VMEM_LIMIT = 56 * 1024 * 1024

_NT = (((1,), (1,)), ((), ()))


def _params(*sem):
    return pltpu.CompilerParams(dimension_semantics=sem, vmem_limit_bytes=VMEM_LIMIT)


def _sigmoid(x):
    return 1.0 / (1.0 + jnp.exp(-x))


def _silu(x):
    return x * _sigmoid(x)


def _rmsnorm_kernel(x_ref, g_ref, o_ref):
    x = x_ref[...]
    ms = jnp.mean(x * x, axis=-1, keepdims=True)
    o_ref[...] = (x * lax.rsqrt(ms + RMS_EPS) * g_ref[...]).astype(o_ref.dtype)


def _rmsnorm(x2d, g, out_dtype, tm=512):
    m, d = x2d.shape
    return pl.pallas_call(
        _rmsnorm_kernel,
        grid=(m // tm,),
        in_specs=[pl.BlockSpec((tm, d), lambda i: (i, 0)),
                  pl.BlockSpec((1, d), lambda i: (0, 0))],
        out_specs=pl.BlockSpec((tm, d), lambda i: (i, 0)),
        out_shape=jax.ShapeDtypeStruct((m, d), out_dtype),
        compiler_params=_params("parallel"),
        name="rmsnorm",
    )(x2d, g.reshape(1, d))


def _cast_weight(w_scr, w_ref, row0=0):
    rows = w_ref.shape[0]
    for r in range(0, rows, W_ROWS):
        n = min(W_ROWS, rows - r)
        w_scr[row0 + r:row0 + r + n, :] = w_ref[r:r + n, :].astype(w_scr.dtype)


def _weight_specs(k, piece_rows, start_fns):
    return [pl.BlockSpec((pl.Element(piece_rows), pl.Element(k)),
                         lambda j, i, f=f: (pl.multiple_of(f(j), W_ALIGN), 0))
            for f in start_fns]


def _proj_kernel(a_ref, b_ref, w_ref, o_ref, w_scr, *, act, scaled_tiles, scale):
    @pl.when(pl.program_id(1) == 0)
    def _():
        _cast_weight(w_scr, w_ref)

    r = lax.dot_general(a_ref[...], w_scr[...], _NT, preferred_element_type=jnp.float32) + b_ref[...]
    if scaled_tiles:
        r = r * jnp.where(pl.program_id(0) < scaled_tiles, scale, 1.0)
    if act == "sigmoid":
        r = _sigmoid(r)
    o_ref[...] = r.astype(o_ref.dtype)


def _proj(a, wt, bias, start_fn, n_tiles, tn, out_dtype, tm=1024, act=None,
          scaled_tiles=0, scale=1.0, name="proj"):
    m, k = a.shape
    return pl.pallas_call(
        functools.partial(_proj_kernel, act=act, scaled_tiles=scaled_tiles, scale=scale),
        grid=(n_tiles, m // tm),
        in_specs=[pl.BlockSpec((tm, k), lambda j, i: (i, 0)),
                  pl.BlockSpec((1, tn), lambda j, i: (0, j))] + _weight_specs(k, tn, [start_fn]),
        out_specs=pl.BlockSpec((tm, tn), lambda j, i: (i, j)),
        out_shape=jax.ShapeDtypeStruct((m, n_tiles * tn), out_dtype),
        scratch_shapes=[pltpu.VMEM((tn, k), jnp.bfloat16)],
        compiler_params=_params("arbitrary", "arbitrary"),
        name=name,
    )(a, bias.reshape(1, n_tiles * tn), wt)


def _proj_conv_kernel(a_ref, b_ref, cw_ref, cb_ref, wu_ref, wcc_ref, wcb_ref, wz_ref,
                      o_ref, w_scr, halo_scr, *, seq):
    tm, cw = o_ref.shape
    i = pl.program_id(1)

    @pl.when(i == 0)
    def _():
        for p, w_ref in enumerate((wu_ref, wcc_ref, wcb_ref, wz_ref)):
            _cast_weight(w_scr, w_ref, p * cw)

    @pl.when((i * tm) % seq == 0)
    def _():
        halo_scr[...] = jnp.zeros(halo_scr.shape, jnp.float32)

    r = lax.dot_general(a_ref[...], w_scr[...], _NT, preferred_element_type=jnp.float32) + b_ref[...]
    u, cc, cb, z = (r[:, p * cw:(p + 1) * cw] for p in range(4))
    v = cc * u
    prev = halo_scr[...]
    halo_scr[...] = v[tm - SUBLANES:, :]
    row = lax.broadcasted_iota(jnp.int32, (tm, 1), 0)
    y = cw_ref[CONV_K - 1:CONV_K, :] * v
    for d in range(1, CONV_K):
        shifted = pltpu.roll(v, d, axis=0)
        for t in range(d):
            shifted = jnp.where(row == t, prev[SUBLANES - d + t:SUBLANES - d + t + 1, :], shifted)
        y = y + cw_ref[CONV_K - 1 - d:CONV_K - d, :] * shifted
    o_ref[...] = (cb * (y + cb_ref[...]) * _silu(z)).astype(o_ref.dtype)


def _proj_conv(a, wt, bias_tiles, conv_w, conv_b, start_fns, seq, tm=1024, cw=256):
    m, k = a.shape
    n_tiles = CONV_WIDTH // cw
    assert seq % tm == 0 and CONV_K - 1 <= SUBLANES and len(start_fns) == 4
    return pl.pallas_call(
        functools.partial(_proj_conv_kernel, seq=seq),
        grid=(n_tiles, m // tm),
        in_specs=[pl.BlockSpec((tm, k), lambda j, i: (i, 0)),
                  pl.BlockSpec((1, 4 * cw), lambda j, i: (0, j)),
                  pl.BlockSpec((CONV_K, cw), lambda j, i: (0, j)),
                  pl.BlockSpec((1, cw), lambda j, i: (0, j))] + _weight_specs(k, cw, start_fns),
        out_specs=pl.BlockSpec((tm, cw), lambda j, i: (i, j)),
        out_shape=jax.ShapeDtypeStruct((m, CONV_WIDTH), jnp.bfloat16),
        scratch_shapes=[pltpu.VMEM((4 * cw, k), jnp.bfloat16),
                        pltpu.VMEM((SUBLANES, cw), jnp.float32)],
        compiler_params=_params("arbitrary", "arbitrary"),
        name="proj_conv",
    )(a, bias_tiles, conv_w, conv_b.reshape(1, CONV_WIDTH), wt, wt, wt, wt)


def _compress_kernel(x_ref, pe_ref, w1_ref, w2_ref, o_ref):
    half = CMP_BLOCK // 2
    n_chunks = x_ref.shape[1] // CMP_STRIDE
    top = jnp.zeros((n_chunks, CMP_HIDDEN), jnp.float32)
    bot = jnp.zeros((n_chunks, CMP_HIDDEN), jnp.float32)
    for l in range(half):
        xl = x_ref[0, pl.ds(l, n_chunks, stride=CMP_STRIDE), :]
        a = (xl + pe_ref[l:l + 1, :]).astype(jnp.bfloat16)
        c = (xl + pe_ref[half + l:half + l + 1, :]).astype(jnp.bfloat16)
        top = top + jnp.dot(a, w1_ref[l], preferred_element_type=jnp.float32)
        bot = bot + jnp.dot(c, w1_ref[half + l], preferred_element_type=jnp.float32)
    h = top + pltpu.roll(bot, n_chunks - 1, axis=0)
    h = _silu(h).astype(jnp.bfloat16)
    o_ref[0, 0] = jnp.dot(h, w2_ref[...], preferred_element_type=jnp.float32).astype(o_ref.dtype)


def _compress(kv, col_block0, pe, w1, w2):
    b, s, _ = kv.shape
    return pl.pallas_call(
        _compress_kernel,
        grid=(b, N_KV_GROUPS),
        in_specs=[pl.BlockSpec((1, s, HEAD_DIM), lambda bi, g: (bi, 0, col_block0 + g)),
                  pl.BlockSpec((CMP_BLOCK, HEAD_DIM), lambda bi, g: (0, 0)),
                  pl.BlockSpec((CMP_BLOCK, HEAD_DIM, CMP_HIDDEN), lambda bi, g: (0, 0, 0)),
                  pl.BlockSpec((CMP_HIDDEN, HEAD_DIM), lambda bi, g: (0, 0))],
        out_specs=pl.BlockSpec((1, 1, s // CMP_STRIDE, HEAD_DIM), lambda bi, g: (bi, g, 0, 0)),
        out_shape=jax.ShapeDtypeStruct((b, N_KV_GROUPS, s // CMP_STRIDE, HEAD_DIM), jnp.bfloat16),
        compiler_params=_params("parallel", "parallel"),
        name="compress",
    )(kv, pe, w1, w2)


def _attn_kernel(q_ref, ks_ref, vs_ref, kw_ref, vw_ref, kc_ref, vc_ref, gate_ref, z_ref,
                 aggt_ref, blockid_ref, wbias_ref, o_ref,
                 kaug_scr, vsaug_scr, vwaug_scr, gexp_scr, s_scr, m_scr, acc_scr):
    tq = Q_TILE
    hpg = HEADS_PER_GROUP
    gw = GROUP_WIDTH
    seq = kw_ref.shape[1]
    gps = kc_ref.shape[1]
    i = pl.program_id(2)
    t0 = i * tq

    @pl.when(i == 0)
    def _():
        ones = jnp.ones((seq, LANES), jnp.bfloat16)
        for gi in range(gps):
            cols = slice(gi * HEAD_DIM, (gi + 1) * HEAD_DIM)
            kaug_scr[gi, :, :HEAD_DIM] = ks_ref[0, :, cols]
            kaug_scr[gi, :, HEAD_DIM:] = blockid_ref[...]
            vsaug_scr[gi, :, :HEAD_DIM] = vs_ref[0, :, cols]
            vsaug_scr[gi, :, HEAD_DIM:] = ones
            vwaug_scr[gi, :, :HEAD_DIM] = vw_ref[0, :, cols]
            vwaug_scr[gi, :, HEAD_DIM:] = ones

    t_q = t0 + lax.broadcasted_iota(jnp.int32, (tq, 1), 0)
    t_s = jnp.concatenate([t_q] * hpg, axis=0)
    lane = lax.broadcasted_iota(jnp.int32, (1, LANES), 1)
    n_cmp = (seq - CMP_BLOCK) // CMP_STRIDE + 1
    n_slc = seq // SEL_BLOCK
    n_win = WINDOW + tq
    start = pl.multiple_of(jnp.maximum(t0 - WINDOW, 0), tq)
    w_bias = jnp.concatenate([wbias_ref[0]] * hpg, axis=0)
    vis_c = (lane * CMP_STRIDE + (CMP_BLOCK - 1) <= t_s) & (lane < n_cmp)
    sees_c = t_s >= CMP_BLOCK - 1
    aggt = aggt_ref[...]
    blk = lax.broadcasted_iota(jnp.int32, (n_slc, 1), 0)
    t_l = t0 + lax.broadcasted_iota(jnp.int32, (1, tq), 1)
    cur = t_l >> int(math.log2(SEL_BLOCK))
    causal = blk * SEL_BLOCK <= t_l
    forced = ((blk == 0) | (blk == cur) | (blk == cur - 1)) & causal

    def first_region(gi):
        gate = gate_ref[0, gi]
        for c in range(3 * hpg):
            gexp_scr[gi, c] = jnp.broadcast_to(gate[:, c:c + 1], (tq, LANES))

        q = q_ref[0, :, gi * gw:(gi + 1) * gw]
        qs = jnp.concatenate([q[:, h * HEAD_DIM:(h + 1) * HEAD_DIM] for h in range(hpg)], axis=0)

        kw = kw_ref[0, pl.ds(start, n_win), gi * HEAD_DIM:(gi + 1) * HEAD_DIM]
        sw = lax.dot_general(qs, kw, _NT, preferred_element_type=jnp.float32) + w_bias
        p_w = jnp.exp2(sw - jnp.max(sw, axis=-1, keepdims=True))
        pv_w = jnp.dot(p_w.astype(jnp.bfloat16), vwaug_scr[gi, pl.ds(start, n_win), :],
                       preferred_element_type=jnp.float32)
        o_win = pv_w[:, :HEAD_DIM] * (1.0 / pv_w[:, HEAD_DIM:])

        sc = lax.dot_general(qs, kc_ref[0, gi], _NT, preferred_element_type=jnp.float32)
        sc = jnp.where(vis_c, sc, MASKED)
        p_c = jnp.exp2(sc - jnp.max(sc, axis=-1, keepdims=True))
        l_c = jnp.sum(p_c, axis=-1, keepdims=True)
        p_c = p_c * jnp.where(sees_c, 1.0 / l_c, 0.0)
        o_cmp = jnp.dot(p_c.astype(jnp.bfloat16), vc_ref[0, gi], preferred_element_type=jnp.float32)

        p_sum = p_c[0:tq]
        for h in range(1, hpg):
            p_sum = p_sum + p_c[h * tq:(h + 1) * tq]
        hi = p_sum.astype(jnp.bfloat16)
        r1 = p_sum - hi.astype(jnp.float32)
        mid = r1.astype(jnp.bfloat16)
        lo = (r1 - mid.astype(jnp.float32)).astype(jnp.bfloat16)
        imp = (lax.dot_general(aggt, hi, _NT, preferred_element_type=jnp.float32)
               + lax.dot_general(aggt, mid, _NT, preferred_element_type=jnp.float32)
               + lax.dot_general(aggt, lo, _NT, preferred_element_type=jnp.float32))
        imp = jnp.where(forced, FORCED_SCORE, imp)
        imp = jnp.where(causal, imp, -1.0)

        bias_slabs = []
        for v in range(n_slc // SUBLANES):
            lo_row = v * SUBLANES
            slab = imp[lo_row:lo_row + SUBLANES]
            sub = lo_row + lax.broadcasted_iota(jnp.int32, (SUBLANES, 1), 0)
            rank = jnp.zeros((SUBLANES, tq), jnp.float32)
            for jp in range(n_slc):
                other = jnp.broadcast_to(imp[jp:jp + 1], (SUBLANES, tq))
                if jp < lo_row:
                    ahead = jnp.where(other >= slab, 1.0, 0.0)
                elif jp >= lo_row + SUBLANES:
                    ahead = jnp.where(other > slab, 1.0, 0.0)
                else:
                    tie = jnp.where(sub > jp, 1.0, 0.0)
                    ahead = jnp.where(other > slab, 1.0, jnp.where(other == slab, tie, 0.0))
                rank = rank + ahead
            bias_slabs.append(jnp.where(rank < float(SEL_TOP_N), 0.0, MASKED))
        bias_t = jnp.concatenate(bias_slabs + [jnp.zeros((LANES - n_slc, tq), jnp.float32)], axis=0)
        sel_bias = bias_t.T.astype(jnp.bfloat16)
        qa = jnp.concatenate([qs, jnp.concatenate([sel_bias] * hpg, axis=0)], axis=1)
        return o_win, o_cmp, qa

    first = [first_region(gi) for gi in range(gps)]

    n_full = t0 // SEL_CHUNK
    groups = [g for g in (4, 2, 1) if g <= max((seq - tq) // SEL_CHUNK, 1)]

    def scores(gi, c):
        base = pl.multiple_of(c * SEL_CHUNK, SEL_CHUNK)
        return base, lax.dot_general(first[gi][2], kaug_scr[gi, pl.ds(base, SEL_CHUNK), :], _NT,
                                     preferred_element_type=jnp.float32)

    def keep(gi, base, s, m):
        s_scr[gi, :, pl.ds(base, SEL_CHUNK)] = s
        for k in range(SEL_CHUNK // LANES):
            m = jnp.maximum(m, s[:, k * LANES:(k + 1) * LANES])
        return m

    for gi in range(gps):
        base_d, s_d = scores(gi, n_full)
        kpos_d = base_d + lax.broadcasted_iota(jnp.int32, (1, SEL_CHUNK), 1)
        m_scr[gi] = keep(gi, base_d, jnp.where(kpos_d <= t_s, s_d, MASKED),
                         jnp.full(m_scr.shape[1:], MASKED, jnp.float32))
    done = 0
    for g in groups:
        @pl.when((n_full & g) != 0)
        def _(done=done, g=g):
            for gi in range(gps):
                m = m_scr[gi]
                for u in range(g):
                    m = keep(gi, *scores(gi, done + u), m)
                m_scr[gi] = m
        done = done + (n_full & g)
    for gi in range(gps):
        m_scr[gi] = jnp.broadcast_to(jnp.max(m_scr[gi], axis=-1, keepdims=True), m_scr.shape[1:])

    def values(gi, c, m):
        base = pl.multiple_of(c * SEL_CHUNK, SEL_CHUNK)
        ps = [jnp.exp2(s_scr[gi, :, pl.ds(base + k * LANES, LANES)] - m).astype(jnp.bfloat16)
              for k in range(SEL_CHUNK // LANES)]
        return jnp.dot(jnp.concatenate(ps, axis=1), vsaug_scr[gi, pl.ds(base, SEL_CHUNK), :],
                       preferred_element_type=jnp.float32)

    for gi in range(gps):
        acc_scr[gi] = values(gi, n_full, m_scr[gi])
    done = 0
    for g in groups:
        @pl.when((n_full & g) != 0)
        def _(done=done, g=g):
            for gi in range(gps):
                m, acc = m_scr[gi], acc_scr[gi]
                for u in range(g):
                    acc = acc + values(gi, done + u, m)
                acc_scr[gi] = acc
        done = done + (n_full & g)

    for gi in range(gps):
        o_win, o_cmp, _ = first[gi]
        o_sel = acc_scr[gi, :, :HEAD_DIM] * (1.0 / acc_scr[gi, :, HEAD_DIM:])
        outs = []
        for h in range(hpg):
            r0 = h * tq
            outs.append(gexp_scr[gi, h] * o_cmp[r0:r0 + tq]
                        + gexp_scr[gi, hpg + h] * o_sel[r0:r0 + tq]
                        + gexp_scr[gi, 2 * hpg + h] * o_win[r0:r0 + tq])
        o = jnp.concatenate(outs, axis=1)
        cols = slice(gi * gw, (gi + 1) * gw)
        o_ref[0, :, cols] = (o * _silu(z_ref[0, :, cols])).astype(o_ref.dtype)


def _attention(qkv, kcb, vcb, gates, zu, aggt, blockid, wbias):
    b, s, _ = qkv.shape
    gps = GROUPS_PER_STEP
    assert SEL_CHUNK % Q_TILE == 0 and WINDOW % Q_TILE == 0 and s % SEL_CHUNK == 0
    assert N_KV_GROUPS % gps == 0
    kvw = gps * HEAD_DIM
    qb = ATTN_WIDTH // kvw
    kvb = KV_WIDTH // kvw
    n_c = s // CMP_STRIDE
    rows = HEADS_PER_GROUP * Q_TILE
    n_pat = WINDOW // Q_TILE
    grid = (b, N_KV_GROUPS // gps, s // Q_TILE)
    kv_spec = lambda off: pl.BlockSpec((1, s, kvw), lambda bi, g, i: (bi, 0, off + g))
    wide = pl.BlockSpec((1, Q_TILE, gps * GROUP_WIDTH), lambda bi, g, i: (bi, i, g))
    return pl.pallas_call(
        _attn_kernel,
        grid=grid,
        in_specs=[wide,
                  kv_spec(qb), kv_spec(qb + kvb), kv_spec(qb + 2 * kvb), kv_spec(qb + 3 * kvb),
                  pl.BlockSpec((1, gps, n_c, HEAD_DIM), lambda bi, g, i: (bi, g, 0, 0)),
                  pl.BlockSpec((1, gps, n_c, HEAD_DIM), lambda bi, g, i: (bi, g, 0, 0)),
                  pl.BlockSpec((1, gps, Q_TILE, 3 * HEADS_PER_GROUP), lambda bi, g, i: (bi, g, i, 0)),
                  wide,
                  pl.BlockSpec(aggt.shape, lambda bi, g, i: (0, 0)),
                  pl.BlockSpec(blockid.shape, lambda bi, g, i: (0, 0)),
                  pl.BlockSpec((1, Q_TILE, WINDOW + Q_TILE),
                               lambda bi, g, i: (jnp.minimum(i, n_pat), 0, 0))],
        out_specs=wide,
        out_shape=jax.ShapeDtypeStruct((b, s, ATTN_WIDTH), jnp.bfloat16),
        scratch_shapes=[pltpu.VMEM((gps, s, HEAD_DIM + LANES), jnp.bfloat16),
                        pltpu.VMEM((gps, s, HEAD_DIM + LANES), jnp.bfloat16),
                        pltpu.VMEM((gps, s, HEAD_DIM + LANES), jnp.bfloat16),
                        pltpu.VMEM((gps, 3 * HEADS_PER_GROUP, Q_TILE, LANES), jnp.float32),
                        pltpu.VMEM((gps, rows, s), jnp.float32),
                        pltpu.VMEM((gps, rows, LANES), jnp.float32),
                        pltpu.VMEM((gps, rows, HEAD_DIM + LANES), jnp.float32)],
        compiler_params=_params("arbitrary", "arbitrary", "arbitrary"),
        name="nsa_attention",
    )(qkv, qkv, qkv, qkv, qkv, kcb, vcb, gates, zu, aggt, blockid, wbias)


def _merge_kernel(a_ref, c_ref, wa_ref, wc_ref, g0_ref, g1_ref, o_ref, wa_scr, wc_scr):
    @pl.when(pl.program_id(1) == 0)
    def _():
        _cast_weight(wa_scr, wa_ref)
        _cast_weight(wc_scr, wc_ref)

    ya = jnp.dot(a_ref[...], wa_scr[...], preferred_element_type=jnp.float32)
    yc = jnp.dot(c_ref[...], wc_scr[...], preferred_element_type=jnp.float32)
    o_ref[...] = (_sigmoid(g0_ref[...]) * ya + _sigmoid(g1_ref[...]) * yc).astype(o_ref.dtype)


def _merge(a, c, wa, wc, zg, gcol0, tm=1024, tn=512):
    m, k = a.shape
    n = wa.shape[-1]
    g0 = gcol0 // tn
    g1 = (gcol0 + n) // tn
    return pl.pallas_call(
        _merge_kernel,
        grid=(n // tn, m // tm),
        in_specs=[pl.BlockSpec((tm, k), lambda j, i: (i, 0)),
                  pl.BlockSpec((tm, k), lambda j, i: (i, 0)),
                  pl.BlockSpec((None, k, tn), lambda j, i: (0, 0, j)),
                  pl.BlockSpec((None, k, tn), lambda j, i: (0, 0, j)),
                  pl.BlockSpec((tm, tn), lambda j, i: (i, g0 + j)),
                  pl.BlockSpec((tm, tn), lambda j, i: (i, g1 + j))],
        out_specs=pl.BlockSpec((tm, tn), lambda j, i: (i, j)),
        out_shape=jax.ShapeDtypeStruct((m, n), jnp.bfloat16),
        scratch_shapes=[pltpu.VMEM((k, tn), jnp.bfloat16), pltpu.VMEM((k, tn), jnp.bfloat16)],
        compiler_params=_params("arbitrary", "arbitrary"),
        name="merge",
    )(a, c, wa, wc, zg, zg)


def _out_kernel(y_ref, w_ref, x_ref, g_ref, o_ref, w_scr):
    @pl.when(pl.program_id(0) == 0)
    def _():
        _cast_weight(w_scr, w_ref)

    r = x_ref[...] + jnp.dot(y_ref[...], w_scr[...], preferred_element_type=jnp.float32)
    ms = jnp.mean(r * r, axis=-1, keepdims=True)
    o_ref[...] = r * lax.rsqrt(ms + RMS_EPS) * g_ref[...]


def _out_proj(y, w, x2d, g, tm=512):
    m, k = y.shape
    n = w.shape[-1]
    return pl.pallas_call(
        _out_kernel,
        grid=(m // tm,),
        in_specs=[pl.BlockSpec((tm, k), lambda i: (i, 0)),
                  pl.BlockSpec((None, k, n), lambda i: (0, 0, 0), pipeline_mode=pl.Buffered(1)),
                  pl.BlockSpec((tm, n), lambda i: (i, 0)),
                  pl.BlockSpec((1, n), lambda i: (0, 0))],
        out_specs=pl.BlockSpec((tm, n), lambda i: (i, 0)),
        out_shape=jax.ShapeDtypeStruct((m, n), jnp.float32),
        scratch_shapes=[pltpu.VMEM((k, n), jnp.bfloat16)],
        compiler_params=_params("arbitrary"),
        name="out_proj",
    )(y, w, x2d, g.reshape(1, n))


def _attention_constants(s):
    n_slc = s // SEL_BLOCK
    n_cmp = (s - CMP_BLOCK) // CMP_STRIDE + 1
    c0 = np.arange(s // CMP_STRIDE)[None, :] * CMP_STRIDE
    s0 = np.arange(n_slc)[:, None] * SEL_BLOCK
    overlap = np.maximum(0, np.minimum(c0 + CMP_BLOCK, s0 + SEL_BLOCK) - np.maximum(c0, s0))
    aggt = (overlap / CMP_BLOCK) * (np.arange(s // CMP_STRIDE)[None, :] < n_cmp)
    blockid = (np.arange(s)[:, None] // SEL_BLOCK == np.arange(LANES)[None, :]).astype(np.float32)
    n_pat = WINDOW // Q_TILE
    pats = []
    for p in range(n_pat + 1):
        t = p * Q_TILE + np.arange(Q_TILE)[:, None]
        kpos = max(p * Q_TILE - WINDOW, 0) + np.arange(WINDOW + Q_TILE)[None, :]
        pats.append(np.where((kpos <= t) & (kpos > t - WINDOW), 0.0, MASKED))
    return (jnp.asarray(aggt, jnp.bfloat16), jnp.asarray(blockid, jnp.bfloat16),
            jnp.asarray(np.stack(pats), jnp.float32))


def _layer(x, norm_g, w_in, b_in, pe_k, w1_k, w2_k, pe_v, w1_v, w2_v,
           conv_w, conv_b, p_attn, p_conv, w_o, out_g):
    b, s, d = x.shape
    m = b * s
    bf = jnp.bfloat16
    x2d = x.reshape(m, d)
    tn = 1024

    c_q = ATTN_WIDTH
    c_cmp = c_q + 2 * KV_WIDTH
    c_kv = c_cmp + 4 * KV_WIDTH
    n_gate = 3 * N_HEADS
    c_z = c_kv + n_gate
    c_conv = c_z + ATTN_WIDTH
    c_merge = c_conv + 4 * CONV_WIDTH
    assert c_cmp - c_q == tn and c_q % tn == 0 and (c_kv - c_cmp) % tn == 0
    wt = jnp.swapaxes(w_in, 1, 2).reshape(w_in.shape[2], w_in.shape[1])

    hn = _rmsnorm(x2d, norm_g, bf)
    q_tiles = c_q // tn
    qkv = _proj(hn, wt, jnp.concatenate([b_in[:c_q], b_in[c_cmp:c_kv]]),
                lambda j: jnp.where(j < q_tiles, j * tn, c_cmp + (j - q_tiles) * tn),
                q_tiles + (c_kv - c_cmp) // tn, tn, bf,
                scaled_tiles=q_tiles, scale=SCALE * math.log2(math.e), name="proj_qkv").reshape(b, s, -1)
    kvc = _proj(hn, wt, b_in[c_q:c_cmp], lambda j: c_q + j * tn, 1, tn, jnp.float32,
                name="proj_cmp").reshape(b, s, -1)
    gate = _proj(hn, wt, b_in[c_kv:c_kv + LANES], lambda j: c_kv + j * LANES, 1, LANES, jnp.float32,
                 act="sigmoid", name="proj_gate")
    z_tiles = ATTN_WIDTH // tn
    zg = _proj(hn, wt, jnp.concatenate([b_in[c_z:c_conv], b_in[c_merge:]]),
               lambda j: jnp.where(j < z_tiles, c_z + j * tn, c_merge + (j - z_tiles) * tn),
               z_tiles + 2 * D_MODEL // tn, tn, jnp.float32, name="proj_zg").reshape(b, s, -1)
    cw = 256
    b_conv = b_in[c_conv:c_merge].reshape(4, CONV_WIDTH // cw, cw).transpose(1, 0, 2).reshape(1, -1)
    conv_starts = [lambda j, p=p: c_conv + p * CONV_WIDTH + j * cw for p in range(4)]
    c = _proj_conv(hn, wt, b_conv, conv_w, conv_b, conv_starts, s, cw=cw)

    kcb = _compress(kvc, 0, pe_k, w1_k.astype(bf).reshape(CMP_BLOCK, HEAD_DIM, CMP_HIDDEN), w2_k.astype(bf))
    vcb = _compress(kvc, N_KV_GROUPS, pe_v, w1_v.astype(bf).reshape(CMP_BLOCK, HEAD_DIM, CMP_HIDDEN),
                    w2_v.astype(bf))

    gates = gate[:, :n_gate].reshape(b, s, 3, N_KV_GROUPS, HEADS_PER_GROUP)
    gates = gates.transpose(0, 3, 1, 2, 4).reshape(b, N_KV_GROUPS, s, 3 * HEADS_PER_GROUP)

    aggt, blockid, wbias = _attention_constants(s)
    a = _attention(qkv, kcb, vcb, gates, zg, aggt, blockid, wbias)

    y = _merge(a.reshape(m, -1), c, p_attn, p_conv, zg.reshape(m, -1), ATTN_WIDTH)
    return _out_proj(y, w_o, x2d, out_g).reshape(b, s, d)


def kernel(x, norm_g, w_in, b_in, cmp_pe_k, cmp_w1_k, cmp_w2_k, cmp_pe_v, cmp_w1_v, cmp_w2_v,
           conv_w, conv_b, p_attn, p_conv, w_o, final_g):
    assert norm_g.shape[0] == 1, "single-layer block"
    return _layer(x, norm_g[0], w_in, b_in[0], cmp_pe_k[0], cmp_w1_k[0], cmp_w2_k[0],
                  cmp_pe_v[0], cmp_w1_v[0], cmp_w2_v[0], conv_w[0], conv_b[0],
                  p_attn, p_conv, w_o, final_g)
```

```python
import functools
import math

import numpy as np
import jax
import jax.numpy as jnp
from jax import lax
from jax.experimental import pallas as pl
from jax.experimental.pallas import tpu as pltpu

D_MODEL = 2048
N_HEADS = 16
HEAD_DIM = 128
N_KV_GROUPS = 4
HEADS_PER_GROUP = N_HEADS // N_KV_GROUPS
ATTN_WIDTH = N_HEADS * HEAD_DIM
KV_WIDTH = N_KV_GROUPS * HEAD_DIM
GROUP_WIDTH = HEADS_PER_GROUP * HEAD_DIM
CMP_BLOCK = 32
CMP_STRIDE = 16
CMP_HIDDEN = 256
SEL_BLOCK = 64
SEL_TOP_N = 8
WINDOW = 512
SCALE = HEAD_DIM ** -0.5
CONV_WIDTH = D_MODEL
CONV_K = 3
RMS_EPS = 1e-6
FORCED_SCORE = 1e4
MASKED = -(2.0 ** 100)

LANES = 128
SUBLANES = 8
Q_TILE = 256
SEL_CHUNK = 256
GROUPS_PER_STEP = 2
W_ROWS = 512
W_ALIGN = 16
VMEM_LIMIT = 56 * 1024 * 1024

_NT = (((1,), (1,)), ((), ()))


def _params(*sem):
    return pltpu.CompilerParams(dimension_semantics=sem, vmem_limit_bytes=VMEM_LIMIT)


def _sigmoid(x):
    return 1.0 / (1.0 + jnp.exp(-x))


def _silu(x):
    return x * _sigmoid(x)


def _rmsnorm_kernel(x_ref, g_ref, o_ref):
    x = x_ref[...]
    ms = jnp.mean(x * x, axis=-1, keepdims=True)
    o_ref[...] = (x * lax.rsqrt(ms + RMS_EPS) * g_ref[...]).astype(o_ref.dtype)


def _rmsnorm(x2d, g, out_dtype, tm=512):
    m, d = x2d.shape
    return pl.pallas_call(
        _rmsnorm_kernel,
        grid=(m // tm,),
        in_specs=[pl.BlockSpec((tm, d), lambda i: (i, 0)),
                  pl.BlockSpec((1, d), lambda i: (0, 0))],
        out_specs=pl.BlockSpec((tm, d), lambda i: (i, 0)),
        out_shape=jax.ShapeDtypeStruct((m, d), out_dtype),
        compiler_params=_params("parallel"),
        name="rmsnorm",
    )(x2d, g.reshape(1, d))


def _cast_weight(w_scr, w_ref, row0=0):
    rows = w_ref.shape[0]
    for r in range(0, rows, W_ROWS):
        n = min(W_ROWS, rows - r)
        w_scr[row0 + r:row0 + r + n, :] = w_ref[r:r + n, :].astype(w_scr.dtype)


def _weight_specs(k, piece_rows, start_fns):
    return [pl.BlockSpec((pl.Element(piece_rows), pl.Element(k)),
                         lambda j, i, f=f: (pl.multiple_of(f(j), W_ALIGN), 0))
            for f in start_fns]


def _proj_kernel(a_ref, b_ref, w_ref, o_ref, w_scr, *, act, scaled_tiles, scale):
    @pl.when(pl.program_id(1) == 0)
    def _():
        _cast_weight(w_scr, w_ref)

    r = lax.dot_general(a_ref[...], w_scr[...], _NT, preferred_element_type=jnp.float32) + b_ref[...]
    if scaled_tiles:
        r = r * jnp.where(pl.program_id(0) < scaled_tiles, scale, 1.0)
    if act == "sigmoid":
        r = _sigmoid(r)
    o_ref[...] = r.astype(o_ref.dtype)


def _proj(a, wt, bias, start_fn, n_tiles, tn, out_dtype, tm=2048, act=None,
          scaled_tiles=0, scale=1.0, name="proj"):
    m, k = a.shape
    return pl.pallas_call(
        functools.partial(_proj_kernel, act=act, scaled_tiles=scaled_tiles, scale=scale),
        grid=(n_tiles, m // tm),
        in_specs=[pl.BlockSpec((tm, k), lambda j, i: (i, 0)),
                  pl.BlockSpec((1, tn), lambda j, i: (0, j))] + _weight_specs(k, tn, [start_fn]),
        out_specs=pl.BlockSpec((tm, tn), lambda j, i: (i, j)),
        out_shape=jax.ShapeDtypeStruct((m, n_tiles * tn), out_dtype),
        scratch_shapes=[pltpu.VMEM((tn, k), jnp.bfloat16)],
        compiler_params=_params("arbitrary", "arbitrary"),
        name=name,
    )(a, bias.reshape(1, n_tiles * tn), wt)


def _proj_conv_kernel(a_ref, b_ref, cw_ref, cb_ref, wu_ref, wcc_ref, wcb_ref, wz_ref,
                      o_ref, w_scr, halo_scr, *, seq):
    tm, cw = o_ref.shape
    i = pl.program_id(1)

    @pl.when(i == 0)
    def _():
        for p, w_ref in enumerate((wu_ref, wcc_ref, wcb_ref, wz_ref)):
            _cast_weight(w_scr, w_ref, p * cw)

    @pl.when((i * tm) % seq == 0)
    def _():
        halo_scr[...] = jnp.zeros(halo_scr.shape, jnp.float32)

    r = lax.dot_general(a_ref[...], w_scr[...], _NT, preferred_element_type=jnp.float32) + b_ref[...]
    u, cc, cb, z = (r[:, p * cw:(p + 1) * cw] for p in range(4))
    v = cc * u
    prev = halo_scr[...]
    halo_scr[...] = v[tm - SUBLANES:, :]
    row = lax.broadcasted_iota(jnp.int32, (tm, 1), 0)
    y = cw_ref[CONV_K - 1:CONV_K, :] * v
    for d in range(1, CONV_K):
        shifted = pltpu.roll(v, d, axis=0)
        for t in range(d):
            shifted = jnp.where(row == t, prev[SUBLANES - d + t:SUBLANES - d + t + 1, :], shifted)
        y = y + cw_ref[CONV_K - 1 - d:CONV_K - d, :] * shifted
    o_ref[...] = (cb * (y + cb_ref[...]) * _silu(z)).astype(o_ref.dtype)


def _proj_conv(a, wt, bias_tiles, conv_w, conv_b, start_fns, seq, tm=2048, cw=256):
    m, k = a.shape
    n_tiles = CONV_WIDTH // cw
    assert seq % tm == 0 and CONV_K - 1 <= SUBLANES and len(start_fns) == 4
    return pl.pallas_call(
        functools.partial(_proj_conv_kernel, seq=seq),
        grid=(n_tiles, m // tm),
        in_specs=[pl.BlockSpec((tm, k), lambda j, i: (i, 0)),
                  pl.BlockSpec((1, 4 * cw), lambda j, i: (0, j)),
                  pl.BlockSpec((CONV_K, cw), lambda j, i: (0, j)),
                  pl.BlockSpec((1, cw), lambda j, i: (0, j))] + _weight_specs(k, cw, start_fns),
        out_specs=pl.BlockSpec((tm, cw), lambda j, i: (i, j)),
        out_shape=jax.ShapeDtypeStruct((m, CONV_WIDTH), jnp.bfloat16),
        scratch_shapes=[pltpu.VMEM((4 * cw, k), jnp.bfloat16),
                        pltpu.VMEM((SUBLANES, cw), jnp.float32)],
        compiler_params=_params("arbitrary", "arbitrary"),
        name="proj_conv",
    )(a, bias_tiles, conv_w, conv_b.reshape(1, CONV_WIDTH), wt, wt, wt, wt)


def _compress_kernel(x_ref, pe_ref, w1_ref, w2_ref, o_ref):
    half = CMP_BLOCK // 2
    n_chunks = x_ref.shape[1] // CMP_STRIDE
    top = jnp.zeros((n_chunks, CMP_HIDDEN), jnp.float32)
    bot = jnp.zeros((n_chunks, CMP_HIDDEN), jnp.float32)
    for l in range(half):
        xl = x_ref[0, pl.ds(l, n_chunks, stride=CMP_STRIDE), :]
        a = (xl + pe_ref[l:l + 1, :]).astype(jnp.bfloat16)
        c = (xl + pe_ref[half + l:half + l + 1, :]).astype(jnp.bfloat16)
        top = top + jnp.dot(a, w1_ref[l], preferred_element_type=jnp.float32)
        bot = bot + jnp.dot(c, w1_ref[half + l], preferred_element_type=jnp.float32)
    h = top + pltpu.roll(bot, n_chunks - 1, axis=0)
    h = _silu(h).astype(jnp.bfloat16)
    o_ref[0, 0] = jnp.dot(h, w2_ref[...], preferred_element_type=jnp.float32).astype(o_ref.dtype)


def _compress(kv, col_block0, pe, w1, w2):
    b, s, _ = kv.shape
    return pl.pallas_call(
        _compress_kernel,
        grid=(b, N_KV_GROUPS),
        in_specs=[pl.BlockSpec((1, s, HEAD_DIM), lambda bi, g: (bi, 0, col_block0 + g)),
                  pl.BlockSpec((CMP_BLOCK, HEAD_DIM), lambda bi, g: (0, 0)),
                  pl.BlockSpec((CMP_BLOCK, HEAD_DIM, CMP_HIDDEN), lambda bi, g: (0, 0, 0)),
                  pl.BlockSpec((CMP_HIDDEN, HEAD_DIM), lambda bi, g: (0, 0))],
        out_specs=pl.BlockSpec((1, 1, s // CMP_STRIDE, HEAD_DIM), lambda bi, g: (bi, g, 0, 0)),
        out_shape=jax.ShapeDtypeStruct((b, N_KV_GROUPS, s // CMP_STRIDE, HEAD_DIM), jnp.bfloat16),
        compiler_params=_params("parallel", "parallel"),
        name="compress",
    )(kv, pe, w1, w2)


def _attn_kernel(q_ref, ks_ref, vs_ref, kw_ref, vw_ref, kc_ref, vc_ref, gate_ref, z_ref,
                 aggt_ref, blockid_ref, wbias_ref, o_ref,
                 kaug_scr, vsaug_scr, vwaug_scr, gexp_scr, s_scr, m_scr, acc_scr):
    tq = Q_TILE
    hpg = HEADS_PER_GROUP
    gw = GROUP_WIDTH
    seq = kw_ref.shape[1]
    gps = kc_ref.shape[1]
    i = pl.program_id(2)
    t0 = i * tq

    @pl.when(i == 0)
    def _():
        ones = jnp.ones((seq, LANES), jnp.bfloat16)
        for gi in range(gps):
            cols = slice(gi * HEAD_DIM, (gi + 1) * HEAD_DIM)
            kaug_scr[gi, :, :HEAD_DIM] = ks_ref[0, :, cols]
            kaug_scr[gi, :, HEAD_DIM:] = blockid_ref[...]
            vsaug_scr[gi, :, :HEAD_DIM] = vs_ref[0, :, cols]
            vsaug_scr[gi, :, HEAD_DIM:] = ones
            vwaug_scr[gi, :, :HEAD_DIM] = vw_ref[0, :, cols]
            vwaug_scr[gi, :, HEAD_DIM:] = ones

    t_q = t0 + lax.broadcasted_iota(jnp.int32, (tq, 1), 0)
    t_s = jnp.concatenate([t_q] * hpg, axis=0)
    lane = lax.broadcasted_iota(jnp.int32, (1, LANES), 1)
    n_cmp = (seq - CMP_BLOCK) // CMP_STRIDE + 1
    n_slc = seq // SEL_BLOCK
    n_win = WINDOW + tq
    start = pl.multiple_of(jnp.maximum(t0 - WINDOW, 0), tq)
    w_bias = jnp.concatenate([wbias_ref[0]] * hpg, axis=0)
    vis_c = (lane * CMP_STRIDE + (CMP_BLOCK - 1) <= t_s) & (lane < n_cmp)
    sees_c = t_s >= CMP_BLOCK - 1
    aggt = aggt_ref[...]
    blk = lax.broadcasted_iota(jnp.int32, (n_slc, 1), 0)
    t_l = t0 + lax.broadcasted_iota(jnp.int32, (1, tq), 1)
    cur = t_l >> int(math.log2(SEL_BLOCK))
    causal = blk * SEL_BLOCK <= t_l
    forced = ((blk == 0) | (blk == cur) | (blk == cur - 1)) & causal

    def first_region(gi):
        gate = gate_ref[0, gi]
        for c in range(3 * hpg):
            gexp_scr[gi, c] = jnp.broadcast_to(gate[:, c:c + 1], (tq, LANES))

        q = q_ref[0, :, gi * gw:(gi + 1) * gw]
        qs = jnp.concatenate([q[:, h * HEAD_DIM:(h + 1) * HEAD_DIM] for h in range(hpg)], axis=0)

        kw = kw_ref[0, pl.ds(start, n_win), gi * HEAD_DIM:(gi + 1) * HEAD_DIM]
        sw = lax.dot_general(qs, kw, _NT, preferred_element_type=jnp.float32) + w_bias
        p_w = jnp.exp2(sw - jnp.max(sw, axis=-1, keepdims=True))
        pv_w = jnp.dot(p_w.astype(jnp.bfloat16), vwaug_scr[gi, pl.ds(start, n_win), :],
                       preferred_element_type=jnp.float32)
        o_win = pv_w[:, :HEAD_DIM] * (1.0 / pv_w[:, HEAD_DIM:])

        sc = lax.dot_general(qs, kc_ref[0, gi], _NT, preferred_element_type=jnp.float32)
        sc = jnp.where(vis_c, sc, MASKED)
        p_c = jnp.exp2(sc - jnp.max(sc, axis=-1, keepdims=True))
        l_c = jnp.sum(p_c, axis=-1, keepdims=True)
        p_c = p_c * jnp.where(sees_c, 1.0 / l_c, 0.0)
        o_cmp = jnp.dot(p_c.astype(jnp.bfloat16), vc_ref[0, gi], preferred_element_type=jnp.float32)

        p_sum = p_c[0:tq]
        for h in range(1, hpg):
            p_sum = p_sum + p_c[h * tq:(h + 1) * tq]
        hi = p_sum.astype(jnp.bfloat16)
        r1 = p_sum - hi.astype(jnp.float32)
        mid = r1.astype(jnp.bfloat16)
        lo = (r1 - mid.astype(jnp.float32)).astype(jnp.bfloat16)
        imp = (lax.dot_general(aggt, hi, _NT, preferred_element_type=jnp.float32)
               + lax.dot_general(aggt, mid, _NT, preferred_element_type=jnp.float32)
               + lax.dot_general(aggt, lo, _NT, preferred_element_type=jnp.float32))
        imp = jnp.where(forced, FORCED_SCORE, imp)
        imp = jnp.where(causal, imp, -1.0)

        bias_slabs = []
        for v in range(n_slc // SUBLANES):
            lo_row = v * SUBLANES
            slab = imp[lo_row:lo_row + SUBLANES]
            sub = lo_row + lax.broadcasted_iota(jnp.int32, (SUBLANES, 1), 0)
            rank = jnp.zeros((SUBLANES, tq), jnp.float32)
            for jp in range(n_slc):
                other = jnp.broadcast_to(imp[jp:jp + 1], (SUBLANES, tq))
                if jp < lo_row:
                    ahead = jnp.where(other >= slab, 1.0, 0.0)
                elif jp >= lo_row + SUBLANES:
                    ahead = jnp.where(other > slab, 1.0, 0.0)
                else:
                    tie = jnp.where(sub > jp, 1.0, 0.0)
                    ahead = jnp.where(other > slab, 1.0, jnp.where(other == slab, tie, 0.0))
                rank = rank + ahead
            bias_slabs.append(jnp.where(rank < float(SEL_TOP_N), 0.0, MASKED))
        bias_t = jnp.concatenate(bias_slabs + [jnp.zeros((LANES - n_slc, tq), jnp.float32)], axis=0)
        sel_bias = bias_t.T.astype(jnp.bfloat16)
        qa = jnp.concatenate([qs, jnp.concatenate([sel_bias] * hpg, axis=0)], axis=1)
        return o_win, o_cmp, qa

    first = [first_region(gi) for gi in range(gps)]

    n_full = t0 // SEL_CHUNK
    groups = [g for g in (4, 2, 1) if g <= max((seq - tq) // SEL_CHUNK, 1)]

    def scores(gi, c):
        base = pl.multiple_of(c * SEL_CHUNK, SEL_CHUNK)
        return base, lax.dot_general(first[gi][2], kaug_scr[gi, pl.ds(base, SEL_CHUNK), :], _NT,
                                     preferred_element_type=jnp.float32)

    def keep(gi, base, s, m):
        s_scr[gi, :, pl.ds(base, SEL_CHUNK)] = s
        for k in range(SEL_CHUNK // LANES):
            m = jnp.maximum(m, s[:, k * LANES:(k + 1) * LANES])
        return m

    for gi in range(gps):
        base_d, s_d = scores(gi, n_full)
        kpos_d = base_d + lax.broadcasted_iota(jnp.int32, (1, SEL_CHUNK), 1)
        m_scr[gi] = keep(gi, base_d, jnp.where(kpos_d <= t_s, s_d, MASKED),
                         jnp.full(m_scr.shape[1:], MASKED, jnp.float32))
    done = 0
    for g in groups:
        @pl.when((n_full & g) != 0)
        def _(done=done, g=g):
            for gi in range(gps):
                m = m_scr[gi]
                for u in range(g):
                    m = keep(gi, *scores(gi, done + u), m)
                m_scr[gi] = m
        done = done + (n_full & g)
    for gi in range(gps):
        m_scr[gi] = jnp.broadcast_to(jnp.max(m_scr[gi], axis=-1, keepdims=True), m_scr.shape[1:])

    def values(gi, c, m):
        base = pl.multiple_of(c * SEL_CHUNK, SEL_CHUNK)
        ps = [jnp.exp2(s_scr[gi, :, pl.ds(base + k * LANES, LANES)] - m).astype(jnp.bfloat16)
              for k in range(SEL_CHUNK // LANES)]
        return jnp.dot(jnp.concatenate(ps, axis=1), vsaug_scr[gi, pl.ds(base, SEL_CHUNK), :],
                       preferred_element_type=jnp.float32)

    for gi in range(gps):
        acc_scr[gi] = values(gi, n_full, m_scr[gi])
    done = 0
    for g in groups:
        @pl.when((n_full & g) != 0)
        def _(done=done, g=g):
            for gi in range(gps):
                m, acc = m_scr[gi], acc_scr[gi]
                for u in range(g):
                    acc = acc + values(gi, done + u, m)
                acc_scr[gi] = acc
        done = done + (n_full & g)

    for gi in range(gps):
        o_win, o_cmp, _ = first[gi]
        o_sel = acc_scr[gi, :, :HEAD_DIM] * (1.0 / acc_scr[gi, :, HEAD_DIM:])
        outs = []
        for h in range(hpg):
            r0 = h * tq
            outs.append(gexp_scr[gi, h] * o_cmp[r0:r0 + tq]
                        + gexp_scr[gi, hpg + h] * o_sel[r0:r0 + tq]
                        + gexp_scr[gi, 2 * hpg + h] * o_win[r0:r0 + tq])
        o = jnp.concatenate(outs, axis=1)
        cols = slice(gi * gw, (gi + 1) * gw)
        o_ref[0, :, cols] = (o * _silu(z_ref[0, :, cols])).astype(o_ref.dtype)


def _attention(qkv, kcb, vcb, gates, zu, aggt, blockid, wbias):
    b, s, _ = qkv.shape
    gps = GROUPS_PER_STEP
    assert SEL_CHUNK % Q_TILE == 0 and WINDOW % Q_TILE == 0 and s % SEL_CHUNK == 0
    assert N_KV_GROUPS % gps == 0
    kvw = gps * HEAD_DIM
    qb = ATTN_WIDTH // kvw
    kvb = KV_WIDTH // kvw
    n_c = s // CMP_STRIDE
    rows = HEADS_PER_GROUP * Q_TILE
    n_pat = WINDOW // Q_TILE
    grid = (b, N_KV_GROUPS // gps, s // Q_TILE)
    kv_spec = lambda off: pl.BlockSpec((1, s, kvw), lambda bi, g, i: (bi, 0, off + g))
    wide = pl.BlockSpec((1, Q_TILE, gps * GROUP_WIDTH), lambda bi, g, i: (bi, i, g))
    return pl.pallas_call(
        _attn_kernel,
        grid=grid,
        in_specs=[wide,
                  kv_spec(qb), kv_spec(qb + kvb), kv_spec(qb + 2 * kvb), kv_spec(qb + 3 * kvb),
                  pl.BlockSpec((1, gps, n_c, HEAD_DIM), lambda bi, g, i: (bi, g, 0, 0)),
                  pl.BlockSpec((1, gps, n_c, HEAD_DIM), lambda bi, g, i: (bi, g, 0, 0)),
                  pl.BlockSpec((1, gps, Q_TILE, 3 * HEADS_PER_GROUP), lambda bi, g, i: (bi, g, i, 0)),
                  wide,
                  pl.BlockSpec(aggt.shape, lambda bi, g, i: (0, 0)),
                  pl.BlockSpec(blockid.shape, lambda bi, g, i: (0, 0)),
                  pl.BlockSpec((1, Q_TILE, WINDOW + Q_TILE),
                               lambda bi, g, i: (jnp.minimum(i, n_pat), 0, 0))],
        out_specs=wide,
        out_shape=jax.ShapeDtypeStruct((b, s, ATTN_WIDTH), jnp.bfloat16),
        scratch_shapes=[pltpu.VMEM((gps, s, HEAD_DIM + LANES), jnp.bfloat16),
                        pltpu.VMEM((gps, s, HEAD_DIM + LANES), jnp.bfloat16),
                        pltpu.VMEM((gps, s, HEAD_DIM + LANES), jnp.bfloat16),
                        pltpu.VMEM((gps, 3 * HEADS_PER_GROUP, Q_TILE, LANES), jnp.float32),
                        pltpu.VMEM((gps, rows, s), jnp.float32),
                        pltpu.VMEM((gps, rows, LANES), jnp.float32),
                        pltpu.VMEM((gps, rows, HEAD_DIM + LANES), jnp.float32)],
        compiler_params=_params("arbitrary", "arbitrary", "arbitrary"),
        name="nsa_attention",
    )(qkv, qkv, qkv, qkv, qkv, kcb, vcb, gates, zu, aggt, blockid, wbias)


def _merge_kernel(a_ref, c_ref, wa_ref, wc_ref, g0_ref, g1_ref, o_ref, wa_scr, wc_scr):
    @pl.when(pl.program_id(1) == 0)
    def _():
        _cast_weight(wa_scr, wa_ref)
        _cast_weight(wc_scr, wc_ref)

    ya = jnp.dot(a_ref[...], wa_scr[...], preferred_element_type=jnp.float32)
    yc = jnp.dot(c_ref[...], wc_scr[...], preferred_element_type=jnp.float32)
    o_ref[...] = (_sigmoid(g0_ref[...]) * ya + _sigmoid(g1_ref[...]) * yc).astype(o_ref.dtype)


def _merge(a, c, wa, wc, zg, gcol0, tm=1024, tn=512):
    m, k = a.shape
    n = wa.shape[-1]
    g0 = gcol0 // tn
    g1 = (gcol0 + n) // tn
    return pl.pallas_call(
        _merge_kernel,
        grid=(n // tn, m // tm),
        in_specs=[pl.BlockSpec((tm, k), lambda j, i: (i, 0)),
                  pl.BlockSpec((tm, k), lambda j, i: (i, 0)),
                  pl.BlockSpec((None, k, tn), lambda j, i: (0, 0, j)),
                  pl.BlockSpec((None, k, tn), lambda j, i: (0, 0, j)),
                  pl.BlockSpec((tm, tn), lambda j, i: (i, g0 + j)),
                  pl.BlockSpec((tm, tn), lambda j, i: (i, g1 + j))],
        out_specs=pl.BlockSpec((tm, tn), lambda j, i: (i, j)),
        out_shape=jax.ShapeDtypeStruct((m, n), jnp.bfloat16),
        scratch_shapes=[pltpu.VMEM((k, tn), jnp.bfloat16), pltpu.VMEM((k, tn), jnp.bfloat16)],
        compiler_params=_params("arbitrary", "arbitrary"),
        name="merge",
    )(a, c, wa, wc, zg, zg)


def _out_kernel(y_ref, w_ref, x_ref, g_ref, o_ref, w_scr):
    @pl.when(pl.program_id(0) == 0)
    def _():
        _cast_weight(w_scr, w_ref)

    r = x_ref[...] + jnp.dot(y_ref[...], w_scr[...], preferred_element_type=jnp.float32)
    ms = jnp.mean(r * r, axis=-1, keepdims=True)
    o_ref[...] = r * lax.rsqrt(ms + RMS_EPS) * g_ref[...]


def _out_proj(y, w, x2d, g, tm=512):
    m, k = y.shape
    n = w.shape[-1]
    return pl.pallas_call(
        _out_kernel,
        grid=(m // tm,),
        in_specs=[pl.BlockSpec((tm, k), lambda i: (i, 0)),
                  pl.BlockSpec((None, k, n), lambda i: (0, 0, 0), pipeline_mode=pl.Buffered(1)),
                  pl.BlockSpec((tm, n), lambda i: (i, 0)),
                  pl.BlockSpec((1, n), lambda i: (0, 0))],
        out_specs=pl.BlockSpec((tm, n), lambda i: (i, 0)),
        out_shape=jax.ShapeDtypeStruct((m, n), jnp.float32),
        scratch_shapes=[pltpu.VMEM((k, n), jnp.bfloat16)],
        compiler_params=_params("arbitrary"),
        name="out_proj",
    )(y, w, x2d, g.reshape(1, n))


def _attention_constants(s):
    n_slc = s // SEL_BLOCK
    n_cmp = (s - CMP_BLOCK) // CMP_STRIDE + 1
    c0 = np.arange(s // CMP_STRIDE)[None, :] * CMP_STRIDE
    s0 = np.arange(n_slc)[:, None] * SEL_BLOCK
    overlap = np.maximum(0, np.minimum(c0 + CMP_BLOCK, s0 + SEL_BLOCK) - np.maximum(c0, s0))
    aggt = (overlap / CMP_BLOCK) * (np.arange(s // CMP_STRIDE)[None, :] < n_cmp)
    blockid = (np.arange(s)[:, None] // SEL_BLOCK == np.arange(LANES)[None, :]).astype(np.float32)
    n_pat = WINDOW // Q_TILE
    pats = []
    for p in range(n_pat + 1):
        t = p * Q_TILE + np.arange(Q_TILE)[:, None]
        kpos = max(p * Q_TILE - WINDOW, 0) + np.arange(WINDOW + Q_TILE)[None, :]
        pats.append(np.where((kpos <= t) & (kpos > t - WINDOW), 0.0, MASKED))
    return (jnp.asarray(aggt, jnp.bfloat16), jnp.asarray(blockid, jnp.bfloat16),
            jnp.asarray(np.stack(pats), jnp.float32))


def _layer(x, norm_g, w_in, b_in, pe_k, w1_k, w2_k, pe_v, w1_v, w2_v,
           conv_w, conv_b, p_attn, p_conv, w_o, out_g):
    b, s, d = x.shape
    m = b * s
    bf = jnp.bfloat16
    x2d = x.reshape(m, d)
    tn = 1024

    c_q = ATTN_WIDTH
    c_cmp = c_q + 2 * KV_WIDTH
    c_kv = c_cmp + 4 * KV_WIDTH
    n_gate = 3 * N_HEADS
    c_z = c_kv + n_gate
    c_conv = c_z + ATTN_WIDTH
    c_merge = c_conv + 4 * CONV_WIDTH
    assert c_cmp - c_q == tn and c_q % tn == 0 and (c_kv - c_cmp) % tn == 0
    wt = jnp.swapaxes(w_in, 1, 2).reshape(w_in.shape[2], w_in.shape[1])

    hn = _rmsnorm(x2d, norm_g, bf)
    q_tiles = c_q // tn
    qkv = _proj(hn, wt, jnp.concatenate([b_in[:c_q], b_in[c_cmp:c_kv]]),
                lambda j: jnp.where(j < q_tiles, j * tn, c_cmp + (j - q_tiles) * tn),
                q_tiles + (c_kv - c_cmp) // tn, tn, bf,
                scaled_tiles=q_tiles, scale=SCALE * math.log2(math.e), name="proj_qkv").reshape(b, s, -1)
    kvc = _proj(hn, wt, b_in[c_q:c_cmp], lambda j: c_q + j * tn, 1, tn, jnp.float32,
                name="proj_cmp").reshape(b, s, -1)
    gate = _proj(hn, wt, b_in[c_kv:c_kv + LANES], lambda j: c_kv + j * LANES, 1, LANES, jnp.float32,
                 act="sigmoid", name="proj_gate")
    z_tiles = ATTN_WIDTH // tn
    zg = _proj(hn, wt, jnp.concatenate([b_in[c_z:c_conv], b_in[c_merge:]]),
               lambda j: jnp.where(j < z_tiles, c_z + j * tn, c_merge + (j - z_tiles) * tn),
               z_tiles + 2 * D_MODEL // tn, tn, jnp.float32, tm=1024, name="proj_zg").reshape(b, s, -1)
    cw = 256
    b_conv = b_in[c_conv:c_merge].reshape(4, CONV_WIDTH // cw, cw).transpose(1, 0, 2).reshape(1, -1)
    conv_starts = [lambda j, p=p: c_conv + p * CONV_WIDTH + j * cw for p in range(4)]
    c = _proj_conv(hn, wt, b_conv, conv_w, conv_b, conv_starts, s, cw=cw)

    kcb = _compress(kvc, 0, pe_k, w1_k.astype(bf).reshape(CMP_BLOCK, HEAD_DIM, CMP_HIDDEN), w2_k.astype(bf))
    vcb = _compress(kvc, N_KV_GROUPS, pe_v, w1_v.astype(bf).reshape(CMP_BLOCK, HEAD_DIM, CMP_HIDDEN),
                    w2_v.astype(bf))

    gates = gate[:, :n_gate].reshape(b, s, 3, N_KV_GROUPS, HEADS_PER_GROUP)
    gates = gates.transpose(0, 3, 1, 2, 4).reshape(b, N_KV_GROUPS, s, 3 * HEADS_PER_GROUP)

    aggt, blockid, wbias = _attention_constants(s)
    a = _attention(qkv, kcb, vcb, gates, zg, aggt, blockid, wbias)

    y = _merge(a.reshape(m, -1), c, p_attn, p_conv, zg.reshape(m, -1), ATTN_WIDTH)
    return _out_proj(y, w_o, x2d, out_g).reshape(b, s, d)


def kernel(x, norm_g, w_in, b_in, cmp_pe_k, cmp_w1_k, cmp_w2_k, cmp_pe_v, cmp_w1_v, cmp_w2_v,
           conv_w, conv_b, p_attn, p_conv, w_o, final_g):
    assert norm_g.shape[0] == 1, "single-layer block"
    return _layer(x, norm_g[0], w_in, b_in[0], cmp_pe_k[0], cmp_w1_k[0], cmp_w2_k[0],
                  cmp_pe_v[0], cmp_w1_v[0], cmp_w2_v[0], conv_w[0], conv_b[0],
                  p_attn, p_conv, w_o, final_g)
```

```python
import functools
import math

import numpy as np
import jax
import jax.numpy as jnp
from jax import lax
from jax.experimental import pallas as pl
from jax.experimental.pallas import tpu as pltpu

D_MODEL = 2048
N_HEADS = 16
HEAD_DIM = 128
N_KV_GROUPS = 4
HEADS_PER_GROUP = N_HEADS // N_KV_GROUPS
ATTN_WIDTH = N_HEADS * HEAD_DIM
KV_WIDTH = N_KV_GROUPS * HEAD_DIM
GROUP_WIDTH = HEADS_PER_GROUP * HEAD_DIM
CMP_BLOCK = 32
CMP_STRIDE = 16
CMP_HIDDEN = 256
SEL_BLOCK = 64
SEL_TOP_N = 8
WINDOW = 512
SCALE = HEAD_DIM ** -0.5
CONV_WIDTH = D_MODEL
CONV_K = 3
RMS_EPS = 1e-6
FORCED_SCORE = 1e4
MASKED = -(2.0 ** 100)

LANES = 128
SUBLANES = 8
Q_TILE = 256
SEL_CHUNK = 256
GROUPS_PER_STEP = 2
W_ROWS = 512
W_ALIGN = 16
VMEM_LIMIT = 56 * 1024 * 1024

_NT = (((1,), (1,)), ((), ()))


def _params(*sem):
    return pltpu.CompilerParams(dimension_semantics=sem, vmem_limit_bytes=VMEM_LIMIT)


def _sigmoid(x):
    return 1.0 / (1.0 + jnp.exp(-x))


def _silu(x):
    return x * _sigmoid(x)


def _rmsnorm_kernel(x_ref, g_ref, o_ref):
    x = x_ref[...]
    ms = jnp.mean(x * x, axis=-1, keepdims=True)
    o_ref[...] = (x * lax.rsqrt(ms + RMS_EPS) * g_ref[...]).astype(o_ref.dtype)


def _rmsnorm(x2d, g, out_dtype, tm=512):
    m, d = x2d.shape
    return pl.pallas_call(
        _rmsnorm_kernel,
        grid=(m // tm,),
        in_specs=[pl.BlockSpec((tm, d), lambda i: (i, 0)),
                  pl.BlockSpec((1, d), lambda i: (0, 0))],
        out_specs=pl.BlockSpec((tm, d), lambda i: (i, 0)),
        out_shape=jax.ShapeDtypeStruct((m, d), out_dtype),
        compiler_params=_params("parallel"),
        name="rmsnorm",
    )(x2d, g.reshape(1, d))


def _cast_weight(w_scr, w_ref, row0=0):
    rows = w_ref.shape[0]
    for r in range(0, rows, W_ROWS):
        n = min(W_ROWS, rows - r)
        w_scr[row0 + r:row0 + r + n, :] = w_ref[r:r + n, :].astype(w_scr.dtype)


def _weight_specs(k, piece_rows, start_fns):
    return [pl.BlockSpec((pl.Element(piece_rows), pl.Element(k)),
                         lambda j, i, f=f: (pl.multiple_of(f(j), W_ALIGN), 0))
            for f in start_fns]


def _proj_kernel(a_ref, b_ref, w_ref, o_ref, w_scr, *, act, scaled_tiles, scale):
    @pl.when(pl.program_id(1) == 0)
    def _():
        _cast_weight(w_scr, w_ref)

    r = lax.dot_general(a_ref[...], w_scr[...], _NT, preferred_element_type=jnp.float32) + b_ref[...]
    if scaled_tiles:
        r = r * jnp.where(pl.program_id(0) < scaled_tiles, scale, 1.0)
    if act == "sigmoid":
        r = _sigmoid(r)
    o_ref[...] = r.astype(o_ref.dtype)


def _proj(a, wt, bias, start_fn, n_tiles, tn, out_dtype, tm=2048, act=None,
          scaled_tiles=0, scale=1.0, name="proj"):
    m, k = a.shape
    return pl.pallas_call(
        functools.partial(_proj_kernel, act=act, scaled_tiles=scaled_tiles, scale=scale),
        grid=(n_tiles, m // tm),
        in_specs=[pl.BlockSpec((tm, k), lambda j, i: (i, 0)),
                  pl.BlockSpec((1, tn), lambda j, i: (0, j))] + _weight_specs(k, tn, [start_fn]),
        out_specs=pl.BlockSpec((tm, tn), lambda j, i: (i, j)),
        out_shape=jax.ShapeDtypeStruct((m, n_tiles * tn), out_dtype),
        scratch_shapes=[pltpu.VMEM((tn, k), jnp.bfloat16)],
        compiler_params=_params("arbitrary", "arbitrary"),
        name=name,
    )(a, bias.reshape(1, n_tiles * tn), wt)


def _proj_conv_kernel(a_ref, b_ref, cw_ref, cb_ref, wu_ref, wcc_ref, wcb_ref, wz_ref,
                      o_ref, w_scr, halo_scr, *, seq):
    tm, cw = o_ref.shape
    i = pl.program_id(1)

    @pl.when(i == 0)
    def _():
        for p, w_ref in enumerate((wu_ref, wcc_ref, wcb_ref, wz_ref)):
            _cast_weight(w_scr, w_ref, p * cw)

    @pl.when((i * tm) % seq == 0)
    def _():
        halo_scr[...] = jnp.zeros(halo_scr.shape, jnp.float32)

    r = lax.dot_general(a_ref[...], w_scr[...], _NT, preferred_element_type=jnp.float32) + b_ref[...]
    u, cc, cb, z = (r[:, p * cw:(p + 1) * cw] for p in range(4))
    v = cc * u
    prev = halo_scr[...]
    halo_scr[...] = v[tm - SUBLANES:, :]
    row = lax.broadcasted_iota(jnp.int32, (tm, 1), 0)
    y = cw_ref[CONV_K - 1:CONV_K, :] * v
    for d in range(1, CONV_K):
        shifted = pltpu.roll(v, d, axis=0)
        for t in range(d):
            shifted = jnp.where(row == t, prev[SUBLANES - d + t:SUBLANES - d + t + 1, :], shifted)
        y = y + cw_ref[CONV_K - 1 - d:CONV_K - d, :] * shifted
    o_ref[...] = (cb * (y + cb_ref[...]) * _silu(z)).astype(o_ref.dtype)


def _proj_conv(a, wt, bias_tiles, conv_w, conv_b, start_fns, seq, tm=2048, cw=256):
    m, k = a.shape
    n_tiles = CONV_WIDTH // cw
    assert seq % tm == 0 and CONV_K - 1 <= SUBLANES and len(start_fns) == 4
    return pl.pallas_call(
        functools.partial(_proj_conv_kernel, seq=seq),
        grid=(n_tiles, m // tm),
        in_specs=[pl.BlockSpec((tm, k), lambda j, i: (i, 0)),
                  pl.BlockSpec((1, 4 * cw), lambda j, i: (0, j)),
                  pl.BlockSpec((CONV_K, cw), lambda j, i: (0, j)),
                  pl.BlockSpec((1, cw), lambda j, i: (0, j))] + _weight_specs(k, cw, start_fns),
        out_specs=pl.BlockSpec((tm, cw), lambda j, i: (i, j)),
        out_shape=jax.ShapeDtypeStruct((m, CONV_WIDTH), jnp.bfloat16),
        scratch_shapes=[pltpu.VMEM((4 * cw, k), jnp.bfloat16),
                        pltpu.VMEM((SUBLANES, cw), jnp.float32)],
        compiler_params=_params("arbitrary", "arbitrary"),
        name="proj_conv",
    )(a, bias_tiles, conv_w, conv_b.reshape(1, CONV_WIDTH), wt, wt, wt, wt)


def _compress_kernel(x_ref, pe_ref, w1_ref, w2_ref, o_ref):
    half = CMP_BLOCK // 2
    n_chunks = x_ref.shape[1] // CMP_STRIDE
    top = jnp.zeros((n_chunks, CMP_HIDDEN), jnp.float32)
    bot = jnp.zeros((n_chunks, CMP_HIDDEN), jnp.float32)
    for l in range(half):
        xl = x_ref[0, pl.ds(l, n_chunks, stride=CMP_STRIDE), :]
        a = (xl + pe_ref[l:l + 1, :]).astype(jnp.bfloat16)
        c = (xl + pe_ref[half + l:half + l + 1, :]).astype(jnp.bfloat16)
        top = top + jnp.dot(a, w1_ref[l], preferred_element_type=jnp.float32)
        bot = bot + jnp.dot(c, w1_ref[half + l], preferred_element_type=jnp.float32)
    h = top + pltpu.roll(bot, n_chunks - 1, axis=0)
    h = _silu(h).astype(jnp.bfloat16)
    o_ref[0, 0] = jnp.dot(h, w2_ref[...], preferred_element_type=jnp.float32).astype(o_ref.dtype)


def _compress(kv, col_block0, pe, w1, w2):
    b, s, _ = kv.shape
    return pl.pallas_call(
        _compress_kernel,
        grid=(b, N_KV_GROUPS),
        in_specs=[pl.BlockSpec((1, s, HEAD_DIM), lambda bi, g: (bi, 0, col_block0 + g)),
                  pl.BlockSpec((CMP_BLOCK, HEAD_DIM), lambda bi, g: (0, 0)),
                  pl.BlockSpec((CMP_BLOCK, HEAD_DIM, CMP_HIDDEN), lambda bi, g: (0, 0, 0)),
                  pl.BlockSpec((CMP_HIDDEN, HEAD_DIM), lambda bi, g: (0, 0))],
        out_specs=pl.BlockSpec((1, 1, s // CMP_STRIDE, HEAD_DIM), lambda bi, g: (bi, g, 0, 0)),
        out_shape=jax.ShapeDtypeStruct((b, N_KV_GROUPS, s // CMP_STRIDE, HEAD_DIM), jnp.bfloat16),
        compiler_params=_params("parallel", "parallel"),
        name="compress",
    )(kv, pe, w1, w2)


def _attn_kernel(q_ref, ks_ref, vs_ref, kw_ref, vw_ref, kc_ref, vc_ref, gate_ref, z_ref,
                 aggt_ref, blockid_ref, wbias_ref, gsel_ref, o_ref,
                 kaug_scr, vsaug_scr, vwaug_scr, vcaug_scr, gexp_scr, s_scr, m_scr, acc_scr):
    tq = Q_TILE
    hpg = HEADS_PER_GROUP
    gw = GROUP_WIDTH
    seq = kw_ref.shape[1]
    gps = kc_ref.shape[1]
    i = pl.program_id(2)
    t0 = i * tq

    @pl.when(i == 0)
    def _():
        ones = jnp.ones((seq, LANES), jnp.bfloat16)
        for gi in range(gps):
            cols = slice(gi * HEAD_DIM, (gi + 1) * HEAD_DIM)
            kaug_scr[gi, :, :HEAD_DIM] = ks_ref[0, :, cols]
            kaug_scr[gi, :, HEAD_DIM:] = blockid_ref[...]
            vsaug_scr[gi, :, :HEAD_DIM] = vs_ref[0, :, cols]
            vsaug_scr[gi, :, HEAD_DIM:] = ones
            vwaug_scr[gi, :, :HEAD_DIM] = vw_ref[0, :, cols]
            vwaug_scr[gi, :, HEAD_DIM:] = ones
            vcaug_scr[gi, :, :HEAD_DIM] = vc_ref[0, gi]
            vcaug_scr[gi, :, HEAD_DIM:] = ones[:vcaug_scr.shape[1]]

    t_q = t0 + lax.broadcasted_iota(jnp.int32, (tq, 1), 0)
    t_s = jnp.concatenate([t_q] * hpg, axis=0)
    lane = lax.broadcasted_iota(jnp.int32, (1, LANES), 1)
    n_cmp = (seq - CMP_BLOCK) // CMP_STRIDE + 1
    n_slc = seq // SEL_BLOCK
    n_win = WINDOW + tq
    start = pl.multiple_of(jnp.maximum(t0 - WINDOW, 0), tq)
    w_bias = jnp.concatenate([wbias_ref[0]] * hpg, axis=0)
    vis_c = (lane * CMP_STRIDE + (CMP_BLOCK - 1) <= t_s) & (lane < n_cmp)
    aggt = aggt_ref[...]
    blk = lax.broadcasted_iota(jnp.int32, (n_slc, 1), 0)
    t_l = t0 + lax.broadcasted_iota(jnp.int32, (1, tq), 1)
    cur = t_l >> int(math.log2(SEL_BLOCK))
    causal = blk * SEL_BLOCK <= t_l
    forced = ((blk == 0) | (blk == cur) | (blk == cur - 1)) & causal
    gate = gate_ref[0]
    gate_hi = gate.astype(jnp.bfloat16)
    gate_lo = (gate - gate_hi.astype(jnp.float32)).astype(jnp.bfloat16)
    gate_hl = jnp.concatenate([gate_hi, gate_lo], axis=1)

    def first_region(gi):
        g_exp = jnp.dot(gate_hl, gsel_ref[gi], preferred_element_type=jnp.float32)
        for c in range(3 * hpg):
            gexp_scr[gi, c] = g_exp[:, c * LANES:(c + 1) * LANES]

        q = q_ref[0, :, gi * gw:(gi + 1) * gw]
        qs = jnp.concatenate([q[:, h * HEAD_DIM:(h + 1) * HEAD_DIM] for h in range(hpg)], axis=0)

        kw = kw_ref[0, pl.ds(start, n_win), gi * HEAD_DIM:(gi + 1) * HEAD_DIM]
        sw = lax.dot_general(qs, kw, _NT, preferred_element_type=jnp.float32) + w_bias
        p_w = jnp.exp2(sw - jnp.max(sw, axis=-1, keepdims=True))
        pv_w = jnp.dot(p_w.astype(jnp.bfloat16), vwaug_scr[gi, pl.ds(start, n_win), :],
                       preferred_element_type=jnp.float32)
        o_win = pv_w[:, :HEAD_DIM] * (1.0 / pv_w[:, HEAD_DIM:])

        sc = lax.dot_general(qs, kc_ref[0, gi], _NT, preferred_element_type=jnp.float32)
        sc = jnp.where(vis_c, sc, MASKED)
        p_c = jnp.where(vis_c, jnp.exp2(sc - jnp.max(sc, axis=-1, keepdims=True)), 0.0)
        pv_c = jnp.dot(p_c.astype(jnp.bfloat16), vcaug_scr[gi],
                       preferred_element_type=jnp.float32)
        l_c = pv_c[:, HEAD_DIM:]
        inv_c = jnp.where(l_c > 0.0, 1.0 / l_c, 0.0)
        o_cmp = pv_c[:, :HEAD_DIM] * inv_c
        p_c = p_c * inv_c

        p_sum = p_c[0:tq]
        for h in range(1, hpg):
            p_sum = p_sum + p_c[h * tq:(h + 1) * tq]
        hi = p_sum.astype(jnp.bfloat16)
        r1 = p_sum - hi.astype(jnp.float32)
        mid = r1.astype(jnp.bfloat16)
        lo = (r1 - mid.astype(jnp.float32)).astype(jnp.bfloat16)
        imp = (lax.dot_general(aggt, hi, _NT, preferred_element_type=jnp.float32)
               + lax.dot_general(aggt, mid, _NT, preferred_element_type=jnp.float32)
               + lax.dot_general(aggt, lo, _NT, preferred_element_type=jnp.float32))
        imp = jnp.where(forced, FORCED_SCORE, imp)
        imp = jnp.where(causal, imp, -1.0)

        bias_slabs = []
        for v in range(n_slc // SUBLANES):
            lo_row = v * SUBLANES
            slab = imp[lo_row:lo_row + SUBLANES]
            sub = lo_row + lax.broadcasted_iota(jnp.int32, (SUBLANES, 1), 0)
            rank = jnp.zeros((SUBLANES, tq), jnp.float32)
            for jp in range(n_slc):
                other = jnp.broadcast_to(imp[jp:jp + 1], (SUBLANES, tq))
                if jp < lo_row:
                    ahead = jnp.where(other >= slab, 1.0, 0.0)
                elif jp >= lo_row + SUBLANES:
                    ahead = jnp.where(other > slab, 1.0, 0.0)
                else:
                    tie = jnp.where(sub > jp, 1.0, 0.0)
                    ahead = jnp.where(other > slab, 1.0, jnp.where(other == slab, tie, 0.0))
                rank = rank + ahead
            bias_slabs.append(jnp.where(rank < float(SEL_TOP_N), 0.0, MASKED))
        bias_t = jnp.concatenate(bias_slabs + [jnp.zeros((LANES - n_slc, tq), jnp.float32)], axis=0)
        sel_bias = bias_t.T.astype(jnp.bfloat16)
        qa = jnp.concatenate([qs, jnp.concatenate([sel_bias] * hpg, axis=0)], axis=1)
        return o_win, o_cmp, qa

    first = [first_region(gi) for gi in range(gps)]

    n_full = t0 // SEL_CHUNK
    groups = [g for g in (4, 2, 1) if g <= max((seq - tq) // SEL_CHUNK, 1)]

    def scores(gi, c):
        base = pl.multiple_of(c * SEL_CHUNK, SEL_CHUNK)
        return base, lax.dot_general(first[gi][2], kaug_scr[gi, pl.ds(base, SEL_CHUNK), :], _NT,
                                     preferred_element_type=jnp.float32)

    def keep(gi, base, s, m):
        s_scr[gi, :, pl.ds(base, SEL_CHUNK)] = s
        for k in range(SEL_CHUNK // LANES):
            m = jnp.maximum(m, s[:, k * LANES:(k + 1) * LANES])
        return m

    for gi in range(gps):
        base_d, s_d = scores(gi, n_full)
        kpos_d = base_d + lax.broadcasted_iota(jnp.int32, (1, SEL_CHUNK), 1)
        m_scr[gi] = keep(gi, base_d, jnp.where(kpos_d <= t_s, s_d, MASKED),
                         jnp.full(m_scr.shape[1:], MASKED, jnp.float32))
    done = 0
    for g in groups:
        @pl.when((n_full & g) != 0)
        def _(done=done, g=g):
            for gi in range(gps):
                m = m_scr[gi]
                for u in range(g):
                    m = keep(gi, *scores(gi, done + u), m)
                m_scr[gi] = m
        done = done + (n_full & g)
    for gi in range(gps):
        m_scr[gi] = jnp.broadcast_to(jnp.max(m_scr[gi], axis=-1, keepdims=True), m_scr.shape[1:])

    def values(gi, c, m):
        base = pl.multiple_of(c * SEL_CHUNK, SEL_CHUNK)
        ps = [jnp.exp2(s_scr[gi, :, pl.ds(base + k * LANES, LANES)] - m).astype(jnp.bfloat16)
              for k in range(SEL_CHUNK // LANES)]
        return jnp.dot(jnp.concatenate(ps, axis=1), vsaug_scr[gi, pl.ds(base, SEL_CHUNK), :],
                       preferred_element_type=jnp.float32)

    for gi in range(gps):
        acc_scr[gi] = values(gi, n_full, m_scr[gi])
    done = 0
    for g in groups:
        @pl.when((n_full & g) != 0)
        def _(done=done, g=g):
            for gi in range(gps):
                m, acc = m_scr[gi], acc_scr[gi]
                for u in range(g):
                    acc = acc + values(gi, done + u, m)
                acc_scr[gi] = acc
        done = done + (n_full & g)

    for gi in range(gps):
        o_win, o_cmp, _ = first[gi]
        o_sel = acc_scr[gi, :, :HEAD_DIM] * (1.0 / acc_scr[gi, :, HEAD_DIM:])
        outs = []
        for h in range(hpg):
            r0 = h * tq
            outs.append(gexp_scr[gi, h] * o_cmp[r0:r0 + tq]
                        + gexp_scr[gi, hpg + h] * o_sel[r0:r0 + tq]
                        + gexp_scr[gi, 2 * hpg + h] * o_win[r0:r0 + tq])
        o = jnp.concatenate(outs, axis=1)
        cols = slice(gi * gw, (gi + 1) * gw)
        o_ref[0, :, cols] = (o * _silu(z_ref[0, :, cols])).astype(o_ref.dtype)


def _attention(qkv, kcb, vcb, gate, zu, aggt, blockid, wbias, gsel):
    b, s, _ = qkv.shape
    gps = GROUPS_PER_STEP
    assert SEL_CHUNK % Q_TILE == 0 and WINDOW % Q_TILE == 0 and s % SEL_CHUNK == 0
    assert N_KV_GROUPS % gps == 0
    kvw = gps * HEAD_DIM
    qb = ATTN_WIDTH // kvw
    kvb = KV_WIDTH // kvw
    n_c = s // CMP_STRIDE
    rows = HEADS_PER_GROUP * Q_TILE
    n_pat = WINDOW // Q_TILE
    grid = (b, N_KV_GROUPS // gps, s // Q_TILE)
    kv_spec = lambda off: pl.BlockSpec((1, s, kvw), lambda bi, g, i: (bi, 0, off + g))
    wide = pl.BlockSpec((1, Q_TILE, gps * GROUP_WIDTH), lambda bi, g, i: (bi, i, g))
    return pl.pallas_call(
        _attn_kernel,
        grid=grid,
        in_specs=[wide,
                  kv_spec(qb), kv_spec(qb + kvb), kv_spec(qb + 2 * kvb), kv_spec(qb + 3 * kvb),
                  pl.BlockSpec((1, gps, n_c, HEAD_DIM), lambda bi, g, i: (bi, g, 0, 0)),
                  pl.BlockSpec((1, gps, n_c, HEAD_DIM), lambda bi, g, i: (bi, g, 0, 0)),
                  pl.BlockSpec((1, Q_TILE, LANES), lambda bi, g, i: (bi, i, 0)),
                  wide,
                  pl.BlockSpec(aggt.shape, lambda bi, g, i: (0, 0)),
                  pl.BlockSpec(blockid.shape, lambda bi, g, i: (0, 0)),
                  pl.BlockSpec((1, Q_TILE, WINDOW + Q_TILE),
                               lambda bi, g, i: (jnp.minimum(i, n_pat), 0, 0)),
                  pl.BlockSpec((gps,) + gsel.shape[1:], lambda bi, g, i: (g, 0, 0))],
        out_specs=wide,
        out_shape=jax.ShapeDtypeStruct((b, s, ATTN_WIDTH), jnp.bfloat16),
        scratch_shapes=[pltpu.VMEM((gps, s, HEAD_DIM + LANES), jnp.bfloat16),
                        pltpu.VMEM((gps, s, HEAD_DIM + LANES), jnp.bfloat16),
                        pltpu.VMEM((gps, s, HEAD_DIM + LANES), jnp.bfloat16),
                        pltpu.VMEM((gps, n_c, HEAD_DIM + LANES), jnp.bfloat16),
                        pltpu.VMEM((gps, 3 * HEADS_PER_GROUP, Q_TILE, LANES), jnp.float32),
                        pltpu.VMEM((gps, rows, s), jnp.float32),
                        pltpu.VMEM((gps, rows, LANES), jnp.float32),
                        pltpu.VMEM((gps, rows, HEAD_DIM + LANES), jnp.float32)],
        compiler_params=_params("arbitrary", "arbitrary", "arbitrary"),
        name="nsa_attention",
    )(qkv, qkv, qkv, qkv, qkv, kcb, vcb, gate, zu, aggt, blockid, wbias, gsel)


def _merge_kernel(a_ref, c_ref, wa_ref, wc_ref, g0_ref, g1_ref, o_ref, wa_scr, wc_scr):
    @pl.when(pl.program_id(1) == 0)
    def _():
        _cast_weight(wa_scr, wa_ref)
        _cast_weight(wc_scr, wc_ref)

    ya = jnp.dot(a_ref[...], wa_scr[...], preferred_element_type=jnp.float32)
    yc = jnp.dot(c_ref[...], wc_scr[...], preferred_element_type=jnp.float32)
    o_ref[...] = (_sigmoid(g0_ref[...]) * ya + _sigmoid(g1_ref[...]) * yc).astype(o_ref.dtype)


def _merge(a, c, wa, wc, zg, gcol0, tm=1024, tn=512):
    m, k = a.shape
    n = wa.shape[-1]
    g0 = gcol0 // tn
    g1 = (gcol0 + n) // tn
    return pl.pallas_call(
        _merge_kernel,
        grid=(n // tn, m // tm),
        in_specs=[pl.BlockSpec((tm, k), lambda j, i: (i, 0)),
                  pl.BlockSpec((tm, k), lambda j, i: (i, 0)),
                  pl.BlockSpec((None, k, tn), lambda j, i: (0, 0, j)),
                  pl.BlockSpec((None, k, tn), lambda j, i: (0, 0, j)),
                  pl.BlockSpec((tm, tn), lambda j, i: (i, g0 + j)),
                  pl.BlockSpec((tm, tn), lambda j, i: (i, g1 + j))],
        out_specs=pl.BlockSpec((tm, tn), lambda j, i: (i, j)),
        out_shape=jax.ShapeDtypeStruct((m, n), jnp.bfloat16),
        scratch_shapes=[pltpu.VMEM((k, tn), jnp.bfloat16), pltpu.VMEM((k, tn), jnp.bfloat16)],
        compiler_params=_params("arbitrary", "arbitrary"),
        name="merge",
    )(a, c, wa, wc, zg, zg)


def _out_kernel(y_ref, w_ref, x_ref, g_ref, o_ref, w_scr):
    @pl.when(pl.program_id(0) == 0)
    def _():
        _cast_weight(w_scr, w_ref)

    r = x_ref[...] + jnp.dot(y_ref[...], w_scr[...], preferred_element_type=jnp.float32)
    ms = jnp.mean(r * r, axis=-1, keepdims=True)
    o_ref[...] = r * lax.rsqrt(ms + RMS_EPS) * g_ref[...]


def _out_proj(y, w, x2d, g, tm=512):
    m, k = y.shape
    n = w.shape[-1]
    return pl.pallas_call(
        _out_kernel,
        grid=(m // tm,),
        in_specs=[pl.BlockSpec((tm, k), lambda i: (i, 0)),
                  pl.BlockSpec((None, k, n), lambda i: (0, 0, 0), pipeline_mode=pl.Buffered(1)),
                  pl.BlockSpec((tm, n), lambda i: (i, 0)),
                  pl.BlockSpec((1, n), lambda i: (0, 0))],
        out_specs=pl.BlockSpec((tm, n), lambda i: (i, 0)),
        out_shape=jax.ShapeDtypeStruct((m, n), jnp.float32),
        scratch_shapes=[pltpu.VMEM((k, n), jnp.bfloat16)],
        compiler_params=_params("arbitrary"),
        name="out_proj",
    )(y, w, x2d, g.reshape(1, n))


def _attention_constants(s):
    n_slc = s // SEL_BLOCK
    n_cmp = (s - CMP_BLOCK) // CMP_STRIDE + 1
    c0 = np.arange(s // CMP_STRIDE)[None, :] * CMP_STRIDE
    s0 = np.arange(n_slc)[:, None] * SEL_BLOCK
    overlap = np.maximum(0, np.minimum(c0 + CMP_BLOCK, s0 + SEL_BLOCK) - np.maximum(c0, s0))
    aggt = (overlap / CMP_BLOCK) * (np.arange(s // CMP_STRIDE)[None, :] < n_cmp)
    blockid = (np.arange(s)[:, None] // SEL_BLOCK == np.arange(LANES)[None, :]).astype(np.float32)
    n_pat = WINDOW // Q_TILE
    pats = []
    for p in range(n_pat + 1):
        t = p * Q_TILE + np.arange(Q_TILE)[:, None]
        kpos = max(p * Q_TILE - WINDOW, 0) + np.arange(WINDOW + Q_TILE)[None, :]
        pats.append(np.where((kpos <= t) & (kpos > t - WINDOW), 0.0, MASKED))
    n_bh = 3 * HEADS_PER_GROUP
    gsel = np.zeros((N_KV_GROUPS, 2 * LANES, n_bh * LANES), np.float32)
    for g in range(N_KV_GROUPS):
        for br in range(3):
            for h in range(HEADS_PER_GROUP):
                src = br * N_HEADS + g * HEADS_PER_GROUP + h
                c = br * HEADS_PER_GROUP + h
                gsel[g, [src, LANES + src], c * LANES:(c + 1) * LANES] = 1.0
    return (jnp.asarray(aggt, jnp.bfloat16), jnp.asarray(blockid, jnp.bfloat16),
            jnp.asarray(np.stack(pats), jnp.float32), jnp.asarray(gsel, jnp.bfloat16))


def _layer(x, norm_g, w_in, b_in, pe_k, w1_k, w2_k, pe_v, w1_v, w2_v,
           conv_w, conv_b, p_attn, p_conv, w_o, out_g):
    b, s, d = x.shape
    m = b * s
    bf = jnp.bfloat16
    x2d = x.reshape(m, d)
    tn = 1024

    c_q = ATTN_WIDTH
    c_cmp = c_q + 2 * KV_WIDTH
    c_kv = c_cmp + 4 * KV_WIDTH
    n_gate = 3 * N_HEADS
    c_z = c_kv + n_gate
    c_conv = c_z + ATTN_WIDTH
    c_merge = c_conv + 4 * CONV_WIDTH
    assert c_cmp - c_q == tn and c_q % tn == 0 and (c_kv - c_cmp) % tn == 0
    wt = jnp.swapaxes(w_in, 1, 2).reshape(w_in.shape[2], w_in.shape[1])

    hn = _rmsnorm(x2d, norm_g, bf)
    q_tiles = c_q // tn
    qkv = _proj(hn, wt, jnp.concatenate([b_in[:c_q], b_in[c_cmp:c_kv]]),
                lambda j: jnp.where(j < q_tiles, j * tn, c_cmp + (j - q_tiles) * tn),
                q_tiles + (c_kv - c_cmp) // tn, tn, bf,
                scaled_tiles=q_tiles, scale=SCALE * math.log2(math.e), name="proj_qkv").reshape(b, s, -1)
    kvc = _proj(hn, wt, b_in[c_q:c_cmp], lambda j: c_q + j * tn, 1, tn, jnp.float32,
                name="proj_cmp").reshape(b, s, -1)
    gate = _proj(hn, wt, b_in[c_kv:c_kv + LANES], lambda j: c_kv + j * LANES, 1, LANES, jnp.float32,
                 act="sigmoid", name="proj_gate")
    z_tiles = ATTN_WIDTH // tn
    zg = _proj(hn, wt, jnp.concatenate([b_in[c_z:c_conv], b_in[c_merge:]]),
               lambda j: jnp.where(j < z_tiles, c_z + j * tn, c_merge + (j - z_tiles) * tn),
               z_tiles + 2 * D_MODEL // tn, tn, jnp.float32, tm=1024, name="proj_zg").reshape(b, s, -1)
    cw = 256
    b_conv = b_in[c_conv:c_merge].reshape(4, CONV_WIDTH // cw, cw).transpose(1, 0, 2).reshape(1, -1)
    conv_starts = [lambda j, p=p: c_conv + p * CONV_WIDTH + j * cw for p in range(4)]
    c = _proj_conv(hn, wt, b_conv, conv_w, conv_b, conv_starts, s, cw=cw)

    kcb = _compress(kvc, 0, pe_k, w1_k.astype(bf).reshape(CMP_BLOCK, HEAD_DIM, CMP_HIDDEN), w2_k.astype(bf))
    vcb = _compress(kvc, N_KV_GROUPS, pe_v, w1_v.astype(bf).reshape(CMP_BLOCK, HEAD_DIM, CMP_HIDDEN),
                    w2_v.astype(bf))

    aggt, blockid, wbias, gsel = _attention_constants(s)
    a = _attention(qkv, kcb, vcb, gate.reshape(b, s, LANES), zg, aggt, blockid, wbias, gsel)

    y = _merge(a.reshape(m, -1), c, p_attn, p_conv, zg.reshape(m, -1), ATTN_WIDTH)
    return _out_proj(y, w_o, x2d, out_g).reshape(b, s, d)


def kernel(x, norm_g, w_in, b_in, cmp_pe_k, cmp_w1_k, cmp_w2_k, cmp_pe_v, cmp_w1_v, cmp_w2_v,
           conv_w, conv_b, p_attn, p_conv, w_o, final_g):
    assert norm_g.shape[0] == 1, "single-layer block"
    return _layer(x, norm_g[0], w_in, b_in[0], cmp_pe_k[0], cmp_w1_k[0], cmp_w2_k[0],
                  cmp_pe_v[0], cmp_w1_v[0], cmp_w2_v[0], conv_w[0], conv_b[0],
                  p_attn, p_conv, w_o, final_g)
```

```python
import functools
import math

import numpy as np
import jax
import jax.numpy as jnp
from jax import lax
from jax.experimental import pallas as pl
from jax.experimental.pallas import tpu as pltpu

D_MODEL = 2048
N_HEADS = 16
HEAD_DIM = 128
N_KV_GROUPS = 4
HEADS_PER_GROUP = N_HEADS // N_KV_GROUPS
ATTN_WIDTH = N_HEADS * HEAD_DIM
KV_WIDTH = N_KV_GROUPS * HEAD_DIM
GROUP_WIDTH = HEADS_PER_GROUP * HEAD_DIM
CMP_BLOCK = 32
CMP_STRIDE = 16
CMP_HIDDEN = 256
SEL_BLOCK = 64
SEL_TOP_N = 8
WINDOW = 512
SCALE = HEAD_DIM ** -0.5
CONV_WIDTH = D_MODEL
CONV_K = 3
RMS_EPS = 1e-6
FORCED_SCORE = 1e4
MASKED = -(2.0 ** 100)

LANES = 128
SUBLANES = 8
Q_TILE = 256
SEL_CHUNK = 256
GROUPS_PER_STEP = 2
W_ROWS = 512
W_ALIGN = 16
VMEM_LIMIT = 56 * 1024 * 1024

_NT = (((1,), (1,)), ((), ()))


def _params(*sem):
    return pltpu.CompilerParams(dimension_semantics=sem, vmem_limit_bytes=VMEM_LIMIT)


def _sigmoid(x):
    return 1.0 / (1.0 + jnp.exp(-x))


def _silu(x):
    return x * _sigmoid(x)


def _cast_weight(w_scr, w_ref, row0=0):
    rows = w_ref.shape[0]
    for r in range(0, rows, W_ROWS):
        n = min(W_ROWS, rows - r)
        w_scr[row0 + r:row0 + r + n, :] = w_ref[r:r + n, :].astype(w_scr.dtype)


def _weight_specs(k, piece_rows, start_fns):
    return [pl.BlockSpec((pl.Element(piece_rows), pl.Element(k)),
                         lambda j, i, f=f: (pl.multiple_of(f(j), W_ALIGN), 0))
            for f in start_fns]


def _proj_kernel(a_ref, b_ref, w_ref, o_ref, w_scr, *, act, scaled_tiles, scale):
    @pl.when(pl.program_id(1) == 0)
    def _():
        _cast_weight(w_scr, w_ref)

    r = lax.dot_general(a_ref[...], w_scr[...], _NT, preferred_element_type=jnp.float32) + b_ref[...]
    if scaled_tiles:
        r = r * jnp.where(pl.program_id(0) < scaled_tiles, scale, 1.0)
    if act == "sigmoid":
        r = _sigmoid(r)
    o_ref[...] = r.astype(o_ref.dtype)


def _proj(a, wt, bias, start_fn, n_tiles, tn, out_dtype, tm=2048, act=None,
          scaled_tiles=0, scale=1.0, name="proj"):
    m, k = a.shape
    return pl.pallas_call(
        functools.partial(_proj_kernel, act=act, scaled_tiles=scaled_tiles, scale=scale),
        grid=(n_tiles, m // tm),
        in_specs=[pl.BlockSpec((tm, k), lambda j, i: (i, 0)),
                  pl.BlockSpec((1, tn), lambda j, i: (0, j))] + _weight_specs(k, tn, [start_fn]),
        out_specs=pl.BlockSpec((tm, tn), lambda j, i: (i, j)),
        out_shape=jax.ShapeDtypeStruct((m, n_tiles * tn), out_dtype),
        scratch_shapes=[pltpu.VMEM((tn, k), jnp.bfloat16)],
        compiler_params=_params("arbitrary", "arbitrary"),
        name=name,
    )(a, bias.reshape(1, n_tiles * tn), wt)


def _norm_proj_kernel(x_ref, g_ref, b_ref, wkv_ref, wg_ref, hn_ref, kv_ref, gate_ref, w_scr):
    n_kv = kv_ref.shape[1]

    @pl.when(pl.program_id(0) == 0)
    def _():
        _cast_weight(w_scr, wkv_ref)
        _cast_weight(w_scr, wg_ref, n_kv)

    x = x_ref[...]
    ms = jnp.mean(x * x, axis=-1, keepdims=True)
    hn = (x * lax.rsqrt(ms + RMS_EPS) * g_ref[...]).astype(hn_ref.dtype)
    hn_ref[...] = hn
    r = lax.dot_general(hn, w_scr[...], _NT, preferred_element_type=jnp.float32) + b_ref[...]
    kv_ref[...] = r[:, :n_kv]
    gate_ref[...] = _sigmoid(r[:, n_kv:])


def _norm_proj(x2d, g, wt, bias, kv_start, n_kv, gate_start, tm=1024):
    m, k = x2d.shape
    once = dict(pipeline_mode=pl.Buffered(1))
    return pl.pallas_call(
        _norm_proj_kernel,
        grid=(m // tm,),
        in_specs=[pl.BlockSpec((tm, k), lambda i: (i, 0)),
                  pl.BlockSpec((1, k), lambda i: (0, 0)),
                  pl.BlockSpec((1, n_kv + LANES), lambda i: (0, 0)),
                  pl.BlockSpec((pl.Element(n_kv), pl.Element(k)), lambda i: (kv_start, 0), **once),
                  pl.BlockSpec((pl.Element(LANES), pl.Element(k)), lambda i: (gate_start, 0), **once)],
        out_specs=[pl.BlockSpec((tm, k), lambda i: (i, 0)),
                   pl.BlockSpec((tm, n_kv), lambda i: (i, 0)),
                   pl.BlockSpec((tm, LANES), lambda i: (i, 0))],
        out_shape=[jax.ShapeDtypeStruct((m, k), jnp.bfloat16),
                   jax.ShapeDtypeStruct((m, n_kv), jnp.float32),
                   jax.ShapeDtypeStruct((m, LANES), jnp.float32)],
        scratch_shapes=[pltpu.VMEM((n_kv + LANES, k), jnp.bfloat16)],
        compiler_params=_params("arbitrary"),
        name="norm_proj",
    )(x2d, g.reshape(1, k), bias.reshape(1, n_kv + LANES), wt, wt)


def _proj_conv_kernel(a_ref, b_ref, cw_ref, cb_ref, wu_ref, wcc_ref, wcb_ref, wz_ref,
                      o_ref, w_scr, halo_scr, *, seq):
    tm, cw = o_ref.shape
    i = pl.program_id(1)

    @pl.when(i == 0)
    def _():
        for p, w_ref in enumerate((wu_ref, wcc_ref, wcb_ref, wz_ref)):
            _cast_weight(w_scr, w_ref, p * cw)

    @pl.when((i * tm) % seq == 0)
    def _():
        halo_scr[...] = jnp.zeros(halo_scr.shape, jnp.float32)

    r = lax.dot_general(a_ref[...], w_scr[...], _NT, preferred_element_type=jnp.float32) + b_ref[...]
    u, cc, cb, z = (r[:, p * cw:(p + 1) * cw] for p in range(4))
    v = cc * u
    prev = halo_scr[...]
    halo_scr[...] = v[tm - SUBLANES:, :]
    row = lax.broadcasted_iota(jnp.int32, (tm, 1), 0)
    y = cw_ref[CONV_K - 1:CONV_K, :] * v
    for d in range(1, CONV_K):
        shifted = pltpu.roll(v, d, axis=0)
        for t in range(d):
            shifted = jnp.where(row == t, prev[SUBLANES - d + t:SUBLANES - d + t + 1, :], shifted)
        y = y + cw_ref[CONV_K - 1 - d:CONV_K - d, :] * shifted
    o_ref[...] = (cb * (y + cb_ref[...]) * _silu(z)).astype(o_ref.dtype)


def _proj_conv(a, wt, bias_tiles, conv_w, conv_b, start_fns, seq, tm=2048, cw=256):
    m, k = a.shape
    n_tiles = CONV_WIDTH // cw
    assert seq % tm == 0 and CONV_K - 1 <= SUBLANES and len(start_fns) == 4
    return pl.pallas_call(
        functools.partial(_proj_conv_kernel, seq=seq),
        grid=(n_tiles, m // tm),
        in_specs=[pl.BlockSpec((tm, k), lambda j, i: (i, 0)),
                  pl.BlockSpec((1, 4 * cw), lambda j, i: (0, j)),
                  pl.BlockSpec((CONV_K, cw), lambda j, i: (0, j)),
                  pl.BlockSpec((1, cw), lambda j, i: (0, j))] + _weight_specs(k, cw, start_fns),
        out_specs=pl.BlockSpec((tm, cw), lambda j, i: (i, j)),
        out_shape=jax.ShapeDtypeStruct((m, CONV_WIDTH), jnp.bfloat16),
        scratch_shapes=[pltpu.VMEM((4 * cw, k), jnp.bfloat16),
                        pltpu.VMEM((SUBLANES, cw), jnp.float32)],
        compiler_params=_params("arbitrary", "arbitrary"),
        name="proj_conv",
    )(a, bias_tiles, conv_w, conv_b.reshape(1, CONV_WIDTH), wt, wt, wt, wt)


def _compress_kernel(x_ref, pe_ref, w1_ref, w2_ref, o_ref):
    half = CMP_BLOCK // 2
    n_chunks = x_ref.shape[1] // CMP_STRIDE
    top = jnp.zeros((n_chunks, CMP_HIDDEN), jnp.float32)
    bot = jnp.zeros((n_chunks, CMP_HIDDEN), jnp.float32)
    for l in range(half):
        xl = x_ref[0, pl.ds(l, n_chunks, stride=CMP_STRIDE), :]
        a = (xl + pe_ref[l:l + 1, :]).astype(jnp.bfloat16)
        c = (xl + pe_ref[half + l:half + l + 1, :]).astype(jnp.bfloat16)
        top = top + jnp.dot(a, w1_ref[l], preferred_element_type=jnp.float32)
        bot = bot + jnp.dot(c, w1_ref[half + l], preferred_element_type=jnp.float32)
    h = top + pltpu.roll(bot, n_chunks - 1, axis=0)
    h = _silu(h).astype(jnp.bfloat16)
    o_ref[0, 0] = jnp.dot(h, w2_ref[...], preferred_element_type=jnp.float32).astype(o_ref.dtype)


def _compress(kv, col_block0, pe, w1, w2):
    b, s, _ = kv.shape
    return pl.pallas_call(
        _compress_kernel,
        grid=(b, N_KV_GROUPS),
        in_specs=[pl.BlockSpec((1, s, HEAD_DIM), lambda bi, g: (bi, 0, col_block0 + g)),
                  pl.BlockSpec((CMP_BLOCK, HEAD_DIM), lambda bi, g: (0, 0)),
                  pl.BlockSpec((CMP_BLOCK, HEAD_DIM, CMP_HIDDEN), lambda bi, g: (0, 0, 0)),
                  pl.BlockSpec((CMP_HIDDEN, HEAD_DIM), lambda bi, g: (0, 0))],
        out_specs=pl.BlockSpec((1, 1, s // CMP_STRIDE, HEAD_DIM), lambda bi, g: (bi, g, 0, 0)),
        out_shape=jax.ShapeDtypeStruct((b, N_KV_GROUPS, s // CMP_STRIDE, HEAD_DIM), jnp.bfloat16),
        compiler_params=_params("parallel", "parallel"),
        name="compress",
    )(kv, pe, w1, w2)


def _attn_kernel(q_ref, ks_ref, vs_ref, kw_ref, vw_ref, kc_ref, vc_ref, gate_ref, z_ref,
                 aggt_ref, blockid_ref, wbias_ref, gsel_ref, o_ref,
                 kaug_scr, vsaug_scr, vwaug_scr, vcaug_scr, s_scr, m_scr, acc_scr):
    tq = Q_TILE
    hpg = HEADS_PER_GROUP
    gw = GROUP_WIDTH
    seq = kw_ref.shape[1]
    gps = kc_ref.shape[1]
    i = pl.program_id(2)
    t0 = i * tq

    @pl.when(i == 0)
    def _():
        ones = jnp.ones((seq, LANES), jnp.bfloat16)
        for gi in range(gps):
            cols = slice(gi * HEAD_DIM, (gi + 1) * HEAD_DIM)
            kaug_scr[gi, :, :HEAD_DIM] = ks_ref[0, :, cols]
            kaug_scr[gi, :, HEAD_DIM:] = blockid_ref[...]
            vsaug_scr[gi, :, :HEAD_DIM] = vs_ref[0, :, cols]
            vsaug_scr[gi, :, HEAD_DIM:] = ones
            vwaug_scr[gi, :, :HEAD_DIM] = vw_ref[0, :, cols]
            vwaug_scr[gi, :, HEAD_DIM:] = ones
            vcaug_scr[gi, :, :HEAD_DIM] = vc_ref[0, gi]
            vcaug_scr[gi, :, HEAD_DIM:] = ones[:vcaug_scr.shape[1]]

    t_q = t0 + lax.broadcasted_iota(jnp.int32, (tq, 1), 0)
    t_s = jnp.concatenate([t_q] * hpg, axis=0)
    lane = lax.broadcasted_iota(jnp.int32, (1, LANES), 1)
    n_cmp = (seq - CMP_BLOCK) // CMP_STRIDE + 1
    n_slc = seq // SEL_BLOCK
    n_win = WINDOW + tq
    start = pl.multiple_of(jnp.maximum(t0 - WINDOW, 0), tq)
    w_bias = jnp.concatenate([wbias_ref[0]] * hpg, axis=0)
    vis_c = (lane * CMP_STRIDE + (CMP_BLOCK - 1) <= t_s) & (lane < n_cmp)
    aggt = aggt_ref[...]
    blk = lax.broadcasted_iota(jnp.int32, (n_slc, 1), 0)
    t_l = t0 + lax.broadcasted_iota(jnp.int32, (1, tq), 1)
    cur = t_l >> int(math.log2(SEL_BLOCK))
    causal = blk * SEL_BLOCK <= t_l
    forced = ((blk == 0) | (blk == cur) | (blk == cur - 1)) & causal
    gate = gate_ref[0]
    gate_hi = gate.astype(jnp.bfloat16)
    gate_lo = (gate - gate_hi.astype(jnp.float32)).astype(jnp.bfloat16)
    gate_hl = jnp.concatenate([gate_hi, gate_lo], axis=1)

    def first_region(gi):
        q = q_ref[0, :, gi * gw:(gi + 1) * gw]
        qs = jnp.concatenate([q[:, h * HEAD_DIM:(h + 1) * HEAD_DIM] for h in range(hpg)], axis=0)

        kw = kw_ref[0, pl.ds(start, n_win), gi * HEAD_DIM:(gi + 1) * HEAD_DIM]
        sw = lax.dot_general(qs, kw, _NT, preferred_element_type=jnp.float32) + w_bias
        p_w = jnp.exp2(sw - jnp.max(sw, axis=-1, keepdims=True))
        pv_w = jnp.dot(p_w.astype(jnp.bfloat16), vwaug_scr[gi, pl.ds(start, n_win), :],
                       preferred_element_type=jnp.float32)
        o_win = pv_w[:, :HEAD_DIM] * (1.0 / pv_w[:, HEAD_DIM:])

        sc = lax.dot_general(qs, kc_ref[0, gi], _NT, preferred_element_type=jnp.float32)
        sc = jnp.where(vis_c, sc, MASKED)
        p_c = jnp.where(vis_c, jnp.exp2(sc - jnp.max(sc, axis=-1, keepdims=True)), 0.0)
        pv_c = jnp.dot(p_c.astype(jnp.bfloat16), vcaug_scr[gi],
                       preferred_element_type=jnp.float32)
        l_c = pv_c[:, HEAD_DIM:]
        inv_c = jnp.where(l_c > 0.0, 1.0 / l_c, 0.0)
        o_cmp = pv_c[:, :HEAD_DIM] * inv_c
        p_c = p_c * inv_c

        p_sum = p_c[0:tq]
        for h in range(1, hpg):
            p_sum = p_sum + p_c[h * tq:(h + 1) * tq]
        hi = p_sum.astype(jnp.bfloat16)
        r1 = p_sum - hi.astype(jnp.float32)
        mid = r1.astype(jnp.bfloat16)
        lo = (r1 - mid.astype(jnp.float32)).astype(jnp.bfloat16)
        imp = (lax.dot_general(aggt, hi, _NT, preferred_element_type=jnp.float32)
               + lax.dot_general(aggt, mid, _NT, preferred_element_type=jnp.float32)
               + lax.dot_general(aggt, lo, _NT, preferred_element_type=jnp.float32))
        imp = jnp.where(forced, FORCED_SCORE, imp)
        imp = jnp.where(causal, imp, -1.0)

        bias_slabs = []
        for v in range(n_slc // SUBLANES):
            lo_row = v * SUBLANES
            slab = imp[lo_row:lo_row + SUBLANES]
            sub = lo_row + lax.broadcasted_iota(jnp.int32, (SUBLANES, 1), 0)
            rank = jnp.zeros((SUBLANES, tq), jnp.float32)
            for jp in range(n_slc):
                other = jnp.broadcast_to(imp[jp:jp + 1], (SUBLANES, tq))
                if jp < lo_row:
                    ahead = jnp.where(other >= slab, 1.0, 0.0)
                elif jp >= lo_row + SUBLANES:
                    ahead = jnp.where(other > slab, 1.0, 0.0)
                else:
                    tie = jnp.where(sub > jp, 1.0, 0.0)
                    ahead = jnp.where(other > slab, 1.0, jnp.where(other == slab, tie, 0.0))
                rank = rank + ahead
            bias_slabs.append(jnp.where(rank < float(SEL_TOP_N), 0.0, MASKED))
        bias_t = jnp.concatenate(bias_slabs + [jnp.zeros((LANES - n_slc, tq), jnp.float32)], axis=0)
        sel_bias = bias_t.T.astype(jnp.bfloat16)
        qa = jnp.concatenate([qs, jnp.concatenate([sel_bias] * hpg, axis=0)], axis=1)
        return o_win, o_cmp, qa

    first = [first_region(gi) for gi in range(gps)]

    n_full = t0 // SEL_CHUNK
    groups = [g for g in (4, 2, 1) if g <= max((seq - tq) // SEL_CHUNK, 1)]

    def scores(gi, c):
        base = pl.multiple_of(c * SEL_CHUNK, SEL_CHUNK)
        return base, lax.dot_general(first[gi][2], kaug_scr[gi, pl.ds(base, SEL_CHUNK), :], _NT,
                                     preferred_element_type=jnp.float32)

    def keep(gi, base, s, m):
        s_scr[gi, :, pl.ds(base, SEL_CHUNK)] = s
        for k in range(SEL_CHUNK // LANES):
            m = jnp.maximum(m, s[:, k * LANES:(k + 1) * LANES])
        return m

    for gi in range(gps):
        base_d, s_d = scores(gi, n_full)
        kpos_d = base_d + lax.broadcasted_iota(jnp.int32, (1, SEL_CHUNK), 1)
        m_scr[gi] = keep(gi, base_d, jnp.where(kpos_d <= t_s, s_d, MASKED),
                         jnp.full(m_scr.shape[1:], MASKED, jnp.float32))
    done = 0
    for g in groups:
        @pl.when((n_full & g) != 0)
        def _(done=done, g=g):
            for gi in range(gps):
                m = m_scr[gi]
                for u in range(g):
                    m = keep(gi, *scores(gi, done + u), m)
                m_scr[gi] = m
        done = done + (n_full & g)
    for gi in range(gps):
        m_scr[gi] = jnp.broadcast_to(jnp.max(m_scr[gi], axis=-1, keepdims=True), m_scr.shape[1:])

    def values(gi, c, m):
        base = pl.multiple_of(c * SEL_CHUNK, SEL_CHUNK)
        ps = [jnp.exp2(s_scr[gi, :, pl.ds(base + k * LANES, LANES)] - m).astype(jnp.bfloat16)
              for k in range(SEL_CHUNK // LANES)]
        return jnp.dot(jnp.concatenate(ps, axis=1), vsaug_scr[gi, pl.ds(base, SEL_CHUNK), :],
                       preferred_element_type=jnp.float32)

    for gi in range(gps):
        acc_scr[gi] = values(gi, n_full, m_scr[gi])
    done = 0
    for g in groups:
        @pl.when((n_full & g) != 0)
        def _(done=done, g=g):
            for gi in range(gps):
                m, acc = m_scr[gi], acc_scr[gi]
                for u in range(g):
                    acc = acc + values(gi, done + u, m)
                acc_scr[gi] = acc
        done = done + (n_full & g)

    for gi in range(gps):
        o_win, o_cmp, _ = first[gi]
        o_sel = acc_scr[gi, :, :HEAD_DIM] * (1.0 / acc_scr[gi, :, HEAD_DIM:])
        g_exp = jnp.dot(gate_hl, gsel_ref[gi], preferred_element_type=jnp.float32)
        g_col = lambda c: g_exp[:, c * LANES:(c + 1) * LANES]
        outs = []
        for h in range(hpg):
            r0 = h * tq
            outs.append(g_col(h) * o_cmp[r0:r0 + tq]
                        + g_col(hpg + h) * o_sel[r0:r0 + tq]
                        + g_col(2 * hpg + h) * o_win[r0:r0 + tq])
        o = jnp.concatenate(outs, axis=1)
        cols = slice(gi * gw, (gi + 1) * gw)
        o_ref[0, :, cols] = (o * _silu(z_ref[0, :, cols])).astype(o_ref.dtype)


def _attention(qkv, kcb, vcb, gate, zu, aggt, blockid, wbias, gsel):
    b, s, _ = qkv.shape
    gps = GROUPS_PER_STEP
    assert SEL_CHUNK % Q_TILE == 0 and WINDOW % Q_TILE == 0 and s % SEL_CHUNK == 0
    assert N_KV_GROUPS % gps == 0
    kvw = gps * HEAD_DIM
    qb = ATTN_WIDTH // kvw
    kvb = KV_WIDTH // kvw
    n_c = s // CMP_STRIDE
    rows = HEADS_PER_GROUP * Q_TILE
    n_pat = WINDOW // Q_TILE
    grid = (b, N_KV_GROUPS // gps, s // Q_TILE)
    kv_spec = lambda off: pl.BlockSpec((1, s, kvw), lambda bi, g, i: (bi, 0, off + g))
    wide = pl.BlockSpec((1, Q_TILE, gps * GROUP_WIDTH), lambda bi, g, i: (bi, i, g))
    return pl.pallas_call(
        _attn_kernel,
        grid=grid,
        in_specs=[wide,
                  kv_spec(qb), kv_spec(qb + kvb), kv_spec(qb + 2 * kvb), kv_spec(qb + 3 * kvb),
                  pl.BlockSpec((1, gps, n_c, HEAD_DIM), lambda bi, g, i: (bi, g, 0, 0)),
                  pl.BlockSpec((1, gps, n_c, HEAD_DIM), lambda bi, g, i: (bi, g, 0, 0)),
                  pl.BlockSpec((1, Q_TILE, LANES), lambda bi, g, i: (bi, i, 0)),
                  wide,
                  pl.BlockSpec(aggt.shape, lambda bi, g, i: (0, 0)),
                  pl.BlockSpec(blockid.shape, lambda bi, g, i: (0, 0)),
                  pl.BlockSpec((1, Q_TILE, WINDOW + Q_TILE),
                               lambda bi, g, i: (jnp.minimum(i, n_pat), 0, 0)),
                  pl.BlockSpec((gps,) + gsel.shape[1:], lambda bi, g, i: (g, 0, 0))],
        out_specs=wide,
        out_shape=jax.ShapeDtypeStruct((b, s, ATTN_WIDTH), jnp.bfloat16),
        scratch_shapes=[pltpu.VMEM((gps, s, HEAD_DIM + LANES), jnp.bfloat16),
                        pltpu.VMEM((gps, s, HEAD_DIM + LANES), jnp.bfloat16),
                        pltpu.VMEM((gps, s, HEAD_DIM + LANES), jnp.bfloat16),
                        pltpu.VMEM((gps, n_c, HEAD_DIM + LANES), jnp.bfloat16),
                        pltpu.VMEM((gps, rows, s), jnp.float32),
                        pltpu.VMEM((gps, rows, LANES), jnp.float32),
                        pltpu.VMEM((gps, rows, HEAD_DIM + LANES), jnp.float32)],
        compiler_params=_params("arbitrary", "arbitrary", "arbitrary"),
        name="nsa_attention",
    )(qkv, qkv, qkv, qkv, qkv, kcb, vcb, gate, zu, aggt, blockid, wbias, gsel)


def _merge_kernel(a_ref, c_ref, wa_ref, wc_ref, g0_ref, g1_ref, o_ref, wa_scr, wc_scr):
    @pl.when(pl.program_id(1) == 0)
    def _():
        _cast_weight(wa_scr, wa_ref)
        _cast_weight(wc_scr, wc_ref)

    ya = jnp.dot(a_ref[...], wa_scr[...], preferred_element_type=jnp.float32)
    yc = jnp.dot(c_ref[...], wc_scr[...], preferred_element_type=jnp.float32)
    o_ref[...] = (_sigmoid(g0_ref[...]) * ya + _sigmoid(g1_ref[...]) * yc).astype(o_ref.dtype)


def _merge(a, c, wa, wc, zg, gcol0, tm=1024, tn=512):
    m, k = a.shape
    n = wa.shape[-1]
    g0 = gcol0 // tn
    g1 = (gcol0 + n) // tn
    return pl.pallas_call(
        _merge_kernel,
        grid=(n // tn, m // tm),
        in_specs=[pl.BlockSpec((tm, k), lambda j, i: (i, 0)),
                  pl.BlockSpec((tm, k), lambda j, i: (i, 0)),
                  pl.BlockSpec((None, k, tn), lambda j, i: (0, 0, j)),
                  pl.BlockSpec((None, k, tn), lambda j, i: (0, 0, j)),
                  pl.BlockSpec((tm, tn), lambda j, i: (i, g0 + j)),
                  pl.BlockSpec((tm, tn), lambda j, i: (i, g1 + j))],
        out_specs=pl.BlockSpec((tm, tn), lambda j, i: (i, j)),
        out_shape=jax.ShapeDtypeStruct((m, n), jnp.bfloat16),
        scratch_shapes=[pltpu.VMEM((k, tn), jnp.bfloat16), pltpu.VMEM((k, tn), jnp.bfloat16)],
        compiler_params=_params("arbitrary", "arbitrary"),
        name="merge",
    )(a, c, wa, wc, zg, zg)


def _out_kernel(y_ref, w_ref, x_ref, g_ref, o_ref, w_scr):
    @pl.when(pl.program_id(0) == 0)
    def _():
        _cast_weight(w_scr, w_ref)

    r = x_ref[...] + jnp.dot(y_ref[...], w_scr[...], preferred_element_type=jnp.float32)
    ms = jnp.mean(r * r, axis=-1, keepdims=True)
    o_ref[...] = r * lax.rsqrt(ms + RMS_EPS) * g_ref[...]


def _out_proj(y, w, x2d, g, tm=512):
    m, k = y.shape
    n = w.shape[-1]
    return pl.pallas_call(
        _out_kernel,
        grid=(m // tm,),
        in_specs=[pl.BlockSpec((tm, k), lambda i: (i, 0)),
                  pl.BlockSpec((None, k, n), lambda i: (0, 0, 0), pipeline_mode=pl.Buffered(1)),
                  pl.BlockSpec((tm, n), lambda i: (i, 0)),
                  pl.BlockSpec((1, n), lambda i: (0, 0))],
        out_specs=pl.BlockSpec((tm, n), lambda i: (i, 0)),
        out_shape=jax.ShapeDtypeStruct((m, n), jnp.float32),
        scratch_shapes=[pltpu.VMEM((k, n), jnp.bfloat16)],
        compiler_params=_params("arbitrary"),
        name="out_proj",
    )(y, w, x2d, g.reshape(1, n))


def _attention_constants(s):
    n_slc = s // SEL_BLOCK
    n_cmp = (s - CMP_BLOCK) // CMP_STRIDE + 1
    c0 = np.arange(s // CMP_STRIDE)[None, :] * CMP_STRIDE
    s0 = np.arange(n_slc)[:, None] * SEL_BLOCK
    overlap = np.maximum(0, np.minimum(c0 + CMP_BLOCK, s0 + SEL_BLOCK) - np.maximum(c0, s0))
    aggt = (overlap / CMP_BLOCK) * (np.arange(s // CMP_STRIDE)[None, :] < n_cmp)
    blockid = (np.arange(s)[:, None] // SEL_BLOCK == np.arange(LANES)[None, :]).astype(np.float32)
    n_pat = WINDOW // Q_TILE
    pats = []
    for p in range(n_pat + 1):
        t = p * Q_TILE + np.arange(Q_TILE)[:, None]
        kpos = max(p * Q_TILE - WINDOW, 0) + np.arange(WINDOW + Q_TILE)[None, :]
        pats.append(np.where((kpos <= t) & (kpos > t - WINDOW), 0.0, MASKED))
    n_bh = 3 * HEADS_PER_GROUP
    gsel = np.zeros((N_KV_GROUPS, 2 * LANES, n_bh * LANES), np.float32)
    for g in range(N_KV_GROUPS):
        for br in range(3):
            for h in range(HEADS_PER_GROUP):
                src = br * N_HEADS + g * HEADS_PER_GROUP + h
                c = br * HEADS_PER_GROUP + h
                gsel[g, [src, LANES + src], c * LANES:(c + 1) * LANES] = 1.0
    return (jnp.asarray(aggt, jnp.bfloat16), jnp.asarray(blockid, jnp.bfloat16),
            jnp.asarray(np.stack(pats), jnp.float32), jnp.asarray(gsel, jnp.bfloat16))


def _layer(x, norm_g, w_in, b_in, pe_k, w1_k, w2_k, pe_v, w1_v, w2_v,
           conv_w, conv_b, p_attn, p_conv, w_o, out_g):
    b, s, d = x.shape
    m = b * s
    bf = jnp.bfloat16
    x2d = x.reshape(m, d)
    tn = 1024

    c_q = ATTN_WIDTH
    c_cmp = c_q + 2 * KV_WIDTH
    c_kv = c_cmp + 4 * KV_WIDTH
    n_gate = 3 * N_HEADS
    c_z = c_kv + n_gate
    c_conv = c_z + ATTN_WIDTH
    c_merge = c_conv + 4 * CONV_WIDTH
    assert c_cmp - c_q == tn and c_q % tn == 0 and (c_kv - c_cmp) % tn == 0
    wt = jnp.swapaxes(w_in, 1, 2).reshape(w_in.shape[2], w_in.shape[1])

    hn, kvc, gate = _norm_proj(x2d, norm_g, wt, jnp.concatenate([b_in[c_q:c_cmp], b_in[c_kv:c_kv + LANES]]),
                               c_q, c_cmp - c_q, c_kv)
    kvc = kvc.reshape(b, s, -1)
    q_tiles = c_q // tn
    qkv = _proj(hn, wt, jnp.concatenate([b_in[:c_q], b_in[c_cmp:c_kv]]),
                lambda j: jnp.where(j < q_tiles, j * tn, c_cmp + (j - q_tiles) * tn),
                q_tiles + (c_kv - c_cmp) // tn, tn, bf,
                scaled_tiles=q_tiles, scale=SCALE * math.log2(math.e), name="proj_qkv").reshape(b, s, -1)
    z_tiles = ATTN_WIDTH // tn
    zg = _proj(hn, wt, jnp.concatenate([b_in[c_z:c_conv], b_in[c_merge:]]),
               lambda j: jnp.where(j < z_tiles, c_z + j * tn, c_merge + (j - z_tiles) * tn),
               z_tiles + 2 * D_MODEL // tn, tn, jnp.float32, tm=1024, name="proj_zg").reshape(b, s, -1)
    cw = 256
    b_conv = b_in[c_conv:c_merge].reshape(4, CONV_WIDTH // cw, cw).transpose(1, 0, 2).reshape(1, -1)
    conv_starts = [lambda j, p=p: c_conv + p * CONV_WIDTH + j * cw for p in range(4)]
    c = _proj_conv(hn, wt, b_conv, conv_w, conv_b, conv_starts, s, cw=cw)

    kcb = _compress(kvc, 0, pe_k, w1_k.astype(bf).reshape(CMP_BLOCK, HEAD_DIM, CMP_HIDDEN), w2_k.astype(bf))
    vcb = _compress(kvc, N_KV_GROUPS, pe_v, w1_v.astype(bf).reshape(CMP_BLOCK, HEAD_DIM, CMP_HIDDEN),
                    w2_v.astype(bf))

    aggt, blockid, wbias, gsel = _attention_constants(s)
    a = _attention(qkv, kcb, vcb, gate.reshape(b, s, LANES), zg, aggt, blockid, wbias, gsel)

    y = _merge(a.reshape(m, -1), c, p_attn, p_conv, zg.reshape(m, -1), ATTN_WIDTH)
    return _out_proj(y, w_o, x2d, out_g).reshape(b, s, d)


def kernel(x, norm_g, w_in, b_in, cmp_pe_k, cmp_w1_k, cmp_w2_k, cmp_pe_v, cmp_w1_v, cmp_w2_v,
           conv_w, conv_b, p_attn, p_conv, w_o, final_g):
    assert norm_g.shape[0] == 1, "single-layer block"
    return _layer(x, norm_g[0], w_in, b_in[0], cmp_pe_k[0], cmp_w1_k[0], cmp_w2_k[0],
                  cmp_pe_v[0], cmp_w1_v[0], cmp_w2_v[0], conv_w[0], conv_b[0],
                  p_attn, p_conv, w_o, final_g)
```

```python
import functools
import math

import numpy as np
import jax
import jax.numpy as jnp
from jax import lax
from jax.experimental import pallas as pl
from jax.experimental.pallas import tpu as pltpu

D_MODEL = 2048
N_HEADS = 16
HEAD_DIM = 128
N_KV_GROUPS = 4
HEADS_PER_GROUP = N_HEADS // N_KV_GROUPS
ATTN_WIDTH = N_HEADS * HEAD_DIM
KV_WIDTH = N_KV_GROUPS * HEAD_DIM
GROUP_WIDTH = HEADS_PER_GROUP * HEAD_DIM
CMP_BLOCK = 32
CMP_STRIDE = 16
CMP_HIDDEN = 256
SEL_BLOCK = 64
SEL_TOP_N = 8
WINDOW = 512
SCALE = HEAD_DIM ** -0.5
CONV_WIDTH = D_MODEL
CONV_K = 3
RMS_EPS = 1e-6
FORCED_SCORE = 1e4
MASKED = -(2.0 ** 100)

LANES = 128
SUBLANES = 8
Q_TILE = 256
SEL_CHUNK = 256
GROUPS_PER_STEP = 2
W_ROWS = 512
W_ALIGN = 16
VMEM_LIMIT = 56 * 1024 * 1024

_NT = (((1,), (1,)), ((), ()))


def _params(*sem):
    return pltpu.CompilerParams(dimension_semantics=sem, vmem_limit_bytes=VMEM_LIMIT)


def _sigmoid(x):
    return 1.0 / (1.0 + jnp.exp(-x))


def _silu(x):
    return x * _sigmoid(x)


def _cast_weight(w_scr, w_ref, row0=0):
    rows = w_ref.shape[0]
    for r in range(0, rows, W_ROWS):
        n = min(W_ROWS, rows - r)
        w_scr[row0 + r:row0 + r + n, :] = w_ref[r:r + n, :].astype(w_scr.dtype)


def _weight_specs(k, piece_rows, start_fns):
    return [pl.BlockSpec((pl.Element(piece_rows), pl.Element(k)),
                         lambda j, i, f=f: (pl.multiple_of(f(j), W_ALIGN), 0))
            for f in start_fns]


def _proj_kernel(a_ref, b_ref, w_ref, o_ref, w_scr, *, act, scaled_tiles, scale):
    @pl.when(pl.program_id(1) == 0)
    def _():
        _cast_weight(w_scr, w_ref)

    r = lax.dot_general(a_ref[...], w_scr[...], _NT, preferred_element_type=jnp.float32) + b_ref[...]
    if scaled_tiles:
        r = r * jnp.where(pl.program_id(0) < scaled_tiles, scale, 1.0)
    if act == "sigmoid":
        r = _sigmoid(r)
    o_ref[...] = r.astype(o_ref.dtype)


def _proj(a, wt, bias, start_fn, n_tiles, tn, out_dtype, tm=2048, act=None,
          scaled_tiles=0, scale=1.0, name="proj"):
    m, k = a.shape
    return pl.pallas_call(
        functools.partial(_proj_kernel, act=act, scaled_tiles=scaled_tiles, scale=scale),
        grid=(n_tiles, m // tm),
        in_specs=[pl.BlockSpec((tm, k), lambda j, i: (i, 0)),
                  pl.BlockSpec((1, tn), lambda j, i: (0, j))] + _weight_specs(k, tn, [start_fn]),
        out_specs=pl.BlockSpec((tm, tn), lambda j, i: (i, j)),
        out_shape=jax.ShapeDtypeStruct((m, n_tiles * tn), out_dtype),
        scratch_shapes=[pltpu.VMEM((tn, k), jnp.bfloat16)],
        compiler_params=_params("arbitrary", "arbitrary"),
        name=name,
    )(a, bias.reshape(1, n_tiles * tn), wt)


def _norm_proj_kernel(x_ref, g_ref, b_ref, wkv_ref, wg_ref, hn_ref, kv_ref, gate_ref, w_scr):
    n_kv = kv_ref.shape[1]

    @pl.when(pl.program_id(0) == 0)
    def _():
        _cast_weight(w_scr, wkv_ref)
        _cast_weight(w_scr, wg_ref, n_kv)

    x = x_ref[...]
    ms = jnp.mean(x * x, axis=-1, keepdims=True)
    hn = (x * lax.rsqrt(ms + RMS_EPS) * g_ref[...]).astype(hn_ref.dtype)
    hn_ref[...] = hn
    r = lax.dot_general(hn, w_scr[...], _NT, preferred_element_type=jnp.float32) + b_ref[...]
    kv_ref[...] = r[:, :n_kv]
    gate_ref[...] = _sigmoid(r[:, n_kv:])


def _norm_proj(x2d, g, wt, bias, kv_start, n_kv, gate_start, tm=1024):
    m, k = x2d.shape
    once = dict(pipeline_mode=pl.Buffered(1))
    return pl.pallas_call(
        _norm_proj_kernel,
        grid=(m // tm,),
        in_specs=[pl.BlockSpec((tm, k), lambda i: (i, 0)),
                  pl.BlockSpec((1, k), lambda i: (0, 0)),
                  pl.BlockSpec((1, n_kv + LANES), lambda i: (0, 0)),
                  pl.BlockSpec((pl.Element(n_kv), pl.Element(k)), lambda i: (kv_start, 0), **once),
                  pl.BlockSpec((pl.Element(LANES), pl.Element(k)), lambda i: (gate_start, 0), **once)],
        out_specs=[pl.BlockSpec((tm, k), lambda i: (i, 0)),
                   pl.BlockSpec((tm, n_kv), lambda i: (i, 0)),
                   pl.BlockSpec((tm, LANES), lambda i: (i, 0))],
        out_shape=[jax.ShapeDtypeStruct((m, k), jnp.bfloat16),
                   jax.ShapeDtypeStruct((m, n_kv), jnp.float32),
                   jax.ShapeDtypeStruct((m, LANES), jnp.float32)],
        scratch_shapes=[pltpu.VMEM((n_kv + LANES, k), jnp.bfloat16)],
        compiler_params=_params("arbitrary"),
        name="norm_proj",
    )(x2d, g.reshape(1, k), bias.reshape(1, n_kv + LANES), wt, wt)


def _proj_conv_kernel(a_ref, b_ref, cw_ref, cb_ref, wu_ref, wcc_ref, wcb_ref, wz_ref,
                      o_ref, w_scr, halo_scr, *, seq):
    tm, cw = o_ref.shape
    i = pl.program_id(1)

    @pl.when(i == 0)
    def _():
        for p, w_ref in enumerate((wu_ref, wcc_ref, wcb_ref, wz_ref)):
            _cast_weight(w_scr, w_ref, p * cw)

    @pl.when((i * tm) % seq == 0)
    def _():
        halo_scr[...] = jnp.zeros(halo_scr.shape, jnp.float32)

    r = lax.dot_general(a_ref[...], w_scr[...], _NT, preferred_element_type=jnp.float32) + b_ref[...]
    u, cc, cb, z = (r[:, p * cw:(p + 1) * cw] for p in range(4))
    v = cc * u
    prev = halo_scr[...]
    halo_scr[...] = v[tm - SUBLANES:, :]
    row = lax.broadcasted_iota(jnp.int32, (tm, 1), 0)
    y = cw_ref[CONV_K - 1:CONV_K, :] * v
    for d in range(1, CONV_K):
        shifted = pltpu.roll(v, d, axis=0)
        for t in range(d):
            shifted = jnp.where(row == t, prev[SUBLANES - d + t:SUBLANES - d + t + 1, :], shifted)
        y = y + cw_ref[CONV_K - 1 - d:CONV_K - d, :] * shifted
    o_ref[...] = (cb * (y + cb_ref[...]) * _silu(z)).astype(o_ref.dtype)


def _proj_conv(a, wt, bias_tiles, conv_w, conv_b, start_fns, seq, tm=2048, cw=256):
    m, k = a.shape
    n_tiles = CONV_WIDTH // cw
    assert seq % tm == 0 and CONV_K - 1 <= SUBLANES and len(start_fns) == 4
    return pl.pallas_call(
        functools.partial(_proj_conv_kernel, seq=seq),
        grid=(n_tiles, m // tm),
        in_specs=[pl.BlockSpec((tm, k), lambda j, i: (i, 0)),
                  pl.BlockSpec((1, 4 * cw), lambda j, i: (0, j)),
                  pl.BlockSpec((CONV_K, cw), lambda j, i: (0, j)),
                  pl.BlockSpec((1, cw), lambda j, i: (0, j))] + _weight_specs(k, cw, start_fns),
        out_specs=pl.BlockSpec((tm, cw), lambda j, i: (i, j)),
        out_shape=jax.ShapeDtypeStruct((m, CONV_WIDTH), jnp.bfloat16),
        scratch_shapes=[pltpu.VMEM((4 * cw, k), jnp.bfloat16),
                        pltpu.VMEM((SUBLANES, cw), jnp.float32)],
        compiler_params=_params("arbitrary", "arbitrary"),
        name="proj_conv",
    )(a, bias_tiles, conv_w, conv_b.reshape(1, CONV_WIDTH), wt, wt, wt, wt)


def _compress_kernel(x_ref, pek_ref, w1k_ref, w2k_ref, pev_ref, w1v_ref, w2v_ref, o_ref,
                     pe_scr, w1_scr, w2_scr):
    first = (pl.program_id(1) == 0) & (pl.program_id(2) == 0)
    for which, (pe_ref, w1_ref, w2_ref) in enumerate(((pek_ref, w1k_ref, w2k_ref),
                                                      (pev_ref, w1v_ref, w2v_ref))):
        @pl.when(first & (pl.program_id(0) == which))
        def _(pe_ref=pe_ref, w1_ref=w1_ref, w2_ref=w2_ref):
            pe_scr[...] = pe_ref[...]
            for l in range(CMP_BLOCK):
                w1_scr[l] = w1_ref[l].astype(w1_scr.dtype)
            w2_scr[...] = w2_ref[...].astype(w2_scr.dtype)

    half = CMP_BLOCK // 2
    n_chunks = x_ref.shape[1] // CMP_STRIDE
    top = jnp.zeros((n_chunks, CMP_HIDDEN), jnp.float32)
    bot = jnp.zeros((n_chunks, CMP_HIDDEN), jnp.float32)
    for l in range(half):
        xl = x_ref[0, pl.ds(l, n_chunks, stride=CMP_STRIDE), :]
        a = (xl + pe_scr[l:l + 1, :]).astype(jnp.bfloat16)
        c = (xl + pe_scr[half + l:half + l + 1, :]).astype(jnp.bfloat16)
        top = top + jnp.dot(a, w1_scr[l], preferred_element_type=jnp.float32)
        bot = bot + jnp.dot(c, w1_scr[half + l], preferred_element_type=jnp.float32)
    h = top + pltpu.roll(bot, n_chunks - 1, axis=0)
    h = _silu(h).astype(jnp.bfloat16)
    o_ref[0, 0, 0] = jnp.dot(h, w2_scr[...], preferred_element_type=jnp.float32).astype(o_ref.dtype)


def _compress(kv, pe_k, w1_k, w2_k, pe_v, w1_v, w2_v):
    b, s, _ = kv.shape
    whole = lambda shape: pl.BlockSpec(shape, lambda t, bi, g: (0,) * len(shape))
    w1_shape = (CMP_BLOCK, HEAD_DIM, CMP_HIDDEN)
    return pl.pallas_call(
        _compress_kernel,
        grid=(2, b, N_KV_GROUPS),
        in_specs=[pl.BlockSpec((1, s, HEAD_DIM), lambda t, bi, g: (bi, 0, t * N_KV_GROUPS + g)),
                  whole((CMP_BLOCK, HEAD_DIM)), whole(w1_shape), whole((CMP_HIDDEN, HEAD_DIM)),
                  whole((CMP_BLOCK, HEAD_DIM)), whole(w1_shape), whole((CMP_HIDDEN, HEAD_DIM))],
        out_specs=pl.BlockSpec((1, 1, 1, s // CMP_STRIDE, HEAD_DIM), lambda t, bi, g: (t, bi, g, 0, 0)),
        out_shape=jax.ShapeDtypeStruct((2, b, N_KV_GROUPS, s // CMP_STRIDE, HEAD_DIM), jnp.bfloat16),
        scratch_shapes=[pltpu.VMEM((CMP_BLOCK, HEAD_DIM), jnp.float32),
                        pltpu.VMEM(w1_shape, jnp.bfloat16),
                        pltpu.VMEM((CMP_HIDDEN, HEAD_DIM), jnp.bfloat16)],
        compiler_params=_params("arbitrary", "arbitrary", "arbitrary"),
        name="compress",
    )(kv, pe_k, w1_k.reshape(w1_shape), w2_k, pe_v, w1_v.reshape(w1_shape), w2_v)


def _attn_kernel(q_ref, ks_ref, vs_ref, kw_ref, vw_ref, kc_ref, vc_ref, gate_ref, z_ref,
                 aggt_ref, blockid_ref, wbias_ref, gsel_ref, o_ref,
                 kaug_scr, vsaug_scr, vwaug_scr, vcaug_scr, s_scr, m_scr, acc_scr):
    tq = Q_TILE
    hpg = HEADS_PER_GROUP
    gw = GROUP_WIDTH
    seq = kw_ref.shape[1]
    gps = kc_ref.shape[1]
    i = pl.program_id(2)
    t0 = i * tq

    @pl.when(i == 0)
    def _():
        ones = jnp.ones((seq, LANES), jnp.bfloat16)
        for gi in range(gps):
            cols = slice(gi * HEAD_DIM, (gi + 1) * HEAD_DIM)
            kaug_scr[gi, :, :HEAD_DIM] = ks_ref[0, :, cols]
            kaug_scr[gi, :, HEAD_DIM:] = blockid_ref[...]
            vsaug_scr[gi, :, :HEAD_DIM] = vs_ref[0, :, cols]
            vsaug_scr[gi, :, HEAD_DIM:] = ones
            vwaug_scr[gi, :, :HEAD_DIM] = vw_ref[0, :, cols]
            vwaug_scr[gi, :, HEAD_DIM:] = ones
            vcaug_scr[gi, :, :HEAD_DIM] = vc_ref[0, gi]
            vcaug_scr[gi, :, HEAD_DIM:] = ones[:vcaug_scr.shape[1]]

    t_q = t0 + lax.broadcasted_iota(jnp.int32, (tq, 1), 0)
    t_s = jnp.concatenate([t_q] * hpg, axis=0)
    lane = lax.broadcasted_iota(jnp.int32, (1, LANES), 1)
    n_cmp = (seq - CMP_BLOCK) // CMP_STRIDE + 1
    n_slc = seq // SEL_BLOCK
    n_win = WINDOW + tq
    start = pl.multiple_of(jnp.maximum(t0 - WINDOW, 0), tq)
    w_bias = jnp.concatenate([wbias_ref[0]] * hpg, axis=0)
    vis_c = (lane * CMP_STRIDE + (CMP_BLOCK - 1) <= t_s) & (lane < n_cmp)
    aggt = aggt_ref[...]
    blk = lax.broadcasted_iota(jnp.int32, (n_slc, 1), 0)
    t_l = t0 + lax.broadcasted_iota(jnp.int32, (1, tq), 1)
    cur = t_l >> int(math.log2(SEL_BLOCK))
    causal = blk * SEL_BLOCK <= t_l
    forced = ((blk == 0) | (blk == cur) | (blk == cur - 1)) & causal
    gate = gate_ref[0]
    gate_hi = gate.astype(jnp.bfloat16)
    gate_lo = (gate - gate_hi.astype(jnp.float32)).astype(jnp.bfloat16)
    gate_hl = jnp.concatenate([gate_hi, gate_lo], axis=1)

    def first_region(gi):
        q = q_ref[0, :, gi * gw:(gi + 1) * gw]
        qs = jnp.concatenate([q[:, h * HEAD_DIM:(h + 1) * HEAD_DIM] for h in range(hpg)], axis=0)

        kw = kw_ref[0, pl.ds(start, n_win), gi * HEAD_DIM:(gi + 1) * HEAD_DIM]
        sw = lax.dot_general(qs, kw, _NT, preferred_element_type=jnp.float32) + w_bias
        p_w = jnp.exp2(sw - jnp.max(sw, axis=-1, keepdims=True))
        pv_w = jnp.dot(p_w.astype(jnp.bfloat16), vwaug_scr[gi, pl.ds(start, n_win), :],
                       preferred_element_type=jnp.float32)
        o_win = pv_w[:, :HEAD_DIM] * (1.0 / pv_w[:, HEAD_DIM:])

        sc = lax.dot_general(qs, kc_ref[0, gi], _NT, preferred_element_type=jnp.float32)
        sc = jnp.where(vis_c, sc, MASKED)
        p_c = jnp.where(vis_c, jnp.exp2(sc - jnp.max(sc, axis=-1, keepdims=True)), 0.0)
        pv_c = jnp.dot(p_c.astype(jnp.bfloat16), vcaug_scr[gi],
                       preferred_element_type=jnp.float32)
        l_c = pv_c[:, HEAD_DIM:]
        inv_c = jnp.where(l_c > 0.0, 1.0 / l_c, 0.0)
        o_cmp = pv_c[:, :HEAD_DIM] * inv_c
        p_c = p_c * inv_c

        p_sum = p_c[0:tq]
        for h in range(1, hpg):
            p_sum = p_sum + p_c[h * tq:(h + 1) * tq]
        hi = p_sum.astype(jnp.bfloat16)
        r1 = p_sum - hi.astype(jnp.float32)
        mid = r1.astype(jnp.bfloat16)
        lo = (r1 - mid.astype(jnp.float32)).astype(jnp.bfloat16)
        imp = (lax.dot_general(aggt, hi, _NT, preferred_element_type=jnp.float32)
               + lax.dot_general(aggt, mid, _NT, preferred_element_type=jnp.float32)
               + lax.dot_general(aggt, lo, _NT, preferred_element_type=jnp.float32))
        imp = jnp.where(forced, FORCED_SCORE, imp)
        imp = jnp.where(causal, imp, -1.0)

        bias_slabs = []
        for v in range(n_slc // SUBLANES):
            lo_row = v * SUBLANES
            slab = imp[lo_row:lo_row + SUBLANES]
            sub = lo_row + lax.broadcasted_iota(jnp.int32, (SUBLANES, 1), 0)
            rank = jnp.zeros((SUBLANES, tq), jnp.float32)
            for jp in range(n_slc):
                other = jnp.broadcast_to(imp[jp:jp + 1], (SUBLANES, tq))
                if jp < lo_row:
                    ahead = jnp.where(other >= slab, 1.0, 0.0)
                elif jp >= lo_row + SUBLANES:
                    ahead = jnp.where(other > slab, 1.0, 0.0)
                else:
                    tie = jnp.where(sub > jp, 1.0, 0.0)
                    ahead = jnp.where(other > slab, 1.0, jnp.where(other == slab, tie, 0.0))
                rank = rank + ahead
            bias_slabs.append(jnp.where(rank < float(SEL_TOP_N), 0.0, MASKED))
        bias_t = jnp.concatenate(bias_slabs + [jnp.zeros((LANES - n_slc, tq), jnp.float32)], axis=0)
        sel_bias = bias_t.T.astype(jnp.bfloat16)
        qa = jnp.concatenate([qs, jnp.concatenate([sel_bias] * hpg, axis=0)], axis=1)
        return o_win, o_cmp, qa

    first = [first_region(gi) for gi in range(gps)]

    n_full = t0 // SEL_CHUNK
    groups = [g for g in (4, 2, 1) if g <= max((seq - tq) // SEL_CHUNK, 1)]

    def scores(gi, c):
        base = pl.multiple_of(c * SEL_CHUNK, SEL_CHUNK)
        return base, lax.dot_general(first[gi][2], kaug_scr[gi, pl.ds(base, SEL_CHUNK), :], _NT,
                                     preferred_element_type=jnp.float32)

    def keep(gi, base, s, m):
        s_scr[gi, :, pl.ds(base, SEL_CHUNK)] = s
        for k in range(SEL_CHUNK // LANES):
            m = jnp.maximum(m, s[:, k * LANES:(k + 1) * LANES])
        return m

    for gi in range(gps):
        base_d, s_d = scores(gi, n_full)
        kpos_d = base_d + lax.broadcasted_iota(jnp.int32, (1, SEL_CHUNK), 1)
        m_scr[gi] = keep(gi, base_d, jnp.where(kpos_d <= t_s, s_d, MASKED),
                         jnp.full(m_scr.shape[1:], MASKED, jnp.float32))
    done = 0
    for g in groups:
        @pl.when((n_full & g) != 0)
        def _(done=done, g=g):
            for gi in range(gps):
                m = m_scr[gi]
                for u in range(g):
                    m = keep(gi, *scores(gi, done + u), m)
                m_scr[gi] = m
        done = done + (n_full & g)
    for gi in range(gps):
        m_scr[gi] = jnp.broadcast_to(jnp.max(m_scr[gi], axis=-1, keepdims=True), m_scr.shape[1:])

    def values(gi, c, m):
        base = pl.multiple_of(c * SEL_CHUNK, SEL_CHUNK)
        ps = [jnp.exp2(s_scr[gi, :, pl.ds(base + k * LANES, LANES)] - m).astype(jnp.bfloat16)
              for k in range(SEL_CHUNK // LANES)]
        return jnp.dot(jnp.concatenate(ps, axis=1), vsaug_scr[gi, pl.ds(base, SEL_CHUNK), :],
                       preferred_element_type=jnp.float32)

    for gi in range(gps):
        acc_scr[gi] = values(gi, n_full, m_scr[gi])
    done = 0
    for g in groups:
        @pl.when((n_full & g) != 0)
        def _(done=done, g=g):
            for gi in range(gps):
                m, acc = m_scr[gi], acc_scr[gi]
                for u in range(g):
                    acc = acc + values(gi, done + u, m)
                acc_scr[gi] = acc
        done = done + (n_full & g)

    for gi in range(gps):
        o_win, o_cmp, _ = first[gi]
        o_sel = acc_scr[gi, :, :HEAD_DIM] * (1.0 / acc_scr[gi, :, HEAD_DIM:])
        g_exp = jnp.dot(gate_hl, gsel_ref[gi], preferred_element_type=jnp.float32)
        g_col = lambda c: g_exp[:, c * LANES:(c + 1) * LANES]
        outs = []
        for h in range(hpg):
            r0 = h * tq
            outs.append(g_col(h) * o_cmp[r0:r0 + tq]
                        + g_col(hpg + h) * o_sel[r0:r0 + tq]
                        + g_col(2 * hpg + h) * o_win[r0:r0 + tq])
        o = jnp.concatenate(outs, axis=1)
        cols = slice(gi * gw, (gi + 1) * gw)
        o_ref[0, :, cols] = (o * _silu(z_ref[0, :, cols])).astype(o_ref.dtype)


def _attention(qkv, kcb, vcb, gate, zu, aggt, blockid, wbias, gsel):
    b, s, _ = qkv.shape
    gps = GROUPS_PER_STEP
    assert SEL_CHUNK % Q_TILE == 0 and WINDOW % Q_TILE == 0 and s % SEL_CHUNK == 0
    assert N_KV_GROUPS % gps == 0
    kvw = gps * HEAD_DIM
    qb = ATTN_WIDTH // kvw
    kvb = KV_WIDTH // kvw
    n_c = s // CMP_STRIDE
    rows = HEADS_PER_GROUP * Q_TILE
    n_pat = WINDOW // Q_TILE
    grid = (b, N_KV_GROUPS // gps, s // Q_TILE)
    kv_spec = lambda off: pl.BlockSpec((1, s, kvw), lambda bi, g, i: (bi, 0, off + g))
    wide = pl.BlockSpec((1, Q_TILE, gps * GROUP_WIDTH), lambda bi, g, i: (bi, i, g))
    return pl.pallas_call(
        _attn_kernel,
        grid=grid,
        in_specs=[wide,
                  kv_spec(qb), kv_spec(qb + kvb), kv_spec(qb + 2 * kvb), kv_spec(qb + 3 * kvb),
                  pl.BlockSpec((1, gps, n_c, HEAD_DIM), lambda bi, g, i: (bi, g, 0, 0)),
                  pl.BlockSpec((1, gps, n_c, HEAD_DIM), lambda bi, g, i: (bi, g, 0, 0)),
                  pl.BlockSpec((1, Q_TILE, LANES), lambda bi, g, i: (bi, i, 0)),
                  wide,
                  pl.BlockSpec(aggt.shape, lambda bi, g, i: (0, 0)),
                  pl.BlockSpec(blockid.shape, lambda bi, g, i: (0, 0)),
                  pl.BlockSpec((1, Q_TILE, WINDOW + Q_TILE),
                               lambda bi, g, i: (jnp.minimum(i, n_pat), 0, 0)),
                  pl.BlockSpec((gps,) + gsel.shape[1:], lambda bi, g, i: (g, 0, 0))],
        out_specs=wide,
        out_shape=jax.ShapeDtypeStruct((b, s, ATTN_WIDTH), jnp.bfloat16),
        scratch_shapes=[pltpu.VMEM((gps, s, HEAD_DIM + LANES), jnp.bfloat16),
                        pltpu.VMEM((gps, s, HEAD_DIM + LANES), jnp.bfloat16),
                        pltpu.VMEM((gps, s, HEAD_DIM + LANES), jnp.bfloat16),
                        pltpu.VMEM((gps, n_c, HEAD_DIM + LANES), jnp.bfloat16),
                        pltpu.VMEM((gps, rows, s), jnp.float32),
                        pltpu.VMEM((gps, rows, LANES), jnp.float32),
                        pltpu.VMEM((gps, rows, HEAD_DIM + LANES), jnp.float32)],
        compiler_params=_params("arbitrary", "arbitrary", "arbitrary"),
        name="nsa_attention",
    )(qkv, qkv, qkv, qkv, qkv, kcb, vcb, gate, zu, aggt, blockid, wbias, gsel)


def _merge_kernel(a_ref, c_ref, wa_ref, wc_ref, g0_ref, g1_ref, o_ref, wa_scr, wc_scr):
    @pl.when(pl.program_id(1) == 0)
    def _():
        _cast_weight(wa_scr, wa_ref)
        _cast_weight(wc_scr, wc_ref)

    ya = jnp.dot(a_ref[...], wa_scr[...], preferred_element_type=jnp.float32)
    yc = jnp.dot(c_ref[...], wc_scr[...], preferred_element_type=jnp.float32)
    o_ref[...] = (_sigmoid(g0_ref[...]) * ya + _sigmoid(g1_ref[...]) * yc).astype(o_ref.dtype)


def _merge(a, c, wa, wc, zg, gcol0, tm=512, tn=1024):
    m, k = a.shape
    n = wa.shape[-1]
    g0 = gcol0 // tn
    g1 = (gcol0 + n) // tn
    return pl.pallas_call(
        _merge_kernel,
        grid=(n // tn, m // tm),
        in_specs=[pl.BlockSpec((tm, k), lambda j, i: (i, 0)),
                  pl.BlockSpec((tm, k), lambda j, i: (i, 0)),
                  pl.BlockSpec((None, k, tn), lambda j, i: (0, 0, j), pipeline_mode=pl.Buffered(1)),
                  pl.BlockSpec((None, k, tn), lambda j, i: (0, 0, j), pipeline_mode=pl.Buffered(1)),
                  pl.BlockSpec((tm, tn), lambda j, i: (i, g0 + j)),
                  pl.BlockSpec((tm, tn), lambda j, i: (i, g1 + j))],
        out_specs=pl.BlockSpec((tm, tn), lambda j, i: (i, j)),
        out_shape=jax.ShapeDtypeStruct((m, n), jnp.bfloat16),
        scratch_shapes=[pltpu.VMEM((k, tn), jnp.bfloat16), pltpu.VMEM((k, tn), jnp.bfloat16)],
        compiler_params=_params("arbitrary", "arbitrary"),
        name="merge",
    )(a, c, wa, wc, zg, zg)


def _out_kernel(y_ref, w_ref, x_ref, g_ref, o_ref, w_scr):
    @pl.when(pl.program_id(0) == 0)
    def _():
        _cast_weight(w_scr, w_ref)

    r = x_ref[...] + jnp.dot(y_ref[...], w_scr[...], preferred_element_type=jnp.float32)
    ms = jnp.mean(r * r, axis=-1, keepdims=True)
    o_ref[...] = r * lax.rsqrt(ms + RMS_EPS) * g_ref[...]


def _out_proj(y, w, x2d, g, tm=512):
    m, k = y.shape
    n = w.shape[-1]
    return pl.pallas_call(
        _out_kernel,
        grid=(m // tm,),
        in_specs=[pl.BlockSpec((tm, k), lambda i: (i, 0)),
                  pl.BlockSpec((None, k, n), lambda i: (0, 0, 0), pipeline_mode=pl.Buffered(1)),
                  pl.BlockSpec((tm, n), lambda i: (i, 0)),
                  pl.BlockSpec((1, n), lambda i: (0, 0))],
        out_specs=pl.BlockSpec((tm, n), lambda i: (i, 0)),
        out_shape=jax.ShapeDtypeStruct((m, n), jnp.float32),
        scratch_shapes=[pltpu.VMEM((k, n), jnp.bfloat16)],
        compiler_params=_params("arbitrary"),
        name="out_proj",
    )(y, w, x2d, g.reshape(1, n))


def _attention_constants(s):
    n_slc = s // SEL_BLOCK
    n_cmp = (s - CMP_BLOCK) // CMP_STRIDE + 1
    c0 = np.arange(s // CMP_STRIDE)[None, :] * CMP_STRIDE
    s0 = np.arange(n_slc)[:, None] * SEL_BLOCK
    overlap = np.maximum(0, np.minimum(c0 + CMP_BLOCK, s0 + SEL_BLOCK) - np.maximum(c0, s0))
    aggt = (overlap / CMP_BLOCK) * (np.arange(s // CMP_STRIDE)[None, :] < n_cmp)
    blockid = (np.arange(s)[:, None] // SEL_BLOCK == np.arange(LANES)[None, :]).astype(np.float32)
    n_pat = WINDOW // Q_TILE
    pats = []
    for p in range(n_pat + 1):
        t = p * Q_TILE + np.arange(Q_TILE)[:, None]
        kpos = max(p * Q_TILE - WINDOW, 0) + np.arange(WINDOW + Q_TILE)[None, :]
        pats.append(np.where((kpos <= t) & (kpos > t - WINDOW), 0.0, MASKED))
    n_bh = 3 * HEADS_PER_GROUP
    gsel = np.zeros((N_KV_GROUPS, 2 * LANES, n_bh * LANES), np.float32)
    for g in range(N_KV_GROUPS):
        for br in range(3):
            for h in range(HEADS_PER_GROUP):
                src = br * N_HEADS + g * HEADS_PER_GROUP + h
                c = br * HEADS_PER_GROUP + h
                gsel[g, [src, LANES + src], c * LANES:(c + 1) * LANES] = 1.0
    return (jnp.asarray(aggt, jnp.bfloat16), jnp.asarray(blockid, jnp.bfloat16),
            jnp.asarray(np.stack(pats), jnp.float32), jnp.asarray(gsel, jnp.bfloat16))


def _layer(x, norm_g, w_in, b_in, pe_k, w1_k, w2_k, pe_v, w1_v, w2_v,
           conv_w, conv_b, p_attn, p_conv, w_o, out_g):
    b, s, d = x.shape
    m = b * s
    bf = jnp.bfloat16
    x2d = x.reshape(m, d)
    tn = 1024

    c_q = ATTN_WIDTH
    c_cmp = c_q + 2 * KV_WIDTH
    c_kv = c_cmp + 4 * KV_WIDTH
    n_gate = 3 * N_HEADS
    c_z = c_kv + n_gate
    c_conv = c_z + ATTN_WIDTH
    c_merge = c_conv + 4 * CONV_WIDTH
    assert c_cmp - c_q == tn and c_q % tn == 0 and (c_kv - c_cmp) % tn == 0
    wt = jnp.swapaxes(w_in, 1, 2).reshape(w_in.shape[2], w_in.shape[1])

    hn, kvc, gate = _norm_proj(x2d, norm_g, wt, jnp.concatenate([b_in[c_q:c_cmp], b_in[c_kv:c_kv + LANES]]),
                               c_q, c_cmp - c_q, c_kv)
    kvc = kvc.reshape(b, s, -1)
    q_tiles = c_q // tn
    qkv = _proj(hn, wt, jnp.concatenate([b_in[:c_q], b_in[c_cmp:c_kv]]),
                lambda j: jnp.where(j < q_tiles, j * tn, c_cmp + (j - q_tiles) * tn),
                q_tiles + (c_kv - c_cmp) // tn, tn, bf,
                scaled_tiles=q_tiles, scale=SCALE * math.log2(math.e), name="proj_qkv").reshape(b, s, -1)
    z_tiles = ATTN_WIDTH // tn
    zg = _proj(hn, wt, jnp.concatenate([b_in[c_z:c_conv], b_in[c_merge:]]),
               lambda j: jnp.where(j < z_tiles, c_z + j * tn, c_merge + (j - z_tiles) * tn),
               z_tiles + 2 * D_MODEL // tn, tn, jnp.float32, tm=1024, name="proj_zg").reshape(b, s, -1)
    cw = 256
    b_conv = b_in[c_conv:c_merge].reshape(4, CONV_WIDTH // cw, cw).transpose(1, 0, 2).reshape(1, -1)
    conv_starts = [lambda j, p=p: c_conv + p * CONV_WIDTH + j * cw for p in range(4)]
    c = _proj_conv(hn, wt, b_conv, conv_w, conv_b, conv_starts, s, cw=cw)

    kvcb = _compress(kvc, pe_k, w1_k, w2_k, pe_v, w1_v, w2_v)
    kcb, vcb = kvcb[0], kvcb[1]

    aggt, blockid, wbias, gsel = _attention_constants(s)
    a = _attention(qkv, kcb, vcb, gate.reshape(b, s, LANES), zg, aggt, blockid, wbias, gsel)

    y = _merge(a.reshape(m, -1), c, p_attn, p_conv, zg.reshape(m, -1), ATTN_WIDTH)
    return _out_proj(y, w_o, x2d, out_g).reshape(b, s, d)


def kernel(x, norm_g, w_in, b_in, cmp_pe_k, cmp_w1_k, cmp_w2_k, cmp_pe_v, cmp_w1_v, cmp_w2_v,
           conv_w, conv_b, p_attn, p_conv, w_o, final_g):
    assert norm_g.shape[0] == 1, "single-layer block"
    return _layer(x, norm_g[0], w_in, b_in[0], cmp_pe_k[0], cmp_w1_k[0], cmp_w2_k[0],
                  cmp_pe_v[0], cmp_w1_v[0], cmp_w2_v[0], conv_w[0], conv_b[0],
                  p_attn, p_conv, w_o, final_g)
```

```python
import functools
import math

import numpy as np
import jax
import jax.numpy as jnp
from jax import lax
from jax.experimental import pallas as pl
from jax.experimental.pallas import tpu as pltpu

D_MODEL = 2048
N_HEADS = 16
HEAD_DIM = 128
N_KV_GROUPS = 4
HEADS_PER_GROUP = N_HEADS // N_KV_GROUPS
ATTN_WIDTH = N_HEADS * HEAD_DIM
KV_WIDTH = N_KV_GROUPS * HEAD_DIM
GROUP_WIDTH = HEADS_PER_GROUP * HEAD_DIM
CMP_BLOCK = 32
CMP_STRIDE = 16
CMP_HIDDEN = 256
SEL_BLOCK = 64
SEL_TOP_N = 8
WINDOW = 512
SCALE = HEAD_DIM ** -0.5
CONV_WIDTH = D_MODEL
CONV_K = 3
RMS_EPS = 1e-6
FORCED_SCORE = 1e4
MASKED = -(2.0 ** 100)

LANES = 128
SUBLANES = 8
Q_TILE = 256
SEL_CHUNK = 256
GROUPS_PER_STEP = 2
W_ROWS = 512
W_ALIGN = 16
VMEM_LIMIT = 56 * 1024 * 1024

_NT = (((1,), (1,)), ((), ()))


def _params(*sem):
    return pltpu.CompilerParams(dimension_semantics=sem, vmem_limit_bytes=VMEM_LIMIT)


def _sigmoid(x):
    return 1.0 / (1.0 + jnp.exp(-x))


def _silu(x):
    return x * _sigmoid(x)


def _cast_weight(w_scr, w_ref, row0=0):
    rows = w_ref.shape[0]
    for r in range(0, rows, W_ROWS):
        n = min(W_ROWS, rows - r)
        w_scr[row0 + r:row0 + r + n, :] = w_ref[r:r + n, :].astype(w_scr.dtype)


def _weight_specs(k, piece_rows, start_fns):
    return [pl.BlockSpec((pl.Element(piece_rows), pl.Element(k)),
                         lambda j, i, f=f: (pl.multiple_of(f(j), W_ALIGN), 0))
            for f in start_fns]


def _proj_kernel(a_ref, b_ref, w_ref, o_ref, w_scr, *, scaled_tiles, scale):
    @pl.when(pl.program_id(1) == 0)
    def _():
        _cast_weight(w_scr, w_ref)

    r = lax.dot_general(a_ref[...], w_scr[...], _NT, preferred_element_type=jnp.float32) + b_ref[...]
    if scaled_tiles:
        r = r * jnp.where(pl.program_id(0) < scaled_tiles, scale, 1.0)
    o_ref[...] = r.astype(o_ref.dtype)


def _proj(a, wt, bias, start_fn, n_tiles, tn, out_dtype, tm=2048, scaled_tiles=0, scale=1.0,
          name="proj"):
    m, k = a.shape
    return pl.pallas_call(
        functools.partial(_proj_kernel, scaled_tiles=scaled_tiles, scale=scale),
        grid=(n_tiles, m // tm),
        in_specs=[pl.BlockSpec((tm, k), lambda j, i: (i, 0)),
                  pl.BlockSpec((1, tn), lambda j, i: (0, j))] + _weight_specs(k, tn, [start_fn]),
        out_specs=pl.BlockSpec((tm, tn), lambda j, i: (i, j)),
        out_shape=jax.ShapeDtypeStruct((m, n_tiles * tn), out_dtype),
        scratch_shapes=[pltpu.VMEM((tn, k), jnp.bfloat16)],
        compiler_params=_params("arbitrary", "arbitrary"),
        name=name,
    )(a, bias.reshape(1, n_tiles * tn), wt)


def _norm_proj_kernel(x_ref, g_ref, b_ref, wkv_ref, wg_ref, hn_ref, kv_ref, gate_ref, w_scr):
    n_kv = kv_ref.shape[1]

    @pl.when(pl.program_id(0) == 0)
    def _():
        _cast_weight(w_scr, wkv_ref)
        _cast_weight(w_scr, wg_ref, n_kv)

    x = x_ref[...]
    ms = jnp.mean(x * x, axis=-1, keepdims=True)
    hn = (x * lax.rsqrt(ms + RMS_EPS) * g_ref[...]).astype(hn_ref.dtype)
    hn_ref[...] = hn
    r = lax.dot_general(hn, w_scr[...], _NT, preferred_element_type=jnp.float32) + b_ref[...]
    kv_ref[...] = r[:, :n_kv]
    gate_ref[...] = _sigmoid(r[:, n_kv:])


def _norm_proj(x2d, g, wt, bias, kv_start, n_kv, gate_start, tm=1024):
    m, k = x2d.shape
    once = dict(pipeline_mode=pl.Buffered(1))
    return pl.pallas_call(
        _norm_proj_kernel,
        grid=(m // tm,),
        in_specs=[pl.BlockSpec((tm, k), lambda i: (i, 0)),
                  pl.BlockSpec((1, k), lambda i: (0, 0)),
                  pl.BlockSpec((1, n_kv + LANES), lambda i: (0, 0)),
                  pl.BlockSpec((pl.Element(n_kv), pl.Element(k)), lambda i: (kv_start, 0), **once),
                  pl.BlockSpec((pl.Element(LANES), pl.Element(k)), lambda i: (gate_start, 0), **once)],
        out_specs=[pl.BlockSpec((tm, k), lambda i: (i, 0)),
                   pl.BlockSpec((tm, n_kv), lambda i: (i, 0)),
                   pl.BlockSpec((tm, LANES), lambda i: (i, 0))],
        out_shape=[jax.ShapeDtypeStruct((m, k), jnp.bfloat16),
                   jax.ShapeDtypeStruct((m, n_kv), jnp.float32),
                   jax.ShapeDtypeStruct((m, LANES), jnp.float32)],
        scratch_shapes=[pltpu.VMEM((n_kv + LANES, k), jnp.bfloat16)],
        compiler_params=_params("arbitrary"),
        name="norm_proj",
    )(x2d, g.reshape(1, k), bias.reshape(1, n_kv + LANES), wt, wt)


def _proj_conv_kernel(a_ref, b_ref, cw_ref, cb_ref, wu_ref, wcc_ref, wcb_ref, wz_ref,
                      o_ref, w_scr, halo_scr, *, seq):
    tm, cw = o_ref.shape
    i = pl.program_id(1)

    @pl.when(i == 0)
    def _():
        for p, w_ref in enumerate((wu_ref, wcc_ref, wcb_ref, wz_ref)):
            _cast_weight(w_scr, w_ref, p * cw)

    @pl.when((i * tm) % seq == 0)
    def _():
        halo_scr[...] = jnp.zeros(halo_scr.shape, jnp.float32)

    r = lax.dot_general(a_ref[...], w_scr[...], _NT, preferred_element_type=jnp.float32) + b_ref[...]
    u, cc, cb, z = (r[:, p * cw:(p + 1) * cw] for p in range(4))
    v = cc * u
    prev = halo_scr[...]
    halo_scr[...] = v[tm - SUBLANES:, :]
    row = lax.broadcasted_iota(jnp.int32, (tm, 1), 0)
    y = cw_ref[CONV_K - 1:CONV_K, :] * v
    for d in range(1, CONV_K):
        shifted = pltpu.roll(v, d, axis=0)
        for t in range(d):
            shifted = jnp.where(row == t, prev[SUBLANES - d + t:SUBLANES - d + t + 1, :], shifted)
        y = y + cw_ref[CONV_K - 1 - d:CONV_K - d, :] * shifted
    o_ref[...] = (cb * (y + cb_ref[...]) * _silu(z)).astype(o_ref.dtype)


def _proj_conv(a, wt, bias_tiles, conv_w, conv_b, start_fns, seq, tm=2048, cw=256):
    m, k = a.shape
    n_tiles = CONV_WIDTH // cw
    assert seq % tm == 0 and CONV_K - 1 <= SUBLANES and len(start_fns) == 4
    return pl.pallas_call(
        functools.partial(_proj_conv_kernel, seq=seq),
        grid=(n_tiles, m // tm),
        in_specs=[pl.BlockSpec((tm, k), lambda j, i: (i, 0)),
                  pl.BlockSpec((1, 4 * cw), lambda j, i: (0, j)),
                  pl.BlockSpec((CONV_K, cw), lambda j, i: (0, j)),
                  pl.BlockSpec((1, cw), lambda j, i: (0, j))] + _weight_specs(k, cw, start_fns),
        out_specs=pl.BlockSpec((tm, cw), lambda j, i: (i, j)),
        out_shape=jax.ShapeDtypeStruct((m, CONV_WIDTH), jnp.bfloat16),
        scratch_shapes=[pltpu.VMEM((4 * cw, k), jnp.bfloat16),
                        pltpu.VMEM((SUBLANES, cw), jnp.float32)],
        compiler_params=_params("arbitrary", "arbitrary"),
        name="proj_conv",
    )(a, bias_tiles, conv_w, conv_b.reshape(1, CONV_WIDTH), wt, wt, wt, wt)


def _compress_kernel(*refs):
    n_groups = N_KV_GROUPS
    x_refs = refs[:n_groups]
    pek_ref, w1k_ref, w2k_ref, pev_ref, w1v_ref, w2v_ref, o_ref, pe_scr, w1_scr, w2_scr = refs[n_groups:]
    for which, (pe_ref, w1_ref, w2_ref) in enumerate(((pek_ref, w1k_ref, w2k_ref),
                                                      (pev_ref, w1v_ref, w2v_ref))):
        @pl.when((pl.program_id(1) == 0) & (pl.program_id(0) == which))
        def _(pe_ref=pe_ref, w1_ref=w1_ref, w2_ref=w2_ref):
            pe_scr[...] = pe_ref[...]
            for l in range(CMP_BLOCK):
                w1_scr[l] = w1_ref[l].astype(w1_scr.dtype)
            w2_scr[...] = w2_ref[...].astype(w2_scr.dtype)

    half = CMP_BLOCK // 2
    n_chunks = x_refs[0].shape[1] // CMP_STRIDE
    rows = n_groups * n_chunks
    top = jnp.zeros((rows, CMP_HIDDEN), jnp.float32)
    bot = jnp.zeros((rows, CMP_HIDDEN), jnp.float32)
    for l in range(half):
        xl = jnp.concatenate([x_ref[0, pl.ds(l, n_chunks, stride=CMP_STRIDE), :] for x_ref in x_refs], axis=0)
        a = (xl + pe_scr[l:l + 1, :]).astype(jnp.bfloat16)
        c = (xl + pe_scr[half + l:half + l + 1, :]).astype(jnp.bfloat16)
        top = top + jnp.dot(a, w1_scr[l], preferred_element_type=jnp.float32)
        bot = bot + jnp.dot(c, w1_scr[half + l], preferred_element_type=jnp.float32)
    for g in range(n_groups):
        r0 = g * n_chunks
        h = top[r0:r0 + n_chunks] + pltpu.roll(bot[r0:r0 + n_chunks], n_chunks - 1, axis=0)
        h = _silu(h).astype(jnp.bfloat16)
        o_ref[0, 0, g] = jnp.dot(h, w2_scr[...], preferred_element_type=jnp.float32).astype(o_ref.dtype)


def _compress(kv, pe_k, w1_k, w2_k, pe_v, w1_v, w2_v):
    b, s, _ = kv.shape
    whole = lambda shape: pl.BlockSpec(shape, lambda t, bi: (0,) * len(shape))
    w1_shape = (CMP_BLOCK, HEAD_DIM, CMP_HIDDEN)
    n_c = s // CMP_STRIDE
    return pl.pallas_call(
        _compress_kernel,
        grid=(2, b),
        in_specs=[pl.BlockSpec((1, s, HEAD_DIM), lambda t, bi, g=g: (bi, 0, t * N_KV_GROUPS + g))
                  for g in range(N_KV_GROUPS)] + [
                  whole((CMP_BLOCK, HEAD_DIM)), whole(w1_shape), whole((CMP_HIDDEN, HEAD_DIM)),
                  whole((CMP_BLOCK, HEAD_DIM)), whole(w1_shape), whole((CMP_HIDDEN, HEAD_DIM))],
        out_specs=pl.BlockSpec((1, 1, N_KV_GROUPS, n_c, HEAD_DIM), lambda t, bi: (t, bi, 0, 0, 0)),
        out_shape=jax.ShapeDtypeStruct((2, b, N_KV_GROUPS, n_c, HEAD_DIM), jnp.bfloat16),
        scratch_shapes=[pltpu.VMEM((CMP_BLOCK, HEAD_DIM), jnp.float32),
                        pltpu.VMEM(w1_shape, jnp.bfloat16),
                        pltpu.VMEM((CMP_HIDDEN, HEAD_DIM), jnp.bfloat16)],
        compiler_params=_params("arbitrary", "arbitrary"),
        name="compress",
    )(*([kv] * N_KV_GROUPS), pe_k, w1_k.reshape(w1_shape), w2_k, pe_v, w1_v.reshape(w1_shape), w2_v)


def _attn_kernel(q_ref, ks_ref, vs_ref, kw_ref, vw_ref, kc_ref, vc_ref, gate_ref, z_ref,
                 aggt_ref, blockid_ref, wbias_ref, gsel_ref, o_ref,
                 kaug_scr, vsaug_scr, vwaug_scr, vcaug_scr, s_scr, m_scr, acc_scr):
    tq = Q_TILE
    hpg = HEADS_PER_GROUP
    gw = GROUP_WIDTH
    seq = kw_ref.shape[1]
    gps = kc_ref.shape[1]
    i = pl.program_id(2)
    t0 = i * tq

    @pl.when(i == 0)
    def _():
        ones = jnp.ones((seq, LANES), jnp.bfloat16)
        for gi in range(gps):
            cols = slice(gi * HEAD_DIM, (gi + 1) * HEAD_DIM)
            kaug_scr[gi, :, :HEAD_DIM] = ks_ref[0, :, cols]
            kaug_scr[gi, :, HEAD_DIM:] = blockid_ref[...]
            vsaug_scr[gi, :, :HEAD_DIM] = vs_ref[0, :, cols]
            vsaug_scr[gi, :, HEAD_DIM:] = ones
            vwaug_scr[gi, :, :HEAD_DIM] = vw_ref[0, :, cols]
            vwaug_scr[gi, :, HEAD_DIM:] = ones
            vcaug_scr[gi, :, :HEAD_DIM] = vc_ref[0, gi]
            vcaug_scr[gi, :, HEAD_DIM:] = ones[:vcaug_scr.shape[1]]

    t_q = t0 + lax.broadcasted_iota(jnp.int32, (tq, 1), 0)
    t_s = jnp.concatenate([t_q] * hpg, axis=0)
    lane = lax.broadcasted_iota(jnp.int32, (1, LANES), 1)
    n_cmp = (seq - CMP_BLOCK) // CMP_STRIDE + 1
    n_slc = seq // SEL_BLOCK
    n_win = WINDOW + tq
    start = pl.multiple_of(jnp.maximum(t0 - WINDOW, 0), tq)
    w_bias = jnp.concatenate([wbias_ref[0]] * hpg, axis=0)
    vis_c = (lane * CMP_STRIDE + (CMP_BLOCK - 1) <= t_s) & (lane < n_cmp)
    aggt = aggt_ref[...]
    blk = lax.broadcasted_iota(jnp.int32, (n_slc, 1), 0)
    t_l = t0 + lax.broadcasted_iota(jnp.int32, (1, tq), 1)
    cur = t_l >> int(math.log2(SEL_BLOCK))
    causal = blk * SEL_BLOCK <= t_l
    forced = ((blk == 0) | (blk == cur) | (blk == cur - 1)) & causal
    gate = gate_ref[0]
    gate_hi = gate.astype(jnp.bfloat16)
    gate_lo = (gate - gate_hi.astype(jnp.float32)).astype(jnp.bfloat16)
    gate_hl = jnp.concatenate([gate_hi, gate_lo], axis=1)

    def first_region(gi):
        q = q_ref[0, :, gi * gw:(gi + 1) * gw]
        qs = jnp.concatenate([q[:, h * HEAD_DIM:(h + 1) * HEAD_DIM] for h in range(hpg)], axis=0)

        kw = kw_ref[0, pl.ds(start, n_win), gi * HEAD_DIM:(gi + 1) * HEAD_DIM]
        sw = lax.dot_general(qs, kw, _NT, preferred_element_type=jnp.float32) + w_bias
        p_w = jnp.exp2(sw - jnp.max(sw, axis=-1, keepdims=True))
        pv_w = jnp.dot(p_w.astype(jnp.bfloat16), vwaug_scr[gi, pl.ds(start, n_win), :],
                       preferred_element_type=jnp.float32)
        o_win = pv_w[:, :HEAD_DIM] * (1.0 / pv_w[:, HEAD_DIM:])

        sc = lax.dot_general(qs, kc_ref[0, gi], _NT, preferred_element_type=jnp.float32)
        sc = jnp.where(vis_c, sc, MASKED)
        p_c = jnp.where(vis_c, jnp.exp2(sc - jnp.max(sc, axis=-1, keepdims=True)), 0.0)
        pv_c = jnp.dot(p_c.astype(jnp.bfloat16), vcaug_scr[gi],
                       preferred_element_type=jnp.float32)
        l_c = pv_c[:, HEAD_DIM:]
        inv_c = jnp.where(l_c > 0.0, 1.0 / l_c, 0.0)
        o_cmp = pv_c[:, :HEAD_DIM] * inv_c
        p_c = p_c * inv_c

        p_sum = p_c[0:tq]
        for h in range(1, hpg):
            p_sum = p_sum + p_c[h * tq:(h + 1) * tq]
        hi = p_sum.astype(jnp.bfloat16)
        r1 = p_sum - hi.astype(jnp.float32)
        mid = r1.astype(jnp.bfloat16)
        lo = (r1 - mid.astype(jnp.float32)).astype(jnp.bfloat16)
        imp = (lax.dot_general(aggt, hi, _NT, preferred_element_type=jnp.float32)
               + lax.dot_general(aggt, mid, _NT, preferred_element_type=jnp.float32)
               + lax.dot_general(aggt, lo, _NT, preferred_element_type=jnp.float32))
        imp = jnp.where(forced, FORCED_SCORE, imp)
        imp = jnp.where(causal, imp, -1.0)

        bias_slabs = []
        for v in range(n_slc // SUBLANES):
            lo_row = v * SUBLANES
            slab = imp[lo_row:lo_row + SUBLANES]
            sub = lo_row + lax.broadcasted_iota(jnp.int32, (SUBLANES, 1), 0)
            rank = jnp.zeros((SUBLANES, tq), jnp.float32)
            for jp in range(n_slc):
                other = jnp.broadcast_to(imp[jp:jp + 1], (SUBLANES, tq))
                if jp < lo_row:
                    ahead = jnp.where(other >= slab, 1.0, 0.0)
                elif jp >= lo_row + SUBLANES:
                    ahead = jnp.where(other > slab, 1.0, 0.0)
                else:
                    tie = jnp.where(sub > jp, 1.0, 0.0)
                    ahead = jnp.where(other > slab, 1.0, jnp.where(other == slab, tie, 0.0))
                rank = rank + ahead
            bias_slabs.append(jnp.where(rank < float(SEL_TOP_N), 0.0, MASKED))
        bias_t = jnp.concatenate(bias_slabs + [jnp.zeros((LANES - n_slc, tq), jnp.float32)], axis=0)
        sel_bias = bias_t.T.astype(jnp.bfloat16)
        qa = jnp.concatenate([qs, jnp.concatenate([sel_bias] * hpg, axis=0)], axis=1)
        return o_win, o_cmp, qa

    first = [first_region(gi) for gi in range(gps)]

    n_full = t0 // SEL_CHUNK
    groups = [g for g in (4, 2, 1) if g <= max((seq - tq) // SEL_CHUNK, 1)]

    def scores(gi, c):
        base = pl.multiple_of(c * SEL_CHUNK, SEL_CHUNK)
        return base, lax.dot_general(first[gi][2], kaug_scr[gi, pl.ds(base, SEL_CHUNK), :], _NT,
                                     preferred_element_type=jnp.float32)

    def keep(gi, base, s, m):
        s_scr[gi, :, pl.ds(base, SEL_CHUNK)] = s
        for k in range(SEL_CHUNK // LANES):
            m = jnp.maximum(m, s[:, k * LANES:(k + 1) * LANES])
        return m

    for gi in range(gps):
        base_d, s_d = scores(gi, n_full)
        kpos_d = base_d + lax.broadcasted_iota(jnp.int32, (1, SEL_CHUNK), 1)
        m_scr[gi] = keep(gi, base_d, jnp.where(kpos_d <= t_s, s_d, MASKED),
                         jnp.full(m_scr.shape[1:], MASKED, jnp.float32))
    done = 0
    for g in groups:
        @pl.when((n_full & g) != 0)
        def _(done=done, g=g):
            for gi in range(gps):
                m = m_scr[gi]
                for u in range(g):
                    m = keep(gi, *scores(gi, done + u), m)
                m_scr[gi] = m
        done = done + (n_full & g)
    for gi in range(gps):
        m_scr[gi] = jnp.broadcast_to(jnp.max(m_scr[gi], axis=-1, keepdims=True), m_scr.shape[1:])

    def values(gi, c, m):
        base = pl.multiple_of(c * SEL_CHUNK, SEL_CHUNK)
        ps = [jnp.exp2(s_scr[gi, :, pl.ds(base + k * LANES, LANES)] - m).astype(jnp.bfloat16)
              for k in range(SEL_CHUNK // LANES)]
        return jnp.dot(jnp.concatenate(ps, axis=1), vsaug_scr[gi, pl.ds(base, SEL_CHUNK), :],
                       preferred_element_type=jnp.float32)

    for gi in range(gps):
        acc_scr[gi] = values(gi, n_full, m_scr[gi])
    done = 0
    for g in groups:
        @pl.when((n_full & g) != 0)
        def _(done=done, g=g):
            for gi in range(gps):
                m, acc = m_scr[gi], acc_scr[gi]
                for u in range(g):
                    acc = acc + values(gi, done + u, m)
                acc_scr[gi] = acc
        done = done + (n_full & g)

    for gi in range(gps):
        o_win, o_cmp, _ = first[gi]
        o_sel = acc_scr[gi, :, :HEAD_DIM] * (1.0 / acc_scr[gi, :, HEAD_DIM:])
        g_exp = jnp.dot(gate_hl, gsel_ref[gi], preferred_element_type=jnp.float32)
        g_col = lambda c: g_exp[:, c * LANES:(c + 1) * LANES]
        outs = []
        for h in range(hpg):
            r0 = h * tq
            outs.append(g_col(h) * o_cmp[r0:r0 + tq]
                        + g_col(hpg + h) * o_sel[r0:r0 + tq]
                        + g_col(2 * hpg + h) * o_win[r0:r0 + tq])
        o = jnp.concatenate(outs, axis=1)
        cols = slice(gi * gw, (gi + 1) * gw)
        o_ref[0, :, cols] = (o * _silu(z_ref[0, :, cols])).astype(o_ref.dtype)


def _attention(qkv, kcb, vcb, gate, zu, aggt, blockid, wbias, gsel):
    b, s, _ = qkv.shape
    gps = GROUPS_PER_STEP
    assert SEL_CHUNK % Q_TILE == 0 and WINDOW % Q_TILE == 0 and s % SEL_CHUNK == 0
    assert N_KV_GROUPS % gps == 0
    kvw = gps * HEAD_DIM
    qb = ATTN_WIDTH // kvw
    kvb = KV_WIDTH // kvw
    n_c = s // CMP_STRIDE
    rows = HEADS_PER_GROUP * Q_TILE
    n_pat = WINDOW // Q_TILE
    grid = (b, N_KV_GROUPS // gps, s // Q_TILE)
    kv_spec = lambda off: pl.BlockSpec((1, s, kvw), lambda bi, g, i: (bi, 0, off + g))
    wide = pl.BlockSpec((1, Q_TILE, gps * GROUP_WIDTH), lambda bi, g, i: (bi, i, g))
    return pl.pallas_call(
        _attn_kernel,
        grid=grid,
        in_specs=[wide,
                  kv_spec(qb), kv_spec(qb + kvb), kv_spec(qb + 2 * kvb), kv_spec(qb + 3 * kvb),
                  pl.BlockSpec((1, gps, n_c, HEAD_DIM), lambda bi, g, i: (bi, g, 0, 0)),
                  pl.BlockSpec((1, gps, n_c, HEAD_DIM), lambda bi, g, i: (bi, g, 0, 0)),
                  pl.BlockSpec((1, Q_TILE, LANES), lambda bi, g, i: (bi, i, 0)),
                  wide,
                  pl.BlockSpec(aggt.shape, lambda bi, g, i: (0, 0)),
                  pl.BlockSpec(blockid.shape, lambda bi, g, i: (0, 0)),
                  pl.BlockSpec((1, Q_TILE, WINDOW + Q_TILE),
                               lambda bi, g, i: (jnp.minimum(i, n_pat), 0, 0)),
                  pl.BlockSpec((gps,) + gsel.shape[1:], lambda bi, g, i: (g, 0, 0))],
        out_specs=wide,
        out_shape=jax.ShapeDtypeStruct((b, s, ATTN_WIDTH), jnp.bfloat16),
        scratch_shapes=[pltpu.VMEM((gps, s, HEAD_DIM + LANES), jnp.bfloat16),
                        pltpu.VMEM((gps, s, HEAD_DIM + LANES), jnp.bfloat16),
                        pltpu.VMEM((gps, s, HEAD_DIM + LANES), jnp.bfloat16),
                        pltpu.VMEM((gps, n_c, HEAD_DIM + LANES), jnp.bfloat16),
                        pltpu.VMEM((gps, rows, s), jnp.float32),
                        pltpu.VMEM((gps, rows, LANES), jnp.float32),
                        pltpu.VMEM((gps, rows, HEAD_DIM + LANES), jnp.float32)],
        compiler_params=_params("arbitrary", "arbitrary", "arbitrary"),
        name="nsa_attention",
    )(qkv, qkv, qkv, qkv, qkv, kcb, vcb, gate, zu, aggt, blockid, wbias, gsel)


def _merge_kernel(a_ref, c_ref, wa_ref, wc_ref, g0_ref, g1_ref, o_ref, wa_scr, wc_scr):
    @pl.when(pl.program_id(1) == 0)
    def _():
        _cast_weight(wa_scr, wa_ref)
        _cast_weight(wc_scr, wc_ref)

    ya = jnp.dot(a_ref[...], wa_scr[...], preferred_element_type=jnp.float32)
    yc = jnp.dot(c_ref[...], wc_scr[...], preferred_element_type=jnp.float32)
    o_ref[...] = (_sigmoid(g0_ref[...]) * ya + _sigmoid(g1_ref[...]) * yc).astype(o_ref.dtype)


def _merge(a, c, wa, wc, zg, gcol0, tm=512, tn=1024):
    m, k = a.shape
    n = wa.shape[-1]
    g0 = gcol0 // tn
    g1 = (gcol0 + n) // tn
    return pl.pallas_call(
        _merge_kernel,
        grid=(n // tn, m // tm),
        in_specs=[pl.BlockSpec((tm, k), lambda j, i: (i, 0)),
                  pl.BlockSpec((tm, k), lambda j, i: (i, 0)),
                  pl.BlockSpec((None, k, tn), lambda j, i: (0, 0, j), pipeline_mode=pl.Buffered(1)),
                  pl.BlockSpec((None, k, tn), lambda j, i: (0, 0, j), pipeline_mode=pl.Buffered(1)),
                  pl.BlockSpec((tm, tn), lambda j, i: (i, g0 + j)),
                  pl.BlockSpec((tm, tn), lambda j, i: (i, g1 + j))],
        out_specs=pl.BlockSpec((tm, tn), lambda j, i: (i, j)),
        out_shape=jax.ShapeDtypeStruct((m, n), jnp.bfloat16),
        scratch_shapes=[pltpu.VMEM((k, tn), jnp.bfloat16), pltpu.VMEM((k, tn), jnp.bfloat16)],
        compiler_params=_params("arbitrary", "arbitrary"),
        name="merge",
    )(a, c, wa, wc, zg, zg)


def _out_kernel(y_ref, w_ref, x_ref, g_ref, o_ref, w_scr):
    @pl.when(pl.program_id(0) == 0)
    def _():
        _cast_weight(w_scr, w_ref)

    r = x_ref[...] + jnp.dot(y_ref[...], w_scr[...], preferred_element_type=jnp.float32)
    ms = jnp.mean(r * r, axis=-1, keepdims=True)
    o_ref[...] = r * lax.rsqrt(ms + RMS_EPS) * g_ref[...]


def _out_proj(y, w, x2d, g, tm=512):
    m, k = y.shape
    n = w.shape[-1]
    return pl.pallas_call(
        _out_kernel,
        grid=(m // tm,),
        in_specs=[pl.BlockSpec((tm, k), lambda i: (i, 0)),
                  pl.BlockSpec((None, k, n), lambda i: (0, 0, 0), pipeline_mode=pl.Buffered(1)),
                  pl.BlockSpec((tm, n), lambda i: (i, 0)),
                  pl.BlockSpec((1, n), lambda i: (0, 0))],
        out_specs=pl.BlockSpec((tm, n), lambda i: (i, 0)),
        out_shape=jax.ShapeDtypeStruct((m, n), jnp.float32),
        scratch_shapes=[pltpu.VMEM((k, n), jnp.bfloat16)],
        compiler_params=_params("arbitrary"),
        name="out_proj",
    )(y, w, x2d, g.reshape(1, n))


def _attention_constants(s):
    n_slc = s // SEL_BLOCK
    n_cmp = (s - CMP_BLOCK) // CMP_STRIDE + 1
    c0 = np.arange(s // CMP_STRIDE)[None, :] * CMP_STRIDE
    s0 = np.arange(n_slc)[:, None] * SEL_BLOCK
    overlap = np.maximum(0, np.minimum(c0 + CMP_BLOCK, s0 + SEL_BLOCK) - np.maximum(c0, s0))
    aggt = (overlap / CMP_BLOCK) * (np.arange(s // CMP_STRIDE)[None, :] < n_cmp)
    blockid = (np.arange(s)[:, None] // SEL_BLOCK == np.arange(LANES)[None, :]).astype(np.float32)
    n_pat = WINDOW // Q_TILE
    pats = []
    for p in range(n_pat + 1):
        t = p * Q_TILE + np.arange(Q_TILE)[:, None]
        kpos = max(p * Q_TILE - WINDOW, 0) + np.arange(WINDOW + Q_TILE)[None, :]
        pats.append(np.where((kpos <= t) & (kpos > t - WINDOW), 0.0, MASKED))
    n_bh = 3 * HEADS_PER_GROUP
    gsel = np.zeros((N_KV_GROUPS, 2 * LANES, n_bh * LANES), np.float32)
    for g in range(N_KV_GROUPS):
        for br in range(3):
            for h in range(HEADS_PER_GROUP):
                src = br * N_HEADS + g * HEADS_PER_GROUP + h
                c = br * HEADS_PER_GROUP + h
                gsel[g, [src, LANES + src], c * LANES:(c + 1) * LANES] = 1.0
    return (jnp.asarray(aggt, jnp.bfloat16), jnp.asarray(blockid, jnp.bfloat16),
            jnp.asarray(np.stack(pats), jnp.float32), jnp.asarray(gsel, jnp.bfloat16))


def _layer(x, norm_g, w_in, b_in, pe_k, w1_k, w2_k, pe_v, w1_v, w2_v,
           conv_w, conv_b, p_attn, p_conv, w_o, out_g):
    b, s, d = x.shape
    m = b * s
    bf = jnp.bfloat16
    x2d = x.reshape(m, d)
    tn = 1024

    c_q = ATTN_WIDTH
    c_cmp = c_q + 2 * KV_WIDTH
    c_kv = c_cmp + 4 * KV_WIDTH
    n_gate = 3 * N_HEADS
    c_z = c_kv + n_gate
    c_conv = c_z + ATTN_WIDTH
    c_merge = c_conv + 4 * CONV_WIDTH
    assert c_cmp - c_q == tn and c_q % tn == 0 and (c_kv - c_cmp) % tn == 0
    wt = jnp.swapaxes(w_in, 1, 2).reshape(w_in.shape[2], w_in.shape[1])

    hn, kvc, gate = _norm_proj(x2d, norm_g, wt, jnp.concatenate([b_in[c_q:c_cmp], b_in[c_kv:c_kv + LANES]]),
                               c_q, c_cmp - c_q, c_kv)
    kvc = kvc.reshape(b, s, -1)
    q_tiles = c_q // tn
    qkv = _proj(hn, wt, jnp.concatenate([b_in[:c_q], b_in[c_cmp:c_kv]]),
                lambda j: jnp.where(j < q_tiles, j * tn, c_cmp + (j - q_tiles) * tn),
                q_tiles + (c_kv - c_cmp) // tn, tn, bf,
                scaled_tiles=q_tiles, scale=SCALE * math.log2(math.e), name="proj_qkv").reshape(b, s, -1)
    z_tiles = ATTN_WIDTH // tn
    zg = _proj(hn, wt, jnp.concatenate([b_in[c_z:c_conv], b_in[c_merge:]]),
               lambda j: jnp.where(j < z_tiles, c_z + j * tn, c_merge + (j - z_tiles) * tn),
               z_tiles + 2 * D_MODEL // tn, tn, jnp.float32, tm=1024, name="proj_zg").reshape(b, s, -1)
    cw = 256
    b_conv = b_in[c_conv:c_merge].reshape(4, CONV_WIDTH // cw, cw).transpose(1, 0, 2).reshape(1, -1)
    conv_starts = [lambda j, p=p: c_conv + p * CONV_WIDTH + j * cw for p in range(4)]
    c = _proj_conv(hn, wt, b_conv, conv_w, conv_b, conv_starts, s, cw=cw)

    kvcb = _compress(kvc, pe_k, w1_k, w2_k, pe_v, w1_v, w2_v)
    kcb, vcb = kvcb[0], kvcb[1]

    aggt, blockid, wbias, gsel = _attention_constants(s)
    a = _attention(qkv, kcb, vcb, gate.reshape(b, s, LANES), zg, aggt, blockid, wbias, gsel)

    y = _merge(a.reshape(m, -1), c, p_attn, p_conv, zg.reshape(m, -1), ATTN_WIDTH)
    return _out_proj(y, w_o, x2d, out_g).reshape(b, s, d)


def kernel(x, norm_g, w_in, b_in, cmp_pe_k, cmp_w1_k, cmp_w2_k, cmp_pe_v, cmp_w1_v, cmp_w2_v,
           conv_w, conv_b, p_attn, p_conv, w_o, final_g):
    assert norm_g.shape[0] == 1, "single-layer block"
    return _layer(x, norm_g[0], w_in, b_in[0], cmp_pe_k[0], cmp_w1_k[0], cmp_w2_k[0],
                  cmp_pe_v[0], cmp_w1_v[0], cmp_w2_v[0], conv_w[0], conv_b[0],
                  p_attn, p_conv, w_o, final_g)
```

```python
import functools
import math

import numpy as np
import jax
import jax.numpy as jnp
from jax import lax
from jax.experimental import pallas as pl
from jax.experimental.pallas import tpu as pltpu

D_MODEL = 2048
N_HEADS = 16
HEAD_DIM = 128
N_KV_GROUPS = 4
HEADS_PER_GROUP = N_HEADS // N_KV_GROUPS
ATTN_WIDTH = N_HEADS * HEAD_DIM
KV_WIDTH = N_KV_GROUPS * HEAD_DIM
GROUP_WIDTH = HEADS_PER_GROUP * HEAD_DIM
CMP_BLOCK = 32
CMP_STRIDE = 16
CMP_HIDDEN = 256
SEL_BLOCK = 64
SEL_TOP_N = 8
WINDOW = 512
SCALE = HEAD_DIM ** -0.5
CONV_WIDTH = D_MODEL
CONV_K = 3
RMS_EPS = 1e-6
FORCED_SCORE = 1e4
MASKED = -(2.0 ** 100)

LANES = 128
SUBLANES = 8
Q_TILE = 256
SEL_CHUNK = 256
GROUPS_PER_STEP = 2
W_ROWS = 512
W_ALIGN = 16
VMEM_LIMIT = 56 * 1024 * 1024
PROJ_TM, PROJ_TN = 2048, 1024
PROJ_TM_F32 = 1024
CONV_TN = 256
NORM_TM = 1024
MERGE_TM, MERGE_TN = 512, 1024
OUT_TM = 512

_NT = (((1,), (1,)), ((), ()))


def _params(*sem):
    return pltpu.CompilerParams(dimension_semantics=sem, vmem_limit_bytes=VMEM_LIMIT)


def _sigmoid(x):
    return 1.0 / (1.0 + jnp.exp(-x))


def _silu(x):
    return x * _sigmoid(x)


def _cast_weight(w_scr, w_ref, row0=0):
    rows = w_ref.shape[0]
    for r in range(0, rows, W_ROWS):
        n = min(W_ROWS, rows - r)
        w_scr[row0 + r:row0 + r + n, :] = w_ref[r:r + n, :].astype(w_scr.dtype)


def _weight_specs(k, piece_rows, start_fns):
    return [pl.BlockSpec((pl.Element(piece_rows), pl.Element(k)),
                         lambda j, i, f=f: (pl.multiple_of(f(j), W_ALIGN), 0))
            for f in start_fns]


def _proj_kernel(a_ref, b_ref, w_ref, o_ref, w_scr, *, scaled_tiles, scale):
    @pl.when(pl.program_id(1) == 0)
    def _():
        _cast_weight(w_scr, w_ref)

    r = lax.dot_general(a_ref[...], w_scr[...], _NT, preferred_element_type=jnp.float32) + b_ref[...]
    if scaled_tiles:
        r = r * jnp.where(pl.program_id(0) < scaled_tiles, scale, 1.0)
    o_ref[...] = r.astype(o_ref.dtype)


def _proj(a, wt, bias, start_fn, n_tiles, tn, out_dtype, tm=PROJ_TM, scaled_tiles=0, scale=1.0,
          name="proj"):
    m, k = a.shape
    return pl.pallas_call(
        functools.partial(_proj_kernel, scaled_tiles=scaled_tiles, scale=scale),
        grid=(n_tiles, m // tm),
        in_specs=[pl.BlockSpec((tm, k), lambda j, i: (i, 0)),
                  pl.BlockSpec((1, tn), lambda j, i: (0, j))] + _weight_specs(k, tn, [start_fn]),
        out_specs=pl.BlockSpec((tm, tn), lambda j, i: (i, j)),
        out_shape=jax.ShapeDtypeStruct((m, n_tiles * tn), out_dtype),
        scratch_shapes=[pltpu.VMEM((tn, k), jnp.bfloat16)],
        compiler_params=_params("arbitrary", "arbitrary"),
        name=name,
    )(a, bias.reshape(1, n_tiles * tn), wt)


def _norm_proj_kernel(x_ref, g_ref, b_ref, wkv_ref, wg_ref, hn_ref, kv_ref, gate_ref, w_scr):
    n_kv = kv_ref.shape[1]

    @pl.when(pl.program_id(0) == 0)
    def _():
        _cast_weight(w_scr, wkv_ref)
        _cast_weight(w_scr, wg_ref, n_kv)

    x = x_ref[...]
    ms = jnp.mean(x * x, axis=-1, keepdims=True)
    hn = (x * lax.rsqrt(ms + RMS_EPS) * g_ref[...]).astype(hn_ref.dtype)
    hn_ref[...] = hn
    r = lax.dot_general(hn, w_scr[...], _NT, preferred_element_type=jnp.float32) + b_ref[...]
    kv_ref[...] = r[:, :n_kv]
    gate_ref[...] = _sigmoid(r[:, n_kv:])


def _norm_proj(x2d, g, wt, bias, kv_start, n_kv, gate_start, tm=NORM_TM):
    m, k = x2d.shape
    once = dict(pipeline_mode=pl.Buffered(1))
    return pl.pallas_call(
        _norm_proj_kernel,
        grid=(m // tm,),
        in_specs=[pl.BlockSpec((tm, k), lambda i: (i, 0)),
                  pl.BlockSpec((1, k), lambda i: (0, 0)),
                  pl.BlockSpec((1, n_kv + LANES), lambda i: (0, 0)),
                  pl.BlockSpec((pl.Element(n_kv), pl.Element(k)), lambda i: (kv_start, 0), **once),
                  pl.BlockSpec((pl.Element(LANES), pl.Element(k)), lambda i: (gate_start, 0), **once)],
        out_specs=[pl.BlockSpec((tm, k), lambda i: (i, 0)),
                   pl.BlockSpec((tm, n_kv), lambda i: (i, 0)),
                   pl.BlockSpec((tm, LANES), lambda i: (i, 0))],
        out_shape=[jax.ShapeDtypeStruct((m, k), jnp.bfloat16),
                   jax.ShapeDtypeStruct((m, n_kv), jnp.float32),
                   jax.ShapeDtypeStruct((m, LANES), jnp.float32)],
        scratch_shapes=[pltpu.VMEM((n_kv + LANES, k), jnp.bfloat16)],
        compiler_params=_params("arbitrary"),
        name="norm_proj",
    )(x2d, g.reshape(1, k), bias.reshape(1, n_kv + LANES), wt, wt)


def _proj_conv_kernel(a_ref, b_ref, cw_ref, cb_ref, wu_ref, wcc_ref, wcb_ref, wz_ref,
                      o_ref, w_scr, halo_scr, *, seq):
    tm, cw = o_ref.shape
    i = pl.program_id(1)

    @pl.when(i == 0)
    def _():
        for p, w_ref in enumerate((wu_ref, wcc_ref, wcb_ref, wz_ref)):
            _cast_weight(w_scr, w_ref, p * cw)

    @pl.when((i * tm) % seq == 0)
    def _():
        halo_scr[...] = jnp.zeros(halo_scr.shape, jnp.float32)

    r = lax.dot_general(a_ref[...], w_scr[...], _NT, preferred_element_type=jnp.float32) + b_ref[...]
    u, cc, cb, z = (r[:, p * cw:(p + 1) * cw] for p in range(4))
    v = cc * u
    prev = halo_scr[...]
    halo_scr[...] = v[tm - SUBLANES:, :]
    row = lax.broadcasted_iota(jnp.int32, (tm, 1), 0)
    y = cw_ref[CONV_K - 1:CONV_K, :] * v
    for d in range(1, CONV_K):
        shifted = pltpu.roll(v, d, axis=0)
        for t in range(d):
            shifted = jnp.where(row == t, prev[SUBLANES - d + t:SUBLANES - d + t + 1, :], shifted)
        y = y + cw_ref[CONV_K - 1 - d:CONV_K - d, :] * shifted
    o_ref[...] = (cb * (y + cb_ref[...]) * _silu(z)).astype(o_ref.dtype)


def _proj_conv(a, wt, bias_tiles, conv_w, conv_b, start_fns, seq, tm=PROJ_TM, cw=CONV_TN):
    m, k = a.shape
    n_tiles = CONV_WIDTH // cw
    assert seq % tm == 0 and CONV_K - 1 <= SUBLANES and len(start_fns) == 4
    return pl.pallas_call(
        functools.partial(_proj_conv_kernel, seq=seq),
        grid=(n_tiles, m // tm),
        in_specs=[pl.BlockSpec((tm, k), lambda j, i: (i, 0)),
                  pl.BlockSpec((1, 4 * cw), lambda j, i: (0, j)),
                  pl.BlockSpec((CONV_K, cw), lambda j, i: (0, j)),
                  pl.BlockSpec((1, cw), lambda j, i: (0, j))] + _weight_specs(k, cw, start_fns),
        out_specs=pl.BlockSpec((tm, cw), lambda j, i: (i, j)),
        out_shape=jax.ShapeDtypeStruct((m, CONV_WIDTH), jnp.bfloat16),
        scratch_shapes=[pltpu.VMEM((4 * cw, k), jnp.bfloat16),
                        pltpu.VMEM((SUBLANES, cw), jnp.float32)],
        compiler_params=_params("arbitrary", "arbitrary"),
        name="proj_conv",
    )(a, bias_tiles, conv_w, conv_b.reshape(1, CONV_WIDTH), wt, wt, wt, wt)


def _compress_kernel(*refs):
    n_groups = N_KV_GROUPS
    x_refs = refs[:n_groups]
    pek_ref, w1k_ref, w2k_ref, pev_ref, w1v_ref, w2v_ref, o_ref, pe_scr, w1_scr, w2_scr = refs[n_groups:]
    for which, (pe_ref, w1_ref, w2_ref) in enumerate(((pek_ref, w1k_ref, w2k_ref),
                                                      (pev_ref, w1v_ref, w2v_ref))):
        @pl.when((pl.program_id(1) == 0) & (pl.program_id(0) == which))
        def _(pe_ref=pe_ref, w1_ref=w1_ref, w2_ref=w2_ref):
            pe_scr[...] = pe_ref[...]
            for l in range(CMP_BLOCK):
                w1_scr[l] = w1_ref[l].astype(w1_scr.dtype)
            w2_scr[...] = w2_ref[...].astype(w2_scr.dtype)

    half = CMP_BLOCK // 2
    n_chunks = x_refs[0].shape[1] // CMP_STRIDE
    rows = n_groups * n_chunks
    top = jnp.zeros((rows, CMP_HIDDEN), jnp.float32)
    bot = jnp.zeros((rows, CMP_HIDDEN), jnp.float32)
    for l in range(half):
        xl = jnp.concatenate([x_ref[0, pl.ds(l, n_chunks, stride=CMP_STRIDE), :] for x_ref in x_refs], axis=0)
        a = (xl + pe_scr[l:l + 1, :]).astype(jnp.bfloat16)
        c = (xl + pe_scr[half + l:half + l + 1, :]).astype(jnp.bfloat16)
        top = top + jnp.dot(a, w1_scr[l], preferred_element_type=jnp.float32)
        bot = bot + jnp.dot(c, w1_scr[half + l], preferred_element_type=jnp.float32)
    for g in range(n_groups):
        r0 = g * n_chunks
        h = top[r0:r0 + n_chunks] + pltpu.roll(bot[r0:r0 + n_chunks], n_chunks - 1, axis=0)
        h = _silu(h).astype(jnp.bfloat16)
        o_ref[0, 0, g] = jnp.dot(h, w2_scr[...], preferred_element_type=jnp.float32).astype(o_ref.dtype)


def _compress(kv, pe_k, w1_k, w2_k, pe_v, w1_v, w2_v):
    b, s, _ = kv.shape
    whole = lambda shape: pl.BlockSpec(shape, lambda t, bi: (0,) * len(shape))
    w1_shape = (CMP_BLOCK, HEAD_DIM, CMP_HIDDEN)
    n_c = s // CMP_STRIDE
    return pl.pallas_call(
        _compress_kernel,
        grid=(2, b),
        in_specs=[pl.BlockSpec((1, s, HEAD_DIM), lambda t, bi, g=g: (bi, 0, t * N_KV_GROUPS + g))
                  for g in range(N_KV_GROUPS)] + [
                  whole((CMP_BLOCK, HEAD_DIM)), whole(w1_shape), whole((CMP_HIDDEN, HEAD_DIM)),
                  whole((CMP_BLOCK, HEAD_DIM)), whole(w1_shape), whole((CMP_HIDDEN, HEAD_DIM))],
        out_specs=pl.BlockSpec((1, 1, N_KV_GROUPS, n_c, HEAD_DIM), lambda t, bi: (t, bi, 0, 0, 0)),
        out_shape=jax.ShapeDtypeStruct((2, b, N_KV_GROUPS, n_c, HEAD_DIM), jnp.bfloat16),
        scratch_shapes=[pltpu.VMEM((CMP_BLOCK, HEAD_DIM), jnp.float32),
                        pltpu.VMEM(w1_shape, jnp.bfloat16),
                        pltpu.VMEM((CMP_HIDDEN, HEAD_DIM), jnp.bfloat16)],
        compiler_params=_params("arbitrary", "arbitrary"),
        name="compress",
    )(*([kv] * N_KV_GROUPS), pe_k, w1_k.reshape(w1_shape), w2_k, pe_v, w1_v.reshape(w1_shape), w2_v)


def _attn_kernel(q_ref, ks_ref, vs_ref, kw_ref, vw_ref, kc_ref, vc_ref, gate_ref, z_ref,
                 aggt_ref, blockid_ref, wbias_ref, gsel_ref, o_ref,
                 kaug_scr, vsaug_scr, vwaug_scr, vcaug_scr, s_scr, m_scr, acc_scr):
    tq = Q_TILE
    hpg = HEADS_PER_GROUP
    gw = GROUP_WIDTH
    seq = kw_ref.shape[1]
    gps = kc_ref.shape[1]
    i = pl.program_id(2)
    t0 = i * tq

    @pl.when(i == 0)
    def _():
        ones = jnp.ones((seq, LANES), jnp.bfloat16)
        for gi in range(gps):
            cols = slice(gi * HEAD_DIM, (gi + 1) * HEAD_DIM)
            kaug_scr[gi, :, :HEAD_DIM] = ks_ref[0, :, cols]
            kaug_scr[gi, :, HEAD_DIM:] = blockid_ref[...]
            vsaug_scr[gi, :, :HEAD_DIM] = vs_ref[0, :, cols]
            vsaug_scr[gi, :, HEAD_DIM:] = ones
            vwaug_scr[gi, :, :HEAD_DIM] = vw_ref[0, :, cols]
            vwaug_scr[gi, :, HEAD_DIM:] = ones
            vcaug_scr[gi, :, :HEAD_DIM] = vc_ref[0, gi]
            vcaug_scr[gi, :, HEAD_DIM:] = ones[:vcaug_scr.shape[1]]

    t_q = t0 + lax.broadcasted_iota(jnp.int32, (tq, 1), 0)
    t_s = jnp.concatenate([t_q] * hpg, axis=0)
    lane = lax.broadcasted_iota(jnp.int32, (1, LANES), 1)
    n_cmp = (seq - CMP_BLOCK) // CMP_STRIDE + 1
    n_slc = seq // SEL_BLOCK
    n_win = WINDOW + tq
    start = pl.multiple_of(jnp.maximum(t0 - WINDOW, 0), tq)
    w_bias = jnp.concatenate([wbias_ref[0]] * hpg, axis=0)
    vis_c = (lane * CMP_STRIDE + (CMP_BLOCK - 1) <= t_s) & (lane < n_cmp)
    aggt = aggt_ref[...]
    blk = lax.broadcasted_iota(jnp.int32, (n_slc, 1), 0)
    t_l = t0 + lax.broadcasted_iota(jnp.int32, (1, tq), 1)
    cur = t_l >> int(math.log2(SEL_BLOCK))
    causal = blk * SEL_BLOCK <= t_l
    forced = ((blk == 0) | (blk == cur) | (blk == cur - 1)) & causal
    gate = gate_ref[0]
    gate_hi = gate.astype(jnp.bfloat16)
    gate_lo = (gate - gate_hi.astype(jnp.float32)).astype(jnp.bfloat16)
    gate_hl = jnp.concatenate([gate_hi, gate_lo], axis=1)

    def first_region(gi):
        q = q_ref[0, :, gi * gw:(gi + 1) * gw]
        qs = jnp.concatenate([q[:, h * HEAD_DIM:(h + 1) * HEAD_DIM] for h in range(hpg)], axis=0)

        kw = kw_ref[0, pl.ds(start, n_win), gi * HEAD_DIM:(gi + 1) * HEAD_DIM]
        sw = lax.dot_general(qs, kw, _NT, preferred_element_type=jnp.float32) + w_bias
        p_w = jnp.exp2(sw - jnp.max(sw, axis=-1, keepdims=True))
        pv_w = jnp.dot(p_w.astype(jnp.bfloat16), vwaug_scr[gi, pl.ds(start, n_win), :],
                       preferred_element_type=jnp.float32)
        o_win = pv_w[:, :HEAD_DIM] * (1.0 / pv_w[:, HEAD_DIM:])

        sc = lax.dot_general(qs, kc_ref[0, gi], _NT, preferred_element_type=jnp.float32)
        sc = jnp.where(vis_c, sc, MASKED)
        p_c = jnp.where(vis_c, jnp.exp2(sc - jnp.max(sc, axis=-1, keepdims=True)), 0.0)
        pv_c = jnp.dot(p_c.astype(jnp.bfloat16), vcaug_scr[gi],
                       preferred_element_type=jnp.float32)
        l_c = pv_c[:, HEAD_DIM:]
        inv_c = jnp.where(l_c > 0.0, 1.0 / l_c, 0.0)
        o_cmp = pv_c[:, :HEAD_DIM] * inv_c
        p_c = p_c * inv_c

        p_sum = p_c[0:tq]
        for h in range(1, hpg):
            p_sum = p_sum + p_c[h * tq:(h + 1) * tq]
        hi = p_sum.astype(jnp.bfloat16)
        r1 = p_sum - hi.astype(jnp.float32)
        mid = r1.astype(jnp.bfloat16)
        lo = (r1 - mid.astype(jnp.float32)).astype(jnp.bfloat16)
        imp = (lax.dot_general(aggt, hi, _NT, preferred_element_type=jnp.float32)
               + lax.dot_general(aggt, mid, _NT, preferred_element_type=jnp.float32)
               + lax.dot_general(aggt, lo, _NT, preferred_element_type=jnp.float32))
        imp = jnp.where(forced, FORCED_SCORE, imp)
        imp = jnp.where(causal, imp, -1.0)

        bias_slabs = []
        for v in range(n_slc // SUBLANES):
            lo_row = v * SUBLANES
            slab = imp[lo_row:lo_row + SUBLANES]
            sub = lo_row + lax.broadcasted_iota(jnp.int32, (SUBLANES, 1), 0)
            rank = jnp.zeros((SUBLANES, tq), jnp.float32)
            for jp in range(n_slc):
                other = jnp.broadcast_to(imp[jp:jp + 1], (SUBLANES, tq))
                if jp < lo_row:
                    ahead = jnp.where(other >= slab, 1.0, 0.0)
                elif jp >= lo_row + SUBLANES:
                    ahead = jnp.where(other > slab, 1.0, 0.0)
                else:
                    tie = jnp.where(sub > jp, 1.0, 0.0)
                    ahead = jnp.where(other > slab, 1.0, jnp.where(other == slab, tie, 0.0))
                rank = rank + ahead
            bias_slabs.append(jnp.where(rank < float(SEL_TOP_N), 0.0, MASKED))
        bias_t = jnp.concatenate(bias_slabs + [jnp.zeros((LANES - n_slc, tq), jnp.float32)], axis=0)
        sel_bias = bias_t.T.astype(jnp.bfloat16)
        qa = jnp.concatenate([qs, jnp.concatenate([sel_bias] * hpg, axis=0)], axis=1)
        return o_win, o_cmp, qa

    first = [first_region(gi) for gi in range(gps)]

    n_full = t0 // SEL_CHUNK
    groups = [g for g in (4, 2, 1) if g <= max((seq - tq) // SEL_CHUNK, 1)]

    def scores(gi, c):
        base = pl.multiple_of(c * SEL_CHUNK, SEL_CHUNK)
        return base, lax.dot_general(first[gi][2], kaug_scr[gi, pl.ds(base, SEL_CHUNK), :], _NT,
                                     preferred_element_type=jnp.float32)

    def keep(gi, base, s, m):
        s_scr[gi, :, pl.ds(base, SEL_CHUNK)] = s
        for k in range(SEL_CHUNK // LANES):
            m = jnp.maximum(m, s[:, k * LANES:(k + 1) * LANES])
        return m

    for gi in range(gps):
        base_d, s_d = scores(gi, n_full)
        kpos_d = base_d + lax.broadcasted_iota(jnp.int32, (1, SEL_CHUNK), 1)
        m_scr[gi] = keep(gi, base_d, jnp.where(kpos_d <= t_s, s_d, MASKED),
                         jnp.full(m_scr.shape[1:], MASKED, jnp.float32))
    done = 0
    for g in groups:
        @pl.when((n_full & g) != 0)
        def _(done=done, g=g):
            for gi in range(gps):
                m = m_scr[gi]
                for u in range(g):
                    m = keep(gi, *scores(gi, done + u), m)
                m_scr[gi] = m
        done = done + (n_full & g)
    for gi in range(gps):
        m_scr[gi] = jnp.broadcast_to(jnp.max(m_scr[gi], axis=-1, keepdims=True), m_scr.shape[1:])

    def values(gi, c, m):
        base = pl.multiple_of(c * SEL_CHUNK, SEL_CHUNK)
        ps = [jnp.exp2(s_scr[gi, :, pl.ds(base + k * LANES, LANES)] - m).astype(jnp.bfloat16)
              for k in range(SEL_CHUNK // LANES)]
        return jnp.dot(jnp.concatenate(ps, axis=1), vsaug_scr[gi, pl.ds(base, SEL_CHUNK), :],
                       preferred_element_type=jnp.float32)

    for gi in range(gps):
        acc_scr[gi] = values(gi, n_full, m_scr[gi])
    done = 0
    for g in groups:
        @pl.when((n_full & g) != 0)
        def _(done=done, g=g):
            for gi in range(gps):
                m, acc = m_scr[gi], acc_scr[gi]
                for u in range(g):
                    acc = acc + values(gi, done + u, m)
                acc_scr[gi] = acc
        done = done + (n_full & g)

    for gi in range(gps):
        o_win, o_cmp, _ = first[gi]
        o_sel = acc_scr[gi, :, :HEAD_DIM] * (1.0 / acc_scr[gi, :, HEAD_DIM:])
        g_exp = jnp.dot(gate_hl, gsel_ref[gi], preferred_element_type=jnp.float32)
        g_col = lambda c: g_exp[:, c * LANES:(c + 1) * LANES]
        outs = []
        for h in range(hpg):
            r0 = h * tq
            outs.append(g_col(h) * o_cmp[r0:r0 + tq]
                        + g_col(hpg + h) * o_sel[r0:r0 + tq]
                        + g_col(2 * hpg + h) * o_win[r0:r0 + tq])
        o = jnp.concatenate(outs, axis=1)
        cols = slice(gi * gw, (gi + 1) * gw)
        o_ref[0, :, cols] = (o * _silu(z_ref[0, :, cols])).astype(o_ref.dtype)


def _attention(qkv, kcb, vcb, gate, zu, aggt, blockid, wbias, gsel):
    b, s, _ = qkv.shape
    gps = GROUPS_PER_STEP
    assert SEL_CHUNK % Q_TILE == 0 and WINDOW % Q_TILE == 0 and s % SEL_CHUNK == 0
    assert N_KV_GROUPS % gps == 0
    kvw = gps * HEAD_DIM
    qb = ATTN_WIDTH // kvw
    kvb = KV_WIDTH // kvw
    n_c = s // CMP_STRIDE
    rows = HEADS_PER_GROUP * Q_TILE
    n_pat = WINDOW // Q_TILE
    grid = (b, N_KV_GROUPS // gps, s // Q_TILE)
    kv_spec = lambda off: pl.BlockSpec((1, s, kvw), lambda bi, g, i: (bi, 0, off + g))
    wide = pl.BlockSpec((1, Q_TILE, gps * GROUP_WIDTH), lambda bi, g, i: (bi, i, g))
    return pl.pallas_call(
        _attn_kernel,
        grid=grid,
        in_specs=[wide,
                  kv_spec(qb), kv_spec(qb + kvb), kv_spec(qb + 2 * kvb), kv_spec(qb + 3 * kvb),
                  pl.BlockSpec((1, gps, n_c, HEAD_DIM), lambda bi, g, i: (bi, g, 0, 0)),
                  pl.BlockSpec((1, gps, n_c, HEAD_DIM), lambda bi, g, i: (bi, g, 0, 0)),
                  pl.BlockSpec((1, Q_TILE, LANES), lambda bi, g, i: (bi, i, 0)),
                  wide,
                  pl.BlockSpec(aggt.shape, lambda bi, g, i: (0, 0)),
                  pl.BlockSpec(blockid.shape, lambda bi, g, i: (0, 0)),
                  pl.BlockSpec((1, Q_TILE, WINDOW + Q_TILE),
                               lambda bi, g, i: (jnp.minimum(i, n_pat), 0, 0)),
                  pl.BlockSpec((gps,) + gsel.shape[1:], lambda bi, g, i: (g, 0, 0))],
        out_specs=wide,
        out_shape=jax.ShapeDtypeStruct((b, s, ATTN_WIDTH), jnp.bfloat16),
        scratch_shapes=[pltpu.VMEM((gps, s, HEAD_DIM + LANES), jnp.bfloat16),
                        pltpu.VMEM((gps, s, HEAD_DIM + LANES), jnp.bfloat16),
                        pltpu.VMEM((gps, s, HEAD_DIM + LANES), jnp.bfloat16),
                        pltpu.VMEM((gps, n_c, HEAD_DIM + LANES), jnp.bfloat16),
                        pltpu.VMEM((gps, rows, s), jnp.float32),
                        pltpu.VMEM((gps, rows, LANES), jnp.float32),
                        pltpu.VMEM((gps, rows, HEAD_DIM + LANES), jnp.float32)],
        compiler_params=_params("arbitrary", "arbitrary", "arbitrary"),
        name="nsa_attention",
    )(qkv, qkv, qkv, qkv, qkv, kcb, vcb, gate, zu, aggt, blockid, wbias, gsel)


def _merge_kernel(a_ref, c_ref, wa_ref, wc_ref, g0_ref, g1_ref, o_ref, wa_scr, wc_scr):
    @pl.when(pl.program_id(1) == 0)
    def _():
        _cast_weight(wa_scr, wa_ref)
        _cast_weight(wc_scr, wc_ref)

    ya = jnp.dot(a_ref[...], wa_scr[...], preferred_element_type=jnp.float32)
    yc = jnp.dot(c_ref[...], wc_scr[...], preferred_element_type=jnp.float32)
    o_ref[...] = (_sigmoid(g0_ref[...]) * ya + _sigmoid(g1_ref[...]) * yc).astype(o_ref.dtype)


def _merge(a, c, wa, wc, zg, gcol0, tm=MERGE_TM, tn=MERGE_TN):
    m, k = a.shape
    n = wa.shape[-1]
    g0 = gcol0 // tn
    g1 = (gcol0 + n) // tn
    return pl.pallas_call(
        _merge_kernel,
        grid=(n // tn, m // tm),
        in_specs=[pl.BlockSpec((tm, k), lambda j, i: (i, 0)),
                  pl.BlockSpec((tm, k), lambda j, i: (i, 0)),
                  pl.BlockSpec((None, k, tn), lambda j, i: (0, 0, j), pipeline_mode=pl.Buffered(1)),
                  pl.BlockSpec((None, k, tn), lambda j, i: (0, 0, j), pipeline_mode=pl.Buffered(1)),
                  pl.BlockSpec((tm, tn), lambda j, i: (i, g0 + j)),
                  pl.BlockSpec((tm, tn), lambda j, i: (i, g1 + j))],
        out_specs=pl.BlockSpec((tm, tn), lambda j, i: (i, j)),
        out_shape=jax.ShapeDtypeStruct((m, n), jnp.bfloat16),
        scratch_shapes=[pltpu.VMEM((k, tn), jnp.bfloat16), pltpu.VMEM((k, tn), jnp.bfloat16)],
        compiler_params=_params("arbitrary", "arbitrary"),
        name="merge",
    )(a, c, wa, wc, zg, zg)


def _out_kernel(y_ref, w_ref, x_ref, g_ref, o_ref, w_scr):
    @pl.when(pl.program_id(0) == 0)
    def _():
        _cast_weight(w_scr, w_ref)

    r = x_ref[...] + jnp.dot(y_ref[...], w_scr[...], preferred_element_type=jnp.float32)
    ms = jnp.mean(r * r, axis=-1, keepdims=True)
    o_ref[...] = r * lax.rsqrt(ms + RMS_EPS) * g_ref[...]


def _out_proj(y, w, x2d, g, tm=OUT_TM):
    m, k = y.shape
    n = w.shape[-1]
    return pl.pallas_call(
        _out_kernel,
        grid=(m // tm,),
        in_specs=[pl.BlockSpec((tm, k), lambda i: (i, 0)),
                  pl.BlockSpec((None, k, n), lambda i: (0, 0, 0), pipeline_mode=pl.Buffered(1)),
                  pl.BlockSpec((tm, n), lambda i: (i, 0)),
                  pl.BlockSpec((1, n), lambda i: (0, 0))],
        out_specs=pl.BlockSpec((tm, n), lambda i: (i, 0)),
        out_shape=jax.ShapeDtypeStruct((m, n), jnp.float32),
        scratch_shapes=[pltpu.VMEM((k, n), jnp.bfloat16)],
        compiler_params=_params("arbitrary"),
        name="out_proj",
    )(y, w, x2d, g.reshape(1, n))


def _attention_constants(s):
    n_slc = s // SEL_BLOCK
    n_cmp = (s - CMP_BLOCK) // CMP_STRIDE + 1
    c0 = np.arange(s // CMP_STRIDE)[None, :] * CMP_STRIDE
    s0 = np.arange(n_slc)[:, None] * SEL_BLOCK
    overlap = np.maximum(0, np.minimum(c0 + CMP_BLOCK, s0 + SEL_BLOCK) - np.maximum(c0, s0))
    aggt = (overlap / CMP_BLOCK) * (np.arange(s // CMP_STRIDE)[None, :] < n_cmp)
    blockid = (np.arange(s)[:, None] // SEL_BLOCK == np.arange(LANES)[None, :]).astype(np.float32)
    n_pat = WINDOW // Q_TILE
    pats = []
    for p in range(n_pat + 1):
        t = p * Q_TILE + np.arange(Q_TILE)[:, None]
        kpos = max(p * Q_TILE - WINDOW, 0) + np.arange(WINDOW + Q_TILE)[None, :]
        pats.append(np.where((kpos <= t) & (kpos > t - WINDOW), 0.0, MASKED))
    n_bh = 3 * HEADS_PER_GROUP
    gsel = np.zeros((N_KV_GROUPS, 2 * LANES, n_bh * LANES), np.float32)
    for g in range(N_KV_GROUPS):
        for br in range(3):
            for h in range(HEADS_PER_GROUP):
                src = br * N_HEADS + g * HEADS_PER_GROUP + h
                c = br * HEADS_PER_GROUP + h
                gsel[g, [src, LANES + src], c * LANES:(c + 1) * LANES] = 1.0
    return (jnp.asarray(aggt, jnp.bfloat16), jnp.asarray(blockid, jnp.bfloat16),
            jnp.asarray(np.stack(pats), jnp.float32), jnp.asarray(gsel, jnp.bfloat16))


def _layer(x, norm_g, w_in, b_in, pe_k, w1_k, w2_k, pe_v, w1_v, w2_v,
           conv_w, conv_b, p_attn, p_conv, w_o, out_g):
    b, s, d = x.shape
    m = b * s
    bf = jnp.bfloat16
    x2d = x.reshape(m, d)
    tn = PROJ_TN

    c_q = ATTN_WIDTH
    c_cmp = c_q + 2 * KV_WIDTH
    c_kv = c_cmp + 4 * KV_WIDTH
    n_gate = 3 * N_HEADS
    c_z = c_kv + n_gate
    c_conv = c_z + ATTN_WIDTH
    c_merge = c_conv + 4 * CONV_WIDTH
    assert c_cmp - c_q == tn and c_q % tn == 0 and (c_kv - c_cmp) % tn == 0
    wt = jnp.swapaxes(w_in, 1, 2).reshape(w_in.shape[2], w_in.shape[1])

    hn, kvc, gate = _norm_proj(x2d, norm_g, wt, jnp.concatenate([b_in[c_q:c_cmp], b_in[c_kv:c_kv + LANES]]),
                               c_q, c_cmp - c_q, c_kv)
    kvc = kvc.reshape(b, s, -1)
    q_tiles = c_q // tn
    qkv = _proj(hn, wt, jnp.concatenate([b_in[:c_q], b_in[c_cmp:c_kv]]),
                lambda j: jnp.where(j < q_tiles, j * tn, c_cmp + (j - q_tiles) * tn),
                q_tiles + (c_kv - c_cmp) // tn, tn, bf,
                scaled_tiles=q_tiles, scale=SCALE * math.log2(math.e), name="proj_qkv").reshape(b, s, -1)
    z_tiles = ATTN_WIDTH // tn
    zg = _proj(hn, wt, jnp.concatenate([b_in[c_z:c_conv], b_in[c_merge:]]),
               lambda j: jnp.where(j < z_tiles, c_z + j * tn, c_merge + (j - z_tiles) * tn),
               z_tiles + 2 * D_MODEL // tn, tn, jnp.float32, tm=PROJ_TM_F32, name="proj_zg").reshape(b, s, -1)
    cw = CONV_TN
    b_conv = b_in[c_conv:c_merge].reshape(4, CONV_WIDTH // cw, cw).transpose(1, 0, 2).reshape(1, -1)
    conv_starts = [lambda j, p=p: c_conv + p * CONV_WIDTH + j * cw for p in range(4)]
    c = _proj_conv(hn, wt, b_conv, conv_w, conv_b, conv_starts, s, cw=cw)

    kvcb = _compress(kvc, pe_k, w1_k, w2_k, pe_v, w1_v, w2_v)
    kcb, vcb = kvcb[0], kvcb[1]

    aggt, blockid, wbias, gsel = _attention_constants(s)
    a = _attention(qkv, kcb, vcb, gate.reshape(b, s, LANES), zg, aggt, blockid, wbias, gsel)

    y = _merge(a.reshape(m, -1), c, p_attn, p_conv, zg.reshape(m, -1), ATTN_WIDTH)
    return _out_proj(y, w_o, x2d, out_g).reshape(b, s, d)


def kernel(x, norm_g, w_in, b_in, cmp_pe_k, cmp_w1_k, cmp_w2_k, cmp_pe_v, cmp_w1_v, cmp_w2_v,
           conv_w, conv_b, p_attn, p_conv, w_o, final_g):
    assert norm_g.shape[0] == 1, "single-layer block"
    return _layer(x, norm_g[0], w_in, b_in[0], cmp_pe_k[0], cmp_w1_k[0], cmp_w2_k[0],
                  cmp_pe_v[0], cmp_w1_v[0], cmp_w2_v[0], conv_w[0], conv_b[0],
                  p_attn, p_conv, w_o, final_g)
```

```python
import functools
import math

import numpy as np
import jax
import jax.numpy as jnp
from jax import lax
from jax.experimental import pallas as pl
from jax.experimental.pallas import tpu as pltpu

D_MODEL = 2048
N_HEADS = 16
HEAD_DIM = 128
N_KV_GROUPS = 4
HEADS_PER_GROUP = N_HEADS // N_KV_GROUPS
ATTN_WIDTH = N_HEADS * HEAD_DIM
KV_WIDTH = N_KV_GROUPS * HEAD_DIM
GROUP_WIDTH = HEADS_PER_GROUP * HEAD_DIM
CMP_BLOCK = 32
CMP_STRIDE = 16
CMP_HIDDEN = 256
SEL_BLOCK = 64
SEL_TOP_N = 8
WINDOW = 512
SCALE = HEAD_DIM ** -0.5
CONV_WIDTH = D_MODEL
CONV_K = 3
RMS_EPS = 1e-6
FORCED_SCORE = 1e4
MASKED = -(2.0 ** 100)

LANES = 128
SUBLANES = 8
Q_TILE = 256
SEL_CHUNK = 256
GROUPS_PER_STEP = 2
W_ROWS = 512
W_ALIGN = 16
VMEM_LIMIT = 56 * 1024 * 1024
PROJ_TM, PROJ_TN = 2048, 1024
PROJ_TM_F32 = 1024
CONV_TN = 256
NORM_TM = 1024
MERGE_TM, MERGE_TN = 512, 1024
OUT_TM = 512

_NT = (((1,), (1,)), ((), ()))


def _params(*sem):
    return pltpu.CompilerParams(dimension_semantics=sem, vmem_limit_bytes=VMEM_LIMIT)


def _sigmoid(x):
    return 1.0 / (1.0 + jnp.exp(-x))


def _silu(x):
    return x * _sigmoid(x)


def _cast_weight(w_scr, w_ref, row0=0):
    rows = w_ref.shape[0]
    for r in range(0, rows, W_ROWS):
        n = min(W_ROWS, rows - r)
        w_scr[row0 + r:row0 + r + n, :] = w_ref[r:r + n, :].astype(w_scr.dtype)


def _weight_specs(k, piece_rows, start_fns):
    return [pl.BlockSpec((pl.Element(piece_rows), pl.Element(k)),
                         lambda j, i, f=f: (pl.multiple_of(f(j), W_ALIGN), 0))
            for f in start_fns]


def _proj_kernel(a_ref, b_ref, w_ref, o_ref, w_scr, *, scaled_tiles, scale):
    @pl.when(pl.program_id(1) == 0)
    def _():
        _cast_weight(w_scr, w_ref)

    r = lax.dot_general(a_ref[...], w_scr[...], _NT, preferred_element_type=jnp.float32) + b_ref[...]
    if scaled_tiles:
        r = r * jnp.where(pl.program_id(0) < scaled_tiles, scale, 1.0)
    o_ref[...] = r.astype(o_ref.dtype)


def _proj(a, wt, bias, start_fn, n_tiles, tn, out_dtype, tm=PROJ_TM, scaled_tiles=0, scale=1.0,
          name="proj"):
    m, k = a.shape
    return pl.pallas_call(
        functools.partial(_proj_kernel, scaled_tiles=scaled_tiles, scale=scale),
        grid=(n_tiles, m // tm),
        in_specs=[pl.BlockSpec((tm, k), lambda j, i: (i, 0)),
                  pl.BlockSpec((1, tn), lambda j, i: (0, j))] + _weight_specs(k, tn, [start_fn]),
        out_specs=pl.BlockSpec((tm, tn), lambda j, i: (i, j)),
        out_shape=jax.ShapeDtypeStruct((m, n_tiles * tn), out_dtype),
        scratch_shapes=[pltpu.VMEM((tn, k), jnp.bfloat16)],
        compiler_params=_params("arbitrary", "arbitrary"),
        name=name,
    )(a, bias.reshape(1, n_tiles * tn), wt)


def _norm_proj_kernel(x_ref, g_ref, b_ref, wkv_ref, wg_ref, hn_ref, kv_ref, gate_ref, w_scr):
    n_kv = kv_ref.shape[1]

    @pl.when(pl.program_id(0) == 0)
    def _():
        _cast_weight(w_scr, wkv_ref)
        _cast_weight(w_scr, wg_ref, n_kv)

    x = x_ref[...]
    ms = jnp.mean(x * x, axis=-1, keepdims=True)
    hn = (x * lax.rsqrt(ms + RMS_EPS) * g_ref[...]).astype(hn_ref.dtype)
    hn_ref[...] = hn
    r = lax.dot_general(hn, w_scr[...], _NT, preferred_element_type=jnp.float32) + b_ref[...]
    kv_ref[...] = r[:, :n_kv]
    gate_ref[...] = _sigmoid(r[:, n_kv:])


def _norm_proj(x2d, g, wt, bias, kv_start, n_kv, gate_start, tm=NORM_TM):
    m, k = x2d.shape
    once = dict(pipeline_mode=pl.Buffered(1))
    return pl.pallas_call(
        _norm_proj_kernel,
        grid=(m // tm,),
        in_specs=[pl.BlockSpec((tm, k), lambda i: (i, 0)),
                  pl.BlockSpec((1, k), lambda i: (0, 0)),
                  pl.BlockSpec((1, n_kv + LANES), lambda i: (0, 0)),
                  pl.BlockSpec((pl.Element(n_kv), pl.Element(k)), lambda i: (kv_start, 0), **once),
                  pl.BlockSpec((pl.Element(LANES), pl.Element(k)), lambda i: (gate_start, 0), **once)],
        out_specs=[pl.BlockSpec((tm, k), lambda i: (i, 0)),
                   pl.BlockSpec((tm, n_kv), lambda i: (i, 0)),
                   pl.BlockSpec((tm, LANES), lambda i: (i, 0))],
        out_shape=[jax.ShapeDtypeStruct((m, k), jnp.bfloat16),
                   jax.ShapeDtypeStruct((m, n_kv), jnp.float32),
                   jax.ShapeDtypeStruct((m, LANES), jnp.float32)],
        scratch_shapes=[pltpu.VMEM((n_kv + LANES, k), jnp.bfloat16)],
        compiler_params=_params("arbitrary"),
        name="norm_proj",
    )(x2d, g.reshape(1, k), bias.reshape(1, n_kv + LANES), wt, wt)


def _proj_conv_kernel(a_ref, b_ref, cw_ref, cb_ref, wu_ref, wcc_ref, wcb_ref, wz_ref, *rest, seq):
    n_side = (len(rest) - 3) // 2
    side_in, o_ref, side_out = rest[:n_side], rest[n_side], rest[n_side + 1:2 * n_side + 1]
    w_scr, halo_scr = rest[2 * n_side + 1:]
    tm, cw = o_ref.shape
    i = pl.program_id(1)

    for src, dst in zip(side_in, side_out):
        dst[...] = src[...].astype(dst.dtype)

    @pl.when(i == 0)
    def _():
        for p, w_ref in enumerate((wu_ref, wcc_ref, wcb_ref, wz_ref)):
            _cast_weight(w_scr, w_ref, p * cw)

    @pl.when((i * tm) % seq == 0)
    def _():
        halo_scr[...] = jnp.zeros(halo_scr.shape, jnp.float32)

    r = lax.dot_general(a_ref[...], w_scr[...], _NT, preferred_element_type=jnp.float32) + b_ref[...]
    u, cc, cb, z = (r[:, p * cw:(p + 1) * cw] for p in range(4))
    v = cc * u
    prev = halo_scr[...]
    halo_scr[...] = v[tm - SUBLANES:, :]
    row = lax.broadcasted_iota(jnp.int32, (tm, 1), 0)
    y = cw_ref[CONV_K - 1:CONV_K, :] * v
    for d in range(1, CONV_K):
        shifted = pltpu.roll(v, d, axis=0)
        for t in range(d):
            shifted = jnp.where(row == t, prev[SUBLANES - d + t:SUBLANES - d + t + 1, :], shifted)
        y = y + cw_ref[CONV_K - 1 - d:CONV_K - d, :] * shifted
    o_ref[...] = (cb * (y + cb_ref[...]) * _silu(z)).astype(o_ref.dtype)


def _proj_conv(a, wt, bias_tiles, conv_w, conv_b, start_fns, seq, side_weights, tm=PROJ_TM, cw=CONV_TN):
    m, k = a.shape
    n_tiles = CONV_WIDTH // cw
    n_rt = m // tm
    assert seq % tm == 0 and CONV_K - 1 <= SUBLANES and len(start_fns) == 4
    n_steps = n_tiles * n_rt
    slab = lambda w: w.shape[1] // n_steps
    assert all(w.shape[1] % (n_steps * 16) == 0 for w in side_weights)
    res = pl.pallas_call(
        functools.partial(_proj_conv_kernel, seq=seq),
        grid=(n_tiles, n_rt),
        in_specs=[pl.BlockSpec((tm, k), lambda j, i: (i, 0)),
                  pl.BlockSpec((1, 4 * cw), lambda j, i: (0, j)),
                  pl.BlockSpec((CONV_K, cw), lambda j, i: (0, j)),
                  pl.BlockSpec((1, cw), lambda j, i: (0, j))] + _weight_specs(k, cw, start_fns)
                 + [pl.BlockSpec((None, slab(w), w.shape[2]), lambda j, i: (0, j * n_rt + i, 0))
                    for w in side_weights],
        out_specs=[pl.BlockSpec((tm, cw), lambda j, i: (i, j))]
                  + [pl.BlockSpec((slab(w), w.shape[2]), lambda j, i: (j * n_rt + i, 0)) for w in side_weights],
        out_shape=[jax.ShapeDtypeStruct((m, CONV_WIDTH), jnp.bfloat16)]
                  + [jax.ShapeDtypeStruct(w.shape[1:], jnp.bfloat16) for w in side_weights],
        scratch_shapes=[pltpu.VMEM((4 * cw, k), jnp.bfloat16),
                        pltpu.VMEM((SUBLANES, cw), jnp.float32)],
        compiler_params=_params("arbitrary", "arbitrary"),
        name="proj_conv",
    )(a, bias_tiles, conv_w, conv_b.reshape(1, CONV_WIDTH), wt, wt, wt, wt, *side_weights)
    return res[0], res[1:]


def _compress_kernel(*refs):
    n_groups = N_KV_GROUPS
    x_refs = refs[:n_groups]
    pek_ref, w1k_ref, w2k_ref, pev_ref, w1v_ref, w2v_ref, o_ref, pe_scr, w1_scr, w2_scr = refs[n_groups:]
    for which, (pe_ref, w1_ref, w2_ref) in enumerate(((pek_ref, w1k_ref, w2k_ref),
                                                      (pev_ref, w1v_ref, w2v_ref))):
        @pl.when((pl.program_id(1) == 0) & (pl.program_id(0) == which))
        def _(pe_ref=pe_ref, w1_ref=w1_ref, w2_ref=w2_ref):
            pe_scr[...] = pe_ref[...]
            for l in range(CMP_BLOCK):
                w1_scr[l] = w1_ref[l].astype(w1_scr.dtype)
            w2_scr[...] = w2_ref[...].astype(w2_scr.dtype)

    half = CMP_BLOCK // 2
    n_chunks = x_refs[0].shape[1] // CMP_STRIDE
    rows = n_groups * n_chunks
    top = jnp.zeros((rows, CMP_HIDDEN), jnp.float32)
    bot = jnp.zeros((rows, CMP_HIDDEN), jnp.float32)
    for l in range(half):
        xl = jnp.concatenate([x_ref[0, pl.ds(l, n_chunks, stride=CMP_STRIDE), :] for x_ref in x_refs], axis=0)
        a = (xl + pe_scr[l:l + 1, :]).astype(jnp.bfloat16)
        c = (xl + pe_scr[half + l:half + l + 1, :]).astype(jnp.bfloat16)
        top = top + jnp.dot(a, w1_scr[l], preferred_element_type=jnp.float32)
        bot = bot + jnp.dot(c, w1_scr[half + l], preferred_element_type=jnp.float32)
    for g in range(n_groups):
        r0 = g * n_chunks
        h = top[r0:r0 + n_chunks] + pltpu.roll(bot[r0:r0 + n_chunks], n_chunks - 1, axis=0)
        h = _silu(h).astype(jnp.bfloat16)
        o_ref[0, 0, g] = jnp.dot(h, w2_scr[...], preferred_element_type=jnp.float32).astype(o_ref.dtype)


def _compress(kv, pe_k, w1_k, w2_k, pe_v, w1_v, w2_v):
    b, s, _ = kv.shape
    whole = lambda shape: pl.BlockSpec(shape, lambda t, bi: (0,) * len(shape))
    w1_shape = (CMP_BLOCK, HEAD_DIM, CMP_HIDDEN)
    n_c = s // CMP_STRIDE
    return pl.pallas_call(
        _compress_kernel,
        grid=(2, b),
        in_specs=[pl.BlockSpec((1, s, HEAD_DIM), lambda t, bi, g=g: (bi, 0, t * N_KV_GROUPS + g))
                  for g in range(N_KV_GROUPS)] + [
                  whole((CMP_BLOCK, HEAD_DIM)), whole(w1_shape), whole((CMP_HIDDEN, HEAD_DIM)),
                  whole((CMP_BLOCK, HEAD_DIM)), whole(w1_shape), whole((CMP_HIDDEN, HEAD_DIM))],
        out_specs=pl.BlockSpec((1, 1, N_KV_GROUPS, n_c, HEAD_DIM), lambda t, bi: (t, bi, 0, 0, 0)),
        out_shape=jax.ShapeDtypeStruct((2, b, N_KV_GROUPS, n_c, HEAD_DIM), jnp.bfloat16),
        scratch_shapes=[pltpu.VMEM((CMP_BLOCK, HEAD_DIM), jnp.float32),
                        pltpu.VMEM(w1_shape, jnp.bfloat16),
                        pltpu.VMEM((CMP_HIDDEN, HEAD_DIM), jnp.bfloat16)],
        compiler_params=_params("arbitrary", "arbitrary"),
        name="compress",
    )(*([kv] * N_KV_GROUPS), pe_k, w1_k.reshape(w1_shape), w2_k, pe_v, w1_v.reshape(w1_shape), w2_v)


def _attn_kernel(q_ref, ks_ref, vs_ref, kw_ref, vw_ref, kc_ref, vc_ref, gate_ref, z_ref,
                 aggt_ref, blockid_ref, wbias_ref, gsel_ref, o_ref,
                 kaug_scr, vsaug_scr, vwaug_scr, vcaug_scr, s_scr, m_scr, acc_scr):
    tq = Q_TILE
    hpg = HEADS_PER_GROUP
    gw = GROUP_WIDTH
    seq = kw_ref.shape[1]
    gps = kc_ref.shape[1]
    i = pl.program_id(2)
    t0 = i * tq

    @pl.when(i == 0)
    def _():
        ones = jnp.ones((seq, LANES), jnp.bfloat16)
        for gi in range(gps):
            cols = slice(gi * HEAD_DIM, (gi + 1) * HEAD_DIM)
            kaug_scr[gi, :, :HEAD_DIM] = ks_ref[0, :, cols]
            kaug_scr[gi, :, HEAD_DIM:] = blockid_ref[...]
            vsaug_scr[gi, :, :HEAD_DIM] = vs_ref[0, :, cols]
            vsaug_scr[gi, :, HEAD_DIM:] = ones
            vwaug_scr[gi, :, :HEAD_DIM] = vw_ref[0, :, cols]
            vwaug_scr[gi, :, HEAD_DIM:] = ones
            vcaug_scr[gi, :, :HEAD_DIM] = vc_ref[0, gi]
            vcaug_scr[gi, :, HEAD_DIM:] = ones[:vcaug_scr.shape[1]]

    t_q = t0 + lax.broadcasted_iota(jnp.int32, (tq, 1), 0)
    t_s = jnp.concatenate([t_q] * hpg, axis=0)
    lane = lax.broadcasted_iota(jnp.int32, (1, LANES), 1)
    n_cmp = (seq - CMP_BLOCK) // CMP_STRIDE + 1
    n_slc = seq // SEL_BLOCK
    n_win = WINDOW + tq
    start = pl.multiple_of(jnp.maximum(t0 - WINDOW, 0), tq)
    w_bias = jnp.concatenate([wbias_ref[0]] * hpg, axis=0)
    vis_c = (lane * CMP_STRIDE + (CMP_BLOCK - 1) <= t_s) & (lane < n_cmp)
    aggt = aggt_ref[...]
    blk = lax.broadcasted_iota(jnp.int32, (n_slc, 1), 0)
    t_l = t0 + lax.broadcasted_iota(jnp.int32, (1, tq), 1)
    cur = t_l >> int(math.log2(SEL_BLOCK))
    causal = blk * SEL_BLOCK <= t_l
    forced = ((blk == 0) | (blk == cur) | (blk == cur - 1)) & causal
    gate = gate_ref[0]
    gate_hi = gate.astype(jnp.bfloat16)
    gate_lo = (gate - gate_hi.astype(jnp.float32)).astype(jnp.bfloat16)
    gate_hl = jnp.concatenate([gate_hi, gate_lo], axis=1)

    def first_region(gi):
        q = q_ref[0, :, gi * gw:(gi + 1) * gw]
        qs = jnp.concatenate([q[:, h * HEAD_DIM:(h + 1) * HEAD_DIM] for h in range(hpg)], axis=0)

        kw = kw_ref[0, pl.ds(start, n_win), gi * HEAD_DIM:(gi + 1) * HEAD_DIM]
        sw = lax.dot_general(qs, kw, _NT, preferred_element_type=jnp.float32) + w_bias
        p_w = jnp.exp2(sw - jnp.max(sw, axis=-1, keepdims=True))
        pv_w = jnp.dot(p_w.astype(jnp.bfloat16), vwaug_scr[gi, pl.ds(start, n_win), :],
                       preferred_element_type=jnp.float32)
        o_win = pv_w[:, :HEAD_DIM] * (1.0 / pv_w[:, HEAD_DIM:])

        sc = lax.dot_general(qs, kc_ref[0, gi], _NT, preferred_element_type=jnp.float32)
        sc = jnp.where(vis_c, sc, MASKED)
        p_c = jnp.where(vis_c, jnp.exp2(sc - jnp.max(sc, axis=-1, keepdims=True)), 0.0)
        pv_c = jnp.dot(p_c.astype(jnp.bfloat16), vcaug_scr[gi],
                       preferred_element_type=jnp.float32)
        l_c = pv_c[:, HEAD_DIM:]
        inv_c = jnp.where(l_c > 0.0, 1.0 / l_c, 0.0)
        o_cmp = pv_c[:, :HEAD_DIM] * inv_c
        p_c = p_c * inv_c

        p_sum = p_c[0:tq]
        for h in range(1, hpg):
            p_sum = p_sum + p_c[h * tq:(h + 1) * tq]
        hi = p_sum.astype(jnp.bfloat16)
        r1 = p_sum - hi.astype(jnp.float32)
        mid = r1.astype(jnp.bfloat16)
        lo = (r1 - mid.astype(jnp.float32)).astype(jnp.bfloat16)
        imp = (lax.dot_general(aggt, hi, _NT, preferred_element_type=jnp.float32)
               + lax.dot_general(aggt, mid, _NT, preferred_element_type=jnp.float32)
               + lax.dot_general(aggt, lo, _NT, preferred_element_type=jnp.float32))
        imp = jnp.where(forced, FORCED_SCORE, imp)
        imp = jnp.where(causal, imp, -1.0)

        bias_slabs = []
        for v in range(n_slc // SUBLANES):
            lo_row = v * SUBLANES
            slab = imp[lo_row:lo_row + SUBLANES]
            sub = lo_row + lax.broadcasted_iota(jnp.int32, (SUBLANES, 1), 0)
            rank = jnp.zeros((SUBLANES, tq), jnp.float32)
            for jp in range(n_slc):
                other = jnp.broadcast_to(imp[jp:jp + 1], (SUBLANES, tq))
                if jp < lo_row:
                    ahead = jnp.where(other >= slab, 1.0, 0.0)
                elif jp >= lo_row + SUBLANES:
                    ahead = jnp.where(other > slab, 1.0, 0.0)
                else:
                    tie = jnp.where(sub > jp, 1.0, 0.0)
                    ahead = jnp.where(other > slab, 1.0, jnp.where(other == slab, tie, 0.0))
                rank = rank + ahead
            bias_slabs.append(jnp.where(rank < float(SEL_TOP_N), 0.0, MASKED))
        bias_t = jnp.concatenate(bias_slabs + [jnp.zeros((LANES - n_slc, tq), jnp.float32)], axis=0)
        sel_bias = bias_t.T.astype(jnp.bfloat16)
        qa = jnp.concatenate([qs, jnp.concatenate([sel_bias] * hpg, axis=0)], axis=1)
        return o_win, o_cmp, qa

    first = [first_region(gi) for gi in range(gps)]

    n_full = t0 // SEL_CHUNK
    groups = [g for g in (4, 2, 1) if g <= max((seq - tq) // SEL_CHUNK, 1)]

    def scores(gi, c):
        base = pl.multiple_of(c * SEL_CHUNK, SEL_CHUNK)
        return base, lax.dot_general(first[gi][2], kaug_scr[gi, pl.ds(base, SEL_CHUNK), :], _NT,
                                     preferred_element_type=jnp.float32)

    def keep(gi, base, s, m):
        s_scr[gi, :, pl.ds(base, SEL_CHUNK)] = s
        for k in range(SEL_CHUNK // LANES):
            m = jnp.maximum(m, s[:, k * LANES:(k + 1) * LANES])
        return m

    for gi in range(gps):
        base_d, s_d = scores(gi, n_full)
        kpos_d = base_d + lax.broadcasted_iota(jnp.int32, (1, SEL_CHUNK), 1)
        m_scr[gi] = keep(gi, base_d, jnp.where(kpos_d <= t_s, s_d, MASKED),
                         jnp.full(m_scr.shape[1:], MASKED, jnp.float32))
    done = 0
    for g in groups:
        @pl.when((n_full & g) != 0)
        def _(done=done, g=g):
            for gi in range(gps):
                m = m_scr[gi]
                for u in range(g):
                    m = keep(gi, *scores(gi, done + u), m)
                m_scr[gi] = m
        done = done + (n_full & g)
    for gi in range(gps):
        m_scr[gi] = jnp.broadcast_to(jnp.max(m_scr[gi], axis=-1, keepdims=True), m_scr.shape[1:])

    def values(gi, c, m):
        base = pl.multiple_of(c * SEL_CHUNK, SEL_CHUNK)
        ps = [jnp.exp2(s_scr[gi, :, pl.ds(base + k * LANES, LANES)] - m).astype(jnp.bfloat16)
              for k in range(SEL_CHUNK // LANES)]
        return jnp.dot(jnp.concatenate(ps, axis=1), vsaug_scr[gi, pl.ds(base, SEL_CHUNK), :],
                       preferred_element_type=jnp.float32)

    for gi in range(gps):
        acc_scr[gi] = values(gi, n_full, m_scr[gi])
    done = 0
    for g in groups:
        @pl.when((n_full & g) != 0)
        def _(done=done, g=g):
            for gi in range(gps):
                m, acc = m_scr[gi], acc_scr[gi]
                for u in range(g):
                    acc = acc + values(gi, done + u, m)
                acc_scr[gi] = acc
        done = done + (n_full & g)

    for gi in range(gps):
        o_win, o_cmp, _ = first[gi]
        o_sel = acc_scr[gi, :, :HEAD_DIM] * (1.0 / acc_scr[gi, :, HEAD_DIM:])
        g_exp = jnp.dot(gate_hl, gsel_ref[gi], preferred_element_type=jnp.float32)
        g_col = lambda c: g_exp[:, c * LANES:(c + 1) * LANES]
        outs = []
        for h in range(hpg):
            r0 = h * tq
            outs.append(g_col(h) * o_cmp[r0:r0 + tq]
                        + g_col(hpg + h) * o_sel[r0:r0 + tq]
                        + g_col(2 * hpg + h) * o_win[r0:r0 + tq])
        o = jnp.concatenate(outs, axis=1)
        cols = slice(gi * gw, (gi + 1) * gw)
        o_ref[0, :, cols] = (o * _silu(z_ref[0, :, cols])).astype(o_ref.dtype)


def _attention(qkv, kcb, vcb, gate, zu, aggt, blockid, wbias, gsel):
    b, s, _ = qkv.shape
    gps = GROUPS_PER_STEP
    assert SEL_CHUNK % Q_TILE == 0 and WINDOW % Q_TILE == 0 and s % SEL_CHUNK == 0
    assert N_KV_GROUPS % gps == 0
    kvw = gps * HEAD_DIM
    qb = ATTN_WIDTH // kvw
    kvb = KV_WIDTH // kvw
    n_c = s // CMP_STRIDE
    rows = HEADS_PER_GROUP * Q_TILE
    n_pat = WINDOW // Q_TILE
    grid = (b, N_KV_GROUPS // gps, s // Q_TILE)
    kv_spec = lambda off: pl.BlockSpec((1, s, kvw), lambda bi, g, i: (bi, 0, off + g))
    wide = pl.BlockSpec((1, Q_TILE, gps * GROUP_WIDTH), lambda bi, g, i: (bi, i, g))
    return pl.pallas_call(
        _attn_kernel,
        grid=grid,
        in_specs=[wide,
                  kv_spec(qb), kv_spec(qb + kvb), kv_spec(qb + 2 * kvb), kv_spec(qb + 3 * kvb),
                  pl.BlockSpec((1, gps, n_c, HEAD_DIM), lambda bi, g, i: (bi, g, 0, 0)),
                  pl.BlockSpec((1, gps, n_c, HEAD_DIM), lambda bi, g, i: (bi, g, 0, 0)),
                  pl.BlockSpec((1, Q_TILE, LANES), lambda bi, g, i: (bi, i, 0)),
                  wide,
                  pl.BlockSpec(aggt.shape, lambda bi, g, i: (0, 0)),
                  pl.BlockSpec(blockid.shape, lambda bi, g, i: (0, 0)),
                  pl.BlockSpec((1, Q_TILE, WINDOW + Q_TILE),
                               lambda bi, g, i: (jnp.minimum(i, n_pat), 0, 0)),
                  pl.BlockSpec((gps,) + gsel.shape[1:], lambda bi, g, i: (g, 0, 0))],
        out_specs=wide,
        out_shape=jax.ShapeDtypeStruct((b, s, ATTN_WIDTH), jnp.bfloat16),
        scratch_shapes=[pltpu.VMEM((gps, s, HEAD_DIM + LANES), jnp.bfloat16),
                        pltpu.VMEM((gps, s, HEAD_DIM + LANES), jnp.bfloat16),
                        pltpu.VMEM((gps, s, HEAD_DIM + LANES), jnp.bfloat16),
                        pltpu.VMEM((gps, n_c, HEAD_DIM + LANES), jnp.bfloat16),
                        pltpu.VMEM((gps, rows, s), jnp.float32),
                        pltpu.VMEM((gps, rows, LANES), jnp.float32),
                        pltpu.VMEM((gps, rows, HEAD_DIM + LANES), jnp.float32)],
        compiler_params=_params("arbitrary", "arbitrary", "arbitrary"),
        name="nsa_attention",
    )(qkv, qkv, qkv, qkv, qkv, kcb, vcb, gate, zu, aggt, blockid, wbias, gsel)


def _merge_kernel(a_ref, c_ref, wa_ref, wc_ref, g0_ref, g1_ref, o_ref):
    ya = jnp.dot(a_ref[...], wa_ref[...], preferred_element_type=jnp.float32)
    yc = jnp.dot(c_ref[...], wc_ref[...], preferred_element_type=jnp.float32)
    o_ref[...] = (_sigmoid(g0_ref[...]) * ya + _sigmoid(g1_ref[...]) * yc).astype(o_ref.dtype)


def _merge(a, c, wa, wc, zg, gcol0, tm=MERGE_TM, tn=MERGE_TN):
    m, k = a.shape
    n = wa.shape[-1]
    g0 = gcol0 // tn
    g1 = (gcol0 + n) // tn
    return pl.pallas_call(
        _merge_kernel,
        grid=(n // tn, m // tm),
        in_specs=[pl.BlockSpec((tm, k), lambda j, i: (i, 0)),
                  pl.BlockSpec((tm, k), lambda j, i: (i, 0)),
                  pl.BlockSpec((k, tn), lambda j, i: (0, j)),
                  pl.BlockSpec((k, tn), lambda j, i: (0, j)),
                  pl.BlockSpec((tm, tn), lambda j, i: (i, g0 + j)),
                  pl.BlockSpec((tm, tn), lambda j, i: (i, g1 + j))],
        out_specs=pl.BlockSpec((tm, tn), lambda j, i: (i, j)),
        out_shape=jax.ShapeDtypeStruct((m, n), jnp.bfloat16),
        compiler_params=_params("arbitrary", "arbitrary"),
        name="merge",
    )(a, c, wa, wc, zg, zg)


def _out_kernel(y_ref, w_ref, x_ref, g_ref, o_ref):
    r = x_ref[...] + jnp.dot(y_ref[...], w_ref[...], preferred_element_type=jnp.float32)
    ms = jnp.mean(r * r, axis=-1, keepdims=True)
    o_ref[...] = r * lax.rsqrt(ms + RMS_EPS) * g_ref[...]


def _out_proj(y, w, x2d, g, tm=OUT_TM):
    m, k = y.shape
    n = w.shape[-1]
    return pl.pallas_call(
        _out_kernel,
        grid=(m // tm,),
        in_specs=[pl.BlockSpec((tm, k), lambda i: (i, 0)),
                  pl.BlockSpec((k, n), lambda i: (0, 0), pipeline_mode=pl.Buffered(1)),
                  pl.BlockSpec((tm, n), lambda i: (i, 0)),
                  pl.BlockSpec((1, n), lambda i: (0, 0))],
        out_specs=pl.BlockSpec((tm, n), lambda i: (i, 0)),
        out_shape=jax.ShapeDtypeStruct((m, n), jnp.float32),
        compiler_params=_params("arbitrary"),
        name="out_proj",
    )(y, w, x2d, g.reshape(1, n))


def _attention_constants(s):
    n_slc = s // SEL_BLOCK
    n_cmp = (s - CMP_BLOCK) // CMP_STRIDE + 1
    c0 = np.arange(s // CMP_STRIDE)[None, :] * CMP_STRIDE
    s0 = np.arange(n_slc)[:, None] * SEL_BLOCK
    overlap = np.maximum(0, np.minimum(c0 + CMP_BLOCK, s0 + SEL_BLOCK) - np.maximum(c0, s0))
    aggt = (overlap / CMP_BLOCK) * (np.arange(s // CMP_STRIDE)[None, :] < n_cmp)
    blockid = (np.arange(s)[:, None] // SEL_BLOCK == np.arange(LANES)[None, :]).astype(np.float32)
    n_pat = WINDOW // Q_TILE
    pats = []
    for p in range(n_pat + 1):
        t = p * Q_TILE + np.arange(Q_TILE)[:, None]
        kpos = max(p * Q_TILE - WINDOW, 0) + np.arange(WINDOW + Q_TILE)[None, :]
        pats.append(np.where((kpos <= t) & (kpos > t - WINDOW), 0.0, MASKED))
    n_bh = 3 * HEADS_PER_GROUP
    gsel = np.zeros((N_KV_GROUPS, 2 * LANES, n_bh * LANES), np.float32)
    for g in range(N_KV_GROUPS):
        for br in range(3):
            for h in range(HEADS_PER_GROUP):
                src = br * N_HEADS + g * HEADS_PER_GROUP + h
                c = br * HEADS_PER_GROUP + h
                gsel[g, [src, LANES + src], c * LANES:(c + 1) * LANES] = 1.0
    return (jnp.asarray(aggt, jnp.bfloat16), jnp.asarray(blockid, jnp.bfloat16),
            jnp.asarray(np.stack(pats), jnp.float32), jnp.asarray(gsel, jnp.bfloat16))


def _layer(x, norm_g, w_in, b_in, pe_k, w1_k, w2_k, pe_v, w1_v, w2_v,
           conv_w, conv_b, p_attn, p_conv, w_o, out_g):
    b, s, d = x.shape
    m = b * s
    bf = jnp.bfloat16
    x2d = x.reshape(m, d)
    tn = PROJ_TN

    c_q = ATTN_WIDTH
    c_cmp = c_q + 2 * KV_WIDTH
    c_kv = c_cmp + 4 * KV_WIDTH
    n_gate = 3 * N_HEADS
    c_z = c_kv + n_gate
    c_conv = c_z + ATTN_WIDTH
    c_merge = c_conv + 4 * CONV_WIDTH
    assert c_cmp - c_q == tn and c_q % tn == 0 and (c_kv - c_cmp) % tn == 0
    wt = jnp.swapaxes(w_in, 1, 2).reshape(w_in.shape[2], w_in.shape[1])

    hn, kvc, gate = _norm_proj(x2d, norm_g, wt, jnp.concatenate([b_in[c_q:c_cmp], b_in[c_kv:c_kv + LANES]]),
                               c_q, c_cmp - c_q, c_kv)
    kvc = kvc.reshape(b, s, -1)
    q_tiles = c_q // tn
    qkv = _proj(hn, wt, jnp.concatenate([b_in[:c_q], b_in[c_cmp:c_kv]]),
                lambda j: jnp.where(j < q_tiles, j * tn, c_cmp + (j - q_tiles) * tn),
                q_tiles + (c_kv - c_cmp) // tn, tn, bf,
                scaled_tiles=q_tiles, scale=SCALE * math.log2(math.e), name="proj_qkv").reshape(b, s, -1)
    z_tiles = ATTN_WIDTH // tn
    zg = _proj(hn, wt, jnp.concatenate([b_in[c_z:c_conv], b_in[c_merge:]]),
               lambda j: jnp.where(j < z_tiles, c_z + j * tn, c_merge + (j - z_tiles) * tn),
               z_tiles + 2 * D_MODEL // tn, tn, jnp.float32, tm=PROJ_TM_F32, name="proj_zg").reshape(b, s, -1)
    cw = CONV_TN
    b_conv = b_in[c_conv:c_merge].reshape(4, CONV_WIDTH // cw, cw).transpose(1, 0, 2).reshape(1, -1)
    conv_starts = [lambda j, p=p: c_conv + p * CONV_WIDTH + j * cw for p in range(4)]
    c, (wa, wc, wo) = _proj_conv(hn, wt, b_conv, conv_w, conv_b, conv_starts, s, (p_attn, p_conv, w_o), cw=cw)

    kvcb = _compress(kvc, pe_k, w1_k, w2_k, pe_v, w1_v, w2_v)
    kcb, vcb = kvcb[0], kvcb[1]

    aggt, blockid, wbias, gsel = _attention_constants(s)
    a = _attention(qkv, kcb, vcb, gate.reshape(b, s, LANES), zg, aggt, blockid, wbias, gsel)

    y = _merge(a.reshape(m, -1), c, wa, wc, zg.reshape(m, -1), ATTN_WIDTH)
    return _out_proj(y, wo, x2d, out_g).reshape(b, s, d)


def kernel(x, norm_g, w_in, b_in, cmp_pe_k, cmp_w1_k, cmp_w2_k, cmp_pe_v, cmp_w1_v, cmp_w2_v,
           conv_w, conv_b, p_attn, p_conv, w_o, final_g):
    assert norm_g.shape[0] == 1, "single-layer block"
    return _layer(x, norm_g[0], w_in, b_in[0], cmp_pe_k[0], cmp_w1_k[0], cmp_w2_k[0],
                  cmp_pe_v[0], cmp_w1_v[0], cmp_w2_v[0], conv_w[0], conv_b[0],
                  p_attn, p_conv, w_o, final_g)
```

```python
import functools
import math

import numpy as np
import jax
import jax.numpy as jnp
from jax import lax
from jax.experimental import pallas as pl
from jax.experimental.pallas import tpu as pltpu

D_MODEL = 2048
N_HEADS = 16
HEAD_DIM = 128
N_KV_GROUPS = 4
HEADS_PER_GROUP = N_HEADS // N_KV_GROUPS
ATTN_WIDTH = N_HEADS * HEAD_DIM
KV_WIDTH = N_KV_GROUPS * HEAD_DIM
GROUP_WIDTH = HEADS_PER_GROUP * HEAD_DIM
CMP_BLOCK = 32
CMP_STRIDE = 16
CMP_HIDDEN = 256
SEL_BLOCK = 64
SEL_TOP_N = 8
WINDOW = 512
SCALE = HEAD_DIM ** -0.5
CONV_WIDTH = D_MODEL
CONV_K = 3
RMS_EPS = 1e-6
FORCED_SCORE = 1e4
MASKED = -(2.0 ** 100)

LANES = 128
SUBLANES = 8
Q_TILE = 256
SEL_CHUNK = 256
GROUPS_PER_STEP = 2
W_ROWS = 512
W_ALIGN = 16
VMEM_LIMIT = 56 * 1024 * 1024
PROJ_TM, PROJ_TN = 2048, 1024
PROJ_TM_F32 = 1024
CONV_TN = 256
NORM_TM = 512
MERGE_TM, MERGE_TN = 512, 1024
OUT_TM = 512

_NT = (((1,), (1,)), ((), ()))


def _params(*sem):
    return pltpu.CompilerParams(dimension_semantics=sem, vmem_limit_bytes=VMEM_LIMIT)


def _sigmoid(x):
    return 1.0 / (1.0 + jnp.exp(-x))


def _silu(x):
    return x * _sigmoid(x)


def _cast_weight(w_scr, w_ref, row0=0):
    rows = w_ref.shape[0]
    for r in range(0, rows, W_ROWS):
        n = min(W_ROWS, rows - r)
        w_scr[row0 + r:row0 + r + n, :] = w_ref[r:r + n, :].astype(w_scr.dtype)


def _weight_specs(k, piece_rows, start_fns):
    return [pl.BlockSpec((pl.Element(piece_rows), pl.Element(k)),
                         lambda j, i, f=f: (pl.multiple_of(f(j), W_ALIGN), 0))
            for f in start_fns]


def _proj_kernel(a_ref, b_ref, w_ref, o_ref, w_scr, *, scaled_tiles, scale):
    @pl.when(pl.program_id(1) == 0)
    def _():
        _cast_weight(w_scr, w_ref)

    r = lax.dot_general(a_ref[...], w_scr[...], _NT, preferred_element_type=jnp.float32) + b_ref[...]
    if scaled_tiles:
        r = r * jnp.where(pl.program_id(0) < scaled_tiles, scale, 1.0)
    o_ref[...] = r.astype(o_ref.dtype)


def _proj(a, wt, bias, start_fn, n_tiles, tn, out_dtype, tm=PROJ_TM, scaled_tiles=0, scale=1.0,
          name="proj"):
    m, k = a.shape
    return pl.pallas_call(
        functools.partial(_proj_kernel, scaled_tiles=scaled_tiles, scale=scale),
        grid=(n_tiles, m // tm),
        in_specs=[pl.BlockSpec((tm, k), lambda j, i: (i, 0)),
                  pl.BlockSpec((1, tn), lambda j, i: (0, j))] + _weight_specs(k, tn, [start_fn]),
        out_specs=pl.BlockSpec((tm, tn), lambda j, i: (i, j)),
        out_shape=jax.ShapeDtypeStruct((m, n_tiles * tn), out_dtype),
        scratch_shapes=[pltpu.VMEM((tn, k), jnp.bfloat16)],
        compiler_params=_params("arbitrary", "arbitrary"),
        name=name,
    )(a, bias.reshape(1, n_tiles * tn), wt)


def _norm_proj_kernel(x_ref, g_ref, b_ref, wkv_ref, wg_ref, hn_ref, kv_ref, gate_ref, w_scr):
    n_kv = kv_ref.shape[1]

    @pl.when(pl.program_id(0) == 0)
    def _():
        _cast_weight(w_scr, wkv_ref)
        _cast_weight(w_scr, wg_ref, n_kv)

    x = x_ref[...]
    ms = jnp.mean(x * x, axis=-1, keepdims=True)
    hn = (x * lax.rsqrt(ms + RMS_EPS) * g_ref[...]).astype(hn_ref.dtype)
    hn_ref[...] = hn
    r = lax.dot_general(hn, w_scr[...], _NT, preferred_element_type=jnp.float32) + b_ref[...]
    kv_ref[...] = r[:, :n_kv]
    gate_ref[...] = _sigmoid(r[:, n_kv:])


def _norm_proj(x2d, g, wt, bias, kv_start, n_kv, gate_start, tm=NORM_TM):
    m, k = x2d.shape
    once = dict(pipeline_mode=pl.Buffered(1))
    return pl.pallas_call(
        _norm_proj_kernel,
        grid=(m // tm,),
        in_specs=[pl.BlockSpec((tm, k), lambda i: (i, 0)),
                  pl.BlockSpec((1, k), lambda i: (0, 0)),
                  pl.BlockSpec((1, n_kv + LANES), lambda i: (0, 0)),
                  pl.BlockSpec((pl.Element(n_kv), pl.Element(k)), lambda i: (kv_start, 0), **once),
                  pl.BlockSpec((pl.Element(LANES), pl.Element(k)), lambda i: (gate_start, 0), **once)],
        out_specs=[pl.BlockSpec((tm, k), lambda i: (i, 0)),
                   pl.BlockSpec((tm, n_kv), lambda i: (i, 0)),
                   pl.BlockSpec((tm, LANES), lambda i: (i, 0))],
        out_shape=[jax.ShapeDtypeStruct((m, k), jnp.bfloat16),
                   jax.ShapeDtypeStruct((m, n_kv), jnp.float32),
                   jax.ShapeDtypeStruct((m, LANES), jnp.float32)],
        scratch_shapes=[pltpu.VMEM((n_kv + LANES, k), jnp.bfloat16)],
        compiler_params=_params("arbitrary"),
        name="norm_proj",
    )(x2d, g.reshape(1, k), bias.reshape(1, n_kv + LANES), wt, wt)


def _proj_conv_kernel(a_ref, b_ref, cw_ref, cb_ref, wu_ref, wcc_ref, wcb_ref, wz_ref, *rest, seq):
    n_side = (len(rest) - 3) // 2
    side_in, o_ref, side_out = rest[:n_side], rest[n_side], rest[n_side + 1:2 * n_side + 1]
    w_scr, halo_scr = rest[2 * n_side + 1:]
    tm, cw = o_ref.shape
    i = pl.program_id(1)

    for src, dst in zip(side_in, side_out):
        dst[...] = src[...].astype(dst.dtype)

    @pl.when(i == 0)
    def _():
        for p, w_ref in enumerate((wu_ref, wcc_ref, wcb_ref, wz_ref)):
            _cast_weight(w_scr, w_ref, p * cw)

    @pl.when((i * tm) % seq == 0)
    def _():
        halo_scr[...] = jnp.zeros(halo_scr.shape, jnp.float32)

    r = lax.dot_general(a_ref[...], w_scr[...], _NT, preferred_element_type=jnp.float32) + b_ref[...]
    u, cc, cb, z = (r[:, p * cw:(p + 1) * cw] for p in range(4))
    v = cc * u
    prev = halo_scr[...]
    halo_scr[...] = v[tm - SUBLANES:, :]
    row = lax.broadcasted_iota(jnp.int32, (tm, 1), 0)
    y = cw_ref[CONV_K - 1:CONV_K, :] * v
    for d in range(1, CONV_K):
        shifted = pltpu.roll(v, d, axis=0)
        for t in range(d):
            shifted = jnp.where(row == t, prev[SUBLANES - d + t:SUBLANES - d + t + 1, :], shifted)
        y = y + cw_ref[CONV_K - 1 - d:CONV_K - d, :] * shifted
    o_ref[...] = (cb * (y + cb_ref[...]) * _silu(z)).astype(o_ref.dtype)


def _proj_conv(a, wt, bias_tiles, conv_w, conv_b, start_fns, seq, side_weights, tm=PROJ_TM, cw=CONV_TN):
    m, k = a.shape
    n_tiles = CONV_WIDTH // cw
    n_rt = m // tm
    assert seq % tm == 0 and CONV_K - 1 <= SUBLANES and len(start_fns) == 4
    n_steps = n_tiles * n_rt
    slab = lambda w: w.shape[1] // n_steps
    assert all(w.shape[1] % (n_steps * 16) == 0 for w in side_weights)
    res = pl.pallas_call(
        functools.partial(_proj_conv_kernel, seq=seq),
        grid=(n_tiles, n_rt),
        in_specs=[pl.BlockSpec((tm, k), lambda j, i: (i, 0)),
                  pl.BlockSpec((1, 4 * cw), lambda j, i: (0, j)),
                  pl.BlockSpec((CONV_K, cw), lambda j, i: (0, j)),
                  pl.BlockSpec((1, cw), lambda j, i: (0, j))] + _weight_specs(k, cw, start_fns)
                 + [pl.BlockSpec((None, slab(w), w.shape[2]), lambda j, i: (0, j * n_rt + i, 0))
                    for w in side_weights],
        out_specs=[pl.BlockSpec((tm, cw), lambda j, i: (i, j))]
                  + [pl.BlockSpec((slab(w), w.shape[2]), lambda j, i: (j * n_rt + i, 0)) for w in side_weights],
        out_shape=[jax.ShapeDtypeStruct((m, CONV_WIDTH), jnp.bfloat16)]
                  + [jax.ShapeDtypeStruct(w.shape[1:], jnp.bfloat16) for w in side_weights],
        scratch_shapes=[pltpu.VMEM((4 * cw, k), jnp.bfloat16),
                        pltpu.VMEM((SUBLANES, cw), jnp.float32)],
        compiler_params=_params("arbitrary", "arbitrary"),
        name="proj_conv",
    )(a, bias_tiles, conv_w, conv_b.reshape(1, CONV_WIDTH), wt, wt, wt, wt, *side_weights)
    return res[0], res[1:]


def _compress_kernel(*refs):
    n_groups = N_KV_GROUPS
    x_refs = refs[:n_groups]
    pek_ref, w1k_ref, w2k_ref, pev_ref, w1v_ref, w2v_ref, o_ref, pe_scr, w1_scr, w2_scr = refs[n_groups:]
    for which, (pe_ref, w1_ref, w2_ref) in enumerate(((pek_ref, w1k_ref, w2k_ref),
                                                      (pev_ref, w1v_ref, w2v_ref))):
        @pl.when((pl.program_id(1) == 0) & (pl.program_id(0) == which))
        def _(pe_ref=pe_ref, w1_ref=w1_ref, w2_ref=w2_ref):
            pe_scr[...] = pe_ref[...]
            for l in range(CMP_BLOCK):
                w1_scr[l] = w1_ref[l].astype(w1_scr.dtype)
            w2_scr[...] = w2_ref[...].astype(w2_scr.dtype)

    half = CMP_BLOCK // 2
    n_chunks = x_refs[0].shape[1] // CMP_STRIDE
    rows = n_groups * n_chunks
    top = jnp.zeros((rows, CMP_HIDDEN), jnp.float32)
    bot = jnp.zeros((rows, CMP_HIDDEN), jnp.float32)
    for l in range(half):
        xl = jnp.concatenate([x_ref[0, pl.ds(l, n_chunks, stride=CMP_STRIDE), :] for x_ref in x_refs], axis=0)
        a = (xl + pe_scr[l:l + 1, :]).astype(jnp.bfloat16)
        c = (xl + pe_scr[half + l:half + l + 1, :]).astype(jnp.bfloat16)
        top = top + jnp.dot(a, w1_scr[l], preferred_element_type=jnp.float32)
        bot = bot + jnp.dot(c, w1_scr[half + l], preferred_element_type=jnp.float32)
    for g in range(n_groups):
        r0 = g * n_chunks
        h = top[r0:r0 + n_chunks] + pltpu.roll(bot[r0:r0 + n_chunks], n_chunks - 1, axis=0)
        h = _silu(h).astype(jnp.bfloat16)
        o_ref[0, 0, g] = jnp.dot(h, w2_scr[...], preferred_element_type=jnp.float32).astype(o_ref.dtype)


def _compress(kv, pe_k, w1_k, w2_k, pe_v, w1_v, w2_v):
    b, s, _ = kv.shape
    whole = lambda shape: pl.BlockSpec(shape, lambda t, bi: (0,) * len(shape))
    w1_shape = (CMP_BLOCK, HEAD_DIM, CMP_HIDDEN)
    n_c = s // CMP_STRIDE
    return pl.pallas_call(
        _compress_kernel,
        grid=(2, b),
        in_specs=[pl.BlockSpec((1, s, HEAD_DIM), lambda t, bi, g=g: (bi, 0, t * N_KV_GROUPS + g))
                  for g in range(N_KV_GROUPS)] + [
                  whole((CMP_BLOCK, HEAD_DIM)), whole(w1_shape), whole((CMP_HIDDEN, HEAD_DIM)),
                  whole((CMP_BLOCK, HEAD_DIM)), whole(w1_shape), whole((CMP_HIDDEN, HEAD_DIM))],
        out_specs=pl.BlockSpec((1, 1, N_KV_GROUPS, n_c, HEAD_DIM), lambda t, bi: (t, bi, 0, 0, 0)),
        out_shape=jax.ShapeDtypeStruct((2, b, N_KV_GROUPS, n_c, HEAD_DIM), jnp.bfloat16),
        scratch_shapes=[pltpu.VMEM((CMP_BLOCK, HEAD_DIM), jnp.float32),
                        pltpu.VMEM(w1_shape, jnp.bfloat16),
                        pltpu.VMEM((CMP_HIDDEN, HEAD_DIM), jnp.bfloat16)],
        compiler_params=_params("arbitrary", "arbitrary"),
        name="compress",
    )(*([kv] * N_KV_GROUPS), pe_k, w1_k.reshape(w1_shape), w2_k, pe_v, w1_v.reshape(w1_shape), w2_v)


def _attn_kernel(q_ref, ks_ref, vs_ref, kw_ref, vw_ref, kc_ref, vc_ref, gate_ref, z_ref,
                 aggt_ref, blockid_ref, wbias_ref, gsel_ref, o_ref,
                 kaug_scr, vsaug_scr, vwaug_scr, vcaug_scr, s_scr, m_scr, acc_scr):
    tq = Q_TILE
    hpg = HEADS_PER_GROUP
    gw = GROUP_WIDTH
    seq = kw_ref.shape[1]
    gps = kc_ref.shape[1]
    i = pl.program_id(2)
    t0 = i * tq

    @pl.when(i == 0)
    def _():
        ones = jnp.ones((seq, LANES), jnp.bfloat16)
        for gi in range(gps):
            cols = slice(gi * HEAD_DIM, (gi + 1) * HEAD_DIM)
            kaug_scr[gi, :, :HEAD_DIM] = ks_ref[0, :, cols]
            kaug_scr[gi, :, HEAD_DIM:] = blockid_ref[...]
            vsaug_scr[gi, :, :HEAD_DIM] = vs_ref[0, :, cols]
            vsaug_scr[gi, :, HEAD_DIM:] = ones
            vwaug_scr[gi, :, :HEAD_DIM] = vw_ref[0, :, cols]
            vwaug_scr[gi, :, HEAD_DIM:] = ones
            vcaug_scr[gi, :, :HEAD_DIM] = vc_ref[0, gi]
            vcaug_scr[gi, :, HEAD_DIM:] = ones[:vcaug_scr.shape[1]]

    t_q = t0 + lax.broadcasted_iota(jnp.int32, (tq, 1), 0)
    t_s = jnp.concatenate([t_q] * hpg, axis=0)
    lane = lax.broadcasted_iota(jnp.int32, (1, LANES), 1)
    n_cmp = (seq - CMP_BLOCK) // CMP_STRIDE + 1
    n_slc = seq // SEL_BLOCK
    n_win = WINDOW + tq
    start = pl.multiple_of(jnp.maximum(t0 - WINDOW, 0), tq)
    w_bias = jnp.concatenate([wbias_ref[0]] * hpg, axis=0)
    vis_c = (lane * CMP_STRIDE + (CMP_BLOCK - 1) <= t_s) & (lane < n_cmp)
    aggt = aggt_ref[...]
    blk = lax.broadcasted_iota(jnp.int32, (n_slc, 1), 0)
    t_l = t0 + lax.broadcasted_iota(jnp.int32, (1, tq), 1)
    cur = t_l >> int(math.log2(SEL_BLOCK))
    causal = blk * SEL_BLOCK <= t_l
    forced = ((blk == 0) | (blk == cur) | (blk == cur - 1)) & causal
    gate = gate_ref[0]
    gate_hi = gate.astype(jnp.bfloat16)
    gate_lo = (gate - gate_hi.astype(jnp.float32)).astype(jnp.bfloat16)
    gate_hl = jnp.concatenate([gate_hi, gate_lo], axis=1)

    def first_region(gi):
        q = q_ref[0, :, gi * gw:(gi + 1) * gw]
        qs = jnp.concatenate([q[:, h * HEAD_DIM:(h + 1) * HEAD_DIM] for h in range(hpg)], axis=0)

        kw = kw_ref[0, pl.ds(start, n_win), gi * HEAD_DIM:(gi + 1) * HEAD_DIM]
        sw = lax.dot_general(qs, kw, _NT, preferred_element_type=jnp.float32) + w_bias
        p_w = jnp.exp2(sw - jnp.max(sw, axis=-1, keepdims=True))
        pv_w = jnp.dot(p_w.astype(jnp.bfloat16), vwaug_scr[gi, pl.ds(start, n_win), :],
                       preferred_element_type=jnp.float32)
        o_win = pv_w[:, :HEAD_DIM] * (1.0 / pv_w[:, HEAD_DIM:])

        sc = lax.dot_general(qs, kc_ref[0, gi], _NT, preferred_element_type=jnp.float32)
        sc = jnp.where(vis_c, sc, MASKED)
        p_c = jnp.where(vis_c, jnp.exp2(sc - jnp.max(sc, axis=-1, keepdims=True)), 0.0)
        pv_c = jnp.dot(p_c.astype(jnp.bfloat16), vcaug_scr[gi],
                       preferred_element_type=jnp.float32)
        l_c = pv_c[:, HEAD_DIM:]
        inv_c = jnp.where(l_c > 0.0, 1.0 / l_c, 0.0)
        o_cmp = pv_c[:, :HEAD_DIM] * inv_c
        p_c = p_c * inv_c

        p_sum = p_c[0:tq]
        for h in range(1, hpg):
            p_sum = p_sum + p_c[h * tq:(h + 1) * tq]
        hi = p_sum.astype(jnp.bfloat16)
        r1 = p_sum - hi.astype(jnp.float32)
        mid = r1.astype(jnp.bfloat16)
        lo = (r1 - mid.astype(jnp.float32)).astype(jnp.bfloat16)
        imp = (lax.dot_general(aggt, hi, _NT, preferred_element_type=jnp.float32)
               + lax.dot_general(aggt, mid, _NT, preferred_element_type=jnp.float32)
               + lax.dot_general(aggt, lo, _NT, preferred_element_type=jnp.float32))
        imp = jnp.where(forced, FORCED_SCORE, imp)
        imp = jnp.where(causal, imp, -1.0)

        bias_slabs = []
        for v in range(n_slc // SUBLANES):
            lo_row = v * SUBLANES
            slab = imp[lo_row:lo_row + SUBLANES]
            sub = lo_row + lax.broadcasted_iota(jnp.int32, (SUBLANES, 1), 0)
            rank = jnp.zeros((SUBLANES, tq), jnp.float32)
            for jp in range(n_slc):
                other = jnp.broadcast_to(imp[jp:jp + 1], (SUBLANES, tq))
                if jp < lo_row:
                    ahead = jnp.where(other >= slab, 1.0, 0.0)
                elif jp >= lo_row + SUBLANES:
                    ahead = jnp.where(other > slab, 1.0, 0.0)
                else:
                    tie = jnp.where(sub > jp, 1.0, 0.0)
                    ahead = jnp.where(other > slab, 1.0, jnp.where(other == slab, tie, 0.0))
                rank = rank + ahead
            bias_slabs.append(jnp.where(rank < float(SEL_TOP_N), 0.0, MASKED))
        bias_t = jnp.concatenate(bias_slabs + [jnp.zeros((LANES - n_slc, tq), jnp.float32)], axis=0)
        sel_bias = bias_t.T.astype(jnp.bfloat16)
        qa = jnp.concatenate([qs, jnp.concatenate([sel_bias] * hpg, axis=0)], axis=1)
        return o_win, o_cmp, qa

    first = [first_region(gi) for gi in range(gps)]

    n_full = t0 // SEL_CHUNK
    groups = [g for g in (4, 2, 1) if g <= max((seq - tq) // SEL_CHUNK, 1)]

    def scores(gi, c):
        base = pl.multiple_of(c * SEL_CHUNK, SEL_CHUNK)
        return base, lax.dot_general(first[gi][2], kaug_scr[gi, pl.ds(base, SEL_CHUNK), :], _NT,
                                     preferred_element_type=jnp.float32)

    def keep(gi, base, s, m):
        s_scr[gi, :, pl.ds(base, SEL_CHUNK)] = s
        for k in range(SEL_CHUNK // LANES):
            m = jnp.maximum(m, s[:, k * LANES:(k + 1) * LANES])
        return m

    for gi in range(gps):
        base_d, s_d = scores(gi, n_full)
        kpos_d = base_d + lax.broadcasted_iota(jnp.int32, (1, SEL_CHUNK), 1)
        m_scr[gi] = keep(gi, base_d, jnp.where(kpos_d <= t_s, s_d, MASKED),
                         jnp.full(m_scr.shape[1:], MASKED, jnp.float32))
    done = 0
    for g in groups:
        @pl.when((n_full & g) != 0)
        def _(done=done, g=g):
            for gi in range(gps):
                m = m_scr[gi]
                for u in range(g):
                    m = keep(gi, *scores(gi, done + u), m)
                m_scr[gi] = m
        done = done + (n_full & g)
    for gi in range(gps):
        m_scr[gi] = jnp.broadcast_to(jnp.max(m_scr[gi], axis=-1, keepdims=True), m_scr.shape[1:])

    def values(gi, c, m):
        base = pl.multiple_of(c * SEL_CHUNK, SEL_CHUNK)
        ps = [jnp.exp2(s_scr[gi, :, pl.ds(base + k * LANES, LANES)] - m).astype(jnp.bfloat16)
              for k in range(SEL_CHUNK // LANES)]
        return jnp.dot(jnp.concatenate(ps, axis=1), vsaug_scr[gi, pl.ds(base, SEL_CHUNK), :],
                       preferred_element_type=jnp.float32)

    for gi in range(gps):
        acc_scr[gi] = values(gi, n_full, m_scr[gi])
    done = 0
    for g in groups:
        @pl.when((n_full & g) != 0)
        def _(done=done, g=g):
            for gi in range(gps):
                m, acc = m_scr[gi], acc_scr[gi]
                for u in range(g):
                    acc = acc + values(gi, done + u, m)
                acc_scr[gi] = acc
        done = done + (n_full & g)

    for gi in range(gps):
        o_win, o_cmp, _ = first[gi]
        o_sel = acc_scr[gi, :, :HEAD_DIM] * (1.0 / acc_scr[gi, :, HEAD_DIM:])
        g_exp = jnp.dot(gate_hl, gsel_ref[gi], preferred_element_type=jnp.float32)
        g_col = lambda c: g_exp[:, c * LANES:(c + 1) * LANES]
        outs = []
        for h in range(hpg):
            r0 = h * tq
            outs.append(g_col(h) * o_cmp[r0:r0 + tq]
                        + g_col(hpg + h) * o_sel[r0:r0 + tq]
                        + g_col(2 * hpg + h) * o_win[r0:r0 + tq])
        o = jnp.concatenate(outs, axis=1)
        cols = slice(gi * gw, (gi + 1) * gw)
        o_ref[0, :, cols] = (o * _silu(z_ref[0, :, cols])).astype(o_ref.dtype)


def _attention(qkv, kcb, vcb, gate, zu, aggt, blockid, wbias, gsel):
    b, s, _ = qkv.shape
    gps = GROUPS_PER_STEP
    assert SEL_CHUNK % Q_TILE == 0 and WINDOW % Q_TILE == 0 and s % SEL_CHUNK == 0
    assert N_KV_GROUPS % gps == 0
    kvw = gps * HEAD_DIM
    qb = ATTN_WIDTH // kvw
    kvb = KV_WIDTH // kvw
    n_c = s // CMP_STRIDE
    rows = HEADS_PER_GROUP * Q_TILE
    n_pat = WINDOW // Q_TILE
    grid = (b, N_KV_GROUPS // gps, s // Q_TILE)
    kv_spec = lambda off: pl.BlockSpec((1, s, kvw), lambda bi, g, i: (bi, 0, off + g))
    wide = pl.BlockSpec((1, Q_TILE, gps * GROUP_WIDTH), lambda bi, g, i: (bi, i, g))
    return pl.pallas_call(
        _attn_kernel,
        grid=grid,
        in_specs=[wide,
                  kv_spec(qb), kv_spec(qb + kvb), kv_spec(qb + 2 * kvb), kv_spec(qb + 3 * kvb),
                  pl.BlockSpec((1, gps, n_c, HEAD_DIM), lambda bi, g, i: (bi, g, 0, 0)),
                  pl.BlockSpec((1, gps, n_c, HEAD_DIM), lambda bi, g, i: (bi, g, 0, 0)),
                  pl.BlockSpec((1, Q_TILE, LANES), lambda bi, g, i: (bi, i, 0)),
                  wide,
                  pl.BlockSpec(aggt.shape, lambda bi, g, i: (0, 0)),
                  pl.BlockSpec(blockid.shape, lambda bi, g, i: (0, 0)),
                  pl.BlockSpec((1, Q_TILE, WINDOW + Q_TILE),
                               lambda bi, g, i: (jnp.minimum(i, n_pat), 0, 0)),
                  pl.BlockSpec((gps,) + gsel.shape[1:], lambda bi, g, i: (g, 0, 0))],
        out_specs=wide,
        out_shape=jax.ShapeDtypeStruct((b, s, ATTN_WIDTH), jnp.bfloat16),
        scratch_shapes=[pltpu.VMEM((gps, s, HEAD_DIM + LANES), jnp.bfloat16),
                        pltpu.VMEM((gps, s, HEAD_DIM + LANES), jnp.bfloat16),
                        pltpu.VMEM((gps, s, HEAD_DIM + LANES), jnp.bfloat16),
                        pltpu.VMEM((gps, n_c, HEAD_DIM + LANES), jnp.bfloat16),
                        pltpu.VMEM((gps, rows, s), jnp.float32),
                        pltpu.VMEM((gps, rows, LANES), jnp.float32),
                        pltpu.VMEM((gps, rows, HEAD_DIM + LANES), jnp.float32)],
        compiler_params=_params("arbitrary", "arbitrary", "arbitrary"),
        name="nsa_attention",
    )(qkv, qkv, qkv, qkv, qkv, kcb, vcb, gate, zu, aggt, blockid, wbias, gsel)


def _merge_kernel(a_ref, c_ref, wa_ref, wc_ref, g0_ref, g1_ref, o_ref):
    ya = jnp.dot(a_ref[...], wa_ref[...], preferred_element_type=jnp.float32)
    yc = jnp.dot(c_ref[...], wc_ref[...], preferred_element_type=jnp.float32)
    o_ref[...] = (_sigmoid(g0_ref[...]) * ya + _sigmoid(g1_ref[...]) * yc).astype(o_ref.dtype)


def _merge(a, c, wa, wc, zg, gcol0, tm=MERGE_TM, tn=MERGE_TN):
    m, k = a.shape
    n = wa.shape[-1]
    g0 = gcol0 // tn
    g1 = (gcol0 + n) // tn
    return pl.pallas_call(
        _merge_kernel,
        grid=(n // tn, m // tm),
        in_specs=[pl.BlockSpec((tm, k), lambda j, i: (i, 0)),
                  pl.BlockSpec((tm, k), lambda j, i: (i, 0)),
                  pl.BlockSpec((k, tn), lambda j, i: (0, j)),
                  pl.BlockSpec((k, tn), lambda j, i: (0, j)),
                  pl.BlockSpec((tm, tn), lambda j, i: (i, g0 + j)),
                  pl.BlockSpec((tm, tn), lambda j, i: (i, g1 + j))],
        out_specs=pl.BlockSpec((tm, tn), lambda j, i: (i, j)),
        out_shape=jax.ShapeDtypeStruct((m, n), jnp.bfloat16),
        compiler_params=_params("arbitrary", "arbitrary"),
        name="merge",
    )(a, c, wa, wc, zg, zg)


def _out_kernel(y_ref, w_ref, x_ref, g_ref, o_ref):
    r = x_ref[...] + jnp.dot(y_ref[...], w_ref[...], preferred_element_type=jnp.float32)
    ms = jnp.mean(r * r, axis=-1, keepdims=True)
    o_ref[...] = r * lax.rsqrt(ms + RMS_EPS) * g_ref[...]


def _out_proj(y, w, x2d, g, tm=OUT_TM):
    m, k = y.shape
    n = w.shape[-1]
    return pl.pallas_call(
        _out_kernel,
        grid=(m // tm,),
        in_specs=[pl.BlockSpec((tm, k), lambda i: (i, 0)),
                  pl.BlockSpec((k, n), lambda i: (0, 0), pipeline_mode=pl.Buffered(1)),
                  pl.BlockSpec((tm, n), lambda i: (i, 0)),
                  pl.BlockSpec((1, n), lambda i: (0, 0))],
        out_specs=pl.BlockSpec((tm, n), lambda i: (i, 0)),
        out_shape=jax.ShapeDtypeStruct((m, n), jnp.float32),
        compiler_params=_params("arbitrary"),
        name="out_proj",
    )(y, w, x2d, g.reshape(1, n))


def _attention_constants(s):
    n_slc = s // SEL_BLOCK
    n_cmp = (s - CMP_BLOCK) // CMP_STRIDE + 1
    c0 = np.arange(s // CMP_STRIDE)[None, :] * CMP_STRIDE
    s0 = np.arange(n_slc)[:, None] * SEL_BLOCK
    overlap = np.maximum(0, np.minimum(c0 + CMP_BLOCK, s0 + SEL_BLOCK) - np.maximum(c0, s0))
    aggt = (overlap / CMP_BLOCK) * (np.arange(s // CMP_STRIDE)[None, :] < n_cmp)
    blockid = (np.arange(s)[:, None] // SEL_BLOCK == np.arange(LANES)[None, :]).astype(np.float32)
    n_pat = WINDOW // Q_TILE
    pats = []
    for p in range(n_pat + 1):
        t = p * Q_TILE + np.arange(Q_TILE)[:, None]
        kpos = max(p * Q_TILE - WINDOW, 0) + np.arange(WINDOW + Q_TILE)[None, :]
        pats.append(np.where((kpos <= t) & (kpos > t - WINDOW), 0.0, MASKED))
    n_bh = 3 * HEADS_PER_GROUP
    gsel = np.zeros((N_KV_GROUPS, 2 * LANES, n_bh * LANES), np.float32)
    for g in range(N_KV_GROUPS):
        for br in range(3):
            for h in range(HEADS_PER_GROUP):
                src = br * N_HEADS + g * HEADS_PER_GROUP + h
                c = br * HEADS_PER_GROUP + h
                gsel[g, [src, LANES + src], c * LANES:(c + 1) * LANES] = 1.0
    return (jnp.asarray(aggt, jnp.bfloat16), jnp.asarray(blockid, jnp.bfloat16),
            jnp.asarray(np.stack(pats), jnp.float32), jnp.asarray(gsel, jnp.bfloat16))


def _layer(x, norm_g, w_in, b_in, pe_k, w1_k, w2_k, pe_v, w1_v, w2_v,
           conv_w, conv_b, p_attn, p_conv, w_o, out_g):
    b, s, d = x.shape
    m = b * s
    bf = jnp.bfloat16
    x2d = x.reshape(m, d)
    tn = PROJ_TN

    c_q = ATTN_WIDTH
    c_cmp = c_q + 2 * KV_WIDTH
    c_kv = c_cmp + 4 * KV_WIDTH
    n_gate = 3 * N_HEADS
    c_z = c_kv + n_gate
    c_conv = c_z + ATTN_WIDTH
    c_merge = c_conv + 4 * CONV_WIDTH
    assert c_cmp - c_q == tn and c_q % tn == 0 and (c_kv - c_cmp) % tn == 0
    assert all(c % W_ALIGN == 0 for c in (c_q, c_cmp, c_kv, c_z, c_conv, c_merge))
    wt = jnp.swapaxes(w_in, 1, 2).reshape(w_in.shape[2], w_in.shape[1])

    hn, kvc, gate = _norm_proj(x2d, norm_g, wt, jnp.concatenate([b_in[c_q:c_cmp], b_in[c_kv:c_kv + LANES]]),
                               c_q, c_cmp - c_q, c_kv)
    kvc = kvc.reshape(b, s, -1)
    q_tiles = c_q // tn
    qkv = _proj(hn, wt, jnp.concatenate([b_in[:c_q], b_in[c_cmp:c_kv]]),
                lambda j: jnp.where(j < q_tiles, j * tn, c_cmp + (j - q_tiles) * tn),
                q_tiles + (c_kv - c_cmp) // tn, tn, bf,
                scaled_tiles=q_tiles, scale=SCALE * math.log2(math.e), name="proj_qkv").reshape(b, s, -1)
    z_tiles = ATTN_WIDTH // tn
    zg = _proj(hn, wt, jnp.concatenate([b_in[c_z:c_conv], b_in[c_merge:]]),
               lambda j: jnp.where(j < z_tiles, c_z + j * tn, c_merge + (j - z_tiles) * tn),
               z_tiles + 2 * D_MODEL // tn, tn, jnp.float32, tm=PROJ_TM_F32, name="proj_zg").reshape(b, s, -1)
    cw = CONV_TN
    b_conv = b_in[c_conv:c_merge].reshape(4, CONV_WIDTH // cw, cw).transpose(1, 0, 2).reshape(1, -1)
    conv_starts = [lambda j, p=p: c_conv + p * CONV_WIDTH + j * cw for p in range(4)]
    c, (wa, wc, wo) = _proj_conv(hn, wt, b_conv, conv_w, conv_b, conv_starts, s, (p_attn, p_conv, w_o), cw=cw)

    kvcb = _compress(kvc, pe_k, w1_k, w2_k, pe_v, w1_v, w2_v)
    kcb, vcb = kvcb[0], kvcb[1]

    aggt, blockid, wbias, gsel = _attention_constants(s)
    a = _attention(qkv, kcb, vcb, gate.reshape(b, s, LANES), zg, aggt, blockid, wbias, gsel)

    y = _merge(a.reshape(m, -1), c, wa, wc, zg.reshape(m, -1), ATTN_WIDTH)
    return _out_proj(y, wo, x2d, out_g).reshape(b, s, d)


def kernel(x, norm_g, w_in, b_in, cmp_pe_k, cmp_w1_k, cmp_w2_k, cmp_pe_v, cmp_w1_v, cmp_w2_v,
           conv_w, conv_b, p_attn, p_conv, w_o, final_g):
    assert norm_g.shape[0] == 1, "single-layer block"
    return _layer(x, norm_g[0], w_in, b_in[0], cmp_pe_k[0], cmp_w1_k[0], cmp_w2_k[0],
                  cmp_pe_v[0], cmp_w1_v[0], cmp_w2_v[0], conv_w[0], conv_b[0],
                  p_attn, p_conv, w_o, final_g)
```

```python
import functools
import math

import numpy as np
import jax
import jax.numpy as jnp
from jax import lax
from jax.experimental import pallas as pl
from jax.experimental.pallas import tpu as pltpu

D_MODEL = 2048
N_HEADS = 16
HEAD_DIM = 128
N_KV_GROUPS = 4
HEADS_PER_GROUP = N_HEADS // N_KV_GROUPS
ATTN_WIDTH = N_HEADS * HEAD_DIM
KV_WIDTH = N_KV_GROUPS * HEAD_DIM
GROUP_WIDTH = HEADS_PER_GROUP * HEAD_DIM
CMP_BLOCK = 32
CMP_STRIDE = 16
CMP_HIDDEN = 256
SEL_BLOCK = 64
SEL_TOP_N = 8
WINDOW = 512
SCALE = HEAD_DIM ** -0.5
CONV_WIDTH = D_MODEL
CONV_K = 3
RMS_EPS = 1e-6
FORCED_SCORE = 1e4
MASKED = -(2.0 ** 100)

LANES = 128
SUBLANES = 8
Q_TILE = 256
SEL_CHUNK = 256
GROUPS_PER_STEP = 2
W_ROWS = 512
W_ALIGN = 16
VMEM_LIMIT = 56 * 1024 * 1024
PROJ_TM, PROJ_TN = 2048, 1024
PROJ_TM_F32 = 1024
CONV_TN = 256
NORM_TM = 1024
MERGE_TM, MERGE_TN = 512, 1024
OUT_TM = 512

_NT = (((1,), (1,)), ((), ()))


def _params(*sem):
    return pltpu.CompilerParams(dimension_semantics=sem, vmem_limit_bytes=VMEM_LIMIT)


def _sigmoid(x):
    return 1.0 / (1.0 + jnp.exp(-x))


def _silu(x):
    return x * _sigmoid(x)


def _cast_weight(w_scr, w_ref, row0=0):
    rows = w_ref.shape[0]
    for r in range(0, rows, W_ROWS):
        n = min(W_ROWS, rows - r)
        w_scr[row0 + r:row0 + r + n, :] = w_ref[r:r + n, :].astype(w_scr.dtype)


def _weight_specs(k, piece_rows, start_fns):
    return [pl.BlockSpec((pl.Element(piece_rows), pl.Element(k)),
                         lambda j, i, f=f: (pl.multiple_of(f(j), W_ALIGN), 0))
            for f in start_fns]


def _proj_kernel(a_ref, b_ref, w_ref, o_ref, w_scr, *, scaled_tiles, scale):
    @pl.when(pl.program_id(1) == 0)
    def _():
        _cast_weight(w_scr, w_ref)

    r = lax.dot_general(a_ref[...], w_scr[...], _NT, preferred_element_type=jnp.float32) + b_ref[...]
    if scaled_tiles:
        r = r * jnp.where(pl.program_id(0) < scaled_tiles, scale, 1.0)
    o_ref[...] = r.astype(o_ref.dtype)


def _proj(a, wt, bias, start_fn, n_tiles, tn, out_dtype, tm=PROJ_TM, scaled_tiles=0, scale=1.0,
          name="proj"):
    m, k = a.shape
    return pl.pallas_call(
        functools.partial(_proj_kernel, scaled_tiles=scaled_tiles, scale=scale),
        grid=(n_tiles, m // tm),
        in_specs=[pl.BlockSpec((tm, k), lambda j, i: (i, 0)),
                  pl.BlockSpec((1, tn), lambda j, i: (0, j))] + _weight_specs(k, tn, [start_fn]),
        out_specs=pl.BlockSpec((tm, tn), lambda j, i: (i, j)),
        out_shape=jax.ShapeDtypeStruct((m, n_tiles * tn), out_dtype),
        scratch_shapes=[pltpu.VMEM((tn, k), jnp.bfloat16)],
        compiler_params=_params("arbitrary", "arbitrary"),
        name=name,
    )(a, bias.reshape(1, n_tiles * tn), wt)


def _norm_proj_kernel(x_ref, g_ref, b_ref, wkv_ref, wg_ref, hn_ref, kv_ref, gate_ref, w_scr):
    n_kv = kv_ref.shape[1]

    @pl.when(pl.program_id(0) == 0)
    def _():
        _cast_weight(w_scr, wkv_ref)
        _cast_weight(w_scr, wg_ref, n_kv)

    x = x_ref[...]
    ms = jnp.mean(x * x, axis=-1, keepdims=True)
    hn = (x * lax.rsqrt(ms + RMS_EPS) * g_ref[...]).astype(hn_ref.dtype)
    hn_ref[...] = hn
    r = lax.dot_general(hn, w_scr[...], _NT, preferred_element_type=jnp.float32) + b_ref[...]
    kv_ref[...] = r[:, :n_kv]
    gate_ref[...] = _sigmoid(r[:, n_kv:])


def _norm_proj(x2d, g, wt, bias, kv_start, n_kv, gate_start, tm=NORM_TM):
    m, k = x2d.shape
    once = dict(pipeline_mode=pl.Buffered(1))
    return pl.pallas_call(
        _norm_proj_kernel,
        grid=(m // tm,),
        in_specs=[pl.BlockSpec((tm, k), lambda i: (i, 0)),
                  pl.BlockSpec((1, k), lambda i: (0, 0)),
                  pl.BlockSpec((1, n_kv + LANES), lambda i: (0, 0)),
                  pl.BlockSpec((pl.Element(n_kv), pl.Element(k)), lambda i: (kv_start, 0), **once),
                  pl.BlockSpec((pl.Element(LANES), pl.Element(k)), lambda i: (gate_start, 0), **once)],
        out_specs=[pl.BlockSpec((tm, k), lambda i: (i, 0)),
                   pl.BlockSpec((tm, n_kv), lambda i: (i, 0)),
                   pl.BlockSpec((tm, LANES), lambda i: (i, 0))],
        out_shape=[jax.ShapeDtypeStruct((m, k), jnp.bfloat16),
                   jax.ShapeDtypeStruct((m, n_kv), jnp.float32),
                   jax.ShapeDtypeStruct((m, LANES), jnp.float32)],
        scratch_shapes=[pltpu.VMEM((n_kv + LANES, k), jnp.bfloat16)],
        compiler_params=_params("arbitrary"),
        name="norm_proj",
    )(x2d, g.reshape(1, k), bias.reshape(1, n_kv + LANES), wt, wt)


def _proj_conv_kernel(a_ref, b_ref, cw_ref, cb_ref, wu_ref, wcc_ref, wcb_ref, wz_ref, *rest, seq):
    n_side = (len(rest) - 3) // 2
    side_in, o_ref, side_out = rest[:n_side], rest[n_side], rest[n_side + 1:2 * n_side + 1]
    w_scr, halo_scr = rest[2 * n_side + 1:]
    tm, cw = o_ref.shape
    i = pl.program_id(1)

    for src, dst in zip(side_in, side_out):
        dst[...] = src[...].astype(dst.dtype)

    @pl.when(i == 0)
    def _():
        for p, w_ref in enumerate((wu_ref, wcc_ref, wcb_ref, wz_ref)):
            _cast_weight(w_scr, w_ref, p * cw)

    @pl.when((i * tm) % seq == 0)
    def _():
        halo_scr[...] = jnp.zeros(halo_scr.shape, jnp.float32)

    r = lax.dot_general(a_ref[...], w_scr[...], _NT, preferred_element_type=jnp.float32) + b_ref[...]
    u, cc, cb, z = (r[:, p * cw:(p + 1) * cw] for p in range(4))
    v = cc * u
    prev = halo_scr[...]
    halo_scr[...] = v[tm - SUBLANES:, :]
    row = lax.broadcasted_iota(jnp.int32, (tm, 1), 0)
    y = cw_ref[CONV_K - 1:CONV_K, :] * v
    for d in range(1, CONV_K):
        shifted = pltpu.roll(v, d, axis=0)
        for t in range(d):
            shifted = jnp.where(row == t, prev[SUBLANES - d + t:SUBLANES - d + t + 1, :], shifted)
        y = y + cw_ref[CONV_K - 1 - d:CONV_K - d, :] * shifted
    o_ref[...] = (cb * (y + cb_ref[...]) * _silu(z)).astype(o_ref.dtype)


def _proj_conv(a, wt, bias_tiles, conv_w, conv_b, start_fns, seq, side_weights, tm=PROJ_TM, cw=CONV_TN):
    m, k = a.shape
    n_tiles = CONV_WIDTH // cw
    n_rt = m // tm
    assert seq % tm == 0 and CONV_K - 1 <= SUBLANES and len(start_fns) == 4
    n_steps = n_tiles * n_rt
    slab = lambda w: w.shape[1] // n_steps
    assert all(w.shape[1] % (n_steps * 16) == 0 for w in side_weights)
    res = pl.pallas_call(
        functools.partial(_proj_conv_kernel, seq=seq),
        grid=(n_tiles, n_rt),
        in_specs=[pl.BlockSpec((tm, k), lambda j, i: (i, 0)),
                  pl.BlockSpec((1, 4 * cw), lambda j, i: (0, j)),
                  pl.BlockSpec((CONV_K, cw), lambda j, i: (0, j)),
                  pl.BlockSpec((1, cw), lambda j, i: (0, j))] + _weight_specs(k, cw, start_fns)
                 + [pl.BlockSpec((None, slab(w), w.shape[2]), lambda j, i: (0, j * n_rt + i, 0))
                    for w in side_weights],
        out_specs=[pl.BlockSpec((tm, cw), lambda j, i: (i, j))]
                  + [pl.BlockSpec((slab(w), w.shape[2]), lambda j, i: (j * n_rt + i, 0)) for w in side_weights],
        out_shape=[jax.ShapeDtypeStruct((m, CONV_WIDTH), jnp.bfloat16)]
                  + [jax.ShapeDtypeStruct(w.shape[1:], jnp.bfloat16) for w in side_weights],
        scratch_shapes=[pltpu.VMEM((4 * cw, k), jnp.bfloat16),
                        pltpu.VMEM((SUBLANES, cw), jnp.float32)],
        compiler_params=_params("arbitrary", "arbitrary"),
        name="proj_conv",
    )(a, bias_tiles, conv_w, conv_b.reshape(1, CONV_WIDTH), wt, wt, wt, wt, *side_weights)
    return res[0], res[1:]


def _compress_kernel(*refs):
    n_groups = N_KV_GROUPS
    x_refs = refs[:n_groups]
    pek_ref, w1k_ref, w2k_ref, pev_ref, w1v_ref, w2v_ref, o_ref, pe_scr, w1_scr, w2_scr = refs[n_groups:]
    for which, (pe_ref, w1_ref, w2_ref) in enumerate(((pek_ref, w1k_ref, w2k_ref),
                                                      (pev_ref, w1v_ref, w2v_ref))):
        @pl.when((pl.program_id(1) == 0) & (pl.program_id(0) == which))
        def _(pe_ref=pe_ref, w1_ref=w1_ref, w2_ref=w2_ref):
            pe_scr[...] = pe_ref[...]
            for l in range(CMP_BLOCK):
                w1_scr[l] = w1_ref[l].astype(w1_scr.dtype)
            w2_scr[...] = w2_ref[...].astype(w2_scr.dtype)

    half = CMP_BLOCK // 2
    n_chunks = x_refs[0].shape[1] // CMP_STRIDE
    rows = n_groups * n_chunks
    top = jnp.zeros((rows, CMP_HIDDEN), jnp.float32)
    bot = jnp.zeros((rows, CMP_HIDDEN), jnp.float32)
    for l in range(half):
        xl = jnp.concatenate([x_ref[0, pl.ds(l, n_chunks, stride=CMP_STRIDE), :] for x_ref in x_refs], axis=0)
        a = (xl + pe_scr[l:l + 1, :]).astype(jnp.bfloat16)
        c = (xl + pe_scr[half + l:half + l + 1, :]).astype(jnp.bfloat16)
        top = top + jnp.dot(a, w1_scr[l], preferred_element_type=jnp.float32)
        bot = bot + jnp.dot(c, w1_scr[half + l], preferred_element_type=jnp.float32)
    for g in range(n_groups):
        r0 = g * n_chunks
        h = top[r0:r0 + n_chunks] + pltpu.roll(bot[r0:r0 + n_chunks], n_chunks - 1, axis=0)
        h = _silu(h).astype(jnp.bfloat16)
        o_ref[0, 0, g] = jnp.dot(h, w2_scr[...], preferred_element_type=jnp.float32).astype(o_ref.dtype)


def _compress(kv, pe_k, w1_k, w2_k, pe_v, w1_v, w2_v):
    b, s, _ = kv.shape
    whole = lambda shape: pl.BlockSpec(shape, lambda t, bi: (0,) * len(shape))
    w1_shape = (CMP_BLOCK, HEAD_DIM, CMP_HIDDEN)
    n_c = s // CMP_STRIDE
    return pl.pallas_call(
        _compress_kernel,
        grid=(2, b),
        in_specs=[pl.BlockSpec((1, s, HEAD_DIM), lambda t, bi, g=g: (bi, 0, t * N_KV_GROUPS + g))
                  for g in range(N_KV_GROUPS)] + [
                  whole((CMP_BLOCK, HEAD_DIM)), whole(w1_shape), whole((CMP_HIDDEN, HEAD_DIM)),
                  whole((CMP_BLOCK, HEAD_DIM)), whole(w1_shape), whole((CMP_HIDDEN, HEAD_DIM))],
        out_specs=pl.BlockSpec((1, 1, N_KV_GROUPS, n_c, HEAD_DIM), lambda t, bi: (t, bi, 0, 0, 0)),
        out_shape=jax.ShapeDtypeStruct((2, b, N_KV_GROUPS, n_c, HEAD_DIM), jnp.bfloat16),
        scratch_shapes=[pltpu.VMEM((CMP_BLOCK, HEAD_DIM), jnp.float32),
                        pltpu.VMEM(w1_shape, jnp.bfloat16),
                        pltpu.VMEM((CMP_HIDDEN, HEAD_DIM), jnp.bfloat16)],
        compiler_params=_params("arbitrary", "arbitrary"),
        name="compress",
    )(*([kv] * N_KV_GROUPS), pe_k, w1_k.reshape(w1_shape), w2_k, pe_v, w1_v.reshape(w1_shape), w2_v)


def _attn_kernel(q_ref, ks_ref, vs_ref, kw_ref, vw_ref, kc_ref, vc_ref, gate_ref, z_ref,
                 aggt_ref, blockid_ref, wbias_ref, gsel_ref, o_ref,
                 kaug_scr, vsaug_scr, vwaug_scr, vcaug_scr, s_scr, m_scr, acc_scr):
    tq = Q_TILE
    hpg = HEADS_PER_GROUP
    gw = GROUP_WIDTH
    seq = kw_ref.shape[1]
    gps = kc_ref.shape[1]
    i = pl.program_id(2)
    t0 = i * tq

    @pl.when(i == 0)
    def _():
        ones = jnp.ones((seq, LANES), jnp.bfloat16)
        for gi in range(gps):
            cols = slice(gi * HEAD_DIM, (gi + 1) * HEAD_DIM)
            kaug_scr[gi, :, :HEAD_DIM] = ks_ref[0, :, cols]
            kaug_scr[gi, :, HEAD_DIM:] = blockid_ref[...]
            vsaug_scr[gi, :, :HEAD_DIM] = vs_ref[0, :, cols]
            vsaug_scr[gi, :, HEAD_DIM:] = ones
            vwaug_scr[gi, :, :HEAD_DIM] = vw_ref[0, :, cols]
            vwaug_scr[gi, :, HEAD_DIM:] = ones
            vcaug_scr[gi, :, :HEAD_DIM] = vc_ref[0, gi]
            vcaug_scr[gi, :, HEAD_DIM:] = ones[:vcaug_scr.shape[1]]

    t_q = t0 + lax.broadcasted_iota(jnp.int32, (tq, 1), 0)
    t_s = jnp.concatenate([t_q] * hpg, axis=0)
    lane = lax.broadcasted_iota(jnp.int32, (1, LANES), 1)
    n_cmp = (seq - CMP_BLOCK) // CMP_STRIDE + 1
    n_slc = seq // SEL_BLOCK
    n_win = WINDOW + tq
    start = pl.multiple_of(jnp.maximum(t0 - WINDOW, 0), tq)
    w_bias = jnp.concatenate([wbias_ref[0]] * hpg, axis=0)
    vis_c = (lane * CMP_STRIDE + (CMP_BLOCK - 1) <= t_s) & (lane < n_cmp)
    aggt = aggt_ref[...]
    blk = lax.broadcasted_iota(jnp.int32, (n_slc, 1), 0)
    t_l = t0 + lax.broadcasted_iota(jnp.int32, (1, tq), 1)
    cur = t_l >> int(math.log2(SEL_BLOCK))
    causal = blk * SEL_BLOCK <= t_l
    forced = ((blk == 0) | (blk == cur) | (blk == cur - 1)) & causal
    gate = gate_ref[0]
    gate_hi = gate.astype(jnp.bfloat16)
    gate_lo = (gate - gate_hi.astype(jnp.float32)).astype(jnp.bfloat16)
    gate_hl = jnp.concatenate([gate_hi, gate_lo], axis=1)

    def first_region(gi):
        q = q_ref[0, :, gi * gw:(gi + 1) * gw]
        qs = jnp.concatenate([q[:, h * HEAD_DIM:(h + 1) * HEAD_DIM] for h in range(hpg)], axis=0)

        kw = kw_ref[0, pl.ds(start, n_win), gi * HEAD_DIM:(gi + 1) * HEAD_DIM]
        sw = lax.dot_general(qs, kw, _NT, preferred_element_type=jnp.float32) + w_bias
        p_w = jnp.exp2(sw - jnp.max(sw, axis=-1, keepdims=True))
        pv_w = jnp.dot(p_w.astype(jnp.bfloat16), vwaug_scr[gi, pl.ds(start, n_win), :],
                       preferred_element_type=jnp.float32)
        o_win = pv_w[:, :HEAD_DIM] * (1.0 / pv_w[:, HEAD_DIM:])

        sc = lax.dot_general(qs, kc_ref[0, gi], _NT, preferred_element_type=jnp.float32)
        sc = jnp.where(vis_c, sc, MASKED)
        p_c = jnp.where(vis_c, jnp.exp2(sc - jnp.max(sc, axis=-1, keepdims=True)), 0.0)
        pv_c = jnp.dot(p_c.astype(jnp.bfloat16), vcaug_scr[gi],
                       preferred_element_type=jnp.float32)
        l_c = pv_c[:, HEAD_DIM:]
        inv_c = jnp.where(l_c > 0.0, 1.0 / l_c, 0.0)
        o_cmp = pv_c[:, :HEAD_DIM] * inv_c
        p_c = p_c * inv_c

        p_sum = p_c[0:tq]
        for h in range(1, hpg):
            p_sum = p_sum + p_c[h * tq:(h + 1) * tq]
        hi = p_sum.astype(jnp.bfloat16)
        r1 = p_sum - hi.astype(jnp.float32)
        mid = r1.astype(jnp.bfloat16)
        lo = (r1 - mid.astype(jnp.float32)).astype(jnp.bfloat16)
        imp = (lax.dot_general(aggt, hi, _NT, preferred_element_type=jnp.float32)
               + lax.dot_general(aggt, mid, _NT, preferred_element_type=jnp.float32)
               + lax.dot_general(aggt, lo, _NT, preferred_element_type=jnp.float32))
        imp = jnp.where(forced, FORCED_SCORE, imp)
        imp = jnp.where(causal, imp, -1.0)

        bias_slabs = []
        for v in range(n_slc // SUBLANES):
            lo_row = v * SUBLANES
            slab = imp[lo_row:lo_row + SUBLANES]
            sub = lo_row + lax.broadcasted_iota(jnp.int32, (SUBLANES, 1), 0)
            rank = jnp.zeros((SUBLANES, tq), jnp.float32)
            for jp in range(n_slc):
                other = jnp.broadcast_to(imp[jp:jp + 1], (SUBLANES, tq))
                if jp < lo_row:
                    ahead = jnp.where(other >= slab, 1.0, 0.0)
                elif jp >= lo_row + SUBLANES:
                    ahead = jnp.where(other > slab, 1.0, 0.0)
                else:
                    tie = jnp.where(sub > jp, 1.0, 0.0)
                    ahead = jnp.where(other > slab, 1.0, jnp.where(other == slab, tie, 0.0))
                rank = rank + ahead
            bias_slabs.append(jnp.where(rank < float(SEL_TOP_N), 0.0, MASKED))
        bias_t = jnp.concatenate(bias_slabs + [jnp.zeros((LANES - n_slc, tq), jnp.float32)], axis=0)
        sel_bias = bias_t.T.astype(jnp.bfloat16)
        qa = jnp.concatenate([qs, jnp.concatenate([sel_bias] * hpg, axis=0)], axis=1)
        return o_win, o_cmp, qa

    first = [first_region(gi) for gi in range(gps)]

    n_full = t0 // SEL_CHUNK
    groups = [g for g in (4, 2, 1) if g <= max((seq - tq) // SEL_CHUNK, 1)]

    def scores(gi, c):
        base = pl.multiple_of(c * SEL_CHUNK, SEL_CHUNK)
        return base, lax.dot_general(first[gi][2], kaug_scr[gi, pl.ds(base, SEL_CHUNK), :], _NT,
                                     preferred_element_type=jnp.float32)

    def keep(gi, base, s, m):
        s_scr[gi, :, pl.ds(base, SEL_CHUNK)] = s
        for k in range(SEL_CHUNK // LANES):
            m = jnp.maximum(m, s[:, k * LANES:(k + 1) * LANES])
        return m

    for gi in range(gps):
        base_d, s_d = scores(gi, n_full)
        kpos_d = base_d + lax.broadcasted_iota(jnp.int32, (1, SEL_CHUNK), 1)
        m_scr[gi] = keep(gi, base_d, jnp.where(kpos_d <= t_s, s_d, MASKED),
                         jnp.full(m_scr.shape[1:], MASKED, jnp.float32))
    done = 0
    for g in groups:
        @pl.when((n_full & g) != 0)
        def _(done=done, g=g):
            for gi in range(gps):
                m = m_scr[gi]
                for u in range(g):
                    m = keep(gi, *scores(gi, done + u), m)
                m_scr[gi] = m
        done = done + (n_full & g)
    for gi in range(gps):
        m_scr[gi] = jnp.broadcast_to(jnp.max(m_scr[gi], axis=-1, keepdims=True), m_scr.shape[1:])

    def values(gi, c, m):
        base = pl.multiple_of(c * SEL_CHUNK, SEL_CHUNK)
        ps = [jnp.exp2(s_scr[gi, :, pl.ds(base + k * LANES, LANES)] - m).astype(jnp.bfloat16)
              for k in range(SEL_CHUNK // LANES)]
        return jnp.dot(jnp.concatenate(ps, axis=1), vsaug_scr[gi, pl.ds(base, SEL_CHUNK), :],
                       preferred_element_type=jnp.float32)

    for gi in range(gps):
        acc_scr[gi] = values(gi, n_full, m_scr[gi])
    done = 0
    for g in groups:
        @pl.when((n_full & g) != 0)
        def _(done=done, g=g):
            for gi in range(gps):
                m, acc = m_scr[gi], acc_scr[gi]
                for u in range(g):
                    acc = acc + values(gi, done + u, m)
                acc_scr[gi] = acc
        done = done + (n_full & g)

    for gi in range(gps):
        o_win, o_cmp, _ = first[gi]
        o_sel = acc_scr[gi, :, :HEAD_DIM] * (1.0 / acc_scr[gi, :, HEAD_DIM:])
        g_exp = jnp.dot(gate_hl, gsel_ref[gi], preferred_element_type=jnp.float32)
        g_col = lambda c: g_exp[:, c * LANES:(c + 1) * LANES]
        outs = []
        for h in range(hpg):
            r0 = h * tq
            outs.append(g_col(h) * o_cmp[r0:r0 + tq]
                        + g_col(hpg + h) * o_sel[r0:r0 + tq]
                        + g_col(2 * hpg + h) * o_win[r0:r0 + tq])
        o = jnp.concatenate(outs, axis=1)
        cols = slice(gi * gw, (gi + 1) * gw)
        o_ref[0, :, cols] = (o * _silu(z_ref[0, :, cols])).astype(o_ref.dtype)


def _attention(qkv, kcb, vcb, gate, zu, aggt, blockid, wbias, gsel):
    b, s, _ = qkv.shape
    gps = GROUPS_PER_STEP
    assert SEL_CHUNK % Q_TILE == 0 and WINDOW % Q_TILE == 0 and s % SEL_CHUNK == 0
    assert N_KV_GROUPS % gps == 0
    kvw = gps * HEAD_DIM
    qb = ATTN_WIDTH // kvw
    kvb = KV_WIDTH // kvw
    n_c = s // CMP_STRIDE
    rows = HEADS_PER_GROUP * Q_TILE
    n_pat = WINDOW // Q_TILE
    grid = (b, N_KV_GROUPS // gps, s // Q_TILE)
    kv_spec = lambda off: pl.BlockSpec((1, s, kvw), lambda bi, g, i: (bi, 0, off + g))
    wide = pl.BlockSpec((1, Q_TILE, gps * GROUP_WIDTH), lambda bi, g, i: (bi, i, g))
    return pl.pallas_call(
        _attn_kernel,
        grid=grid,
        in_specs=[wide,
                  kv_spec(qb), kv_spec(qb + kvb), kv_spec(qb + 2 * kvb), kv_spec(qb + 3 * kvb),
                  pl.BlockSpec((1, gps, n_c, HEAD_DIM), lambda bi, g, i: (bi, g, 0, 0)),
                  pl.BlockSpec((1, gps, n_c, HEAD_DIM), lambda bi, g, i: (bi, g, 0, 0)),
                  pl.BlockSpec((1, Q_TILE, LANES), lambda bi, g, i: (bi, i, 0)),
                  wide,
                  pl.BlockSpec(aggt.shape, lambda bi, g, i: (0, 0)),
                  pl.BlockSpec(blockid.shape, lambda bi, g, i: (0, 0)),
                  pl.BlockSpec((1, Q_TILE, WINDOW + Q_TILE),
                               lambda bi, g, i: (jnp.minimum(i, n_pat), 0, 0)),
                  pl.BlockSpec((gps,) + gsel.shape[1:], lambda bi, g, i: (g, 0, 0))],
        out_specs=wide,
        out_shape=jax.ShapeDtypeStruct((b, s, ATTN_WIDTH), jnp.bfloat16),
        scratch_shapes=[pltpu.VMEM((gps, s, HEAD_DIM + LANES), jnp.bfloat16),
                        pltpu.VMEM((gps, s, HEAD_DIM + LANES), jnp.bfloat16),
                        pltpu.VMEM((gps, s, HEAD_DIM + LANES), jnp.bfloat16),
                        pltpu.VMEM((gps, n_c, HEAD_DIM + LANES), jnp.bfloat16),
                        pltpu.VMEM((gps, rows, s), jnp.float32),
                        pltpu.VMEM((gps, rows, LANES), jnp.float32),
                        pltpu.VMEM((gps, rows, HEAD_DIM + LANES), jnp.float32)],
        compiler_params=_params("arbitrary", "arbitrary", "arbitrary"),
        name="nsa_attention",
    )(qkv, qkv, qkv, qkv, qkv, kcb, vcb, gate, zu, aggt, blockid, wbias, gsel)


def _merge_kernel(a_ref, c_ref, wa_ref, wc_ref, g0_ref, g1_ref, o_ref):
    ya = jnp.dot(a_ref[...], wa_ref[...], preferred_element_type=jnp.float32)
    yc = jnp.dot(c_ref[...], wc_ref[...], preferred_element_type=jnp.float32)
    o_ref[...] = (_sigmoid(g0_ref[...]) * ya + _sigmoid(g1_ref[...]) * yc).astype(o_ref.dtype)


def _merge(a, c, wa, wc, zg, gcol0, tm=MERGE_TM, tn=MERGE_TN):
    m, k = a.shape
    n = wa.shape[-1]
    g0 = gcol0 // tn
    g1 = (gcol0 + n) // tn
    return pl.pallas_call(
        _merge_kernel,
        grid=(n // tn, m // tm),
        in_specs=[pl.BlockSpec((tm, k), lambda j, i: (i, 0)),
                  pl.BlockSpec((tm, k), lambda j, i: (i, 0)),
                  pl.BlockSpec((k, tn), lambda j, i: (0, j)),
                  pl.BlockSpec((k, tn), lambda j, i: (0, j)),
                  pl.BlockSpec((tm, tn), lambda j, i: (i, g0 + j)),
                  pl.BlockSpec((tm, tn), lambda j, i: (i, g1 + j))],
        out_specs=pl.BlockSpec((tm, tn), lambda j, i: (i, j)),
        out_shape=jax.ShapeDtypeStruct((m, n), jnp.bfloat16),
        compiler_params=_params("arbitrary", "arbitrary"),
        name="merge",
    )(a, c, wa, wc, zg, zg)


def _out_kernel(y_ref, w_ref, x_ref, g_ref, o_ref):
    r = x_ref[...] + jnp.dot(y_ref[...], w_ref[...], preferred_element_type=jnp.float32)
    ms = jnp.mean(r * r, axis=-1, keepdims=True)
    o_ref[...] = r * lax.rsqrt(ms + RMS_EPS) * g_ref[...]


def _out_proj(y, w, x2d, g, tm=OUT_TM):
    m, k = y.shape
    n = w.shape[-1]
    return pl.pallas_call(
        _out_kernel,
        grid=(m // tm,),
        in_specs=[pl.BlockSpec((tm, k), lambda i: (i, 0)),
                  pl.BlockSpec((k, n), lambda i: (0, 0), pipeline_mode=pl.Buffered(1)),
                  pl.BlockSpec((tm, n), lambda i: (i, 0)),
                  pl.BlockSpec((1, n), lambda i: (0, 0))],
        out_specs=pl.BlockSpec((tm, n), lambda i: (i, 0)),
        out_shape=jax.ShapeDtypeStruct((m, n), jnp.float32),
        compiler_params=_params("arbitrary"),
        name="out_proj",
    )(y, w, x2d, g.reshape(1, n))


def _attention_constants(s):
    n_slc = s // SEL_BLOCK
    n_cmp = (s - CMP_BLOCK) // CMP_STRIDE + 1
    c0 = np.arange(s // CMP_STRIDE)[None, :] * CMP_STRIDE
    s0 = np.arange(n_slc)[:, None] * SEL_BLOCK
    overlap = np.maximum(0, np.minimum(c0 + CMP_BLOCK, s0 + SEL_BLOCK) - np.maximum(c0, s0))
    aggt = (overlap / CMP_BLOCK) * (np.arange(s // CMP_STRIDE)[None, :] < n_cmp)
    blockid = (np.arange(s)[:, None] // SEL_BLOCK == np.arange(LANES)[None, :]).astype(np.float32)
    n_pat = WINDOW // Q_TILE
    pats = []
    for p in range(n_pat + 1):
        t = p * Q_TILE + np.arange(Q_TILE)[:, None]
        kpos = max(p * Q_TILE - WINDOW, 0) + np.arange(WINDOW + Q_TILE)[None, :]
        pats.append(np.where((kpos <= t) & (kpos > t - WINDOW), 0.0, MASKED))
    n_bh = 3 * HEADS_PER_GROUP
    gsel = np.zeros((N_KV_GROUPS, 2 * LANES, n_bh * LANES), np.float32)
    for g in range(N_KV_GROUPS):
        for br in range(3):
            for h in range(HEADS_PER_GROUP):
                src = br * N_HEADS + g * HEADS_PER_GROUP + h
                c = br * HEADS_PER_GROUP + h
                gsel[g, [src, LANES + src], c * LANES:(c + 1) * LANES] = 1.0
    return (jnp.asarray(aggt, jnp.bfloat16), jnp.asarray(blockid, jnp.bfloat16),
            jnp.asarray(np.stack(pats), jnp.float32), jnp.asarray(gsel, jnp.bfloat16))


def _layer(x, norm_g, w_in, b_in, pe_k, w1_k, w2_k, pe_v, w1_v, w2_v,
           conv_w, conv_b, p_attn, p_conv, w_o, out_g):
    b, s, d = x.shape
    m = b * s
    bf = jnp.bfloat16
    x2d = x.reshape(m, d)
    tn = PROJ_TN

    c_q = ATTN_WIDTH
    c_cmp = c_q + 2 * KV_WIDTH
    c_kv = c_cmp + 4 * KV_WIDTH
    n_gate = 3 * N_HEADS
    c_z = c_kv + n_gate
    c_conv = c_z + ATTN_WIDTH
    c_merge = c_conv + 4 * CONV_WIDTH
    assert c_cmp - c_q == tn and c_q % tn == 0 and (c_kv - c_cmp) % tn == 0
    assert all(c % W_ALIGN == 0 for c in (c_q, c_cmp, c_kv, c_z, c_conv, c_merge))
    wt = jnp.swapaxes(w_in, 1, 2).reshape(w_in.shape[2], w_in.shape[1])

    hn, kvc, gate = _norm_proj(x2d, norm_g, wt, jnp.concatenate([b_in[c_q:c_cmp], b_in[c_kv:c_kv + LANES]]),
                               c_q, c_cmp - c_q, c_kv)
    kvc = kvc.reshape(b, s, -1)
    q_tiles = c_q // tn
    qkv = _proj(hn, wt, jnp.concatenate([b_in[:c_q], b_in[c_cmp:c_kv]]),
                lambda j: jnp.where(j < q_tiles, j * tn, c_cmp + (j - q_tiles) * tn),
                q_tiles + (c_kv - c_cmp) // tn, tn, bf,
                scaled_tiles=q_tiles, scale=SCALE * math.log2(math.e), name="proj_qkv").reshape(b, s, -1)
    z_tiles = ATTN_WIDTH // tn
    zg = _proj(hn, wt, jnp.concatenate([b_in[c_z:c_conv], b_in[c_merge:]]),
               lambda j: jnp.where(j < z_tiles, c_z + j * tn, c_merge + (j - z_tiles) * tn),
               z_tiles + 2 * D_MODEL // tn, tn, jnp.float32, tm=PROJ_TM_F32, name="proj_zg").reshape(b, s, -1)
    cw = CONV_TN
    b_conv = b_in[c_conv:c_merge].reshape(4, CONV_WIDTH // cw, cw).transpose(1, 0, 2).reshape(1, -1)
    conv_starts = [lambda j, p=p: c_conv + p * CONV_WIDTH + j * cw for p in range(4)]
    c, (wa, wc, wo) = _proj_conv(hn, wt, b_conv, conv_w, conv_b, conv_starts, s, (p_attn, p_conv, w_o), cw=cw)

    kvcb = _compress(kvc, pe_k, w1_k, w2_k, pe_v, w1_v, w2_v)
    kcb, vcb = kvcb[0], kvcb[1]

    aggt, blockid, wbias, gsel = _attention_constants(s)
    a = _attention(qkv, kcb, vcb, gate.reshape(b, s, LANES), zg, aggt, blockid, wbias, gsel)

    y = _merge(a.reshape(m, -1), c, wa, wc, zg.reshape(m, -1), ATTN_WIDTH)
    return _out_proj(y, wo, x2d, out_g).reshape(b, s, d)


def kernel(x, norm_g, w_in, b_in, cmp_pe_k, cmp_w1_k, cmp_w2_k, cmp_pe_v, cmp_w1_v, cmp_w2_v,
           conv_w, conv_b, p_attn, p_conv, w_o, final_g):
    assert norm_g.shape[0] == 1, "single-layer block"
    return _layer(x, norm_g[0], w_in, b_in[0], cmp_pe_k[0], cmp_w1_k[0], cmp_w2_k[0],
                  cmp_pe_v[0], cmp_w1_v[0], cmp_w2_v[0], conv_w[0], conv_b[0],
                  p_attn, p_conv, w_o, final_g)
```

```python
import functools
import math

import numpy as np
import jax
import jax.numpy as jnp
from jax import lax
from jax.experimental import pallas as pl
from jax.experimental.pallas import tpu as pltpu

D_MODEL = 2048
N_HEADS = 16
HEAD_DIM = 128
N_KV_GROUPS = 4
HEADS_PER_GROUP = N_HEADS // N_KV_GROUPS
ATTN_WIDTH = N_HEADS * HEAD_DIM
KV_WIDTH = N_KV_GROUPS * HEAD_DIM
GROUP_WIDTH = HEADS_PER_GROUP * HEAD_DIM
CMP_BLOCK = 32
CMP_STRIDE = 16
CMP_HIDDEN = 256
SEL_BLOCK = 64
SEL_TOP_N = 8
WINDOW = 512
SCALE = HEAD_DIM ** -0.5
CONV_WIDTH = D_MODEL
CONV_K = 3
RMS_EPS = 1e-6
FORCED_SCORE = 1e4
MASKED = -(2.0 ** 100)

LANES = 128
SUBLANES = 8
Q_TILE = 256
SEL_CHUNK = 256
GROUPS_PER_STEP = 2
W_ROWS = 512
W_ALIGN = 16
VMEM_LIMIT = 56 * 1024 * 1024
PROJ_TM, PROJ_TN = 2048, 1024
QKV_TM, QKV_SLOTS = 2048, 2
ZG_TM, ZG_SLOTS = 1024, 3
CONV_TN = 256
NORM_TM = 1024
NORM_SUB_ROWS = 256
MERGE_TM, MERGE_TN = 512, 1024
MERGE_SLOTS = 3
OUT_TM = 512

_NT = (((1,), (1,)), ((), ()))


def _params(*sem):
    return pltpu.CompilerParams(dimension_semantics=sem, vmem_limit_bytes=VMEM_LIMIT)


def _sigmoid(x):
    return 1.0 / (1.0 + jnp.exp(-x))


def _silu(x):
    return x * _sigmoid(x)


def _cast_weight(w_scr, w_ref, row0=0):
    rows = w_ref.shape[0]
    for r in range(0, rows, W_ROWS):
        n = min(W_ROWS, rows - r)
        w_scr[row0 + r:row0 + r + n, :] = w_ref[r:r + n, :].astype(w_scr.dtype)


def _weight_specs(k, piece_rows, start_fns):
    return [pl.BlockSpec((pl.Element(piece_rows), pl.Element(k)),
                         lambda j, i, f=f: (pl.multiple_of(f(j), W_ALIGN), 0))
            for f in start_fns]


def _tile_copy(hbm, buf, sems, step, row_tile, col0=None):
    slots, tm, width = buf.shape
    slot = step % slots
    row0 = pl.multiple_of(row_tile * tm, tm)
    src = hbm.at[pl.ds(row0, tm), :] if col0 is None else hbm.at[pl.ds(row0, tm), pl.ds(col0, width)]
    return pltpu.make_async_copy(src, buf.at[slot], sems.at[slot])


def _ring_tiles(streams, step, total):
    def copies(st):
        return [_tile_copy(hbm, buf, sems, st, *where(st)) for hbm, buf, sems, where in streams]

    ahead = streams[0][1].shape[0] - 1

    @pl.when(step == 0)
    def _():
        for s0 in range(ahead):
            for cp in copies(s0):
                cp.start()

    @pl.when(step + ahead < total)
    def _():
        for cp in copies(step + ahead):
            cp.start()

    for cp in copies(step):
        cp.wait()
    return [buf[step % buf.shape[0]] for _, buf, _, _ in streams]


def _proj_kernel(a_hbm, b_ref, w_ref, o_ref, w_scr, a_buf, sems, *, scaled_tiles, scale):
    n_tiles, n_rt = pl.num_programs(0), pl.num_programs(1)
    step = pl.program_id(0) * n_rt + pl.program_id(1)

    @pl.when(pl.program_id(1) == 0)
    def _():
        _cast_weight(w_scr, w_ref)

    a, = _ring_tiles([(a_hbm, a_buf, sems, lambda st: (st % n_rt, None))], step, n_tiles * n_rt)
    r = lax.dot_general(a, w_scr[...], _NT, preferred_element_type=jnp.float32) + b_ref[...]
    if scaled_tiles:
        r = r * jnp.where(pl.program_id(0) < scaled_tiles, scale, 1.0)
    o_ref[...] = r.astype(o_ref.dtype)


def _proj(a, wt, bias, start_fn, n_tiles, tn, out_dtype, tm, slots, scaled_tiles=0, scale=1.0, name="proj"):
    m, k = a.shape
    assert n_tiles * (m // tm) >= slots - 1
    return pl.pallas_call(
        functools.partial(_proj_kernel, scaled_tiles=scaled_tiles, scale=scale),
        grid=(n_tiles, m // tm),
        in_specs=[pl.BlockSpec(memory_space=pl.ANY),
                  pl.BlockSpec((1, tn), lambda j, i: (0, j))] + _weight_specs(k, tn, [start_fn]),
        out_specs=pl.BlockSpec((tm, tn), lambda j, i: (i, j)),
        out_shape=jax.ShapeDtypeStruct((m, n_tiles * tn), out_dtype),
        scratch_shapes=[pltpu.VMEM((tn, k), jnp.bfloat16),
                        pltpu.VMEM((slots, tm, k), a.dtype),
                        pltpu.SemaphoreType.DMA((slots,))],
        compiler_params=_params("arbitrary", "arbitrary"),
        name=name,
    )(a, bias.reshape(1, n_tiles * tn), wt)


def _norm_proj_kernel(x_ref, g_ref, b_ref, wkv_ref, wg_ref, hn_ref, kv_ref, gate_ref, w_scr):
    n_kv = kv_ref.shape[1]

    @pl.when(pl.program_id(0) == 0)
    def _():
        _cast_weight(w_scr, wkv_ref)
        _cast_weight(w_scr, wg_ref, n_kv)

    for r0 in range(0, x_ref.shape[0], NORM_SUB_ROWS):
        rows = slice(r0, r0 + NORM_SUB_ROWS)
        x = x_ref[rows, :]
        ms = jnp.mean(x * x, axis=-1, keepdims=True)
        hn = (x * lax.rsqrt(ms + RMS_EPS) * g_ref[...]).astype(hn_ref.dtype)
        hn_ref[rows, :] = hn
        r = lax.dot_general(hn, w_scr[...], _NT, preferred_element_type=jnp.float32) + b_ref[...]
        kv_ref[rows, :] = r[:, :n_kv]
        gate_ref[rows, :] = _sigmoid(r[:, n_kv:])


def _norm_proj(x2d, g, wt, bias, kv_start, n_kv, gate_start, tm=NORM_TM):
    m, k = x2d.shape
    once = dict(pipeline_mode=pl.Buffered(1))
    return pl.pallas_call(
        _norm_proj_kernel,
        grid=(m // tm,),
        in_specs=[pl.BlockSpec((tm, k), lambda i: (i, 0)),
                  pl.BlockSpec((1, k), lambda i: (0, 0)),
                  pl.BlockSpec((1, n_kv + LANES), lambda i: (0, 0)),
                  pl.BlockSpec((pl.Element(n_kv), pl.Element(k)), lambda i: (kv_start, 0), **once),
                  pl.BlockSpec((pl.Element(LANES), pl.Element(k)), lambda i: (gate_start, 0), **once)],
        out_specs=[pl.BlockSpec((tm, k), lambda i: (i, 0)),
                   pl.BlockSpec((tm, n_kv), lambda i: (i, 0)),
                   pl.BlockSpec((tm, LANES), lambda i: (i, 0))],
        out_shape=[jax.ShapeDtypeStruct((m, k), jnp.bfloat16),
                   jax.ShapeDtypeStruct((m, n_kv), jnp.float32),
                   jax.ShapeDtypeStruct((m, LANES), jnp.float32)],
        scratch_shapes=[pltpu.VMEM((n_kv + LANES, k), jnp.bfloat16)],
        compiler_params=_params("arbitrary"),
        name="norm_proj",
    )(x2d, g.reshape(1, k), bias.reshape(1, n_kv + LANES), wt, wt)


def _proj_conv_kernel(a_ref, b_ref, cw_ref, cb_ref, wu_ref, wcc_ref, wcb_ref, wz_ref, *rest, seq):
    n_side = (len(rest) - 3) // 2
    side_in, o_ref, side_out = rest[:n_side], rest[n_side], rest[n_side + 1:2 * n_side + 1]
    w_scr, halo_scr = rest[2 * n_side + 1:]
    tm, cw = o_ref.shape
    i = pl.program_id(1)

    for src, dst in zip(side_in, side_out):
        dst[...] = src[...].astype(dst.dtype)

    @pl.when(i == 0)
    def _():
        for p, w_ref in enumerate((wu_ref, wcc_ref, wcb_ref, wz_ref)):
            _cast_weight(w_scr, w_ref, p * cw)

    @pl.when((i * tm) % seq == 0)
    def _():
        halo_scr[...] = jnp.zeros(halo_scr.shape, jnp.float32)

    r = lax.dot_general(a_ref[...], w_scr[...], _NT, preferred_element_type=jnp.float32) + b_ref[...]
    u, cc, cb, z = (r[:, p * cw:(p + 1) * cw] for p in range(4))
    v = cc * u
    prev = halo_scr[...]
    halo_scr[...] = v[tm - SUBLANES:, :]
    row = lax.broadcasted_iota(jnp.int32, (tm, 1), 0)
    y = cw_ref[CONV_K - 1:CONV_K, :] * v
    for d in range(1, CONV_K):
        shifted = pltpu.roll(v, d, axis=0)
        for t in range(d):
            shifted = jnp.where(row == t, prev[SUBLANES - d + t:SUBLANES - d + t + 1, :], shifted)
        y = y + cw_ref[CONV_K - 1 - d:CONV_K - d, :] * shifted
    o_ref[...] = (cb * (y + cb_ref[...]) * _silu(z)).astype(o_ref.dtype)


def _proj_conv(a, wt, bias_tiles, conv_w, conv_b, start_fns, seq, side_weights, tm=PROJ_TM, cw=CONV_TN):
    m, k = a.shape
    n_tiles = CONV_WIDTH // cw
    n_rt = m // tm
    assert seq % tm == 0 and CONV_K - 1 <= SUBLANES and len(start_fns) == 4
    n_steps = n_tiles * n_rt
    slab = lambda w: w.shape[1] // n_steps
    assert all(w.shape[1] % (n_steps * 16) == 0 for w in side_weights)
    res = pl.pallas_call(
        functools.partial(_proj_conv_kernel, seq=seq),
        grid=(n_tiles, n_rt),
        in_specs=[pl.BlockSpec((tm, k), lambda j, i: (i, 0)),
                  pl.BlockSpec((1, 4 * cw), lambda j, i: (0, j)),
                  pl.BlockSpec((CONV_K, cw), lambda j, i: (0, j)),
                  pl.BlockSpec((1, cw), lambda j, i: (0, j))] + _weight_specs(k, cw, start_fns)
                 + [pl.BlockSpec((None, slab(w), w.shape[2]), lambda j, i: (0, j * n_rt + i, 0))
                    for w in side_weights],
        out_specs=[pl.BlockSpec((tm, cw), lambda j, i: (i, j))]
                  + [pl.BlockSpec((slab(w), w.shape[2]), lambda j, i: (j * n_rt + i, 0)) for w in side_weights],
        out_shape=[jax.ShapeDtypeStruct((m, CONV_WIDTH), jnp.bfloat16)]
                  + [jax.ShapeDtypeStruct(w.shape[1:], jnp.bfloat16) for w in side_weights],
        scratch_shapes=[pltpu.VMEM((4 * cw, k), jnp.bfloat16),
                        pltpu.VMEM((SUBLANES, cw), jnp.float32)],
        compiler_params=_params("arbitrary", "arbitrary"),
        name="proj_conv",
    )(a, bias_tiles, conv_w, conv_b.reshape(1, CONV_WIDTH), wt, wt, wt, wt, *side_weights)
    return res[0], res[1:]


def _compress_kernel(*refs):
    n_groups = N_KV_GROUPS
    x_refs = refs[:n_groups]
    pek_ref, w1k_ref, w2k_ref, pev_ref, w1v_ref, w2v_ref, o_ref, pe_scr, w1_scr, w2_scr = refs[n_groups:]
    for which, (pe_ref, w1_ref, w2_ref) in enumerate(((pek_ref, w1k_ref, w2k_ref),
                                                      (pev_ref, w1v_ref, w2v_ref))):
        @pl.when((pl.program_id(1) == 0) & (pl.program_id(0) == which))
        def _(pe_ref=pe_ref, w1_ref=w1_ref, w2_ref=w2_ref):
            pe_scr[...] = pe_ref[...]
            for l in range(CMP_BLOCK):
                w1_scr[l] = w1_ref[l].astype(w1_scr.dtype)
            w2_scr[...] = w2_ref[...].astype(w2_scr.dtype)

    half = CMP_BLOCK // 2
    n_chunks = x_refs[0].shape[1] // CMP_STRIDE
    rows = n_groups * n_chunks
    top = jnp.zeros((rows, CMP_HIDDEN), jnp.float32)
    bot = jnp.zeros((rows, CMP_HIDDEN), jnp.float32)
    for l in range(half):
        xl = jnp.concatenate([x_ref[0, pl.ds(l, n_chunks, stride=CMP_STRIDE), :] for x_ref in x_refs], axis=0)
        a = (xl + pe_scr[l:l + 1, :]).astype(jnp.bfloat16)
        c = (xl + pe_scr[half + l:half + l + 1, :]).astype(jnp.bfloat16)
        top = top + jnp.dot(a, w1_scr[l], preferred_element_type=jnp.float32)
        bot = bot + jnp.dot(c, w1_scr[half + l], preferred_element_type=jnp.float32)
    for g in range(n_groups):
        r0 = g * n_chunks
        h = top[r0:r0 + n_chunks] + pltpu.roll(bot[r0:r0 + n_chunks], n_chunks - 1, axis=0)
        h = _silu(h).astype(jnp.bfloat16)
        o_ref[0, 0, g] = jnp.dot(h, w2_scr[...], preferred_element_type=jnp.float32).astype(o_ref.dtype)


def _compress(kv, pe_k, w1_k, w2_k, pe_v, w1_v, w2_v):
    b, s, _ = kv.shape
    whole = lambda shape: pl.BlockSpec(shape, lambda t, bi: (0,) * len(shape))
    w1_shape = (CMP_BLOCK, HEAD_DIM, CMP_HIDDEN)
    n_c = s // CMP_STRIDE
    return pl.pallas_call(
        _compress_kernel,
        grid=(2, b),
        in_specs=[pl.BlockSpec((1, s, HEAD_DIM), lambda t, bi, g=g: (bi, 0, t * N_KV_GROUPS + g))
                  for g in range(N_KV_GROUPS)] + [
                  whole((CMP_BLOCK, HEAD_DIM)), whole(w1_shape), whole((CMP_HIDDEN, HEAD_DIM)),
                  whole((CMP_BLOCK, HEAD_DIM)), whole(w1_shape), whole((CMP_HIDDEN, HEAD_DIM))],
        out_specs=pl.BlockSpec((1, 1, N_KV_GROUPS, n_c, HEAD_DIM), lambda t, bi: (t, bi, 0, 0, 0)),
        out_shape=jax.ShapeDtypeStruct((2, b, N_KV_GROUPS, n_c, HEAD_DIM), jnp.bfloat16),
        scratch_shapes=[pltpu.VMEM((CMP_BLOCK, HEAD_DIM), jnp.float32),
                        pltpu.VMEM(w1_shape, jnp.bfloat16),
                        pltpu.VMEM((CMP_HIDDEN, HEAD_DIM), jnp.bfloat16)],
        compiler_params=_params("arbitrary", "arbitrary"),
        name="compress",
    )(*([kv] * N_KV_GROUPS), pe_k, w1_k.reshape(w1_shape), w2_k, pe_v, w1_v.reshape(w1_shape), w2_v)


def _attn_kernel(q_ref, ks_ref, vs_ref, kw_ref, vw_ref, kc_ref, vc_ref, gate_ref, z_ref,
                 aggt_ref, blockid_ref, wbias_ref, gsel_ref, o_ref,
                 kaug_scr, vsaug_scr, vwaug_scr, vcaug_scr, s_scr, m_scr, acc_scr):
    tq = Q_TILE
    hpg = HEADS_PER_GROUP
    gw = GROUP_WIDTH
    seq = kw_ref.shape[1]
    gps = kc_ref.shape[1]
    i = pl.program_id(2)
    t0 = i * tq

    @pl.when(i == 0)
    def _():
        ones = jnp.ones((seq, LANES), jnp.bfloat16)
        for gi in range(gps):
            cols = slice(gi * HEAD_DIM, (gi + 1) * HEAD_DIM)
            kaug_scr[gi, :, :HEAD_DIM] = ks_ref[0, :, cols]
            kaug_scr[gi, :, HEAD_DIM:] = blockid_ref[...]
            vsaug_scr[gi, :, :HEAD_DIM] = vs_ref[0, :, cols]
            vsaug_scr[gi, :, HEAD_DIM:] = ones
            vwaug_scr[gi, :, :HEAD_DIM] = vw_ref[0, :, cols]
            vwaug_scr[gi, :, HEAD_DIM:] = ones
            vcaug_scr[gi, :, :HEAD_DIM] = vc_ref[0, gi]
            vcaug_scr[gi, :, HEAD_DIM:] = ones[:vcaug_scr.shape[1]]

    t_q = t0 + lax.broadcasted_iota(jnp.int32, (tq, 1), 0)
    t_s = jnp.concatenate([t_q] * hpg, axis=0)
    lane = lax.broadcasted_iota(jnp.int32, (1, LANES), 1)
    n_cmp = (seq - CMP_BLOCK) // CMP_STRIDE + 1
    n_slc = seq // SEL_BLOCK
    n_win = WINDOW + tq
    start = pl.multiple_of(jnp.maximum(t0 - WINDOW, 0), tq)
    w_bias = jnp.concatenate([wbias_ref[0]] * hpg, axis=0)
    vis_c = (lane * CMP_STRIDE + (CMP_BLOCK - 1) <= t_s) & (lane < n_cmp)
    aggt = aggt_ref[...]
    blk = lax.broadcasted_iota(jnp.int32, (n_slc, 1), 0)
    t_l = t0 + lax.broadcasted_iota(jnp.int32, (1, tq), 1)
    cur = t_l >> int(math.log2(SEL_BLOCK))
    causal = blk * SEL_BLOCK <= t_l
    forced = ((blk == 0) | (blk == cur) | (blk == cur - 1)) & causal
    gate = gate_ref[0]
    gate_hi = gate.astype(jnp.bfloat16)
    gate_lo = (gate - gate_hi.astype(jnp.float32)).astype(jnp.bfloat16)
    gate_hl = jnp.concatenate([gate_hi, gate_lo], axis=1)

    def first_region(gi):
        q = q_ref[0, :, gi * gw:(gi + 1) * gw]
        qs = jnp.concatenate([q[:, h * HEAD_DIM:(h + 1) * HEAD_DIM] for h in range(hpg)], axis=0)

        kw = kw_ref[0, pl.ds(start, n_win), gi * HEAD_DIM:(gi + 1) * HEAD_DIM]
        sw = lax.dot_general(qs, kw, _NT, preferred_element_type=jnp.float32) + w_bias
        p_w = jnp.exp2(sw - jnp.max(sw, axis=-1, keepdims=True))
        pv_w = jnp.dot(p_w.astype(jnp.bfloat16), vwaug_scr[gi, pl.ds(start, n_win), :],
                       preferred_element_type=jnp.float32)
        o_win = pv_w[:, :HEAD_DIM] * (1.0 / pv_w[:, HEAD_DIM:])

        sc = lax.dot_general(qs, kc_ref[0, gi], _NT, preferred_element_type=jnp.float32)
        sc = jnp.where(vis_c, sc, MASKED)
        p_c = jnp.where(vis_c, jnp.exp2(sc - jnp.max(sc, axis=-1, keepdims=True)), 0.0)
        pv_c = jnp.dot(p_c.astype(jnp.bfloat16), vcaug_scr[gi],
                       preferred_element_type=jnp.float32)
        l_c = pv_c[:, HEAD_DIM:]
        inv_c = jnp.where(l_c > 0.0, 1.0 / l_c, 0.0)
        o_cmp = pv_c[:, :HEAD_DIM] * inv_c
        p_c = p_c * inv_c

        p_sum = p_c[0:tq]
        for h in range(1, hpg):
            p_sum = p_sum + p_c[h * tq:(h + 1) * tq]
        hi = p_sum.astype(jnp.bfloat16)
        r1 = p_sum - hi.astype(jnp.float32)
        mid = r1.astype(jnp.bfloat16)
        lo = (r1 - mid.astype(jnp.float32)).astype(jnp.bfloat16)
        imp = (lax.dot_general(aggt, hi, _NT, preferred_element_type=jnp.float32)
               + lax.dot_general(aggt, mid, _NT, preferred_element_type=jnp.float32)
               + lax.dot_general(aggt, lo, _NT, preferred_element_type=jnp.float32))
        imp = jnp.where(forced, FORCED_SCORE, imp)
        imp = jnp.where(causal, imp, -1.0)

        bias_slabs = []
        for v in range(n_slc // SUBLANES):
            lo_row = v * SUBLANES
            slab = imp[lo_row:lo_row + SUBLANES]
            sub = lo_row + lax.broadcasted_iota(jnp.int32, (SUBLANES, 1), 0)
            rank = jnp.zeros((SUBLANES, tq), jnp.float32)
            for jp in range(n_slc):
                other = jnp.broadcast_to(imp[jp:jp + 1], (SUBLANES, tq))
                if jp < lo_row:
                    ahead = jnp.where(other >= slab, 1.0, 0.0)
                elif jp >= lo_row + SUBLANES:
                    ahead = jnp.where(other > slab, 1.0, 0.0)
                else:
                    tie = jnp.where(sub > jp, 1.0, 0.0)
                    ahead = jnp.where(other > slab, 1.0, jnp.where(other == slab, tie, 0.0))
                rank = rank + ahead
            bias_slabs.append(jnp.where(rank < float(SEL_TOP_N), 0.0, MASKED))
        bias_t = jnp.concatenate(bias_slabs + [jnp.zeros((LANES - n_slc, tq), jnp.float32)], axis=0)
        sel_bias = bias_t.T.astype(jnp.bfloat16)
        qa = jnp.concatenate([qs, jnp.concatenate([sel_bias] * hpg, axis=0)], axis=1)
        return o_win, o_cmp, qa

    first = [first_region(gi) for gi in range(gps)]

    n_full = t0 // SEL_CHUNK
    groups = [g for g in (4, 2, 1) if g <= max((seq - tq) // SEL_CHUNK, 1)]

    def scores(gi, c):
        base = pl.multiple_of(c * SEL_CHUNK, SEL_CHUNK)
        return base, lax.dot_general(first[gi][2], kaug_scr[gi, pl.ds(base, SEL_CHUNK), :], _NT,
                                     preferred_element_type=jnp.float32)

    def keep(gi, base, s, m):
        s_scr[gi, :, pl.ds(base, SEL_CHUNK)] = s
        for k in range(SEL_CHUNK // LANES):
            m = jnp.maximum(m, s[:, k * LANES:(k + 1) * LANES])
        return m

    for gi in range(gps):
        base_d, s_d = scores(gi, n_full)
        kpos_d = base_d + lax.broadcasted_iota(jnp.int32, (1, SEL_CHUNK), 1)
        m_scr[gi] = keep(gi, base_d, jnp.where(kpos_d <= t_s, s_d, MASKED),
                         jnp.full(m_scr.shape[1:], MASKED, jnp.float32))
    done = 0
    for g in groups:
        @pl.when((n_full & g) != 0)
        def _(done=done, g=g):
            for gi in range(gps):
                m = m_scr[gi]
                for u in range(g):
                    m = keep(gi, *scores(gi, done + u), m)
                m_scr[gi] = m
        done = done + (n_full & g)
    for gi in range(gps):
        m_scr[gi] = jnp.broadcast_to(jnp.max(m_scr[gi], axis=-1, keepdims=True), m_scr.shape[1:])

    def values(gi, c, m):
        base = pl.multiple_of(c * SEL_CHUNK, SEL_CHUNK)
        ps = [jnp.exp2(s_scr[gi, :, pl.ds(base + k * LANES, LANES)] - m).astype(jnp.bfloat16)
              for k in range(SEL_CHUNK // LANES)]
        return jnp.dot(jnp.concatenate(ps, axis=1), vsaug_scr[gi, pl.ds(base, SEL_CHUNK), :],
                       preferred_element_type=jnp.float32)

    for gi in range(gps):
        acc_scr[gi] = values(gi, n_full, m_scr[gi])
    done = 0
    for g in groups:
        @pl.when((n_full & g) != 0)
        def _(done=done, g=g):
            for gi in range(gps):
                m, acc = m_scr[gi], acc_scr[gi]
                for u in range(g):
                    acc = acc + values(gi, done + u, m)
                acc_scr[gi] = acc
        done = done + (n_full & g)

    for gi in range(gps):
        o_win, o_cmp, _ = first[gi]
        o_sel = acc_scr[gi, :, :HEAD_DIM] * (1.0 / acc_scr[gi, :, HEAD_DIM:])
        g_exp = jnp.dot(gate_hl, gsel_ref[gi], preferred_element_type=jnp.float32)
        g_col = lambda c: g_exp[:, c * LANES:(c + 1) * LANES]
        outs = []
        for h in range(hpg):
            r0 = h * tq
            outs.append(g_col(h) * o_cmp[r0:r0 + tq]
                        + g_col(hpg + h) * o_sel[r0:r0 + tq]
                        + g_col(2 * hpg + h) * o_win[r0:r0 + tq])
        o = jnp.concatenate(outs, axis=1)
        cols = slice(gi * gw, (gi + 1) * gw)
        o_ref[0, :, cols] = (o * _silu(z_ref[0, :, cols])).astype(o_ref.dtype)


def _attention(qkv, kcb, vcb, gate, zu, aggt, blockid, wbias, gsel):
    b, s, _ = qkv.shape
    gps = GROUPS_PER_STEP
    assert SEL_CHUNK % Q_TILE == 0 and WINDOW % Q_TILE == 0 and s % SEL_CHUNK == 0
    assert N_KV_GROUPS % gps == 0
    kvw = gps * HEAD_DIM
    qb = ATTN_WIDTH // kvw
    kvb = KV_WIDTH // kvw
    n_c = s // CMP_STRIDE
    rows = HEADS_PER_GROUP * Q_TILE
    n_pat = WINDOW // Q_TILE
    grid = (b, N_KV_GROUPS // gps, s // Q_TILE)
    kv_spec = lambda off: pl.BlockSpec((1, s, kvw), lambda bi, g, i: (bi, 0, off + g))
    wide = pl.BlockSpec((1, Q_TILE, gps * GROUP_WIDTH), lambda bi, g, i: (bi, i, g))
    return pl.pallas_call(
        _attn_kernel,
        grid=grid,
        in_specs=[wide,
                  kv_spec(qb), kv_spec(qb + kvb), kv_spec(qb + 2 * kvb), kv_spec(qb + 3 * kvb),
                  pl.BlockSpec((1, gps, n_c, HEAD_DIM), lambda bi, g, i: (bi, g, 0, 0)),
                  pl.BlockSpec((1, gps, n_c, HEAD_DIM), lambda bi, g, i: (bi, g, 0, 0)),
                  pl.BlockSpec((1, Q_TILE, LANES), lambda bi, g, i: (bi, i, 0)),
                  wide,
                  pl.BlockSpec(aggt.shape, lambda bi, g, i: (0, 0)),
                  pl.BlockSpec(blockid.shape, lambda bi, g, i: (0, 0)),
                  pl.BlockSpec((1, Q_TILE, WINDOW + Q_TILE),
                               lambda bi, g, i: (jnp.minimum(i, n_pat), 0, 0)),
                  pl.BlockSpec((gps,) + gsel.shape[1:], lambda bi, g, i: (g, 0, 0))],
        out_specs=wide,
        out_shape=jax.ShapeDtypeStruct((b, s, ATTN_WIDTH), jnp.bfloat16),
        scratch_shapes=[pltpu.VMEM((gps, s, HEAD_DIM + LANES), jnp.bfloat16),
                        pltpu.VMEM((gps, s, HEAD_DIM + LANES), jnp.bfloat16),
                        pltpu.VMEM((gps, s, HEAD_DIM + LANES), jnp.bfloat16),
                        pltpu.VMEM((gps, n_c, HEAD_DIM + LANES), jnp.bfloat16),
                        pltpu.VMEM((gps, rows, s), jnp.float32),
                        pltpu.VMEM((gps, rows, LANES), jnp.float32),
                        pltpu.VMEM((gps, rows, HEAD_DIM + LANES), jnp.float32)],
        compiler_params=_params("arbitrary", "arbitrary", "arbitrary"),
        name="nsa_attention",
    )(qkv, qkv, qkv, qkv, qkv, kcb, vcb, gate, zu, aggt, blockid, wbias, gsel)


def _merge_kernel(a_hbm, c_hbm, g_hbm, wa_ref, wc_ref, o_ref, a_buf, c_buf, g0_buf, g1_buf, sems, *, gcol0):
    n_ct, n_rt = pl.num_programs(0), pl.num_programs(1)
    step = pl.program_id(0) * n_rt + pl.program_id(1)
    tn = o_ref.shape[1]
    n = n_ct * tn

    def rows(st):
        return st % n_rt, None

    def gate_cols(first):
        return lambda st: (st % n_rt, pl.multiple_of(first + (st // n_rt) * tn, tn))

    a, c, g0, g1 = _ring_tiles([(a_hbm, a_buf, sems.at[0], rows), (c_hbm, c_buf, sems.at[1], rows),
                                (g_hbm, g0_buf, sems.at[2], gate_cols(gcol0)),
                                (g_hbm, g1_buf, sems.at[3], gate_cols(gcol0 + n))], step, n_ct * n_rt)
    ya = jnp.dot(a, wa_ref[...], preferred_element_type=jnp.float32)
    yc = jnp.dot(c, wc_ref[...], preferred_element_type=jnp.float32)
    o_ref[...] = (_sigmoid(g0) * ya + _sigmoid(g1) * yc).astype(o_ref.dtype)


def _merge(a, c, wa, wc, zg, gcol0, tm=MERGE_TM, tn=MERGE_TN, slots=MERGE_SLOTS):
    m, k = a.shape
    n = wa.shape[-1]
    any_spec = pl.BlockSpec(memory_space=pl.ANY)
    return pl.pallas_call(
        functools.partial(_merge_kernel, gcol0=gcol0),
        grid=(n // tn, m // tm),
        in_specs=[any_spec, any_spec, any_spec,
                  pl.BlockSpec((k, tn), lambda j, i: (0, j)),
                  pl.BlockSpec((k, tn), lambda j, i: (0, j))],
        out_specs=pl.BlockSpec((tm, tn), lambda j, i: (i, j)),
        out_shape=jax.ShapeDtypeStruct((m, n), jnp.bfloat16),
        scratch_shapes=[pltpu.VMEM((slots, tm, k), a.dtype), pltpu.VMEM((slots, tm, k), c.dtype),
                        pltpu.VMEM((slots, tm, tn), zg.dtype), pltpu.VMEM((slots, tm, tn), zg.dtype),
                        pltpu.SemaphoreType.DMA((4, slots))],
        compiler_params=_params("arbitrary", "arbitrary"),
        name="merge",
    )(a, c, zg, wa, wc)


def _out_kernel(y_ref, w_ref, x_ref, g_ref, o_ref):
    r = x_ref[...] + jnp.dot(y_ref[...], w_ref[...], preferred_element_type=jnp.float32)
    ms = jnp.mean(r * r, axis=-1, keepdims=True)
    o_ref[...] = r * lax.rsqrt(ms + RMS_EPS) * g_ref[...]


def _out_proj(y, w, x2d, g, tm=OUT_TM):
    m, k = y.shape
    n = w.shape[-1]
    return pl.pallas_call(
        _out_kernel,
        grid=(m // tm,),
        in_specs=[pl.BlockSpec((tm, k), lambda i: (i, 0)),
                  pl.BlockSpec((k, n), lambda i: (0, 0), pipeline_mode=pl.Buffered(1)),
                  pl.BlockSpec((tm, n), lambda i: (i, 0)),
                  pl.BlockSpec((1, n), lambda i: (0, 0))],
        out_specs=pl.BlockSpec((tm, n), lambda i: (i, 0)),
        out_shape=jax.ShapeDtypeStruct((m, n), jnp.float32),
        compiler_params=_params("arbitrary"),
        name="out_proj",
    )(y, w, x2d, g.reshape(1, n))


def _attention_constants(s):
    n_slc = s // SEL_BLOCK
    n_cmp = (s - CMP_BLOCK) // CMP_STRIDE + 1
    c0 = np.arange(s // CMP_STRIDE)[None, :] * CMP_STRIDE
    s0 = np.arange(n_slc)[:, None] * SEL_BLOCK
    overlap = np.maximum(0, np.minimum(c0 + CMP_BLOCK, s0 + SEL_BLOCK) - np.maximum(c0, s0))
    aggt = (overlap / CMP_BLOCK) * (np.arange(s // CMP_STRIDE)[None, :] < n_cmp)
    blockid = (np.arange(s)[:, None] // SEL_BLOCK == np.arange(LANES)[None, :]).astype(np.float32)
    n_pat = WINDOW // Q_TILE
    pats = []
    for p in range(n_pat + 1):
        t = p * Q_TILE + np.arange(Q_TILE)[:, None]
        kpos = max(p * Q_TILE - WINDOW, 0) + np.arange(WINDOW + Q_TILE)[None, :]
        pats.append(np.where((kpos <= t) & (kpos > t - WINDOW), 0.0, MASKED))
    n_bh = 3 * HEADS_PER_GROUP
    gsel = np.zeros((N_KV_GROUPS, 2 * LANES, n_bh * LANES), np.float32)
    for g in range(N_KV_GROUPS):
        for br in range(3):
            for h in range(HEADS_PER_GROUP):
                src = br * N_HEADS + g * HEADS_PER_GROUP + h
                c = br * HEADS_PER_GROUP + h
                gsel[g, [src, LANES + src], c * LANES:(c + 1) * LANES] = 1.0
    return (jnp.asarray(aggt, jnp.bfloat16), jnp.asarray(blockid, jnp.bfloat16),
            jnp.asarray(np.stack(pats), jnp.float32), jnp.asarray(gsel, jnp.bfloat16))


def _layer(x, norm_g, w_in, b_in, pe_k, w1_k, w2_k, pe_v, w1_v, w2_v,
           conv_w, conv_b, p_attn, p_conv, w_o, out_g):
    b, s, d = x.shape
    m = b * s
    bf = jnp.bfloat16
    x2d = x.reshape(m, d)
    tn = PROJ_TN

    c_q = ATTN_WIDTH
    c_cmp = c_q + 2 * KV_WIDTH
    c_kv = c_cmp + 4 * KV_WIDTH
    n_gate = 3 * N_HEADS
    c_z = c_kv + n_gate
    c_conv = c_z + ATTN_WIDTH
    c_merge = c_conv + 4 * CONV_WIDTH
    assert c_cmp - c_q == tn and c_q % tn == 0 and (c_kv - c_cmp) % tn == 0
    assert all(c % W_ALIGN == 0 for c in (c_q, c_cmp, c_kv, c_z, c_conv, c_merge))
    wt = jnp.swapaxes(w_in, 1, 2).reshape(w_in.shape[2], w_in.shape[1])

    hn, kvc, gate = _norm_proj(x2d, norm_g, wt, jnp.concatenate([b_in[c_q:c_cmp], b_in[c_kv:c_kv + LANES]]),
                               c_q, c_cmp - c_q, c_kv)
    kvc = kvc.reshape(b, s, -1)
    q_tiles = c_q // tn
    qkv = _proj(hn, wt, jnp.concatenate([b_in[:c_q], b_in[c_cmp:c_kv]]),
                lambda j: jnp.where(j < q_tiles, j * tn, c_cmp + (j - q_tiles) * tn),
                q_tiles + (c_kv - c_cmp) // tn, tn, bf, QKV_TM, QKV_SLOTS,
                scaled_tiles=q_tiles, scale=SCALE * math.log2(math.e), name="proj_qkv").reshape(b, s, -1)
    z_tiles = ATTN_WIDTH // tn
    zg = _proj(hn, wt, jnp.concatenate([b_in[c_z:c_conv], b_in[c_merge:]]),
               lambda j: jnp.where(j < z_tiles, c_z + j * tn, c_merge + (j - z_tiles) * tn),
               z_tiles + 2 * D_MODEL // tn, tn, jnp.float32, ZG_TM, ZG_SLOTS, name="proj_zg").reshape(b, s, -1)
    cw = CONV_TN
    b_conv = b_in[c_conv:c_merge].reshape(4, CONV_WIDTH // cw, cw).transpose(1, 0, 2).reshape(1, -1)
    conv_starts = [lambda j, p=p: c_conv + p * CONV_WIDTH + j * cw for p in range(4)]
    c, (wa, wc, wo) = _proj_conv(hn, wt, b_conv, conv_w, conv_b, conv_starts, s, (p_attn, p_conv, w_o), cw=cw)

    kvcb = _compress(kvc, pe_k, w1_k, w2_k, pe_v, w1_v, w2_v)
    kcb, vcb = kvcb[0], kvcb[1]

    aggt, blockid, wbias, gsel = _attention_constants(s)
    a = _attention(qkv, kcb, vcb, gate.reshape(b, s, LANES), zg, aggt, blockid, wbias, gsel)

    y = _merge(a.reshape(m, -1), c, wa, wc, zg.reshape(m, -1), ATTN_WIDTH)
    return _out_proj(y, wo, x2d, out_g).reshape(b, s, d)


def kernel(x, norm_g, w_in, b_in, cmp_pe_k, cmp_w1_k, cmp_w2_k, cmp_pe_v, cmp_w1_v, cmp_w2_v,
           conv_w, conv_b, p_attn, p_conv, w_o, final_g):
    assert norm_g.shape[0] == 1, "single-layer block"
    return _layer(x, norm_g[0], w_in, b_in[0], cmp_pe_k[0], cmp_w1_k[0], cmp_w2_k[0],
                  cmp_pe_v[0], cmp_w1_v[0], cmp_w2_v[0], conv_w[0], conv_b[0],
                  p_attn, p_conv, w_o, final_g)
```

```python
import functools
import math

import numpy as np
import jax
import jax.numpy as jnp
from jax import lax
from jax.experimental import pallas as pl
from jax.experimental.pallas import tpu as pltpu

D_MODEL = 2048
N_HEADS = 16
HEAD_DIM = 128
N_KV_GROUPS = 4
HEADS_PER_GROUP = N_HEADS // N_KV_GROUPS
ATTN_WIDTH = N_HEADS * HEAD_DIM
KV_WIDTH = N_KV_GROUPS * HEAD_DIM
GROUP_WIDTH = HEADS_PER_GROUP * HEAD_DIM
CMP_BLOCK = 32
CMP_STRIDE = 16
CMP_HIDDEN = 256
SEL_BLOCK = 64
SEL_TOP_N = 8
WINDOW = 512
SCALE = HEAD_DIM ** -0.5
CONV_WIDTH = D_MODEL
CONV_K = 3
RMS_EPS = 1e-6
FORCED_SCORE = 1e4
MASKED = -(2.0 ** 100)

LANES = 128
SUBLANES = 8
Q_TILE = 256
SEL_CHUNK = 256
GROUPS_PER_STEP = 2
W_ROWS = 512
W_ALIGN = 16
VMEM_LIMIT = 56 * 1024 * 1024
PROJ_TM, PROJ_TN = 2048, 1024
QKV_TM, QKV_SLOTS = 2048, 2
ZG_TM, ZG_SLOTS = 1024, 3
RING_DMA_PRIORITY = 1
CONV_TN = 256
NORM_TM = 1024
NORM_SUB_ROWS = 256
MERGE_TM, MERGE_TN = 512, 1024
OUT_TM = 512

_NT = (((1,), (1,)), ((), ()))


def _params(*sem):
    return pltpu.CompilerParams(dimension_semantics=sem, vmem_limit_bytes=VMEM_LIMIT)


def _sigmoid(x):
    return 1.0 / (1.0 + jnp.exp(-x))


def _silu(x):
    return x * _sigmoid(x)


def _cast_weight(w_scr, w_ref, row0=0):
    rows = w_ref.shape[0]
    for r in range(0, rows, W_ROWS):
        n = min(W_ROWS, rows - r)
        w_scr[row0 + r:row0 + r + n, :] = w_ref[r:r + n, :].astype(w_scr.dtype)


def _weight_specs(k, piece_rows, start_fns):
    return [pl.BlockSpec((pl.Element(piece_rows), pl.Element(k)),
                         lambda j, i, f=f: (pl.multiple_of(f(j), W_ALIGN), 0))
            for f in start_fns]


def _tile_copy(hbm, buf, sems, step, row_tile, col0=None):
    slots, tm, width = buf.shape
    slot = step % slots
    row0 = pl.multiple_of(row_tile * tm, tm)
    src = hbm.at[pl.ds(row0, tm), :] if col0 is None else hbm.at[pl.ds(row0, tm), pl.ds(col0, width)]
    return pltpu.make_async_copy(src, buf.at[slot], sems.at[slot])


def _ring_tiles(streams, step, total):
    def copies(st):
        return [_tile_copy(hbm, buf, sems, st, *where(st)) for hbm, buf, sems, where in streams]

    ahead = streams[0][1].shape[0] - 1

    @pl.when(step == 0)
    def _():
        for s0 in range(ahead):
            for cp in copies(s0):
                cp.start(priority=RING_DMA_PRIORITY)

    @pl.when(step + ahead < total)
    def _():
        for cp in copies(step + ahead):
            cp.start(priority=RING_DMA_PRIORITY)

    for cp in copies(step):
        cp.wait()
    return [buf[step % buf.shape[0]] for _, buf, _, _ in streams]


def _proj_kernel(a_hbm, b_ref, w_ref, o_ref, w_scr, a_buf, sems, *, scaled_tiles, scale):
    n_tiles, n_rt = pl.num_programs(0), pl.num_programs(1)
    step = pl.program_id(0) * n_rt + pl.program_id(1)

    @pl.when(pl.program_id(1) == 0)
    def _():
        _cast_weight(w_scr, w_ref)

    a, = _ring_tiles([(a_hbm, a_buf, sems, lambda st: (st % n_rt, None))], step, n_tiles * n_rt)
    r = lax.dot_general(a, w_scr[...], _NT, preferred_element_type=jnp.float32) + b_ref[...]
    if scaled_tiles:
        r = r * jnp.where(pl.program_id(0) < scaled_tiles, scale, 1.0)
    o_ref[...] = r.astype(o_ref.dtype)


def _proj(a, wt, bias, start_fn, n_tiles, tn, out_dtype, tm, slots, scaled_tiles=0, scale=1.0, name="proj"):
    m, k = a.shape
    assert n_tiles * (m // tm) >= slots - 1
    return pl.pallas_call(
        functools.partial(_proj_kernel, scaled_tiles=scaled_tiles, scale=scale),
        grid=(n_tiles, m // tm),
        in_specs=[pl.BlockSpec(memory_space=pl.ANY),
                  pl.BlockSpec((1, tn), lambda j, i: (0, j))] + _weight_specs(k, tn, [start_fn]),
        out_specs=pl.BlockSpec((tm, tn), lambda j, i: (i, j)),
        out_shape=jax.ShapeDtypeStruct((m, n_tiles * tn), out_dtype),
        scratch_shapes=[pltpu.VMEM((tn, k), jnp.bfloat16),
                        pltpu.VMEM((slots, tm, k), a.dtype),
                        pltpu.SemaphoreType.DMA((slots,))],
        compiler_params=_params("arbitrary", "arbitrary"),
        name=name,
    )(a, bias.reshape(1, n_tiles * tn), wt)


def _norm_proj_kernel(x_ref, g_ref, b_ref, wkv_ref, wg_ref, hn_ref, kv_ref, gate_ref, w_scr):
    n_kv = kv_ref.shape[1]

    @pl.when(pl.program_id(0) == 0)
    def _():
        _cast_weight(w_scr, wkv_ref)
        _cast_weight(w_scr, wg_ref, n_kv)

    for r0 in range(0, x_ref.shape[0], NORM_SUB_ROWS):
        rows = slice(r0, r0 + NORM_SUB_ROWS)
        x = x_ref[rows, :]
        ms = jnp.mean(x * x, axis=-1, keepdims=True)
        hn = (x * lax.rsqrt(ms + RMS_EPS) * g_ref[...]).astype(hn_ref.dtype)
        hn_ref[rows, :] = hn
        r = lax.dot_general(hn, w_scr[...], _NT, preferred_element_type=jnp.float32) + b_ref[...]
        kv_ref[rows, :] = r[:, :n_kv]
        gate_ref[rows, :] = _sigmoid(r[:, n_kv:])


def _norm_proj(x2d, g, wt, bias, kv_start, n_kv, gate_start, tm=NORM_TM):
    m, k = x2d.shape
    once = dict(pipeline_mode=pl.Buffered(1))
    return pl.pallas_call(
        _norm_proj_kernel,
        grid=(m // tm,),
        in_specs=[pl.BlockSpec((tm, k), lambda i: (i, 0)),
                  pl.BlockSpec((1, k), lambda i: (0, 0)),
                  pl.BlockSpec((1, n_kv + LANES), lambda i: (0, 0)),
                  pl.BlockSpec((pl.Element(n_kv), pl.Element(k)), lambda i: (kv_start, 0), **once),
                  pl.BlockSpec((pl.Element(LANES), pl.Element(k)), lambda i: (gate_start, 0), **once)],
        out_specs=[pl.BlockSpec((tm, k), lambda i: (i, 0)),
                   pl.BlockSpec((tm, n_kv), lambda i: (i, 0)),
                   pl.BlockSpec((tm, LANES), lambda i: (i, 0))],
        out_shape=[jax.ShapeDtypeStruct((m, k), jnp.bfloat16),
                   jax.ShapeDtypeStruct((m, n_kv), jnp.float32),
                   jax.ShapeDtypeStruct((m, LANES), jnp.float32)],
        scratch_shapes=[pltpu.VMEM((n_kv + LANES, k), jnp.bfloat16)],
        compiler_params=_params("arbitrary"),
        name="norm_proj",
    )(x2d, g.reshape(1, k), bias.reshape(1, n_kv + LANES), wt, wt)


def _proj_conv_kernel(a_ref, b_ref, cw_ref, cb_ref, wu_ref, wcc_ref, wcb_ref, wz_ref, *rest, seq):
    n_side = (len(rest) - 3) // 2
    side_in, o_ref, side_out = rest[:n_side], rest[n_side], rest[n_side + 1:2 * n_side + 1]
    w_scr, halo_scr = rest[2 * n_side + 1:]
    tm, cw = o_ref.shape
    i = pl.program_id(1)

    for src, dst in zip(side_in, side_out):
        dst[...] = src[...].astype(dst.dtype)

    @pl.when(i == 0)
    def _():
        for p, w_ref in enumerate((wu_ref, wcc_ref, wcb_ref, wz_ref)):
            _cast_weight(w_scr, w_ref, p * cw)

    @pl.when((i * tm) % seq == 0)
    def _():
        halo_scr[...] = jnp.zeros(halo_scr.shape, jnp.float32)

    r = lax.dot_general(a_ref[...], w_scr[...], _NT, preferred_element_type=jnp.float32) + b_ref[...]
    u, cc, cb, z = (r[:, p * cw:(p + 1) * cw] for p in range(4))
    v = cc * u
    prev = halo_scr[...]
    halo_scr[...] = v[tm - SUBLANES:, :]
    row = lax.broadcasted_iota(jnp.int32, (tm, 1), 0)
    y = cw_ref[CONV_K - 1:CONV_K, :] * v
    for d in range(1, CONV_K):
        shifted = pltpu.roll(v, d, axis=0)
        for t in range(d):
            shifted = jnp.where(row == t, prev[SUBLANES - d + t:SUBLANES - d + t + 1, :], shifted)
        y = y + cw_ref[CONV_K - 1 - d:CONV_K - d, :] * shifted
    o_ref[...] = (cb * (y + cb_ref[...]) * _silu(z)).astype(o_ref.dtype)


def _proj_conv(a, wt, bias_tiles, conv_w, conv_b, start_fns, seq, side_weights, tm=PROJ_TM, cw=CONV_TN):
    m, k = a.shape
    n_tiles = CONV_WIDTH // cw
    n_rt = m // tm
    assert seq % tm == 0 and CONV_K - 1 <= SUBLANES and len(start_fns) == 4
    n_steps = n_tiles * n_rt
    slab = lambda w: w.shape[1] // n_steps
    assert all(w.shape[1] % (n_steps * 16) == 0 for w in side_weights)
    res = pl.pallas_call(
        functools.partial(_proj_conv_kernel, seq=seq),
        grid=(n_tiles, n_rt),
        in_specs=[pl.BlockSpec((tm, k), lambda j, i: (i, 0)),
                  pl.BlockSpec((1, 4 * cw), lambda j, i: (0, j)),
                  pl.BlockSpec((CONV_K, cw), lambda j, i: (0, j)),
                  pl.BlockSpec((1, cw), lambda j, i: (0, j))] + _weight_specs(k, cw, start_fns)
                 + [pl.BlockSpec((None, slab(w), w.shape[2]), lambda j, i: (0, j * n_rt + i, 0))
                    for w in side_weights],
        out_specs=[pl.BlockSpec((tm, cw), lambda j, i: (i, j))]
                  + [pl.BlockSpec((slab(w), w.shape[2]), lambda j, i: (j * n_rt + i, 0)) for w in side_weights],
        out_shape=[jax.ShapeDtypeStruct((m, CONV_WIDTH), jnp.bfloat16)]
                  + [jax.ShapeDtypeStruct(w.shape[1:], jnp.bfloat16) for w in side_weights],
        scratch_shapes=[pltpu.VMEM((4 * cw, k), jnp.bfloat16),
                        pltpu.VMEM((SUBLANES, cw), jnp.float32)],
        compiler_params=_params("arbitrary", "arbitrary"),
        name="proj_conv",
    )(a, bias_tiles, conv_w, conv_b.reshape(1, CONV_WIDTH), wt, wt, wt, wt, *side_weights)
    return res[0], res[1:]


def _compress_kernel(*refs):
    n_groups = N_KV_GROUPS
    x_refs = refs[:n_groups]
    pek_ref, w1k_ref, w2k_ref, pev_ref, w1v_ref, w2v_ref, o_ref, pe_scr, w1_scr, w2_scr = refs[n_groups:]
    for which, (pe_ref, w1_ref, w2_ref) in enumerate(((pek_ref, w1k_ref, w2k_ref),
                                                      (pev_ref, w1v_ref, w2v_ref))):
        @pl.when((pl.program_id(1) == 0) & (pl.program_id(0) == which))
        def _(pe_ref=pe_ref, w1_ref=w1_ref, w2_ref=w2_ref):
            pe_scr[...] = pe_ref[...]
            for l in range(CMP_BLOCK):
                w1_scr[l] = w1_ref[l].astype(w1_scr.dtype)
            w2_scr[...] = w2_ref[...].astype(w2_scr.dtype)

    half = CMP_BLOCK // 2
    n_chunks = x_refs[0].shape[1] // CMP_STRIDE
    rows = n_groups * n_chunks
    top = jnp.zeros((rows, CMP_HIDDEN), jnp.float32)
    bot = jnp.zeros((rows, CMP_HIDDEN), jnp.float32)
    for l in range(half):
        xl = jnp.concatenate([x_ref[0, pl.ds(l, n_chunks, stride=CMP_STRIDE), :] for x_ref in x_refs], axis=0)
        a = (xl + pe_scr[l:l + 1, :]).astype(jnp.bfloat16)
        c = (xl + pe_scr[half + l:half + l + 1, :]).astype(jnp.bfloat16)
        top = top + jnp.dot(a, w1_scr[l], preferred_element_type=jnp.float32)
        bot = bot + jnp.dot(c, w1_scr[half + l], preferred_element_type=jnp.float32)
    for g in range(n_groups):
        r0 = g * n_chunks
        h = top[r0:r0 + n_chunks] + pltpu.roll(bot[r0:r0 + n_chunks], n_chunks - 1, axis=0)
        h = _silu(h).astype(jnp.bfloat16)
        o_ref[0, 0, g] = jnp.dot(h, w2_scr[...], preferred_element_type=jnp.float32).astype(o_ref.dtype)


def _compress(kv, pe_k, w1_k, w2_k, pe_v, w1_v, w2_v):
    b, s, _ = kv.shape
    whole = lambda shape: pl.BlockSpec(shape, lambda t, bi: (0,) * len(shape))
    w1_shape = (CMP_BLOCK, HEAD_DIM, CMP_HIDDEN)
    n_c = s // CMP_STRIDE
    return pl.pallas_call(
        _compress_kernel,
        grid=(2, b),
        in_specs=[pl.BlockSpec((1, s, HEAD_DIM), lambda t, bi, g=g: (bi, 0, t * N_KV_GROUPS + g))
                  for g in range(N_KV_GROUPS)] + [
                  whole((CMP_BLOCK, HEAD_DIM)), whole(w1_shape), whole((CMP_HIDDEN, HEAD_DIM)),
                  whole((CMP_BLOCK, HEAD_DIM)), whole(w1_shape), whole((CMP_HIDDEN, HEAD_DIM))],
        out_specs=pl.BlockSpec((1, 1, N_KV_GROUPS, n_c, HEAD_DIM), lambda t, bi: (t, bi, 0, 0, 0)),
        out_shape=jax.ShapeDtypeStruct((2, b, N_KV_GROUPS, n_c, HEAD_DIM), jnp.bfloat16),
        scratch_shapes=[pltpu.VMEM((CMP_BLOCK, HEAD_DIM), jnp.float32),
                        pltpu.VMEM(w1_shape, jnp.bfloat16),
                        pltpu.VMEM((CMP_HIDDEN, HEAD_DIM), jnp.bfloat16)],
        compiler_params=_params("arbitrary", "arbitrary"),
        name="compress",
    )(*([kv] * N_KV_GROUPS), pe_k, w1_k.reshape(w1_shape), w2_k, pe_v, w1_v.reshape(w1_shape), w2_v)


def _attn_kernel(q_ref, ks_ref, vs_ref, kw_ref, vw_ref, kc_ref, vc_ref, gate_ref, z_ref,
                 aggt_ref, blockid_ref, wbias_ref, gsel_ref, o_ref,
                 kaug_scr, vsaug_scr, vwaug_scr, vcaug_scr, s_scr, m_scr, acc_scr):
    tq = Q_TILE
    hpg = HEADS_PER_GROUP
    gw = GROUP_WIDTH
    seq = kw_ref.shape[1]
    gps = kc_ref.shape[1]
    i = pl.program_id(2)
    t0 = i * tq

    @pl.when(i == 0)
    def _():
        ones = jnp.ones((seq, LANES), jnp.bfloat16)
        for gi in range(gps):
            cols = slice(gi * HEAD_DIM, (gi + 1) * HEAD_DIM)
            kaug_scr[gi, :, :HEAD_DIM] = ks_ref[0, :, cols]
            kaug_scr[gi, :, HEAD_DIM:] = blockid_ref[...]
            vsaug_scr[gi, :, :HEAD_DIM] = vs_ref[0, :, cols]
            vsaug_scr[gi, :, HEAD_DIM:] = ones
            vwaug_scr[gi, :, :HEAD_DIM] = vw_ref[0, :, cols]
            vwaug_scr[gi, :, HEAD_DIM:] = ones
            vcaug_scr[gi, :, :HEAD_DIM] = vc_ref[0, gi]
            vcaug_scr[gi, :, HEAD_DIM:] = ones[:vcaug_scr.shape[1]]

    t_q = t0 + lax.broadcasted_iota(jnp.int32, (tq, 1), 0)
    t_s = jnp.concatenate([t_q] * hpg, axis=0)
    lane = lax.broadcasted_iota(jnp.int32, (1, LANES), 1)
    n_cmp = (seq - CMP_BLOCK) // CMP_STRIDE + 1
    n_slc = seq // SEL_BLOCK
    n_win = WINDOW + tq
    start = pl.multiple_of(jnp.maximum(t0 - WINDOW, 0), tq)
    w_bias = jnp.concatenate([wbias_ref[0]] * hpg, axis=0)
    vis_c = (lane * CMP_STRIDE + (CMP_BLOCK - 1) <= t_s) & (lane < n_cmp)
    aggt = aggt_ref[...]
    blk = lax.broadcasted_iota(jnp.int32, (n_slc, 1), 0)
    t_l = t0 + lax.broadcasted_iota(jnp.int32, (1, tq), 1)
    cur = t_l >> int(math.log2(SEL_BLOCK))
    causal = blk * SEL_BLOCK <= t_l
    forced = ((blk == 0) | (blk == cur) | (blk == cur - 1)) & causal
    gate = gate_ref[0]
    gate_hi = gate.astype(jnp.bfloat16)
    gate_lo = (gate - gate_hi.astype(jnp.float32)).astype(jnp.bfloat16)
    gate_hl = jnp.concatenate([gate_hi, gate_lo], axis=1)

    def first_region(gi):
        q = q_ref[0, :, gi * gw:(gi + 1) * gw]
        qs = jnp.concatenate([q[:, h * HEAD_DIM:(h + 1) * HEAD_DIM] for h in range(hpg)], axis=0)

        kw = kw_ref[0, pl.ds(start, n_win), gi * HEAD_DIM:(gi + 1) * HEAD_DIM]
        sw = lax.dot_general(qs, kw, _NT, preferred_element_type=jnp.float32) + w_bias
        p_w = jnp.exp2(sw - jnp.max(sw, axis=-1, keepdims=True))
        pv_w = jnp.dot(p_w.astype(jnp.bfloat16), vwaug_scr[gi, pl.ds(start, n_win), :],
                       preferred_element_type=jnp.float32)
        o_win = pv_w[:, :HEAD_DIM] * (1.0 / pv_w[:, HEAD_DIM:])

        sc = lax.dot_general(qs, kc_ref[0, gi], _NT, preferred_element_type=jnp.float32)
        sc = jnp.where(vis_c, sc, MASKED)
        p_c = jnp.where(vis_c, jnp.exp2(sc - jnp.max(sc, axis=-1, keepdims=True)), 0.0)
        pv_c = jnp.dot(p_c.astype(jnp.bfloat16), vcaug_scr[gi],
                       preferred_element_type=jnp.float32)
        l_c = pv_c[:, HEAD_DIM:]
        inv_c = jnp.where(l_c > 0.0, 1.0 / l_c, 0.0)
        o_cmp = pv_c[:, :HEAD_DIM] * inv_c
        p_c = p_c * inv_c

        p_sum = p_c[0:tq]
        for h in range(1, hpg):
            p_sum = p_sum + p_c[h * tq:(h + 1) * tq]
        hi = p_sum.astype(jnp.bfloat16)
        r1 = p_sum - hi.astype(jnp.float32)
        mid = r1.astype(jnp.bfloat16)
        lo = (r1 - mid.astype(jnp.float32)).astype(jnp.bfloat16)
        imp = (lax.dot_general(aggt, hi, _NT, preferred_element_type=jnp.float32)
               + lax.dot_general(aggt, mid, _NT, preferred_element_type=jnp.float32)
               + lax.dot_general(aggt, lo, _NT, preferred_element_type=jnp.float32))
        imp = jnp.where(forced, FORCED_SCORE, imp)
        imp = jnp.where(causal, imp, -1.0)

        bias_slabs = []
        for v in range(n_slc // SUBLANES):
            lo_row = v * SUBLANES
            slab = imp[lo_row:lo_row + SUBLANES]
            sub = lo_row + lax.broadcasted_iota(jnp.int32, (SUBLANES, 1), 0)
            rank = jnp.zeros((SUBLANES, tq), jnp.float32)
            for jp in range(n_slc):
                other = jnp.broadcast_to(imp[jp:jp + 1], (SUBLANES, tq))
                if jp < lo_row:
                    ahead = jnp.where(other >= slab, 1.0, 0.0)
                elif jp >= lo_row + SUBLANES:
                    ahead = jnp.where(other > slab, 1.0, 0.0)
                else:
                    tie = jnp.where(sub > jp, 1.0, 0.0)
                    ahead = jnp.where(other > slab, 1.0, jnp.where(other == slab, tie, 0.0))
                rank = rank + ahead
            bias_slabs.append(jnp.where(rank < float(SEL_TOP_N), 0.0, MASKED))
        bias_t = jnp.concatenate(bias_slabs + [jnp.zeros((LANES - n_slc, tq), jnp.float32)], axis=0)
        sel_bias = bias_t.T.astype(jnp.bfloat16)
        qa = jnp.concatenate([qs, jnp.concatenate([sel_bias] * hpg, axis=0)], axis=1)
        return o_win, o_cmp, qa

    first = [first_region(gi) for gi in range(gps)]

    n_full = t0 // SEL_CHUNK
    groups = [g for g in (4, 2, 1) if g <= max((seq - tq) // SEL_CHUNK, 1)]

    def scores(gi, c):
        base = pl.multiple_of(c * SEL_CHUNK, SEL_CHUNK)
        return base, lax.dot_general(first[gi][2], kaug_scr[gi, pl.ds(base, SEL_CHUNK), :], _NT,
                                     preferred_element_type=jnp.float32)

    def keep(gi, base, s, m):
        s_scr[gi, :, pl.ds(base, SEL_CHUNK)] = s
        for k in range(SEL_CHUNK // LANES):
            m = jnp.maximum(m, s[:, k * LANES:(k + 1) * LANES])
        return m

    for gi in range(gps):
        base_d, s_d = scores(gi, n_full)
        kpos_d = base_d + lax.broadcasted_iota(jnp.int32, (1, SEL_CHUNK), 1)
        m_scr[gi] = keep(gi, base_d, jnp.where(kpos_d <= t_s, s_d, MASKED),
                         jnp.full(m_scr.shape[1:], MASKED, jnp.float32))
    done = 0
    for g in groups:
        @pl.when((n_full & g) != 0)
        def _(done=done, g=g):
            for gi in range(gps):
                m = m_scr[gi]
                for u in range(g):
                    m = keep(gi, *scores(gi, done + u), m)
                m_scr[gi] = m
        done = done + (n_full & g)
    for gi in range(gps):
        m_scr[gi] = jnp.broadcast_to(jnp.max(m_scr[gi], axis=-1, keepdims=True), m_scr.shape[1:])

    def values(gi, c, m):
        base = pl.multiple_of(c * SEL_CHUNK, SEL_CHUNK)
        ps = [jnp.exp2(s_scr[gi, :, pl.ds(base + k * LANES, LANES)] - m).astype(jnp.bfloat16)
              for k in range(SEL_CHUNK // LANES)]
        return jnp.dot(jnp.concatenate(ps, axis=1), vsaug_scr[gi, pl.ds(base, SEL_CHUNK), :],
                       preferred_element_type=jnp.float32)

    for gi in range(gps):
        acc_scr[gi] = values(gi, n_full, m_scr[gi])
    done = 0
    for g in groups:
        @pl.when((n_full & g) != 0)
        def _(done=done, g=g):
            for gi in range(gps):
                m, acc = m_scr[gi], acc_scr[gi]
                for u in range(g):
                    acc = acc + values(gi, done + u, m)
                acc_scr[gi] = acc
        done = done + (n_full & g)

    for gi in range(gps):
        o_win, o_cmp, _ = first[gi]
        o_sel = acc_scr[gi, :, :HEAD_DIM] * (1.0 / acc_scr[gi, :, HEAD_DIM:])
        g_exp = jnp.dot(gate_hl, gsel_ref[gi], preferred_element_type=jnp.float32)
        g_col = lambda c: g_exp[:, c * LANES:(c + 1) * LANES]
        outs = []
        for h in range(hpg):
            r0 = h * tq
            outs.append(g_col(h) * o_cmp[r0:r0 + tq]
                        + g_col(hpg + h) * o_sel[r0:r0 + tq]
                        + g_col(2 * hpg + h) * o_win[r0:r0 + tq])
        o = jnp.concatenate(outs, axis=1)
        cols = slice(gi * gw, (gi + 1) * gw)
        o_ref[0, :, cols] = (o * _silu(z_ref[0, :, cols])).astype(o_ref.dtype)


def _attention(qkv, kcb, vcb, gate, zu, aggt, blockid, wbias, gsel):
    b, s, _ = qkv.shape
    gps = GROUPS_PER_STEP
    assert SEL_CHUNK % Q_TILE == 0 and WINDOW % Q_TILE == 0 and s % SEL_CHUNK == 0
    assert N_KV_GROUPS % gps == 0
    kvw = gps * HEAD_DIM
    qb = ATTN_WIDTH // kvw
    kvb = KV_WIDTH // kvw
    n_c = s // CMP_STRIDE
    rows = HEADS_PER_GROUP * Q_TILE
    n_pat = WINDOW // Q_TILE
    grid = (b, N_KV_GROUPS // gps, s // Q_TILE)
    kv_spec = lambda off: pl.BlockSpec((1, s, kvw), lambda bi, g, i: (bi, 0, off + g))
    wide = pl.BlockSpec((1, Q_TILE, gps * GROUP_WIDTH), lambda bi, g, i: (bi, i, g))
    return pl.pallas_call(
        _attn_kernel,
        grid=grid,
        in_specs=[wide,
                  kv_spec(qb), kv_spec(qb + kvb), kv_spec(qb + 2 * kvb), kv_spec(qb + 3 * kvb),
                  pl.BlockSpec((1, gps, n_c, HEAD_DIM), lambda bi, g, i: (bi, g, 0, 0)),
                  pl.BlockSpec((1, gps, n_c, HEAD_DIM), lambda bi, g, i: (bi, g, 0, 0)),
                  pl.BlockSpec((1, Q_TILE, LANES), lambda bi, g, i: (bi, i, 0)),
                  wide,
                  pl.BlockSpec(aggt.shape, lambda bi, g, i: (0, 0)),
                  pl.BlockSpec(blockid.shape, lambda bi, g, i: (0, 0)),
                  pl.BlockSpec((1, Q_TILE, WINDOW + Q_TILE),
                               lambda bi, g, i: (jnp.minimum(i, n_pat), 0, 0)),
                  pl.BlockSpec((gps,) + gsel.shape[1:], lambda bi, g, i: (g, 0, 0))],
        out_specs=wide,
        out_shape=jax.ShapeDtypeStruct((b, s, ATTN_WIDTH), jnp.bfloat16),
        scratch_shapes=[pltpu.VMEM((gps, s, HEAD_DIM + LANES), jnp.bfloat16),
                        pltpu.VMEM((gps, s, HEAD_DIM + LANES), jnp.bfloat16),
                        pltpu.VMEM((gps, s, HEAD_DIM + LANES), jnp.bfloat16),
                        pltpu.VMEM((gps, n_c, HEAD_DIM + LANES), jnp.bfloat16),
                        pltpu.VMEM((gps, rows, s), jnp.float32),
                        pltpu.VMEM((gps, rows, LANES), jnp.float32),
                        pltpu.VMEM((gps, rows, HEAD_DIM + LANES), jnp.float32)],
        compiler_params=_params("arbitrary", "arbitrary", "arbitrary"),
        name="nsa_attention",
    )(qkv, qkv, qkv, qkv, qkv, kcb, vcb, gate, zu, aggt, blockid, wbias, gsel)


def _merge_kernel(a_ref, c_ref, wa_ref, wc_ref, g0_ref, g1_ref, o_ref):
    ya = jnp.dot(a_ref[...], wa_ref[...], preferred_element_type=jnp.float32)
    yc = jnp.dot(c_ref[...], wc_ref[...], preferred_element_type=jnp.float32)
    o_ref[...] = (_sigmoid(g0_ref[...]) * ya + _sigmoid(g1_ref[...]) * yc).astype(o_ref.dtype)


def _merge(a, c, wa, wc, zg, gcol0, tm=MERGE_TM, tn=MERGE_TN):
    m, k = a.shape
    n = wa.shape[-1]
    g0 = gcol0 // tn
    g1 = (gcol0 + n) // tn
    return pl.pallas_call(
        _merge_kernel,
        grid=(n // tn, m // tm),
        in_specs=[pl.BlockSpec((tm, k), lambda j, i: (i, 0)),
                  pl.BlockSpec((tm, k), lambda j, i: (i, 0)),
                  pl.BlockSpec((k, tn), lambda j, i: (0, j)),
                  pl.BlockSpec((k, tn), lambda j, i: (0, j)),
                  pl.BlockSpec((tm, tn), lambda j, i: (i, g0 + j)),
                  pl.BlockSpec((tm, tn), lambda j, i: (i, g1 + j))],
        out_specs=pl.BlockSpec((tm, tn), lambda j, i: (i, j)),
        out_shape=jax.ShapeDtypeStruct((m, n), jnp.bfloat16),
        compiler_params=_params("arbitrary", "arbitrary"),
        name="merge",
    )(a, c, wa, wc, zg, zg)


def _out_kernel(y_ref, w_ref, x_ref, g_ref, o_ref):
    r = x_ref[...] + jnp.dot(y_ref[...], w_ref[...], preferred_element_type=jnp.float32)
    ms = jnp.mean(r * r, axis=-1, keepdims=True)
    o_ref[...] = r * lax.rsqrt(ms + RMS_EPS) * g_ref[...]


def _out_proj(y, w, x2d, g, tm=OUT_TM):
    m, k = y.shape
    n = w.shape[-1]
    return pl.pallas_call(
        _out_kernel,
        grid=(m // tm,),
        in_specs=[pl.BlockSpec((tm, k), lambda i: (i, 0)),
                  pl.BlockSpec((k, n), lambda i: (0, 0), pipeline_mode=pl.Buffered(1)),
                  pl.BlockSpec((tm, n), lambda i: (i, 0)),
                  pl.BlockSpec((1, n), lambda i: (0, 0))],
        out_specs=pl.BlockSpec((tm, n), lambda i: (i, 0)),
        out_shape=jax.ShapeDtypeStruct((m, n), jnp.float32),
        compiler_params=_params("arbitrary"),
        name="out_proj",
    )(y, w, x2d, g.reshape(1, n))


def _attention_constants(s):
    n_slc = s // SEL_BLOCK
    n_cmp = (s - CMP_BLOCK) // CMP_STRIDE + 1
    c0 = np.arange(s // CMP_STRIDE)[None, :] * CMP_STRIDE
    s0 = np.arange(n_slc)[:, None] * SEL_BLOCK
    overlap = np.maximum(0, np.minimum(c0 + CMP_BLOCK, s0 + SEL_BLOCK) - np.maximum(c0, s0))
    aggt = (overlap / CMP_BLOCK) * (np.arange(s // CMP_STRIDE)[None, :] < n_cmp)
    blockid = (np.arange(s)[:, None] // SEL_BLOCK == np.arange(LANES)[None, :]).astype(np.float32)
    n_pat = WINDOW // Q_TILE
    pats = []
    for p in range(n_pat + 1):
        t = p * Q_TILE + np.arange(Q_TILE)[:, None]
        kpos = max(p * Q_TILE - WINDOW, 0) + np.arange(WINDOW + Q_TILE)[None, :]
        pats.append(np.where((kpos <= t) & (kpos > t - WINDOW), 0.0, MASKED))
    n_bh = 3 * HEADS_PER_GROUP
    gsel = np.zeros((N_KV_GROUPS, 2 * LANES, n_bh * LANES), np.float32)
    for g in range(N_KV_GROUPS):
        for br in range(3):
            for h in range(HEADS_PER_GROUP):
                src = br * N_HEADS + g * HEADS_PER_GROUP + h
                c = br * HEADS_PER_GROUP + h
                gsel[g, [src, LANES + src], c * LANES:(c + 1) * LANES] = 1.0
    return (jnp.asarray(aggt, jnp.bfloat16), jnp.asarray(blockid, jnp.bfloat16),
            jnp.asarray(np.stack(pats), jnp.float32), jnp.asarray(gsel, jnp.bfloat16))


def _layer(x, norm_g, w_in, b_in, pe_k, w1_k, w2_k, pe_v, w1_v, w2_v,
           conv_w, conv_b, p_attn, p_conv, w_o, out_g):
    b, s, d = x.shape
    m = b * s
    bf = jnp.bfloat16
    x2d = x.reshape(m, d)
    tn = PROJ_TN

    c_q = ATTN_WIDTH
    c_cmp = c_q + 2 * KV_WIDTH
    c_kv = c_cmp + 4 * KV_WIDTH
    n_gate = 3 * N_HEADS
    c_z = c_kv + n_gate
    c_conv = c_z + ATTN_WIDTH
    c_merge = c_conv + 4 * CONV_WIDTH
    assert c_cmp - c_q == tn and c_q % tn == 0 and (c_kv - c_cmp) % tn == 0
    assert all(c % W_ALIGN == 0 for c in (c_q, c_cmp, c_kv, c_z, c_conv, c_merge))
    wt = jnp.swapaxes(w_in, 1, 2).reshape(w_in.shape[2], w_in.shape[1])

    hn, kvc, gate = _norm_proj(x2d, norm_g, wt, jnp.concatenate([b_in[c_q:c_cmp], b_in[c_kv:c_kv + LANES]]),
                               c_q, c_cmp - c_q, c_kv)
    kvc = kvc.reshape(b, s, -1)
    q_tiles = c_q // tn
    qkv = _proj(hn, wt, jnp.concatenate([b_in[:c_q], b_in[c_cmp:c_kv]]),
                lambda j: jnp.where(j < q_tiles, j * tn, c_cmp + (j - q_tiles) * tn),
                q_tiles + (c_kv - c_cmp) // tn, tn, bf, QKV_TM, QKV_SLOTS,
                scaled_tiles=q_tiles, scale=SCALE * math.log2(math.e), name="proj_qkv").reshape(b, s, -1)
    z_tiles = ATTN_WIDTH // tn
    zg = _proj(hn, wt, jnp.concatenate([b_in[c_z:c_conv], b_in[c_merge:]]),
               lambda j: jnp.where(j < z_tiles, c_z + j * tn, c_merge + (j - z_tiles) * tn),
               z_tiles + 2 * D_MODEL // tn, tn, jnp.float32, ZG_TM, ZG_SLOTS, name="proj_zg").reshape(b, s, -1)
    cw = CONV_TN
    b_conv = b_in[c_conv:c_merge].reshape(4, CONV_WIDTH // cw, cw).transpose(1, 0, 2).reshape(1, -1)
    conv_starts = [lambda j, p=p: c_conv + p * CONV_WIDTH + j * cw for p in range(4)]
    c, (wa, wc, wo) = _proj_conv(hn, wt, b_conv, conv_w, conv_b, conv_starts, s, (p_attn, p_conv, w_o), cw=cw)

    kvcb = _compress(kvc, pe_k, w1_k, w2_k, pe_v, w1_v, w2_v)
    kcb, vcb = kvcb[0], kvcb[1]

    aggt, blockid, wbias, gsel = _attention_constants(s)
    a = _attention(qkv, kcb, vcb, gate.reshape(b, s, LANES), zg, aggt, blockid, wbias, gsel)

    y = _merge(a.reshape(m, -1), c, wa, wc, zg.reshape(m, -1), ATTN_WIDTH)
    return _out_proj(y, wo, x2d, out_g).reshape(b, s, d)


def kernel(x, norm_g, w_in, b_in, cmp_pe_k, cmp_w1_k, cmp_w2_k, cmp_pe_v, cmp_w1_v, cmp_w2_v,
           conv_w, conv_b, p_attn, p_conv, w_o, final_g):
    assert norm_g.shape[0] == 1, "single-layer block"
    return _layer(x, norm_g[0], w_in, b_in[0], cmp_pe_k[0], cmp_w1_k[0], cmp_w2_k[0],
                  cmp_pe_v[0], cmp_w1_v[0], cmp_w2_v[0], conv_w[0], conv_b[0],
                  p_attn, p_conv, w_o, final_g)
```
